```python
import math
import jax, jax.numpy as jnp
from jax import lax
import numpy as np

D_MODEL = 1024
BATCH = 8
SEQ = 4096
DEPTH = 4

CTX_LEN = 256
GRID_W = 64
N_MIXERS = 4
N_SUB = 3
N_MOD = 3 * N_SUB
D_FF = 2816
RMS_EPS = 1e-6
NEG_INF = -1e30
ROPE_BASE = 10000.0

S5_GROUP = 16
S5_GROUPS = D_MODEL // S5_GROUP
S5_STATE = 64
S5_DT_MIN = 1e-3
S5_DT_MAX = 1e-1

DIFF_HEAD_DIM = 64
DIFF_HEADS = D_MODEL // (2 * DIFF_HEAD_DIM)
Q_BLOCK = 128

NA_HEADS = 16
NA_HEAD_DIM = D_MODEL // NA_HEADS
WIN_H = 8
WIN_W = 16

HG_EXPAND = 128
HG_HEADS = D_MODEL // HG_EXPAND
HG_CHUNK = 64

kernel_name = "hybrid_s5_diffattn_natten_hgrn2_macaron_dit"

f32 = jnp.float32


def rms_norm(x, gain):
    xf = x.astype(f32)
    xf = xf * lax.rsqrt(jnp.mean(jnp.square(xf), axis=-1, keepdims=True) + RMS_EPS)
    return (xf * gain.astype(f32)).astype(x.dtype)


def modulate(h, shift, scale):
    return h * (1 + scale) + shift


def pre_norm(h, g, mod):
    shift, scale, _ = mod
    return modulate(rms_norm(h, g), shift, scale)


def post_residual(h, y, g, mod, weight):
    return h + weight * mod[2] * rms_norm(y, g)


def swiglu(h, w1, w3, w2):
    return (jax.nn.silu(h @ w1) * (h @ w3)) @ w2


def half_ffn(h, g_in, g_out, mod, w1, w3, w2):
    y = swiglu(pre_norm(h, g_in, mod), w1, w3, w2)
    return post_residual(h, y, g_out, mod, 0.5)


def axial_rope(n_tokens, head_dim):
    n_freq = head_dim // 4
    inv_freq = ROPE_BASE ** (-jnp.arange(n_freq, dtype=f32) / n_freq)
    t = jnp.arange(n_tokens)
    row = (t // GRID_W).astype(f32)
    col = (t % GRID_W).astype(f32)
    ang = jnp.concatenate([row[:, None] * inv_freq, col[:, None] * inv_freq], axis=-1)
    return jnp.cos(ang), jnp.sin(ang)


def apply_rope(x, cos, sin):
    xp = x.astype(f32).reshape(x.shape[:-1] + (-1, 2))
    x1, x2 = xp[..., 0], xp[..., 1]
    out = jnp.stack([x1 * cos - x2 * sin, x1 * sin + x2 * cos], axis=-1)
    return out.reshape(x.shape).astype(x.dtype)


def lti_scan(lam_bar, bu, h0):
    if h0 is not None:
        bu = bu.at[0].add(lam_bar * h0)
    a = jnp.broadcast_to(lam_bar, (bu.shape[0], 1) + lam_bar.shape)

    def combine(left, right):
        a1, b1 = left
        a2, b2 = right
        return a1 * a2, a2 * b1 + b2

    _, h = lax.associative_scan(combine, (a, bu), axis=0)
    return h


def s5_mixer(h_lat, h_ctx, a_re, a_im, log_dt, b_re, b_im, c_re, c_im, d_skip, w_glu, b_glu, ctx_out):
    B, S, D = h_lat.shape
    L = h_ctx.shape[1]
    G, N = S5_GROUPS, S5_GROUP
    u_l = h_lat.astype(f32).reshape(B, S, G, N)
    u_c = h_ctx.astype(f32).reshape(B, L, G, N)
    d_vec = d_skip.astype(f32)
    y_l = d_vec * h_lat.astype(f32)
    y_c = d_vec * h_ctx.astype(f32) if ctx_out else None
    for dirn in range(2):
        flip = (lambda a: jnp.flip(a, axis=0)) if dirn else (lambda a: a)
        lam = lax.complex(a_re[dirn].astype(f32), a_im[dirn].astype(f32))
        dt = jnp.exp(log_dt[dirn].astype(f32))[:, None]
        lam_bar = jnp.exp(lam * dt)
        b_bar = ((lam_bar - 1) / lam)[:, :, None] * lax.complex(b_re[dirn].astype(f32), b_im[dirn].astype(f32))
        c_mat = lax.complex(c_re[dirn].astype(f32), c_im[dirn].astype(f32))
        h_c = lti_scan(lam_bar, flip(jnp.einsum('gpn,btgn->tbgp', b_bar, u_c)), None)
        h_l = lti_scan(lam_bar, flip(jnp.einsum('gpn,btgn->tbgp', b_bar, u_l)), h_c[-1])
        y_l = y_l + jnp.real(jnp.einsum('gnp,tbgp->btgn', c_mat, flip(h_l))).reshape(B, S, D)
        if ctx_out:
            y_c = y_c + jnp.real(jnp.einsum('gnp,tbgp->btgn', c_mat, flip(h_c))).reshape(B, L, D)
    w = w_glu.astype(f32)
    bb = b_glu.astype(f32)

    def glu(y):
        z = jax.nn.gelu(y)
        return (z * jax.nn.sigmoid(z @ w + bb)).astype(h_lat.dtype)

    return glu(y_l), (glu(y_c) if ctx_out else None)


def diff_attention(h_lat, h_ctx, w_qkv, w_o, lam_q1, lam_k1, lam_q2, lam_k2, subln_g, layer_idx, ctx_out):
    B, S, D = h_lat.shape
    H, d = DIFF_HEADS, DIFF_HEAD_DIM
    lam_init = 0.8 - 0.6 * math.exp(-0.3 * layer_idx)
    lam = (jnp.exp(jnp.sum(lam_q1.astype(f32) * lam_k1.astype(f32)))
           - jnp.exp(jnp.sum(lam_q2.astype(f32) * lam_k2.astype(f32))) + lam_init)

    def project(h):
        T = h.shape[1]
        q, k, v = jnp.split(h @ w_qkv, 3, axis=-1)
        q = q.reshape(B, T, H, 2, d).transpose(0, 2, 3, 1, 4)
        k = k.reshape(B, T, H, 2, d).transpose(0, 2, 3, 1, 4)
        v = v.reshape(B, T, H, 2 * d).transpose(0, 2, 1, 3)
        return q, k, v

    def mix(q, k, v):
        s = jnp.einsum('bhmqd,bhmkd->bhmqk', q, k).astype(f32) * (d ** -0.5)
        p = jax.nn.softmax(s, axis=-1)
        a = p[:, :, 0] - lam * p[:, :, 1]
        return jnp.einsum('bhqk,bhkd->bhqd', a.astype(v.dtype), v)

    def readout(o):
        T = o.shape[2]
        o = rms_norm(o, subln_g) * (1 - lam_init)
        return o.transpose(0, 2, 1, 3).reshape(B, T, D) @ w_o

    q_c, k_c, v_c = project(h_ctx)
    q_l, k_l, v_l = project(h_lat)
    cos, sin = axial_rope(S, d)
    q_l = apply_rope(q_l, cos, sin)
    k_l = apply_rope(k_l, cos, sin)
    k_all = jnp.concatenate([k_c, k_l], axis=3)
    v_all = jnp.concatenate([v_c, v_l], axis=2)
    n_blk = S // Q_BLOCK
    q_blocks = jnp.moveaxis(q_l.reshape(B, H, 2, n_blk, Q_BLOCK, d), 3, 0)
    o_l = lax.map(lambda qb: mix(qb, k_all, v_all), q_blocks)
    o_l = jnp.moveaxis(o_l, 0, 2).reshape(B, H, S, 2 * d)
    y_ctx = readout(mix(q_c, k_c, v_c)) if ctx_out else None
    return readout(o_l), y_ctx


def neighbourhood_attention(h_lat, h_ctx, w_qkv, w_o, rpb, ctx_out):
    B, S, D = h_lat.shape
    L = h_ctx.shape[1]
    H, d = NA_HEADS, NA_HEAD_DIM
    rows = S // GRID_W
    kh = min(WIN_H, rows)
    scale = d ** -0.5

    def project(h):
        T = h.shape[1]
        qkv = (h @ w_qkv).reshape(B, T, 3, H, d).transpose(2, 0, 3, 1, 4)
        return qkv[0], qkv[1], qkv[2]

    q_c, k_c, v_c = project(h_ctx)
    q_l, k_l, v_l = project(h_lat)
    q_l, k_l, v_l = [a.reshape(B, H, rows, GRID_W, d) for a in (q_l, k_l, v_l)]

    col = jnp.arange(GRID_W)
    c0 = jnp.clip(col - WIN_W // 2, 0, GRID_W - WIN_W)
    col_mask = (col[None, :] >= c0[:, None]) & (col[None, :] < c0[:, None] + WIN_W)
    dc_idx = jnp.clip(col[None, :] - col[:, None] + WIN_W - 1, 0, 2 * WIN_W - 2)
    rpb_cols = rpb.astype(f32)[:, :, dc_idx]

    def row_block(r):
        r0 = jnp.clip(r - kh // 2, 0, rows - kh)
        q_r = lax.dynamic_index_in_dim(q_l, r, axis=2, keepdims=False)
        k_b = lax.dynamic_slice_in_dim(k_l, r0, kh, axis=2)
        v_b = lax.dynamic_slice_in_dim(v_l, r0, kh, axis=2).reshape(B, H, kh * GRID_W, d)
        dr_idx = r0 + jnp.arange(kh) - r + WIN_H - 1
        bias = rpb_cols[:, dr_idx].transpose(0, 2, 1, 3)
        s_lat = jnp.einsum('bhqd,bhrkd->bhqrk', q_r, k_b).astype(f32) * scale + bias[None]
        s_lat = jnp.where(col_mask[:, None, :], s_lat, NEG_INF).reshape(B, H, GRID_W, kh * GRID_W)
        s_ctx = jnp.einsum('bhqd,bhkd->bhqk', q_r, k_c).astype(f32) * scale
        p = jax.nn.softmax(jnp.concatenate([s_ctx, s_lat], axis=-1), axis=-1)
        return (jnp.einsum('bhqk,bhkd->bhqd', p[..., :L].astype(v_c.dtype), v_c)
                + jnp.einsum('bhqk,bhkd->bhqd', p[..., L:].astype(v_b.dtype), v_b))

    o_l = lax.map(row_block, jnp.arange(rows))
    y_lat = o_l.transpose(1, 0, 3, 2, 4).reshape(B, S, D) @ w_o
    y_ctx = None
    if ctx_out:
        p = jax.nn.softmax(jnp.einsum('bhqd,bhkd->bhqk', q_c, k_c).astype(f32) * scale, axis=-1)
        o_c = jnp.einsum('bhqk,bhkd->bhqd', p.astype(v_c.dtype), v_c)
        y_ctx = o_c.transpose(0, 2, 1, 3).reshape(B, L, D) @ w_o
    return y_lat, y_ctx


def gla_chunks(q, k, v, log_f, s0, with_output):
    B, H, T, K = k.shape
    V = v.shape[-1]
    C = HG_CHUNK
    n = T // C
    chunk = lambda a: a.reshape(B, H, n, C, a.shape[-1])
    k, v, log_f = chunk(k), chunk(v), chunk(log_f)
    g = jnp.cumsum(log_f, axis=3)
    g_last = g[:, :, :, -1]
    ds = jnp.einsum('bhnck,bhncv->bhnkv', k * jnp.exp(g_last[:, :, :, None] - g), v)

    def step(s, inp):
        decay, ds_n = inp
        return decay[..., None] * s + ds_n, s

    s_last, s_start = lax.scan(step, s0, (jnp.moveaxis(jnp.exp(g_last), 2, 0), jnp.moveaxis(ds, 2, 0)))
    if not with_output:
        return None, s_last
    q = chunk(q)
    q_dec = q * jnp.exp(g)
    k_inv = k * jnp.exp(-g)
    earlier_or_same = jnp.tril(jnp.ones((C, C), bool))
    att = jnp.where(earlier_or_same, jnp.einsum('bhnck,bhnsk->bhncs', q_dec, k_inv), 0.0)
    o = (jnp.einsum('bhncs,bhnsv->bhncv', att, v)
         + jnp.einsum('bhnck,bhnkv->bhncv', q_dec, jnp.moveaxis(s_start, 0, 2)))
    return o.reshape(B, H, T, V), s_last


def hgrn2_mixer(h_lat, h_ctx, w_qig, w_f, b_f, lb, gn_g, w_o, ctx_out):
    B, S, D = h_lat.shape
    L = h_ctx.shape[1]
    H = HG_HEADS
    lb = lb.astype(f32)
    heads = lambda a: a.astype(f32).reshape(B, a.shape[1], H, -1).transpose(0, 2, 1, 3)

    def log_forget(h, dirn):
        z = (h @ w_f[dirn] + b_f[dirn]).astype(f32)
        return heads(jnp.log(lb + (1 - lb) * jax.nn.sigmoid(z)))

    q_l, i_l, g_l = jnp.split(h_lat @ w_qig, 3, axis=-1)
    q_l, i_l = heads(q_l), heads(i_l)
    if ctx_out:
        q_c, i_c, g_c = jnp.split(h_ctx @ w_qig, 3, axis=-1)
        q_c = heads(q_c)
    else:
        i_c = h_ctx @ w_qig[:, D:2 * D]
    i_c = heads(i_c)
    s0 = jnp.zeros((B, H, HG_EXPAND, D // H), f32)
    o_l = 0.0
    o_c = 0.0
    for dirn in range(2):
        flip = (lambda a: jnp.flip(a, axis=2)) if dirn else (lambda a: a)
        lf_c = log_forget(h_ctx, dirn)
        lf_l = log_forget(h_lat, dirn)
        oc, s_c = gla_chunks(flip(q_c) if ctx_out else None, flip(-jnp.expm1(lf_c)), flip(i_c),
                             flip(lf_c), s0, ctx_out)
        ol, _ = gla_chunks(flip(q_l), flip(-jnp.expm1(lf_l)), flip(i_l), flip(lf_l), s_c, True)
        o_l = o_l + flip(ol)
        if ctx_out:
            o_c = o_c + flip(oc)

    def readout(o, gate):
        T = o.shape[2]
        o = rms_norm(o, gn_g).transpose(0, 2, 1, 3).reshape(B, T, D)
        return (o * jax.nn.silu(gate.astype(f32))).astype(h_lat.dtype) @ w_o

    return readout(o_l, g_l), (readout(o_c, g_c) if ctx_out else None)


def setup_inputs(seed: int = 0) -> dict:
    key = jax.random.key(seed)
    keys = iter(jax.random.split(key, 64))

    def nrm(shape, scale):
        return scale * jax.random.normal(next(keys), shape, f32)

    def gain(shape):
        return 1.0 + 0.02 * jax.random.normal(next(keys), shape, f32)

    nA, nB, nC, nD = [len(range(m, DEPTH, N_MIXERS)) for m in range(N_MIXERS)]
    D = D_MODEL
    G, N, P = S5_GROUPS, S5_GROUP, S5_STATE
    sd = D ** -0.5
    n_idx = jnp.arange(P, dtype=f32)
    return {
        'x': nrm((BATCH, SEQ, D), 1.0),
        'c': nrm((BATCH, D), 1.0),
        'ctx': nrm((BATCH, CTX_LEN, D), 1.0),
        'c_ctx': nrm((D,), 1.0),
        'w_ada': nrm((DEPTH, D, N_MOD * D), sd),
        'b_ada': nrm((DEPTH, N_MOD * D), 0.02),
        'g_pre': gain((DEPTH, N_SUB, D)),
        'g_post': gain((DEPTH, N_SUB, D)),
        'w_ff1': nrm((DEPTH, 2, D, D_FF), sd),
        'w_ff3': nrm((DEPTH, 2, D, D_FF), sd),
        'w_ff2': nrm((DEPTH, 2, D_FF, D), D_FF ** -0.5),
        's5_a_re': -0.5 * gain((nA, 2, G, P)),
        's5_a_im': math.pi * n_idx + nrm((nA, 2, G, P), 0.01),
        's5_log_dt': jax.random.uniform(next(keys), (nA, 2, G), f32,
                                        minval=math.log(S5_DT_MIN), maxval=math.log(S5_DT_MAX)),
        's5_b_re': nrm((nA, 2, G, P, N), (2 * N) ** -0.5),
        's5_b_im': nrm((nA, 2, G, P, N), (2 * N) ** -0.5),
        's5_c_re': nrm((nA, 2, G, N, P), (2 * P) ** -0.5),
        's5_c_im': nrm((nA, 2, G, N, P), (2 * P) ** -0.5),
        's5_d': nrm((nA, D), 1.0),
        's5_w_glu': nrm((nA, D, D), sd),
        's5_b_glu': nrm((nA, D), 0.02),
        'da_w_qkv': nrm((nB, D, 3 * D), sd),
        'da_w_o': nrm((nB, D, D), sd),
        'da_lam_q1': nrm((nB, DIFF_HEAD_DIM), 0.1),
        'da_lam_k1': nrm((nB, DIFF_HEAD_DIM), 0.1),
        'da_lam_q2': nrm((nB, DIFF_HEAD_DIM), 0.1),
        'da_lam_k2': nrm((nB, DIFF_HEAD_DIM), 0.1),
        'da_subln': gain((nB, 2 * DIFF_HEAD_DIM)),
        'na_w_qkv': nrm((nC, D, 3 * D), sd),
        'na_w_o': nrm((nC, D, D), sd),
        'na_rpb': nrm((nC, NA_HEADS, 2 * WIN_H - 1, 2 * WIN_W - 1), 0.02),
        'hg_w_qig': nrm((nD, D, 3 * D), sd),
        'hg_w_f': nrm((nD, 2, D, D), sd),
        'hg_b_f': nrm((nD, 2, D), 0.1),
        'hg_lb_logits': nrm((DEPTH, D), 0.1),
        'hg_gnorm': gain((nD, D // HG_HEADS)),
        'hg_w_o': nrm((nD, D, D), sd),
    }


def reference(x, c, ctx, c_ctx, w_ada, b_ada, g_pre, g_post, w_ff1, w_ff3, w_ff2,
              s5_a_re, s5_a_im, s5_log_dt, s5_b_re, s5_b_im, s5_c_re, s5_c_im, s5_d, s5_w_glu, s5_b_glu,
              da_w_qkv, da_w_o, da_lam_q1, da_lam_k1, da_lam_q2, da_lam_k2, da_subln,
              na_w_qkv, na_w_o, na_rpb,
              hg_w_qig, hg_w_f, hg_b_f, hg_lb_logits, hg_gnorm, hg_w_o):
    B, S, D = x.shape
    lb_p = jax.nn.softmax(hg_lb_logits.astype(f32), axis=0)
    lower_bounds = jnp.cumsum(lb_p, axis=0) - lb_p[0]
    silu_c = jax.nn.silu(c)
    silu_cc = jax.nn.silu(c_ctx)
    x_lat, x_ctx = x, ctx
    for i in range(DEPTH):
        last = i == DEPTH - 1
        occ = i // N_MIXERS
        kind = i % N_MIXERS
        m_lat = (silu_c @ w_ada[i] + b_ada[i]).reshape(B, N_SUB, 3, 1, D)
        m_ctx = (silu_cc @ w_ada[i] + b_ada[i]).reshape(N_SUB, 3, D)
        lat_mod = [(m_lat[:, j, 0], m_lat[:, j, 1], m_lat[:, j, 2]) for j in range(N_SUB)]
        ctx_mod = [(m_ctx[j, 0], m_ctx[j, 1], m_ctx[j, 2]) for j in range(N_SUB)]

        x_lat = half_ffn(x_lat, g_pre[i, 0], g_post[i, 0], lat_mod[0], w_ff1[i, 0], w_ff3[i, 0], w_ff2[i, 0])
        x_ctx = half_ffn(x_ctx, g_pre[i, 0], g_post[i, 0], ctx_mod[0], w_ff1[i, 0], w_ff3[i, 0], w_ff2[i, 0])

        h_lat = pre_norm(x_lat, g_pre[i, 1], lat_mod[1])
        h_ctx = pre_norm(x_ctx, g_pre[i, 1], ctx_mod[1])
        if kind == 0:
            y_lat, y_ctx = s5_mixer(h_lat, h_ctx, s5_a_re[occ], s5_a_im[occ], s5_log_dt[occ], s5_b_re[occ],
                                    s5_b_im[occ], s5_c_re[occ], s5_c_im[occ], s5_d[occ], s5_w_glu[occ],
                                    s5_b_glu[occ], not last)
        elif kind == 1:
            y_lat, y_ctx = diff_attention(h_lat, h_ctx, da_w_qkv[occ], da_w_o[occ], da_lam_q1[occ], da_lam_k1[occ],
                                          da_lam_q2[occ], da_lam_k2[occ], da_subln[occ], i, not last)
        elif kind == 2:
            y_lat, y_ctx = neighbourhood_attention(h_lat, h_ctx, na_w_qkv[occ], na_w_o[occ], na_rpb[occ], not last)
        else:
            y_lat, y_ctx = hgrn2_mixer(h_lat, h_ctx, hg_w_qig[occ], hg_w_f[occ], hg_b_f[occ], lower_bounds[i],
                                       hg_gnorm[occ], hg_w_o[occ], not last)
        x_lat = post_residual(x_lat, y_lat, g_post[i, 1], lat_mod[1], 1.0)

        x_lat = half_ffn(x_lat, g_pre[i, 2], g_post[i, 2], lat_mod[2], w_ff1[i, 1], w_ff3[i, 1], w_ff2[i, 1])
        if not last:
            x_ctx = post_residual(x_ctx, y_ctx, g_post[i, 1], ctx_mod[1], 1.0)
            x_ctx = half_ffn(x_ctx, g_pre[i, 2], g_post[i, 2], ctx_mod[2], w_ff1[i, 1], w_ff3[i, 1], w_ff2[i, 1])
    return x_lat
```

```python
import functools
import math

import jax
import jax.numpy as jnp
from jax import lax
from jax.experimental import pallas as pl
from jax.experimental.pallas import tpu as pltpu

f32 = jnp.float32
bf16 = jnp.bfloat16

N_SUB = 3
RMS_EPS = 1e-6
NEG_INF = -1e30
ROPE_BASE = 10000.0
GRID_W = 64
S5_GROUP = 16
S5_STATE = 64
S5_CHUNK = 16
DIFF_HEAD_DIM = 64
NA_HEAD_DIM = 64
WIN_H = 8
WIN_W = 16
NA_TILE_ROWS = 8
NA_WIN_ROWS = 16
HG_EXPAND = 128
HG_CHUNK = 64
HG_BLOCK = 512
LANES = 128
VMEM_LIMIT = 56 * 1024 * 1024


def _cparams(*sem):
    return pltpu.CompilerParams(dimension_semantics=sem, vmem_limit_bytes=VMEM_LIMIT)


def _dot(a, b):
    return jnp.dot(a, b, preferred_element_type=f32)


def _dot_nt(a, b):
    return lax.dot_general(a, b, (((1,), (1,)), ((), ())), preferred_element_type=f32)


def _rms(x, g):
    return x * lax.rsqrt(jnp.mean(jnp.square(x), axis=-1, keepdims=True) + RMS_EPS) * g


def _pre_norm(x, g, mod_ref, j):
    shift = mod_ref[0, 3 * j:3 * j + 1, :]
    scale = mod_ref[0, 3 * j + 1:3 * j + 2, :]
    return _rms(x, g) * (1 + scale) + shift


def _post_residual(x, y, g, mod_ref, j, weight):
    gate = mod_ref[0, 3 * j + 2:3 * j + 3, :]
    return x + weight * gate * _rms(y, g)


def _tok_tile(t):
    for tm in (512, 256, 128):
        if t % tm == 0:
            return tm
    raise ValueError(f"token count {t} is not a multiple of 128")


def _full(shape):
    return pl.BlockSpec(shape, lambda *_: (0,) * len(shape))


def _mod_spec(mod):
    nd = mod.shape[1]
    d = mod.shape[2]
    if mod.shape[0] == 1:
        return pl.BlockSpec((1, nd, d), lambda b, t: (0, 0, 0))
    return pl.BlockSpec((1, nd, d), lambda b, t: (b, 0, 0))


def _ada_kernel(c_ref, w_ref, b_ref, o_ref):
    c = c_ref[...]
    sc = (c * jax.nn.sigmoid(c)).astype(bf16)
    o_ref[0] = _dot(sc, w_ref[0].astype(bf16)) + b_ref[0]


def _ada(c_all, w_ada, b_ada):
    depth, d, n = w_ada.shape
    rows = c_all.shape[0]
    tn = n // 4
    return pl.pallas_call(
        _ada_kernel,
        grid=(depth, n // tn),
        in_specs=[pl.BlockSpec((rows, d), lambda i, t: (0, 0)),
                  pl.BlockSpec((1, d, tn), lambda i, t: (i, 0, t)),
                  pl.BlockSpec((1, 1, tn), lambda i, t: (i, 0, t))],
        out_specs=pl.BlockSpec((1, rows, tn), lambda i, t: (i, 0, t)),
        out_shape=jax.ShapeDtypeStruct((depth, rows, n), f32),
        compiler_params=_cparams("arbitrary", "arbitrary"),
        name="ada",
    )(c_all, w_ada, b_ada.reshape(depth, 1, n))


def _ffn_kernel(x_ref, mod_ref, gin_ref, gout_ref, w1_ref, w3_ref, w2_ref, o_ref, *, j, chunks):
    x = x_ref[0]
    h = _pre_norm(x, gin_ref[...], mod_ref, j).astype(bf16)
    acc = None
    for s, n in chunks:
        a = _dot(h, w1_ref[:, s:s + n])
        b = _dot(h, w3_ref[:, s:s + n])
        g = (a * jax.nn.sigmoid(a) * b).astype(bf16)
        y = _dot(g, w2_ref[s:s + n, :])
        acc = y if acc is None else acc + y
    o_ref[0] = _post_residual(x, acc, gout_ref[...], mod_ref, j, 0.5)


def _ff_chunks(dff, step=1024):
    out, s = [], 0
    while s < dff:
        n = min(step, dff - s)
        out.append((s, n))
        s += n
    return tuple(out)


def _half_ffn(x, mod, g_in, g_out, w1, w3, w2, j):
    b, t, d = x.shape
    dff = w1.shape[1]
    tm = _tok_tile(t)
    return pl.pallas_call(
        functools.partial(_ffn_kernel, j=j, chunks=_ff_chunks(dff)),
        grid=(b, t // tm),
        in_specs=[pl.BlockSpec((1, tm, d), lambda bb, tt: (bb, tt, 0)),
                  _mod_spec(mod), _full((1, d)), _full((1, d)),
                  _full((d, dff)), _full((d, dff)), _full((dff, d))],
        out_specs=pl.BlockSpec((1, tm, d), lambda bb, tt: (bb, tt, 0)),
        out_shape=jax.ShapeDtypeStruct((b, t, d), f32),
        compiler_params=_cparams("parallel", "parallel"),
        name="half_ffn",
    )(x, mod, g_in.reshape(1, d), g_out.reshape(1, d), w1, w3, w2)


def _prenorm_kernel(x_ref, mod_ref, g_ref, o_ref, *, j):
    o_ref[0] = _pre_norm(x_ref[0], g_ref[...], mod_ref, j)


def _prenorm(x, mod, g, j):
    b, t, d = x.shape
    tm = _tok_tile(t)
    return pl.pallas_call(
        functools.partial(_prenorm_kernel, j=j),
        grid=(b, t // tm),
        in_specs=[pl.BlockSpec((1, tm, d), lambda bb, tt: (bb, tt, 0)), _mod_spec(mod), _full((1, d))],
        out_specs=pl.BlockSpec((1, tm, d), lambda bb, tt: (bb, tt, 0)),
        out_shape=jax.ShapeDtypeStruct((b, t, d), f32),
        compiler_params=_cparams("parallel", "parallel"),
        name="prenorm",
    )(x, mod, g.reshape(1, d))


def _swap_pairs(x):
    lane = lax.broadcasted_iota(jnp.int32, x.shape, 1)
    nxt = pltpu.roll(x, LANES - 1, 1)
    prv = pltpu.roll(x, 1, 1)
    return jnp.where((lane & 1) == 0, nxt, prv)


def _proj_kernel(*refs, j, n_out, n_rope, has_bias, step):
    x_ref, mod_ref, g_ref, w_ref = refs[:4]
    k = 4
    b_ref = None
    if has_bias:
        b_ref = refs[k]
        k += 1
    if n_rope:
        cos_ref, sin_ref = refs[k], refs[k + 1]
        k += 2
    o_ref = refs[k]
    h = _pre_norm(x_ref[0], g_ref[...], mod_ref, j).astype(bf16)
    for s in range(0, n_out, step):
        y = _dot(h, w_ref[:, s:s + step])
        if has_bias:
            y = y + b_ref[:, s:s + step]
        if s < n_rope:
            cos = cos_ref[...]
            sin = sin_ref[...]
            parts = []
            for c in range(0, step, LANES):
                yc = y[:, c:c + LANES]
                parts.append(yc * cos + _swap_pairs(yc) * sin)
            y = jnp.concatenate(parts, axis=1)
        o_ref[0, :, s:s + step] = y.astype(o_ref.dtype)


def _prenorm_proj(x, mod, g, j, w, bias=None, rope=None, n_rope=0, out_dtype=bf16):
    b, t, d = x.shape
    n = w.shape[1]
    tm = min(_tok_tile(t), 256)
    step = 512
    assert n % step == 0 and n_rope % step == 0
    args = [x, mod, g.reshape(1, d), w]
    specs = [pl.BlockSpec((1, tm, d), lambda bb, tt: (bb, tt, 0)), _mod_spec(mod), _full((1, d)), _full((d, n))]
    if bias is not None:
        args.append(bias.reshape(1, n))
        specs.append(_full((1, n)))
    if n_rope:
        args += list(rope)
        specs += [pl.BlockSpec((tm, LANES), lambda bb, tt: (tt, 0))] * 2
    return pl.pallas_call(
        functools.partial(_proj_kernel, j=j, n_out=n, n_rope=n_rope, has_bias=bias is not None, step=step),
        grid=(b, t // tm),
        in_specs=specs,
        out_specs=pl.BlockSpec((1, tm, n), lambda bb, tt: (bb, tt, 0)),
        out_shape=jax.ShapeDtypeStruct((b, t, n), out_dtype),
        compiler_params=_cparams("parallel", "parallel"),
        name="prenorm_proj",
    )(*args)


def _outproj_kernel(y_ref, x_ref, mod_ref, g_ref, w_ref, o_ref, *, j):
    y = _dot(y_ref[0], w_ref[...])
    o_ref[0] = _post_residual(x_ref[0], y, g_ref[...], mod_ref, j, 1.0)


def _outproj_residual(y, x, mod, g, w, j):
    b, t, d = x.shape
    tm = _tok_tile(t)
    tok = pl.BlockSpec((1, tm, d), lambda bb, tt: (bb, tt, 0))
    return pl.pallas_call(
        functools.partial(_outproj_kernel, j=j),
        grid=(b, t // tm),
        in_specs=[tok, tok, _mod_spec(mod), _full((1, d)), _full((d, d))],
        out_specs=tok,
        out_shape=jax.ShapeDtypeStruct((b, t, d), f32),
        compiler_params=_cparams("parallel", "parallel"),
        name="outproj_residual",
    )(y, x, mod, g.reshape(1, d), w)


def _s5_kernel(x_ref, wb_ref, wk_ref, wc_ref, lam_ref, d_ref, o_ref, s_ref, hin_ref, *, nb, nc_ctx, nc):
    x = x_ref[0]
    xb = x.astype(bf16)
    s_ref[...] = _dot(xb, wb_ref[0])
    half = S5_STATE * 2
    lam = lam_ref[0]
    a_f, b_f, a_b, b_b = lam[0:1], lam[1:2], lam[2:3], lam[3:4]

    def step(n, carry):
        h_f, h_b = carry
        r_f = pl.multiple_of(n * nb, nb)
        n_b = jnp.where(n < nc_ctx, nc_ctx - 1 - n, nc - 1 - (n - nc_ctx))
        r_b = pl.multiple_of(n_b * nb, nb)
        hin_ref[pl.ds(r_f, nb), 0:half] = h_f
        hin_ref[pl.ds(r_b, nb), half:2 * half] = h_b
        h_f = a_f * h_f + b_f * pltpu.roll(h_f, S5_STATE, 1) + s_ref[pl.ds(r_f, nb), 0:half]
        h_b = a_b * h_b + b_b * pltpu.roll(h_b, S5_STATE, 1) + s_ref[pl.ds(r_b, nb), half:2 * half]
        return h_f, h_b

    zero = jnp.zeros((nb, half), f32)
    lax.fori_loop(0, nc, step, (zero, zero))
    y = x * d_ref[0] + _dot(xb, wk_ref[0]) + _dot(hin_ref[...].astype(bf16), wc_ref[0])
    o_ref[0] = y


def _s5_weights(a_re, a_im, log_dt, b_re, b_im, c_re, c_im):
    hp = lax.Precision.HIGHEST
    C = S5_CHUNK
    G, P = a_re.shape[1:]
    N = b_re.shape[-1]
    j = jnp.arange(C + 1, dtype=f32)
    wbs, wks, wcs, lams = [], [], [], []
    for dirn in range(2):
        are, aim = a_re[dirn].astype(f32), a_im[dirn].astype(f32)
        dt = jnp.exp(log_dt[dirn].astype(f32))[:, None]
        pw_mag = jnp.exp(j[:, None, None] * (are * dt)[None])
        pw_re = pw_mag * jnp.cos(j[:, None, None] * (aim * dt)[None])
        pw_im = pw_mag * jnp.sin(j[:, None, None] * (aim * dt)[None])
        nr, ni = pw_re[1] - 1.0, pw_im[1]
        den = are * are + aim * aim
        fr = (nr * are + ni * aim) / den
        fi = (ni * are - nr * aim) / den
        bre, bim = b_re[dirn].astype(f32), b_im[dirn].astype(f32)
        bbr = fr[..., None] * bre - fi[..., None] * bim
        bbi = fr[..., None] * bim + fi[..., None] * bre
        cre, cim = c_re[dirn].astype(f32), c_im[dirn].astype(f32)
        dist = jnp.arange(C - 1, -1, -1) if dirn == 0 else jnp.arange(C)
        sr = pw_re[dist][..., None] * bbr[None] - pw_im[dist][..., None] * bbi[None]
        si = pw_re[dist][..., None] * bbi[None] + pw_im[dist][..., None] * bbr[None]
        wb = jnp.concatenate([sr, si], axis=2)
        wbs.append(wb.transpose(1, 0, 3, 2).reshape(G, C * N, 2 * P))
        steps = jnp.arange(1, C + 1) if dirn == 0 else jnp.arange(C, 0, -1)
        pr, pi = pw_re[steps], pw_im[steps]
        or_ = cre[None] * pr[:, :, None, :] - cim[None] * pi[:, :, None, :]
        oi_ = -(cre[None] * pi[:, :, None, :] + cim[None] * pr[:, :, None, :])
        wc = jnp.concatenate([or_, oi_], axis=3)
        wcs.append(wc.transpose(1, 3, 0, 2).reshape(G, 2 * P, C * N))
        clr = cre[None] * pw_re[:C, :, None, :] - cim[None] * pw_im[:C, :, None, :]
        cli = cre[None] * pw_im[:C, :, None, :] + cim[None] * pw_re[:C, :, None, :]
        kk = (jnp.einsum('jgnp,gpm->gjnm', clr, bbr, precision=hp)
              - jnp.einsum('jgnp,gpm->gjnm', cli, bbi, precision=hp))
        s_idx = jnp.arange(C)[:, None]
        t_idx = jnp.arange(C)[None, :]
        lag = (t_idx - s_idx) if dirn == 0 else (s_idx - t_idx)
        kt = jnp.where((lag >= 0)[None, :, :, None, None], kk[:, jnp.clip(lag, 0, C - 1)], 0.0)
        wks.append(kt.transpose(0, 1, 4, 2, 3).reshape(G, C * N, C * N))
        lr, li = pw_re[C], pw_im[C]
        lams += [jnp.concatenate([lr, lr], axis=-1), jnp.concatenate([-li, li], axis=-1)]
    wb = jnp.concatenate(wbs, axis=2).astype(bf16)
    wc = jnp.concatenate(wcs, axis=1).astype(bf16)
    wk = (wks[0] + wks[1]).astype(bf16)
    lam = jnp.stack(lams, axis=1)
    return wb, wk, wc, lam


def _s5_scan(h_lat, h_ctx, a_re, a_im, log_dt, b_re, b_im, c_re, c_im, d_skip):
    B, S, D = h_lat.shape
    L = h_ctx.shape[1]
    C, N, P = S5_CHUNK, S5_GROUP, S5_STATE
    G = D // N
    nc_ctx, nc = L // C, (L + S) // C
    assert 2 * P == LANES and L % C == 0 and S % C == 0
    wb, wk, wc, lam = _s5_weights(a_re, a_im, log_dt, b_re, b_im, c_re, c_im)
    d_t = jnp.tile(d_skip.astype(f32).reshape(G, 1, N), (1, C, 1)).reshape(G, 1, C * N)
    hcat = jnp.concatenate([h_ctx, h_lat], axis=1)
    xg = hcat.reshape(B, nc, C, G, N).transpose(3, 1, 0, 2, 4).reshape(G, nc * B, C * N)
    R = nc * B
    y = pl.pallas_call(
        functools.partial(_s5_kernel, nb=B, nc_ctx=nc_ctx, nc=nc),
        grid=(G,),
        in_specs=[pl.BlockSpec((1, R, C * N), lambda g: (g, 0, 0)),
                  pl.BlockSpec((1, C * N, 4 * P), lambda g: (g, 0, 0)),
                  pl.BlockSpec((1, C * N, C * N), lambda g: (g, 0, 0)),
                  pl.BlockSpec((1, 4 * P, C * N), lambda g: (g, 0, 0)),
                  pl.BlockSpec((1, 4, 2 * P), lambda g: (g, 0, 0)),
                  pl.BlockSpec((1, 1, C * N), lambda g: (g, 0, 0))],
        out_specs=pl.BlockSpec((1, R, C * N), lambda g: (g, 0, 0)),
        out_shape=jax.ShapeDtypeStruct((G, R, C * N), f32),
        scratch_shapes=[pltpu.VMEM((R, 4 * P), f32), pltpu.VMEM((R, 4 * P), f32)],
        compiler_params=_cparams("parallel"),
        name="s5_scan",
    )(xg, wb, wk, wc, lam, d_t)
    ycat = y.reshape(G, nc, B, C, N).transpose(2, 1, 3, 0, 4).reshape(B, L + S, D)
    return ycat[:, L:], ycat[:, :L]


def _glu_kernel(y_ref, x_ref, mod_ref, g_ref, w_ref, b_ref, o_ref, *, j):
    z = jax.nn.gelu(y_ref[0])
    u = _dot(z.astype(bf16), w_ref[...]) + b_ref[...]
    o_ref[0] = _post_residual(x_ref[0], z * jax.nn.sigmoid(u), g_ref[...], mod_ref, j, 1.0)


def _glu_residual(y, x, mod, g, w, bias, j):
    b, t, d = x.shape
    tm = _tok_tile(t)
    tok = pl.BlockSpec((1, tm, d), lambda bb, tt: (bb, tt, 0))
    return pl.pallas_call(
        functools.partial(_glu_kernel, j=j),
        grid=(b, t // tm),
        in_specs=[tok, tok, _mod_spec(mod), _full((1, d)), _full((d, d)), _full((1, d))],
        out_specs=tok,
        out_shape=jax.ShapeDtypeStruct((b, t, d), f32),
        compiler_params=_cparams("parallel", "parallel"),
        name="glu_residual",
    )(y, x, mod, g.reshape(1, d), w, bias.reshape(1, d))


def _softmax_step(s, v, state):
    m, l, acc = state
    m_new = jnp.maximum(m, jnp.max(s, axis=-1, keepdims=True))
    alpha = jnp.exp(m - m_new)
    p = jnp.exp(s - m_new)
    l = alpha * l + jnp.sum(p, axis=-1, keepdims=True)
    acc = alpha * acc + _dot(p.astype(bf16), v)
    return m_new, l, acc


def _diff_kernel(*refs, n_lat, tk, lam_init):
    if n_lat:
        q_ref, kc_ref, vc_ref, kl_ref, vl_ref, lamv_ref, g_ref, o_ref = refs
    else:
        q_ref, kc_ref, vc_ref, lamv_ref, g_ref, o_ref = refs
    q = q_ref[0]
    tq = q.shape[0]
    lane = lax.broadcasted_iota(jnp.int32, q.shape, 1)
    scale = DIFF_HEAD_DIM ** -0.5
    qs = (q.astype(f32) * scale).astype(bf16)
    zero = jnp.zeros_like(qs)
    q1 = jnp.where(lane < DIFF_HEAD_DIM, qs, zero)
    q2 = jnp.where(lane >= DIFF_HEAD_DIM, qs, zero)

    def init():
        return (jnp.full((tq, 1), -jnp.inf, f32), jnp.zeros((tq, 1), f32), jnp.zeros((tq, LANES), f32))

    kc, vc = kc_ref[0], vc_ref[0]
    st1 = _softmax_step(_dot_nt(q1, kc), vc, init())
    st2 = _softmax_step(_dot_nt(q2, kc), vc, init())
    if n_lat:
        def body(i, carry):
            st1, st2 = carry
            r = pl.multiple_of(i * tk, tk)
            k = kl_ref[0, pl.ds(r, tk), :]
            v = vl_ref[0, pl.ds(r, tk), :]
            return _softmax_step(_dot_nt(q1, k), v, st1), _softmax_step(_dot_nt(q2, k), v, st2)

        st1, st2 = lax.fori_loop(0, n_lat // tk, body, (st1, st2))
    lv = lamv_ref[...]
    lam = (jnp.exp(jnp.sum(lv[0:1] * lv[1:2], axis=-1, keepdims=True))
           - jnp.exp(jnp.sum(lv[2:3] * lv[3:4], axis=-1, keepdims=True)) + lam_init)
    o = st1[2] / st1[1] - lam * (st2[2] / st2[1])
    o_ref[0] = (_rms(o, g_ref[...]) * (1 - lam_init)).astype(o_ref.dtype)


def _diff_attention(qkv_q, qkv_ctx, qkv_lat, lamv, subln, lam_init):
    B, T, n3 = qkv_q.shape
    D = n3 // 3
    H = D // LANES
    L = qkv_ctx.shape[1]
    tq = min(_tok_tile(T), 256)
    n_lat = 0 if qkv_lat is None else qkv_lat.shape[1]
    tk = 512 if n_lat % 512 == 0 else 256
    args = [qkv_q, qkv_ctx, qkv_ctx]
    specs = [pl.BlockSpec((1, tq, LANES), lambda b, h, t: (b, t, h)),
             pl.BlockSpec((1, L, LANES), lambda b, h, t: (b, 0, H + h)),
             pl.BlockSpec((1, L, LANES), lambda b, h, t: (b, 0, 2 * H + h))]
    if n_lat:
        args += [qkv_lat, qkv_lat]
        specs += [pl.BlockSpec((1, n_lat, LANES), lambda b, h, t: (b, 0, H + h)),
                  pl.BlockSpec((1, n_lat, LANES), lambda b, h, t: (b, 0, 2 * H + h))]
    args += [lamv, subln.reshape(1, LANES)]
    specs += [_full(lamv.shape), _full((1, LANES))]
    return pl.pallas_call(
        functools.partial(_diff_kernel, n_lat=n_lat, tk=tk, lam_init=lam_init),
        grid=(B, H, T // tq),
        in_specs=specs,
        out_specs=pl.BlockSpec((1, tq, LANES), lambda b, h, t: (b, t, h)),
        out_shape=jax.ShapeDtypeStruct((B, T, D), bf16),
        compiler_params=_cparams("parallel", "parallel", "arbitrary"),
        name="diff_attention",
    )(*args)


def _rope_tables(n_tokens, head_dim):
    n_freq = head_dim // 4
    inv_freq = ROPE_BASE ** (-jnp.arange(n_freq, dtype=f32) / n_freq)
    t = jnp.arange(n_tokens)
    row = (t // GRID_W).astype(f32)
    col = (t % GRID_W).astype(f32)
    ang = jnp.concatenate([row[:, None] * inv_freq, col[:, None] * inv_freq], axis=-1)
    cos = jnp.repeat(jnp.cos(ang), 2, axis=-1)
    sin = jnp.repeat(jnp.sin(ang), 2, axis=-1) * jnp.tile(jnp.array([-1.0, 1.0], f32), head_dim // 2)
    reps = LANES // head_dim
    return jnp.tile(cos, (1, reps)), jnp.tile(sin, (1, reps))


def _na_kernel(*refs, lat, s_tok):
    if lat:
        q_ref, kc_ref, vc_ref, kl_ref, vl_ref, bias_ref, o_ref = refs
    else:
        q_ref, kc_ref, vc_ref, o_ref = refs
    q = q_ref[0]
    tq = q.shape[0]
    lane = lax.broadcasted_iota(jnp.int32, q.shape, 1)
    scale = NA_HEAD_DIM ** -0.5
    qs = (q.astype(f32) * scale).astype(bf16)
    zero = jnp.zeros_like(qs)
    kc, vc = kc_ref[0], vc_ref[0]
    if lat:
        t = pl.program_id(2)
        n_tiles = s_tok // (NA_TILE_ROWS * GRID_W)
        w0 = jnp.clip(t * NA_TILE_ROWS - (NA_WIN_ROWS - NA_TILE_ROWS) // 2, 0,
                      n_tiles * NA_TILE_ROWS - NA_WIN_ROWS)
        r0 = pl.multiple_of(w0 * GRID_W, 4 * GRID_W)
        kw = kl_ref[0, pl.ds(r0, NA_WIN_ROWS * GRID_W), :]
        vw = vl_ref[0, pl.ds(r0, NA_WIN_ROWS * GRID_W), :]
    outs = []
    for hh in range(2):
        sel = (lane < NA_HEAD_DIM) if hh == 0 else (lane >= NA_HEAD_DIM)
        qh = jnp.where(sel, qs, zero)
        s_c = _dot_nt(qh, kc)
        m = jnp.max(s_c, axis=-1, keepdims=True)
        if lat:
            s_l = _dot_nt(qh, kw) + bias_ref[0, 0, hh]
            m = jnp.maximum(m, jnp.max(s_l, axis=-1, keepdims=True))
        p_c = jnp.exp(s_c - m)
        l = jnp.sum(p_c, axis=-1, keepdims=True)
        o = _dot(p_c.astype(bf16), vc)
        if lat:
            p_l = jnp.exp(s_l - m)
            l = l + jnp.sum(p_l, axis=-1, keepdims=True)
            o = o + _dot(p_l.astype(bf16), vw)
        outs.append(o / l)
    o_ref[0] = jnp.where(lane < NA_HEAD_DIM, outs[0], outs[1]).astype(o_ref.dtype)


def _na_bias(rpb, rows):
    H = rpb.shape[0]
    tr, wr = NA_TILE_ROWS, NA_WIN_ROWS
    kh = min(WIN_H, rows)
    col = jnp.arange(GRID_W)
    c0 = jnp.clip(col - WIN_W // 2, 0, GRID_W - WIN_W)
    tables = []
    n_tiles = rows // tr
    for t in (0, min(1, n_tiles - 1), n_tiles - 1):
        w0 = min(max(t * tr - (wr - tr) // 2, 0), rows - wr)
        qr = t * tr + jnp.arange(tr)
        kr = w0 + jnp.arange(wr)
        r0 = jnp.clip(qr - kh // 2, 0, rows - kh)
        row_ok = (kr[None, :] >= r0[:, None]) & (kr[None, :] < r0[:, None] + kh)
        col_ok = (col[None, :] >= c0[:, None]) & (col[None, :] < c0[:, None] + WIN_W)
        dr = jnp.clip(kr[None, :] - qr[:, None] + WIN_H - 1, 0, 2 * WIN_H - 2)
        dc = jnp.clip(col[None, :] - col[:, None] + WIN_W - 1, 0, 2 * WIN_W - 2)
        bias = rpb.astype(f32)[:, dr[:, None, :, None], dc[None, :, None, :]]
        ok = row_ok[:, None, :, None] & col_ok[None, :, None, :]
        tables.append(jnp.where(ok[None], bias, NEG_INF).reshape(H, tr * GRID_W, wr * GRID_W))
    tab = jnp.stack(tables, axis=1)
    return tab.reshape(H // 2, 2, 3, tr * GRID_W, wr * GRID_W).transpose(0, 2, 1, 3, 4)


def _na_attention(qkv_q, qkv_ctx, qkv_lat, bias):
    B, T, n3 = qkv_q.shape
    D = n3 // 3
    HP = D // LANES
    L = qkv_ctx.shape[1]
    lat = qkv_lat is not None
    args = [qkv_q, qkv_ctx, qkv_ctx]
    if lat:
        tq = NA_TILE_ROWS * GRID_W
        n_tiles = T // tq
        assert T % tq == 0 and n_tiles >= 2
    else:
        tq = min(_tok_tile(T), 256)
        n_tiles = T // tq
    specs = [pl.BlockSpec((1, tq, LANES), lambda b, h, t: (b, t, h)),
             pl.BlockSpec((1, L, LANES), lambda b, h, t: (b, 0, HP + h)),
             pl.BlockSpec((1, L, LANES), lambda b, h, t: (b, 0, 2 * HP + h))]
    if lat:
        args += [qkv_lat, qkv_lat, bias]
        last = n_tiles - 1

        def bias_idx(b, h, t):
            return (h, jnp.where(t == 0, 0, jnp.where(t == last, 2, 1)), 0, 0, 0)

        specs += [pl.BlockSpec((1, T, LANES), lambda b, h, t: (b, 0, HP + h)),
                  pl.BlockSpec((1, T, LANES), lambda b, h, t: (b, 0, 2 * HP + h)),
                  pl.BlockSpec((1, 1) + bias.shape[2:], bias_idx)]
    return pl.pallas_call(
        functools.partial(_na_kernel, lat=lat, s_tok=T),
        grid=(B, HP, n_tiles),
        in_specs=specs,
        out_specs=pl.BlockSpec((1, tq, LANES), lambda b, h, t: (b, t, h)),
        out_shape=jax.ShapeDtypeStruct((B, T, D), bf16),
        compiler_params=_cparams("parallel", "parallel", "arbitrary"),
        name="na_attention",
    )(*args)


def _split_dot(mask_bf16, x):
    hi = x.astype(bf16)
    lo = (x - hi.astype(f32)).astype(bf16)
    return _dot(mask_bf16, hi) + _dot(mask_bf16, lo)


def _hg_block(z, v, q, lb, st, reverse, with_out):
    n = z.shape[0]
    nch = n // HG_CHUNK
    f = lb + (1 - lb) * jax.nn.sigmoid(z)
    lf = jnp.log(f)
    kk = 1 - f
    ri = lax.broadcasted_iota(jnp.int32, (n, n), 0)
    ci = lax.broadcasted_iota(jnp.int32, (n, n), 1)
    shift = HG_CHUNK.bit_length() - 1
    same = (ri >> shift) == (ci >> shift)
    tri = same & ((ci >= ri) if reverse else (ci <= ri))
    g = _split_dot(tri.astype(bf16), lf)
    gtot = _split_dot(same.astype(bf16), lf)
    kdec = kk * jnp.exp(gtot - g)
    out = None
    if with_out:
        q_dec = q * jnp.exp(g)
        k_inv = kk * jnp.exp(-g)
        att = jnp.where(tri, _dot_nt(q_dec.astype(bf16), k_inv.astype(bf16)), 0.0)
        out_intra = _dot(att.astype(bf16), v.astype(bf16))
        outs = [None] * nch
    order = range(nch - 1, -1, -1) if reverse else range(nch)
    for c in order:
        sl = slice(c * HG_CHUNK, (c + 1) * HG_CHUNK)
        if with_out:
            outs[c] = _dot_nt(q_dec[sl].astype(bf16), st.astype(bf16))
        ds_t = _dot(v[sl].T.astype(bf16), kdec[sl].astype(bf16))
        st = st * jnp.exp(gtot[c * HG_CHUNK:c * HG_CHUNK + 1]) + ds_t
    if with_out:
        out = out_intra + jnp.concatenate(outs, axis=0)
    return out, st


def _hg_kernel(q_ref, i_ref, gate_ref, zf_ref, zb_ref, ic_ref, zfc_ref, zbc_ref, lb_ref, gn_ref, o_ref, acc_ref,
               *, s_tok, l_tok):
    lb = lb_ref[...]
    nblk = s_tok // HG_BLOCK
    for dirn, (z_ref, zc_ref) in enumerate(((zf_ref, zfc_ref), (zb_ref, zbc_ref))):
        reverse = dirn == 1
        _, st = _hg_block(zc_ref[0], ic_ref[0], None, lb, jnp.zeros((HG_EXPAND, HG_EXPAND), f32), reverse, False)

        def body(n, st, z_ref=z_ref, reverse=reverse, dirn=dirn):
            blk = (nblk - 1 - n) if reverse else n
            r = pl.multiple_of(blk * HG_BLOCK, HG_BLOCK)
            rows = pl.ds(r, HG_BLOCK)
            out, st = _hg_block(z_ref[0, rows, :], i_ref[0, rows, :], q_ref[0, rows, :], lb, st, reverse, True)
            if dirn == 0:
                acc_ref[rows, :] = out
            else:
                acc_ref[rows, :] += out
            return st

        lax.fori_loop(0, nblk, body, st)
    gate = gate_ref[0]
    o_ref[0] = (_rms(acc_ref[...], gn_ref[...]) * (gate * jax.nn.sigmoid(gate))).astype(o_ref.dtype)


def _hgrn2_core(p_lat, p_ctx, lb, gn_g):
    B, S, n5 = p_lat.shape
    D = n5 // 5
    H = D // HG_EXPAND
    L = p_ctx.shape[1]
    assert S % HG_BLOCK == 0 and L % HG_CHUNK == 0

    def lat(k):
        return pl.BlockSpec((1, S, HG_EXPAND), lambda b, h: (b, 0, k * H + h))

    def ctx(k):
        return pl.BlockSpec((1, L, HG_EXPAND), lambda b, h: (b, 0, k * H + h))

    vec = pl.BlockSpec((1, HG_EXPAND), lambda b, h: (0, h))
    return pl.pallas_call(
        functools.partial(_hg_kernel, s_tok=S, l_tok=L),
        grid=(B, H),
        in_specs=[lat(0), lat(1), lat(2), lat(3), lat(4), ctx(1), ctx(3), ctx(4), vec, _full((1, HG_EXPAND))],
        out_specs=pl.BlockSpec((1, S, HG_EXPAND), lambda b, h: (b, 0, h)),
        out_shape=jax.ShapeDtypeStruct((B, S, D), bf16),
        scratch_shapes=[pltpu.VMEM((S, HG_EXPAND), f32)],
        compiler_params=_cparams("parallel", "parallel"),
        name="hgrn2_core",
    )(p_lat, p_lat, p_lat, p_lat, p_lat, p_ctx, p_ctx, p_ctx, lb.reshape(1, D), gn_g.reshape(1, HG_EXPAND))


def kernel(x, c, ctx, c_ctx, w_ada, b_ada, g_pre, g_post, w_ff1, w_ff3, w_ff2, s5_a_re, s5_a_im, s5_log_dt, s5_b_re, s5_b_im, s5_c_re, s5_c_im, s5_d, s5_w_glu, s5_b_glu, da_w_qkv, da_w_o, da_lam_q1, da_lam_k1, da_lam_q2, da_lam_k2, da_subln, na_w_qkv, na_w_o, na_rpb, hg_w_qig, hg_w_f, hg_b_f, hg_lb_logits, hg_gnorm, hg_w_o):
    B, S, D = x.shape
    depth = w_ada.shape[0]
    n_mix = 4
    rows_pad = -(-(B + 1) // 8) * 8
    c_all = jnp.concatenate([c, c_ctx[None], jnp.zeros((rows_pad - B - 1, D), f32)], axis=0)
    mods = _ada(c_all, w_ada, b_ada)
    lb_p = jax.nn.softmax(hg_lb_logits.astype(f32), axis=0)
    lower_bounds = jnp.cumsum(lb_p, axis=0) - lb_p[0]
    w1b, w3b, w2b = w_ff1.astype(bf16), w_ff3.astype(bf16), w_ff2.astype(bf16)

    x_lat, x_ctx = x, ctx
    for i in range(depth):
        last = i == depth - 1
        occ, kind = i // n_mix, i % n_mix
        m_lat = mods[i, :B].reshape(B, 3 * N_SUB, D)
        m_ctx = mods[i, B:B + 1].reshape(1, 3 * N_SUB, D)
        streams = [(x_lat, m_lat), (x_ctx, m_ctx)]

        x_lat, x_ctx = [_half_ffn(xs, ms, g_pre[i, 0], g_post[i, 0], w1b[i, 0], w3b[i, 0], w2b[i, 0], 0)
                        for xs, ms in streams]

        if kind == 0:
            h_lat = _prenorm(x_lat, m_lat, g_pre[i, 1], 1)
            h_ctx = _prenorm(x_ctx, m_ctx, g_pre[i, 1], 1)
            y_lat, y_ctx = _s5_scan(h_lat, h_ctx, s5_a_re[occ], s5_a_im[occ], s5_log_dt[occ], s5_b_re[occ],
                                    s5_b_im[occ], s5_c_re[occ], s5_c_im[occ], s5_d[occ])
            wg = s5_w_glu[occ].astype(bf16)
            x_lat_new = _glu_residual(y_lat, x_lat, m_lat, g_post[i, 1], wg, s5_b_glu[occ], 1)
            x_ctx_new = None if last else _glu_residual(y_ctx, x_ctx, m_ctx, g_post[i, 1], wg, s5_b_glu[occ], 1)
        elif kind == 1:
            wq = da_w_qkv[occ].astype(bf16)
            rope = _rope_tables(S, DIFF_HEAD_DIM)
            qkv_lat = _prenorm_proj(x_lat, m_lat, g_pre[i, 1], 1, wq, rope=rope, n_rope=2 * D)
            qkv_ctx = _prenorm_proj(x_ctx, m_ctx, g_pre[i, 1], 1, wq)
            lamv = jnp.stack([da_lam_q1[occ], da_lam_k1[occ], da_lam_q2[occ], da_lam_k2[occ]]).astype(f32)
            lam_init = 0.8 - 0.6 * math.exp(-0.3 * i)
            wo = da_w_o[occ].astype(bf16)
            o_lat = _diff_attention(qkv_lat, qkv_ctx, qkv_lat, lamv, da_subln[occ], lam_init)
            x_lat_new = _outproj_residual(o_lat, x_lat, m_lat, g_post[i, 1], wo, 1)
            x_ctx_new = None
            if not last:
                o_ctx = _diff_attention(qkv_ctx, qkv_ctx, None, lamv, da_subln[occ], lam_init)
                x_ctx_new = _outproj_residual(o_ctx, x_ctx, m_ctx, g_post[i, 1], wo, 1)
        elif kind == 2:
            wq = na_w_qkv[occ].astype(bf16)
            qkv_lat = _prenorm_proj(x_lat, m_lat, g_pre[i, 1], 1, wq)
            qkv_ctx = _prenorm_proj(x_ctx, m_ctx, g_pre[i, 1], 1, wq)
            wo = na_w_o[occ].astype(bf16)
            o_lat = _na_attention(qkv_lat, qkv_ctx, qkv_lat, _na_bias(na_rpb[occ], S // GRID_W))
            x_lat_new = _outproj_residual(o_lat, x_lat, m_lat, g_post[i, 1], wo, 1)
            x_ctx_new = None
            if not last:
                o_ctx = _na_attention(qkv_ctx, qkv_ctx, None, None)
                x_ctx_new = _outproj_residual(o_ctx, x_ctx, m_ctx, g_post[i, 1], wo, 1)
        else:
            assert last, "HGRN2 context outputs are not needed when it is the last layer"
            wp = jnp.concatenate([hg_w_qig[occ], hg_w_f[occ, 0], hg_w_f[occ, 1]], axis=1).astype(bf16)
            bp = jnp.concatenate([jnp.zeros((3 * D,), f32), hg_b_f[occ, 0], hg_b_f[occ, 1]])
            p_lat = _prenorm_proj(x_lat, m_lat, g_pre[i, 1], 1, wp, bias=bp, out_dtype=f32)
            p_ctx = _prenorm_proj(x_ctx, m_ctx, g_pre[i, 1], 1, wp, bias=bp, out_dtype=f32)
            o_lat = _hgrn2_core(p_lat, p_ctx, lower_bounds[i], hg_gnorm[occ])
            x_lat_new = _outproj_residual(o_lat, x_lat, m_lat, g_post[i, 1], hg_w_o[occ].astype(bf16), 1)
            x_ctx_new = None

        x_lat = _half_ffn(x_lat_new, m_lat, g_pre[i, 2], g_post[i, 2], w1b[i, 1], w3b[i, 1], w2b[i, 1], 2)
        if not last:
            x_ctx = _half_ffn(x_ctx_new, m_ctx, g_pre[i, 2], g_post[i, 2], w1b[i, 1], w3b[i, 1], w2b[i, 1], 2)
    return x_lat
```

```python
import functools
import math

import jax
import jax.numpy as jnp
from jax import lax
from jax.experimental import pallas as pl
from jax.experimental.pallas import tpu as pltpu

f32 = jnp.float32
bf16 = jnp.bfloat16

N_SUB = 3
RMS_EPS = 1e-6
NEG_INF = -1e30
ROPE_BASE = 10000.0
GRID_W = 64
S5_GROUP = 16
S5_STATE = 64
S5_CHUNK = 16
DIFF_HEAD_DIM = 64
NA_HEAD_DIM = 64
WIN_H = 8
WIN_W = 16
NA_TILE_ROWS = 8
NA_WIN_ROWS = 16
HG_EXPAND = 128
HG_CHUNK = 64
HG_BLOCK = 512
LANES = 128
VMEM_LIMIT = 56 * 1024 * 1024


def _cparams(*sem):
    return pltpu.CompilerParams(dimension_semantics=sem, vmem_limit_bytes=VMEM_LIMIT)


def _dot(a, b):
    return jnp.dot(a, b, preferred_element_type=f32)


def _dot_nt(a, b):
    return lax.dot_general(a, b, (((1,), (1,)), ((), ())), preferred_element_type=f32)


def _rms(x, g):
    return x * lax.rsqrt(jnp.mean(jnp.square(x), axis=-1, keepdims=True) + RMS_EPS) * g


def _pre_norm(x, g, mod_ref, j):
    shift = mod_ref[0, 3 * j:3 * j + 1, :]
    scale = mod_ref[0, 3 * j + 1:3 * j + 2, :]
    return _rms(x, g) * (1 + scale) + shift


def _post_residual(x, y, g, mod_ref, j, weight):
    gate = mod_ref[0, 3 * j + 2:3 * j + 3, :]
    return x + weight * gate * _rms(y, g)


def _tok_tile(t):
    for tm in (512, 256, 128):
        if t % tm == 0:
            return tm
    raise ValueError(f"token count {t} is not a multiple of 128")


def _full(shape):
    return pl.BlockSpec(shape, lambda *_: (0,) * len(shape))


def _mod_spec(mod):
    nd = mod.shape[1]
    d = mod.shape[2]
    if mod.shape[0] == 1:
        return pl.BlockSpec((1, nd, d), lambda b, t: (0, 0, 0))
    return pl.BlockSpec((1, nd, d), lambda b, t: (b, 0, 0))


def _ada_kernel(c_ref, w_ref, b_ref, o_ref):
    c = c_ref[...]
    sc = (c * jax.nn.sigmoid(c)).astype(bf16)
    o_ref[0] = _dot(sc, w_ref[0].astype(bf16)) + b_ref[0]


def _ada(c_all, w_ada, b_ada):
    depth, d, n = w_ada.shape
    rows = c_all.shape[0]
    tn = n // 4
    return pl.pallas_call(
        _ada_kernel,
        grid=(depth, n // tn),
        in_specs=[pl.BlockSpec((rows, d), lambda i, t: (0, 0)),
                  pl.BlockSpec((1, d, tn), lambda i, t: (i, 0, t)),
                  pl.BlockSpec((1, 1, tn), lambda i, t: (i, 0, t))],
        out_specs=pl.BlockSpec((1, rows, tn), lambda i, t: (i, 0, t)),
        out_shape=jax.ShapeDtypeStruct((depth, rows, n), f32),
        compiler_params=_cparams("arbitrary", "arbitrary"),
        name="ada",
    )(c_all, w_ada, b_ada.reshape(depth, 1, n))


def _ffn_kernel(x_ref, mod_ref, gin_ref, gout_ref, w1_ref, w3_ref, w2_ref, o_ref, *, j, chunks):
    x = x_ref[0]
    h = _pre_norm(x, gin_ref[...], mod_ref, j).astype(bf16)
    acc = None
    for s, n in chunks:
        a = _dot(h, w1_ref[:, s:s + n])
        b = _dot(h, w3_ref[:, s:s + n])
        g = (a * jax.nn.sigmoid(a) * b).astype(bf16)
        y = _dot(g, w2_ref[s:s + n, :])
        acc = y if acc is None else acc + y
    o_ref[0] = _post_residual(x, acc, gout_ref[...], mod_ref, j, 0.5)


def _ff_chunks(dff, step=1024):
    out, s = [], 0
    while s < dff:
        n = min(step, dff - s)
        out.append((s, n))
        s += n
    return tuple(out)


def _half_ffn(x, mod, g_in, g_out, w1, w3, w2, j):
    b, t, d = x.shape
    dff = w1.shape[1]
    tm = _tok_tile(t)
    return pl.pallas_call(
        functools.partial(_ffn_kernel, j=j, chunks=_ff_chunks(dff)),
        grid=(b, t // tm),
        in_specs=[pl.BlockSpec((1, tm, d), lambda bb, tt: (bb, tt, 0)),
                  _mod_spec(mod), _full((1, d)), _full((1, d)),
                  _full((d, dff)), _full((d, dff)), _full((dff, d))],
        out_specs=pl.BlockSpec((1, tm, d), lambda bb, tt: (bb, tt, 0)),
        out_shape=jax.ShapeDtypeStruct((b, t, d), f32),
        compiler_params=_cparams("parallel", "parallel"),
        name="half_ffn",
    )(x, mod, g_in.reshape(1, d), g_out.reshape(1, d), w1, w3, w2)


def _prenorm_kernel(x_ref, mod_ref, g_ref, o_ref, *, j):
    o_ref[0] = _pre_norm(x_ref[0], g_ref[...], mod_ref, j)


def _prenorm(x, mod, g, j):
    b, t, d = x.shape
    tm = _tok_tile(t)
    return pl.pallas_call(
        functools.partial(_prenorm_kernel, j=j),
        grid=(b, t // tm),
        in_specs=[pl.BlockSpec((1, tm, d), lambda bb, tt: (bb, tt, 0)), _mod_spec(mod), _full((1, d))],
        out_specs=pl.BlockSpec((1, tm, d), lambda bb, tt: (bb, tt, 0)),
        out_shape=jax.ShapeDtypeStruct((b, t, d), f32),
        compiler_params=_cparams("parallel", "parallel"),
        name="prenorm",
    )(x, mod, g.reshape(1, d))


def _swap_pairs(x):
    lane = lax.broadcasted_iota(jnp.int32, x.shape, 1)
    nxt = pltpu.roll(x, LANES - 1, 1)
    prv = pltpu.roll(x, 1, 1)
    return jnp.where((lane & 1) == 0, nxt, prv)


def _proj_kernel(*refs, j, n_out, n_rope, has_bias, step):
    x_ref, mod_ref, g_ref, w_ref = refs[:4]
    k = 4
    b_ref = None
    if has_bias:
        b_ref = refs[k]
        k += 1
    if n_rope:
        cos_ref, sin_ref = refs[k], refs[k + 1]
        k += 2
    o_ref = refs[k]
    h = _pre_norm(x_ref[0], g_ref[...], mod_ref, j).astype(bf16)
    for s in range(0, n_out, step):
        y = _dot(h, w_ref[:, s:s + step])
        if has_bias:
            y = y + b_ref[:, s:s + step]
        if s < n_rope:
            cos = cos_ref[...]
            sin = sin_ref[...]
            parts = []
            for c in range(0, step, LANES):
                yc = y[:, c:c + LANES]
                parts.append(yc * cos + _swap_pairs(yc) * sin)
            y = jnp.concatenate(parts, axis=1)
        o_ref[0, :, s:s + step] = y.astype(o_ref.dtype)


def _prenorm_proj(x, mod, g, j, w, bias=None, rope=None, n_rope=0, out_dtype=bf16):
    b, t, d = x.shape
    n = w.shape[1]
    tm = min(_tok_tile(t), 256)
    step = 512
    assert n % step == 0 and n_rope % step == 0
    args = [x, mod, g.reshape(1, d), w]
    specs = [pl.BlockSpec((1, tm, d), lambda bb, tt: (bb, tt, 0)), _mod_spec(mod), _full((1, d)), _full((d, n))]
    if bias is not None:
        args.append(bias.reshape(1, n))
        specs.append(_full((1, n)))
    if n_rope:
        args += list(rope)
        specs += [pl.BlockSpec((tm, LANES), lambda bb, tt: (tt, 0))] * 2
    return pl.pallas_call(
        functools.partial(_proj_kernel, j=j, n_out=n, n_rope=n_rope, has_bias=bias is not None, step=step),
        grid=(b, t // tm),
        in_specs=specs,
        out_specs=pl.BlockSpec((1, tm, n), lambda bb, tt: (bb, tt, 0)),
        out_shape=jax.ShapeDtypeStruct((b, t, n), out_dtype),
        compiler_params=_cparams("parallel", "parallel"),
        name="prenorm_proj",
    )(*args)


def _outproj_kernel(y_ref, x_ref, mod_ref, g_ref, w_ref, o_ref, *, j):
    y = _dot(y_ref[0], w_ref[...])
    o_ref[0] = _post_residual(x_ref[0], y, g_ref[...], mod_ref, j, 1.0)


def _outproj_residual(y, x, mod, g, w, j):
    b, t, d = x.shape
    tm = _tok_tile(t)
    tok = pl.BlockSpec((1, tm, d), lambda bb, tt: (bb, tt, 0))
    return pl.pallas_call(
        functools.partial(_outproj_kernel, j=j),
        grid=(b, t // tm),
        in_specs=[tok, tok, _mod_spec(mod), _full((1, d)), _full((d, d))],
        out_specs=tok,
        out_shape=jax.ShapeDtypeStruct((b, t, d), f32),
        compiler_params=_cparams("parallel", "parallel"),
        name="outproj_residual",
    )(y, x, mod, g.reshape(1, d), w)


def _s5_kernel(x_ref, wb_ref, wk_ref, wc_ref, lam_ref, d_ref, o_ref, s_ref, hin_ref, *, nb, nc_ctx, nc):
    x = x_ref[0]
    xb = x.astype(bf16)
    s_ref[...] = _dot(xb, wb_ref[0])
    half = S5_STATE * 2
    lam = lam_ref[0]
    a_f, b_f, a_b, b_b = lam[0:1], lam[1:2], lam[2:3], lam[3:4]

    def step(n, carry):
        h_f, h_b = carry
        r_f = pl.multiple_of(n * nb, nb)
        n_b = jnp.where(n < nc_ctx, nc_ctx - 1 - n, nc - 1 - (n - nc_ctx))
        r_b = pl.multiple_of(n_b * nb, nb)
        hin_ref[pl.ds(r_f, nb), 0:half] = h_f
        hin_ref[pl.ds(r_b, nb), half:2 * half] = h_b
        h_f = a_f * h_f + b_f * pltpu.roll(h_f, S5_STATE, 1) + s_ref[pl.ds(r_f, nb), 0:half]
        h_b = a_b * h_b + b_b * pltpu.roll(h_b, S5_STATE, 1) + s_ref[pl.ds(r_b, nb), half:2 * half]
        return h_f, h_b

    zero = jnp.zeros((nb, half), f32)
    lax.fori_loop(0, nc, step, (zero, zero))
    y = x * d_ref[0] + _dot(xb, wk_ref[0]) + _dot(hin_ref[...].astype(bf16), wc_ref[0])
    o_ref[0] = y


def _s5_weights(a_re, a_im, log_dt, b_re, b_im, c_re, c_im):
    hp = lax.Precision.HIGHEST
    C = S5_CHUNK
    G, P = a_re.shape[1:]
    N = b_re.shape[-1]
    j = jnp.arange(C + 1, dtype=f32)
    wbs, wks, wcs, lams = [], [], [], []
    for dirn in range(2):
        are, aim = a_re[dirn].astype(f32), a_im[dirn].astype(f32)
        dt = jnp.exp(log_dt[dirn].astype(f32))[:, None]
        pw_mag = jnp.exp(j[:, None, None] * (are * dt)[None])
        pw_re = pw_mag * jnp.cos(j[:, None, None] * (aim * dt)[None])
        pw_im = pw_mag * jnp.sin(j[:, None, None] * (aim * dt)[None])
        nr, ni = pw_re[1] - 1.0, pw_im[1]
        den = are * are + aim * aim
        fr = (nr * are + ni * aim) / den
        fi = (ni * are - nr * aim) / den
        bre, bim = b_re[dirn].astype(f32), b_im[dirn].astype(f32)
        bbr = fr[..., None] * bre - fi[..., None] * bim
        bbi = fr[..., None] * bim + fi[..., None] * bre
        cre, cim = c_re[dirn].astype(f32), c_im[dirn].astype(f32)
        dist = jnp.arange(C - 1, -1, -1) if dirn == 0 else jnp.arange(C)
        sr = pw_re[dist][..., None] * bbr[None] - pw_im[dist][..., None] * bbi[None]
        si = pw_re[dist][..., None] * bbi[None] + pw_im[dist][..., None] * bbr[None]
        wb = jnp.concatenate([sr, si], axis=2)
        wbs.append(wb.transpose(1, 0, 3, 2).reshape(G, C * N, 2 * P))
        steps = jnp.arange(1, C + 1) if dirn == 0 else jnp.arange(C, 0, -1)
        pr, pi = pw_re[steps], pw_im[steps]
        or_ = cre[None] * pr[:, :, None, :] - cim[None] * pi[:, :, None, :]
        oi_ = -(cre[None] * pi[:, :, None, :] + cim[None] * pr[:, :, None, :])
        wc = jnp.concatenate([or_, oi_], axis=3)
        wcs.append(wc.transpose(1, 3, 0, 2).reshape(G, 2 * P, C * N))
        clr = cre[None] * pw_re[:C, :, None, :] - cim[None] * pw_im[:C, :, None, :]
        cli = cre[None] * pw_im[:C, :, None, :] + cim[None] * pw_re[:C, :, None, :]
        kk = (jnp.einsum('jgnp,gpm->gjnm', clr, bbr, precision=hp)
              - jnp.einsum('jgnp,gpm->gjnm', cli, bbi, precision=hp))
        s_idx = jnp.arange(C)[:, None]
        t_idx = jnp.arange(C)[None, :]
        lag = (t_idx - s_idx) if dirn == 0 else (s_idx - t_idx)
        kt = jnp.where((lag >= 0)[None, :, :, None, None], kk[:, jnp.clip(lag, 0, C - 1)], 0.0)
        wks.append(kt.transpose(0, 1, 4, 2, 3).reshape(G, C * N, C * N))
        lr, li = pw_re[C], pw_im[C]
        lams += [jnp.concatenate([lr, lr], axis=-1), jnp.concatenate([-li, li], axis=-1)]
    wb = jnp.concatenate(wbs, axis=2).astype(bf16)
    wc = jnp.concatenate(wcs, axis=1).astype(bf16)
    wk = (wks[0] + wks[1]).astype(bf16)
    lam = jnp.stack(lams, axis=1)
    return wb, wk, wc, lam


def _s5_scan(h_lat, h_ctx, a_re, a_im, log_dt, b_re, b_im, c_re, c_im, d_skip):
    B, S, D = h_lat.shape
    L = h_ctx.shape[1]
    C, N, P = S5_CHUNK, S5_GROUP, S5_STATE
    G = D // N
    nc_ctx, nc = L // C, (L + S) // C
    assert 2 * P == LANES and L % C == 0 and S % C == 0
    wb, wk, wc, lam = _s5_weights(a_re, a_im, log_dt, b_re, b_im, c_re, c_im)
    d_t = jnp.tile(d_skip.astype(f32).reshape(G, 1, N), (1, C, 1)).reshape(G, 1, C * N)
    hcat = jnp.concatenate([h_ctx, h_lat], axis=1)
    xg = hcat.reshape(B, nc, C, G, N).transpose(3, 1, 0, 2, 4).reshape(G, nc * B, C * N)
    R = nc * B
    y = pl.pallas_call(
        functools.partial(_s5_kernel, nb=B, nc_ctx=nc_ctx, nc=nc),
        grid=(G,),
        in_specs=[pl.BlockSpec((1, R, C * N), lambda g: (g, 0, 0)),
                  pl.BlockSpec((1, C * N, 4 * P), lambda g: (g, 0, 0)),
                  pl.BlockSpec((1, C * N, C * N), lambda g: (g, 0, 0)),
                  pl.BlockSpec((1, 4 * P, C * N), lambda g: (g, 0, 0)),
                  pl.BlockSpec((1, 4, 2 * P), lambda g: (g, 0, 0)),
                  pl.BlockSpec((1, 1, C * N), lambda g: (g, 0, 0))],
        out_specs=pl.BlockSpec((1, R, C * N), lambda g: (g, 0, 0)),
        out_shape=jax.ShapeDtypeStruct((G, R, C * N), f32),
        scratch_shapes=[pltpu.VMEM((R, 4 * P), f32), pltpu.VMEM((R, 4 * P), f32)],
        compiler_params=_cparams("parallel"),
        name="s5_scan",
    )(xg, wb, wk, wc, lam, d_t)
    ycat = y.reshape(G, nc, B, C, N).transpose(2, 1, 3, 0, 4).reshape(B, L + S, D)
    return ycat[:, L:], ycat[:, :L]


def _glu_kernel(y_ref, x_ref, mod_ref, g_ref, w_ref, b_ref, o_ref, *, j):
    z = jax.nn.gelu(y_ref[0])
    u = _dot(z.astype(bf16), w_ref[...]) + b_ref[...]
    o_ref[0] = _post_residual(x_ref[0], z * jax.nn.sigmoid(u), g_ref[...], mod_ref, j, 1.0)


def _glu_residual(y, x, mod, g, w, bias, j):
    b, t, d = x.shape
    tm = _tok_tile(t)
    tok = pl.BlockSpec((1, tm, d), lambda bb, tt: (bb, tt, 0))
    return pl.pallas_call(
        functools.partial(_glu_kernel, j=j),
        grid=(b, t // tm),
        in_specs=[tok, tok, _mod_spec(mod), _full((1, d)), _full((d, d)), _full((1, d))],
        out_specs=tok,
        out_shape=jax.ShapeDtypeStruct((b, t, d), f32),
        compiler_params=_cparams("parallel", "parallel"),
        name="glu_residual",
    )(y, x, mod, g.reshape(1, d), w, bias.reshape(1, d))


def _row_softmax(parts):
    m = functools.reduce(jnp.maximum, [jnp.max(s, axis=-1, keepdims=True) for s in parts])
    ps = [jnp.exp(s - m) for s in parts]
    l = functools.reduce(lambda a, b: a + b, [jnp.sum(p, axis=-1, keepdims=True) for p in ps])
    return ps, l


def _diff_kernel(*refs, n_lat, lam_init):
    if n_lat:
        q_ref, kc_ref, vc_ref, kl_ref, vl_ref, lamv_ref, g_ref, o_ref = refs
        segs = [(kc_ref, vc_ref), (kl_ref, vl_ref)]
    else:
        q_ref, kc_ref, vc_ref, lamv_ref, g_ref, o_ref = refs
        segs = [(kc_ref, vc_ref)]
    q = q_ref[0]
    lane = lax.broadcasted_iota(jnp.int32, q.shape, 1)
    scale = DIFF_HEAD_DIM ** -0.5
    qs = (q.astype(f32) * scale).astype(bf16)
    zero = jnp.zeros_like(qs)
    q1 = jnp.where(lane < DIFF_HEAD_DIM, qs, zero)
    q2 = jnp.where(lane >= DIFF_HEAD_DIM, qs, zero)
    p1, l1 = _row_softmax([_dot_nt(q1, k_ref[0]) for k_ref, _ in segs])
    p2, l2 = _row_softmax([_dot_nt(q2, k_ref[0]) for k_ref, _ in segs])
    lv = lamv_ref[...]
    lam = (jnp.exp(jnp.sum(lv[0:1] * lv[1:2], axis=-1, keepdims=True))
           - jnp.exp(jnp.sum(lv[2:3] * lv[3:4], axis=-1, keepdims=True)) + lam_init)
    c = lam * l1 / l2
    o = None
    for a, b, (_, v_ref) in zip(p1, p2, segs):
        part = _dot((a - c * b).astype(bf16), v_ref[0])
        o = part if o is None else o + part
    o = o / l1
    o_ref[0] = (_rms(o, g_ref[...]) * (1 - lam_init)).astype(o_ref.dtype)


def _diff_attention(qkv_q, qkv_ctx, qkv_lat, lamv, subln, lam_init):
    B, T, n3 = qkv_q.shape
    D = n3 // 3
    H = D // LANES
    L = qkv_ctx.shape[1]
    tq = min(_tok_tile(T), 256)
    n_lat = 0 if qkv_lat is None else qkv_lat.shape[1]
    args = [qkv_q, qkv_ctx, qkv_ctx]
    specs = [pl.BlockSpec((1, tq, LANES), lambda b, h, t: (b, t, h)),
             pl.BlockSpec((1, L, LANES), lambda b, h, t: (b, 0, H + h)),
             pl.BlockSpec((1, L, LANES), lambda b, h, t: (b, 0, 2 * H + h))]
    if n_lat:
        args += [qkv_lat, qkv_lat]
        specs += [pl.BlockSpec((1, n_lat, LANES), lambda b, h, t: (b, 0, H + h)),
                  pl.BlockSpec((1, n_lat, LANES), lambda b, h, t: (b, 0, 2 * H + h))]
    args += [lamv, subln.reshape(1, LANES)]
    specs += [_full(lamv.shape), _full((1, LANES))]
    return pl.pallas_call(
        functools.partial(_diff_kernel, n_lat=n_lat, lam_init=lam_init),
        grid=(B, H, T // tq),
        in_specs=specs,
        out_specs=pl.BlockSpec((1, tq, LANES), lambda b, h, t: (b, t, h)),
        out_shape=jax.ShapeDtypeStruct((B, T, D), bf16),
        compiler_params=_cparams("parallel", "parallel", "arbitrary"),
        name="diff_attention",
    )(*args)


def _rope_tables(n_tokens, head_dim):
    n_freq = head_dim // 4
    inv_freq = ROPE_BASE ** (-jnp.arange(n_freq, dtype=f32) / n_freq)
    t = jnp.arange(n_tokens)
    row = (t // GRID_W).astype(f32)
    col = (t % GRID_W).astype(f32)
    ang = jnp.concatenate([row[:, None] * inv_freq, col[:, None] * inv_freq], axis=-1)
    cos = jnp.repeat(jnp.cos(ang), 2, axis=-1)
    sin = jnp.repeat(jnp.sin(ang), 2, axis=-1) * jnp.tile(jnp.array([-1.0, 1.0], f32), head_dim // 2)
    reps = LANES // head_dim
    return jnp.tile(cos, (1, reps)), jnp.tile(sin, (1, reps))


def _na_kernel(*refs, lat, s_tok):
    if lat:
        q_ref, kc_ref, vc_ref, kl_ref, vl_ref, tb_ref, o_ref, bias_ref = refs
    else:
        q_ref, kc_ref, vc_ref, o_ref = refs
    q = q_ref[0]
    lane = lax.broadcasted_iota(jnp.int32, q.shape, 1)
    scale = NA_HEAD_DIM ** -0.5
    qs = (q.astype(f32) * scale).astype(bf16)
    zero = jnp.zeros_like(qs)
    kc, vc = kc_ref[0], vc_ref[0]
    if lat:
        t = pl.program_id(2)
        rows = s_tok // GRID_W
        w0 = jnp.clip(t * NA_TILE_ROWS - (NA_WIN_ROWS - NA_TILE_ROWS) // 2, 0, rows - NA_WIN_ROWS)
        tok0 = pl.multiple_of(w0 * GRID_W, 4 * GRID_W)
        kw = kl_ref[0, pl.ds(tok0, NA_WIN_ROWS * GRID_W), :]
        vw = vl_ref[0, pl.ds(tok0, NA_WIN_ROWS * GRID_W), :]
        for i in range(NA_TILE_ROWS):
            qr = t * NA_TILE_ROWS + i
            r0 = jnp.clip(qr - WIN_H // 2, 0, rows - WIN_H)
            for j in range(NA_WIN_ROWS // 2):
                es = []
                for kr in (w0 + 2 * j, w0 + 2 * j + 1):
                    ok = (kr >= r0) & (kr < r0 + WIN_H)
                    es.append(jnp.where(ok, kr - qr + WIN_H, 0))
                for hh in range(2):
                    bias_ref[hh, i * GRID_W:(i + 1) * GRID_W, j * LANES:(j + 1) * LANES] = (
                        tb_ref[0, hh, 0, es[0]] + tb_ref[0, hh, 1, es[1]])
    outs = []
    for hh in range(2):
        sel = (lane < NA_HEAD_DIM) if hh == 0 else (lane >= NA_HEAD_DIM)
        qh = jnp.where(sel, qs, zero)
        s_c = _dot_nt(qh, kc)
        m = jnp.max(s_c, axis=-1, keepdims=True)
        if lat:
            s_l = _dot_nt(qh, kw) + bias_ref[hh]
            m = jnp.maximum(m, jnp.max(s_l, axis=-1, keepdims=True))
        p_c = jnp.exp(s_c - m)
        l = jnp.sum(p_c, axis=-1, keepdims=True)
        o = _dot(p_c.astype(bf16), vc)
        if lat:
            p_l = jnp.exp(s_l - m)
            l = l + jnp.sum(p_l, axis=-1, keepdims=True)
            o = o + _dot(p_l.astype(bf16), vw)
        outs.append(o / l)
    o_ref[0] = jnp.where(lane < NA_HEAD_DIM, outs[0], outs[1]).astype(o_ref.dtype)


def _na_bias_blocks(rpb):
    H, n_dr, n_dc = rpb.shape
    col = jnp.arange(GRID_W)
    c0 = jnp.clip(col - WIN_W // 2, 0, GRID_W - WIN_W)
    col_ok = (col[None, :] >= c0[:, None]) & (col[None, :] < c0[:, None] + WIN_W)
    dc = jnp.clip(col[None, :] - col[:, None] + WIN_W - 1, 0, n_dc - 1)
    onehot = (dc.reshape(1, -1) == jnp.arange(n_dc)[:, None]).astype(f32)
    blk = jnp.einsum('hrj,jx->hrx', rpb.astype(f32), onehot, precision=lax.Precision.HIGHEST)
    blk = jnp.where(col_ok[None, None], blk.reshape(H, n_dr, GRID_W, GRID_W), NEG_INF)
    blk = jnp.concatenate([jnp.full((H, 1, GRID_W, GRID_W), NEG_INF, f32), blk], axis=1)
    z = jnp.zeros_like(blk)
    tab = jnp.stack([jnp.concatenate([blk, z], axis=-1), jnp.concatenate([z, blk], axis=-1)], axis=1)
    return tab.reshape((H // 2, 2) + tab.shape[1:])


def _na_attention(qkv_q, qkv_ctx, qkv_lat, bias):
    B, T, n3 = qkv_q.shape
    D = n3 // 3
    HP = D // LANES
    L = qkv_ctx.shape[1]
    lat = qkv_lat is not None
    args = [qkv_q, qkv_ctx, qkv_ctx]
    if lat:
        tq = NA_TILE_ROWS * GRID_W
        n_tiles = T // tq
        assert T % tq == 0 and n_tiles >= 2
    else:
        tq = min(_tok_tile(T), 256)
        n_tiles = T // tq
    specs = [pl.BlockSpec((1, tq, LANES), lambda b, h, t: (b, t, h)),
             pl.BlockSpec((1, L, LANES), lambda b, h, t: (b, 0, HP + h)),
             pl.BlockSpec((1, L, LANES), lambda b, h, t: (b, 0, 2 * HP + h))]
    scratch = []
    if lat:
        assert T // GRID_W >= NA_WIN_ROWS and bias.shape[3] == 2 * WIN_H
        args += [qkv_lat, qkv_lat, bias]
        specs += [pl.BlockSpec((1, T, LANES), lambda b, h, t: (b, 0, HP + h)),
                  pl.BlockSpec((1, T, LANES), lambda b, h, t: (b, 0, 2 * HP + h)),
                  pl.BlockSpec((1,) + bias.shape[1:], lambda b, h, t: (h, 0, 0, 0, 0, 0))]
        scratch = [pltpu.VMEM((2, tq, NA_WIN_ROWS * GRID_W), f32)]
    return pl.pallas_call(
        functools.partial(_na_kernel, lat=lat, s_tok=T),
        grid=(B, HP, n_tiles),
        in_specs=specs,
        out_specs=pl.BlockSpec((1, tq, LANES), lambda b, h, t: (b, t, h)),
        out_shape=jax.ShapeDtypeStruct((B, T, D), bf16),
        scratch_shapes=scratch,
        compiler_params=_cparams("parallel", "parallel", "arbitrary"),
        name="na_attention",
    )(*args)


def _split_dot(mask_bf16, x):
    hi = x.astype(bf16)
    lo = (x - hi.astype(f32)).astype(bf16)
    return _dot(mask_bf16, hi) + _dot(mask_bf16, lo)


def _hg_block(z, v, q, lb, st, reverse, with_out):
    n = z.shape[0]
    nch = n // HG_CHUNK
    f = lb + (1 - lb) * jax.nn.sigmoid(z)
    lf = jnp.log(f)
    kk = 1 - f
    ri = lax.broadcasted_iota(jnp.int32, (n, n), 0)
    ci = lax.broadcasted_iota(jnp.int32, (n, n), 1)
    shift = HG_CHUNK.bit_length() - 1
    same = (ri >> shift) == (ci >> shift)
    tri = same & ((ci >= ri) if reverse else (ci <= ri))
    g = _split_dot(tri.astype(bf16), lf)
    gtot = _split_dot(same.astype(bf16), lf)
    kdec = kk * jnp.exp(gtot - g)
    out = None
    if with_out:
        q_dec = q * jnp.exp(g)
        k_inv = kk * jnp.exp(-g)
        att = jnp.where(tri, _dot_nt(q_dec.astype(bf16), k_inv.astype(bf16)), 0.0)
        out_intra = _dot(att.astype(bf16), v.astype(bf16))
        outs = [None] * nch
    order = range(nch - 1, -1, -1) if reverse else range(nch)
    for c in order:
        sl = slice(c * HG_CHUNK, (c + 1) * HG_CHUNK)
        if with_out:
            outs[c] = _dot_nt(q_dec[sl].astype(bf16), st.astype(bf16))
        ds_t = _dot(v[sl].T.astype(bf16), kdec[sl].astype(bf16))
        st = st * jnp.exp(gtot[c * HG_CHUNK:c * HG_CHUNK + 1]) + ds_t
    if with_out:
        out = out_intra + jnp.concatenate(outs, axis=0)
    return out, st


def _hg_kernel(q_ref, i_ref, gate_ref, zf_ref, zb_ref, ic_ref, zfc_ref, zbc_ref, lb_ref, gn_ref, o_ref, acc_ref,
               *, s_tok, l_tok):
    lb = lb_ref[...]
    nblk = s_tok // HG_BLOCK
    for dirn, (z_ref, zc_ref) in enumerate(((zf_ref, zfc_ref), (zb_ref, zbc_ref))):
        reverse = dirn == 1
        _, st = _hg_block(zc_ref[0], ic_ref[0], None, lb, jnp.zeros((HG_EXPAND, HG_EXPAND), f32), reverse, False)

        def body(n, st, z_ref=z_ref, reverse=reverse, dirn=dirn):
            blk = (nblk - 1 - n) if reverse else n
            r = pl.multiple_of(blk * HG_BLOCK, HG_BLOCK)
            rows = pl.ds(r, HG_BLOCK)
            out, st = _hg_block(z_ref[0, rows, :], i_ref[0, rows, :], q_ref[0, rows, :], lb, st, reverse, True)
            if dirn == 0:
                acc_ref[rows, :] = out
            else:
                acc_ref[rows, :] += out
            return st

        lax.fori_loop(0, nblk, body, st)
    gate = gate_ref[0]
    o_ref[0] = (_rms(acc_ref[...], gn_ref[...]) * (gate * jax.nn.sigmoid(gate))).astype(o_ref.dtype)


def _hgrn2_core(p_lat, p_ctx, lb, gn_g):
    B, S, n5 = p_lat.shape
    D = n5 // 5
    H = D // HG_EXPAND
    L = p_ctx.shape[1]
    assert S % HG_BLOCK == 0 and L % HG_CHUNK == 0

    def lat(k):
        return pl.BlockSpec((1, S, HG_EXPAND), lambda b, h: (b, 0, k * H + h))

    def ctx(k):
        return pl.BlockSpec((1, L, HG_EXPAND), lambda b, h: (b, 0, k * H + h))

    vec = pl.BlockSpec((1, HG_EXPAND), lambda b, h: (0, h))
    return pl.pallas_call(
        functools.partial(_hg_kernel, s_tok=S, l_tok=L),
        grid=(B, H),
        in_specs=[lat(0), lat(1), lat(2), lat(3), lat(4), ctx(1), ctx(3), ctx(4), vec, _full((1, HG_EXPAND))],
        out_specs=pl.BlockSpec((1, S, HG_EXPAND), lambda b, h: (b, 0, h)),
        out_shape=jax.ShapeDtypeStruct((B, S, D), bf16),
        scratch_shapes=[pltpu.VMEM((S, HG_EXPAND), f32)],
        compiler_params=_cparams("parallel", "parallel"),
        name="hgrn2_core",
    )(p_lat, p_lat, p_lat, p_lat, p_lat, p_ctx, p_ctx, p_ctx, lb.reshape(1, D), gn_g.reshape(1, HG_EXPAND))


def kernel(x, c, ctx, c_ctx, w_ada, b_ada, g_pre, g_post, w_ff1, w_ff3, w_ff2, s5_a_re, s5_a_im, s5_log_dt, s5_b_re, s5_b_im, s5_c_re, s5_c_im, s5_d, s5_w_glu, s5_b_glu, da_w_qkv, da_w_o, da_lam_q1, da_lam_k1, da_lam_q2, da_lam_k2, da_subln, na_w_qkv, na_w_o, na_rpb, hg_w_qig, hg_w_f, hg_b_f, hg_lb_logits, hg_gnorm, hg_w_o):
    B, S, D = x.shape
    depth = w_ada.shape[0]
    n_mix = 4
    rows_pad = -(-(B + 1) // 8) * 8
    c_all = jnp.concatenate([c, c_ctx[None], jnp.zeros((rows_pad - B - 1, D), f32)], axis=0)
    mods = _ada(c_all, w_ada, b_ada)
    lb_p = jax.nn.softmax(hg_lb_logits.astype(f32), axis=0)
    lower_bounds = jnp.cumsum(lb_p, axis=0) - lb_p[0]
    w1b, w3b, w2b = w_ff1.astype(bf16), w_ff3.astype(bf16), w_ff2.astype(bf16)

    x_lat, x_ctx = x, ctx
    for i in range(depth):
        last = i == depth - 1
        occ, kind = i // n_mix, i % n_mix
        m_lat = mods[i, :B].reshape(B, 3 * N_SUB, D)
        m_ctx = mods[i, B:B + 1].reshape(1, 3 * N_SUB, D)
        streams = [(x_lat, m_lat), (x_ctx, m_ctx)]

        x_lat, x_ctx = [_half_ffn(xs, ms, g_pre[i, 0], g_post[i, 0], w1b[i, 0], w3b[i, 0], w2b[i, 0], 0)
                        for xs, ms in streams]

        if kind == 0:
            h_lat = _prenorm(x_lat, m_lat, g_pre[i, 1], 1)
            h_ctx = _prenorm(x_ctx, m_ctx, g_pre[i, 1], 1)
            y_lat, y_ctx = _s5_scan(h_lat, h_ctx, s5_a_re[occ], s5_a_im[occ], s5_log_dt[occ], s5_b_re[occ],
                                    s5_b_im[occ], s5_c_re[occ], s5_c_im[occ], s5_d[occ])
            wg = s5_w_glu[occ].astype(bf16)
            x_lat_new = _glu_residual(y_lat, x_lat, m_lat, g_post[i, 1], wg, s5_b_glu[occ], 1)
            x_ctx_new = None if last else _glu_residual(y_ctx, x_ctx, m_ctx, g_post[i, 1], wg, s5_b_glu[occ], 1)
        elif kind == 1:
            wq = da_w_qkv[occ].astype(bf16)
            rope = _rope_tables(S, DIFF_HEAD_DIM)
            qkv_lat = _prenorm_proj(x_lat, m_lat, g_pre[i, 1], 1, wq, rope=rope, n_rope=2 * D)
            qkv_ctx = _prenorm_proj(x_ctx, m_ctx, g_pre[i, 1], 1, wq)
            lamv = jnp.stack([da_lam_q1[occ], da_lam_k1[occ], da_lam_q2[occ], da_lam_k2[occ]]).astype(f32)
            lam_init = 0.8 - 0.6 * math.exp(-0.3 * i)
            wo = da_w_o[occ].astype(bf16)
            o_lat = _diff_attention(qkv_lat, qkv_ctx, qkv_lat, lamv, da_subln[occ], lam_init)
            x_lat_new = _outproj_residual(o_lat, x_lat, m_lat, g_post[i, 1], wo, 1)
            x_ctx_new = None
            if not last:
                o_ctx = _diff_attention(qkv_ctx, qkv_ctx, None, lamv, da_subln[occ], lam_init)
                x_ctx_new = _outproj_residual(o_ctx, x_ctx, m_ctx, g_post[i, 1], wo, 1)
        elif kind == 2:
            wq = na_w_qkv[occ].astype(bf16)
            qkv_lat = _prenorm_proj(x_lat, m_lat, g_pre[i, 1], 1, wq)
            qkv_ctx = _prenorm_proj(x_ctx, m_ctx, g_pre[i, 1], 1, wq)
            wo = na_w_o[occ].astype(bf16)
            o_lat = _na_attention(qkv_lat, qkv_ctx, qkv_lat, _na_bias_blocks(na_rpb[occ]))
            x_lat_new = _outproj_residual(o_lat, x_lat, m_lat, g_post[i, 1], wo, 1)
            x_ctx_new = None
            if not last:
                o_ctx = _na_attention(qkv_ctx, qkv_ctx, None, None)
                x_ctx_new = _outproj_residual(o_ctx, x_ctx, m_ctx, g_post[i, 1], wo, 1)
        else:
            assert last, "HGRN2 context outputs are not needed when it is the last layer"
            wp = jnp.concatenate([hg_w_qig[occ], hg_w_f[occ, 0], hg_w_f[occ, 1]], axis=1).astype(bf16)
            bp = jnp.concatenate([jnp.zeros((3 * D,), f32), hg_b_f[occ, 0], hg_b_f[occ, 1]])
            p_lat = _prenorm_proj(x_lat, m_lat, g_pre[i, 1], 1, wp, bias=bp, out_dtype=f32)
            p_ctx = _prenorm_proj(x_ctx, m_ctx, g_pre[i, 1], 1, wp, bias=bp, out_dtype=f32)
            o_lat = _hgrn2_core(p_lat, p_ctx, lower_bounds[i], hg_gnorm[occ])
            x_lat_new = _outproj_residual(o_lat, x_lat, m_lat, g_post[i, 1], hg_w_o[occ].astype(bf16), 1)
            x_ctx_new = None

        x_lat = _half_ffn(x_lat_new, m_lat, g_pre[i, 2], g_post[i, 2], w1b[i, 1], w3b[i, 1], w2b[i, 1], 2)
        if not last:
            x_ctx = _half_ffn(x_ctx_new, m_ctx, g_pre[i, 2], g_post[i, 2], w1b[i, 1], w3b[i, 1], w2b[i, 1], 2)
    return x_lat
```

```python
import functools
import math

import jax
import jax.numpy as jnp
from jax import lax
from jax.experimental import pallas as pl
from jax.experimental.pallas import tpu as pltpu

f32 = jnp.float32
bf16 = jnp.bfloat16

N_SUB = 3
RMS_EPS = 1e-6
NEG_INF = -1e30
ROPE_BASE = 10000.0
GRID_W = 64
S5_GROUP = 16
S5_STATE = 64
S5_CHUNK = 16
DIFF_HEAD_DIM = 64
NA_HEAD_DIM = 64
WIN_H = 8
WIN_W = 16
NA_TILE_ROWS = 8
NA_WIN_ROWS = 16
HG_EXPAND = 128
HG_CHUNK = 64
HG_BLOCK = 512
LANES = 128
VMEM_LIMIT = 56 * 1024 * 1024


def _cparams(*sem):
    return pltpu.CompilerParams(dimension_semantics=sem, vmem_limit_bytes=VMEM_LIMIT)


def _dot(a, b):
    return jnp.dot(a, b, preferred_element_type=f32)


def _dot_nt(a, b):
    return lax.dot_general(a, b, (((1,), (1,)), ((), ())), preferred_element_type=f32)


def _rms(x, g):
    return x * lax.rsqrt(jnp.mean(jnp.square(x), axis=-1, keepdims=True) + RMS_EPS) * g


def _pre_norm(x, g, mod_ref, j):
    shift = mod_ref[0, 3 * j:3 * j + 1, :]
    scale = mod_ref[0, 3 * j + 1:3 * j + 2, :]
    return _rms(x, g) * (1 + scale) + shift


def _post_residual(x, y, g, mod_ref, j, weight):
    gate = mod_ref[0, 3 * j + 2:3 * j + 3, :]
    return x + weight * gate * _rms(y, g)


def _tok_tile(t):
    for tm in (512, 256, 128):
        if t % tm == 0:
            return tm
    raise ValueError(f"token count {t} is not a multiple of 128")


def _full(shape):
    return pl.BlockSpec(shape, lambda *_: (0,) * len(shape))


def _mod_spec(mod):
    nd = mod.shape[1]
    d = mod.shape[2]
    if mod.shape[0] == 1:
        return pl.BlockSpec((1, nd, d), lambda b, t: (0, 0, 0))
    return pl.BlockSpec((1, nd, d), lambda b, t: (b, 0, 0))


def _ada_kernel(c_ref, w_ref, b_ref, o_ref):
    c = c_ref[...]
    sc = (c * jax.nn.sigmoid(c)).astype(bf16)
    o_ref[0] = _dot(sc, w_ref[0].astype(bf16)) + b_ref[0]


def _ada(c_all, w_ada, b_ada):
    depth, d, n = w_ada.shape
    rows = c_all.shape[0]
    tn = n // 4
    return pl.pallas_call(
        _ada_kernel,
        grid=(depth, n // tn),
        in_specs=[pl.BlockSpec((rows, d), lambda i, t: (0, 0)),
                  pl.BlockSpec((1, d, tn), lambda i, t: (i, 0, t)),
                  pl.BlockSpec((1, 1, tn), lambda i, t: (i, 0, t))],
        out_specs=pl.BlockSpec((1, rows, tn), lambda i, t: (i, 0, t)),
        out_shape=jax.ShapeDtypeStruct((depth, rows, n), f32),
        compiler_params=_cparams("arbitrary", "arbitrary"),
        name="ada",
    )(c_all, w_ada, b_ada.reshape(depth, 1, n))


def _ffn_kernel(x_ref, mod_ref, gin_ref, gout_ref, w1_ref, w3_ref, w2_ref, o_ref, *, j, chunks):
    x = x_ref[0]
    h = _pre_norm(x, gin_ref[...], mod_ref, j).astype(bf16)
    acc = None
    for s, n in chunks:
        a = _dot(h, w1_ref[:, s:s + n])
        b = _dot(h, w3_ref[:, s:s + n])
        g = (a * jax.nn.sigmoid(a) * b).astype(bf16)
        y = _dot(g, w2_ref[s:s + n, :])
        acc = y if acc is None else acc + y
    o_ref[0] = _post_residual(x, acc, gout_ref[...], mod_ref, j, 0.5)


def _ff_chunks(dff, step=1024):
    out, s = [], 0
    while s < dff:
        n = min(step, dff - s)
        out.append((s, n))
        s += n
    return tuple(out)


def _half_ffn(x, mod, g_in, g_out, w1, w3, w2, j):
    b, t, d = x.shape
    dff = w1.shape[1]
    tm = _tok_tile(t)
    return pl.pallas_call(
        functools.partial(_ffn_kernel, j=j, chunks=_ff_chunks(dff)),
        grid=(b, t // tm),
        in_specs=[pl.BlockSpec((1, tm, d), lambda bb, tt: (bb, tt, 0)),
                  _mod_spec(mod), _full((1, d)), _full((1, d)),
                  _full((d, dff)), _full((d, dff)), _full((dff, d))],
        out_specs=pl.BlockSpec((1, tm, d), lambda bb, tt: (bb, tt, 0)),
        out_shape=jax.ShapeDtypeStruct((b, t, d), f32),
        compiler_params=_cparams("parallel", "parallel"),
        name="half_ffn",
    )(x, mod, g_in.reshape(1, d), g_out.reshape(1, d), w1, w3, w2)


def _prenorm_kernel(x_ref, mod_ref, g_ref, o_ref, *, j):
    o_ref[0] = _pre_norm(x_ref[0], g_ref[...], mod_ref, j)


def _prenorm(x, mod, g, j):
    b, t, d = x.shape
    tm = _tok_tile(t)
    return pl.pallas_call(
        functools.partial(_prenorm_kernel, j=j),
        grid=(b, t // tm),
        in_specs=[pl.BlockSpec((1, tm, d), lambda bb, tt: (bb, tt, 0)), _mod_spec(mod), _full((1, d))],
        out_specs=pl.BlockSpec((1, tm, d), lambda bb, tt: (bb, tt, 0)),
        out_shape=jax.ShapeDtypeStruct((b, t, d), f32),
        compiler_params=_cparams("parallel", "parallel"),
        name="prenorm",
    )(x, mod, g.reshape(1, d))


def _swap_pairs(x):
    lane = lax.broadcasted_iota(jnp.int32, x.shape, 1)
    nxt = pltpu.roll(x, LANES - 1, 1)
    prv = pltpu.roll(x, 1, 1)
    return jnp.where((lane & 1) == 0, nxt, prv)


def _proj_kernel(*refs, j, n_out, n_rope, has_bias, step):
    x_ref, mod_ref, g_ref, w_ref = refs[:4]
    k = 4
    b_ref = None
    if has_bias:
        b_ref = refs[k]
        k += 1
    if n_rope:
        cos_ref, sin_ref = refs[k], refs[k + 1]
        k += 2
    o_ref = refs[k]
    h = _pre_norm(x_ref[0], g_ref[...], mod_ref, j).astype(bf16)
    for s in range(0, n_out, step):
        y = _dot(h, w_ref[:, s:s + step])
        if has_bias:
            y = y + b_ref[:, s:s + step]
        if s < n_rope:
            cos = cos_ref[...]
            sin = sin_ref[...]
            parts = []
            for c in range(0, step, LANES):
                yc = y[:, c:c + LANES]
                parts.append(yc * cos + _swap_pairs(yc) * sin)
            y = jnp.concatenate(parts, axis=1)
        o_ref[0, :, s:s + step] = y.astype(o_ref.dtype)


def _prenorm_proj(x, mod, g, j, w, bias=None, rope=None, n_rope=0, out_dtype=bf16):
    b, t, d = x.shape
    n = w.shape[1]
    tm = min(_tok_tile(t), 256)
    step = 512
    assert n % step == 0 and n_rope % step == 0
    args = [x, mod, g.reshape(1, d), w]
    specs = [pl.BlockSpec((1, tm, d), lambda bb, tt: (bb, tt, 0)), _mod_spec(mod), _full((1, d)), _full((d, n))]
    if bias is not None:
        args.append(bias.reshape(1, n))
        specs.append(_full((1, n)))
    if n_rope:
        args += list(rope)
        specs += [pl.BlockSpec((tm, LANES), lambda bb, tt: (tt, 0))] * 2
    return pl.pallas_call(
        functools.partial(_proj_kernel, j=j, n_out=n, n_rope=n_rope, has_bias=bias is not None, step=step),
        grid=(b, t // tm),
        in_specs=specs,
        out_specs=pl.BlockSpec((1, tm, n), lambda bb, tt: (bb, tt, 0)),
        out_shape=jax.ShapeDtypeStruct((b, t, n), out_dtype),
        compiler_params=_cparams("parallel", "parallel"),
        name="prenorm_proj",
    )(*args)


def _outproj_kernel(y_ref, x_ref, mod_ref, g_ref, w_ref, o_ref, *, j):
    y = _dot(y_ref[0], w_ref[...])
    o_ref[0] = _post_residual(x_ref[0], y, g_ref[...], mod_ref, j, 1.0)


def _outproj_residual(y, x, mod, g, w, j):
    b, t, d = x.shape
    tm = _tok_tile(t)
    tok = pl.BlockSpec((1, tm, d), lambda bb, tt: (bb, tt, 0))
    return pl.pallas_call(
        functools.partial(_outproj_kernel, j=j),
        grid=(b, t // tm),
        in_specs=[tok, tok, _mod_spec(mod), _full((1, d)), _full((d, d))],
        out_specs=tok,
        out_shape=jax.ShapeDtypeStruct((b, t, d), f32),
        compiler_params=_cparams("parallel", "parallel"),
        name="outproj_residual",
    )(y, x, mod, g.reshape(1, d), w)


def _s5_kernel(x_ref, wb_ref, wk_ref, wc_ref, lam_ref, d_ref, o_ref, s_ref, hin_ref, *, nb, nc_ctx, nc):
    x = x_ref[0]
    xb = x.astype(bf16)
    s_ref[...] = _dot(xb, wb_ref[0])
    P = S5_STATE
    lam = lam_ref[0]
    a_re, a_im = lam[0:1], lam[1:2]
    fwd_lane = lax.broadcasted_iota(jnp.int32, (nb, 2 * P), 1) < P

    def step(n, carry):
        h_re, h_im = carry
        r_f = pl.multiple_of(n * nb, nb)
        n_b = jnp.where(n < nc_ctx, nc_ctx - 1 - n, nc - 1 - (n - nc_ctx))
        r_b = pl.multiple_of(n_b * nb, nb)
        rows_f, rows_b = pl.ds(r_f, nb), pl.ds(r_b, nb)
        hin_ref[rows_f, 0:P] = h_re[:, 0:P]
        hin_ref[rows_b, P:2 * P] = h_re[:, P:2 * P]
        hin_ref[rows_f, 2 * P:3 * P] = h_im[:, 0:P]
        hin_ref[rows_b, 3 * P:4 * P] = h_im[:, P:2 * P]
        s_re = jnp.where(fwd_lane, s_ref[rows_f, 0:2 * P], s_ref[rows_b, 0:2 * P])
        s_im = jnp.where(fwd_lane, s_ref[rows_f, 2 * P:4 * P], s_ref[rows_b, 2 * P:4 * P])
        return a_re * h_re - a_im * h_im + s_re, a_re * h_im + a_im * h_re + s_im

    zero = jnp.zeros((nb, 2 * P), f32)
    lax.fori_loop(0, nc, step, (zero, zero), unroll=4 if nc % 4 == 0 else 1)
    y = x * d_ref[0] + _dot(xb, wk_ref[0]) + _dot(hin_ref[...].astype(bf16), wc_ref[0])
    o_ref[0] = y


def _s5_weights(a_re, a_im, log_dt, b_re, b_im, c_re, c_im):
    hp = lax.Precision.HIGHEST
    C = S5_CHUNK
    G, P = a_re.shape[1:]
    N = b_re.shape[-1]
    j = jnp.arange(C + 1, dtype=f32)
    wbs, wks, wcs, lams = [], [], [], []
    for dirn in range(2):
        are, aim = a_re[dirn].astype(f32), a_im[dirn].astype(f32)
        dt = jnp.exp(log_dt[dirn].astype(f32))[:, None]
        pw_mag = jnp.exp(j[:, None, None] * (are * dt)[None])
        pw_re = pw_mag * jnp.cos(j[:, None, None] * (aim * dt)[None])
        pw_im = pw_mag * jnp.sin(j[:, None, None] * (aim * dt)[None])
        nr, ni = pw_re[1] - 1.0, pw_im[1]
        den = are * are + aim * aim
        fr = (nr * are + ni * aim) / den
        fi = (ni * are - nr * aim) / den
        bre, bim = b_re[dirn].astype(f32), b_im[dirn].astype(f32)
        bbr = fr[..., None] * bre - fi[..., None] * bim
        bbi = fr[..., None] * bim + fi[..., None] * bre
        cre, cim = c_re[dirn].astype(f32), c_im[dirn].astype(f32)
        dist = jnp.arange(C - 1, -1, -1) if dirn == 0 else jnp.arange(C)
        sr = pw_re[dist][..., None] * bbr[None] - pw_im[dist][..., None] * bbi[None]
        si = pw_re[dist][..., None] * bbi[None] + pw_im[dist][..., None] * bbr[None]
        wb = jnp.concatenate([sr, si], axis=2)
        wbs.append(wb.transpose(1, 0, 3, 2).reshape(G, C * N, 2 * P))
        steps = jnp.arange(1, C + 1) if dirn == 0 else jnp.arange(C, 0, -1)
        pr, pi = pw_re[steps], pw_im[steps]
        or_ = cre[None] * pr[:, :, None, :] - cim[None] * pi[:, :, None, :]
        oi_ = -(cre[None] * pi[:, :, None, :] + cim[None] * pr[:, :, None, :])
        wc = jnp.concatenate([or_, oi_], axis=3)
        wcs.append(wc.transpose(1, 3, 0, 2).reshape(G, 2 * P, C * N))
        clr = cre[None] * pw_re[:C, :, None, :] - cim[None] * pw_im[:C, :, None, :]
        cli = cre[None] * pw_im[:C, :, None, :] + cim[None] * pw_re[:C, :, None, :]
        kk = (jnp.einsum('jgnp,gpm->gjnm', clr, bbr, precision=hp)
              - jnp.einsum('jgnp,gpm->gjnm', cli, bbi, precision=hp))
        s_idx = jnp.arange(C)[:, None]
        t_idx = jnp.arange(C)[None, :]
        lag = (t_idx - s_idx) if dirn == 0 else (s_idx - t_idx)
        kt = jnp.where((lag >= 0)[None, :, :, None, None], kk[:, jnp.clip(lag, 0, C - 1)], 0.0)
        wks.append(kt.transpose(0, 1, 4, 2, 3).reshape(G, C * N, C * N))
        lams.append((pw_re[C], pw_im[C]))
    wb = jnp.concatenate([wbs[0][..., :P], wbs[1][..., :P], wbs[0][..., P:], wbs[1][..., P:]], axis=2).astype(bf16)
    wc = jnp.concatenate([wcs[0][:, :P], wcs[1][:, :P], wcs[0][:, P:], wcs[1][:, P:]], axis=1).astype(bf16)
    wk = (wks[0] + wks[1]).astype(bf16)
    lam = jnp.stack([jnp.concatenate([lams[0][0], lams[1][0]], axis=-1),
                     jnp.concatenate([lams[0][1], lams[1][1]], axis=-1)], axis=1)
    return wb, wk, wc, lam


def _s5_scan(h_lat, h_ctx, a_re, a_im, log_dt, b_re, b_im, c_re, c_im, d_skip):
    B, S, D = h_lat.shape
    L = h_ctx.shape[1]
    C, N, P = S5_CHUNK, S5_GROUP, S5_STATE
    G = D // N
    nc_ctx, nc = L // C, (L + S) // C
    assert 2 * P == LANES and L % C == 0 and S % C == 0
    wb, wk, wc, lam = _s5_weights(a_re, a_im, log_dt, b_re, b_im, c_re, c_im)
    d_t = jnp.tile(d_skip.astype(f32).reshape(G, 1, N), (1, C, 1)).reshape(G, 1, C * N)
    hcat = jnp.concatenate([h_ctx, h_lat], axis=1)
    xg = hcat.reshape(B, nc, C, G, N).transpose(3, 1, 0, 2, 4).reshape(G, nc * B, C * N)
    R = nc * B
    y = pl.pallas_call(
        functools.partial(_s5_kernel, nb=B, nc_ctx=nc_ctx, nc=nc),
        grid=(G,),
        in_specs=[pl.BlockSpec((1, R, C * N), lambda g: (g, 0, 0)),
                  pl.BlockSpec((1, C * N, 4 * P), lambda g: (g, 0, 0)),
                  pl.BlockSpec((1, C * N, C * N), lambda g: (g, 0, 0)),
                  pl.BlockSpec((1, 4 * P, C * N), lambda g: (g, 0, 0)),
                  pl.BlockSpec((1, 2, 2 * P), lambda g: (g, 0, 0)),
                  pl.BlockSpec((1, 1, C * N), lambda g: (g, 0, 0))],
        out_specs=pl.BlockSpec((1, R, C * N), lambda g: (g, 0, 0)),
        out_shape=jax.ShapeDtypeStruct((G, R, C * N), f32),
        scratch_shapes=[pltpu.VMEM((R, 4 * P), f32), pltpu.VMEM((R, 4 * P), f32)],
        compiler_params=_cparams("parallel"),
        name="s5_scan",
    )(xg, wb, wk, wc, lam, d_t)
    ycat = y.reshape(G, nc, B, C, N).transpose(2, 1, 3, 0, 4).reshape(B, L + S, D)
    return ycat[:, L:], ycat[:, :L]


def _glu_kernel(y_ref, x_ref, mod_ref, g_ref, w_ref, b_ref, o_ref, *, j):
    z = jax.nn.gelu(y_ref[0])
    u = _dot(z.astype(bf16), w_ref[...]) + b_ref[...]
    o_ref[0] = _post_residual(x_ref[0], z * jax.nn.sigmoid(u), g_ref[...], mod_ref, j, 1.0)


def _glu_residual(y, x, mod, g, w, bias, j):
    b, t, d = x.shape
    tm = _tok_tile(t)
    tok = pl.BlockSpec((1, tm, d), lambda bb, tt: (bb, tt, 0))
    return pl.pallas_call(
        functools.partial(_glu_kernel, j=j),
        grid=(b, t // tm),
        in_specs=[tok, tok, _mod_spec(mod), _full((1, d)), _full((d, d)), _full((1, d))],
        out_specs=tok,
        out_shape=jax.ShapeDtypeStruct((b, t, d), f32),
        compiler_params=_cparams("parallel", "parallel"),
        name="glu_residual",
    )(y, x, mod, g.reshape(1, d), w, bias.reshape(1, d))


def _row_softmax(parts):
    m = functools.reduce(jnp.maximum, [jnp.max(s, axis=-1, keepdims=True) for s in parts])
    ps = [jnp.exp(s - m) for s in parts]
    l = functools.reduce(lambda a, b: a + b, [jnp.sum(p, axis=-1, keepdims=True) for p in ps])
    return ps, l


def _diff_scores(q, k_refs, s_refs):
    lane = lax.broadcasted_iota(jnp.int32, q.shape, 1)
    qs = (q.astype(f32) * DIFF_HEAD_DIM ** -0.5).astype(bf16)
    zero = jnp.zeros_like(qs)
    q1 = jnp.where(lane < DIFF_HEAD_DIM, qs, zero)
    q2 = jnp.where(lane >= DIFF_HEAD_DIM, qs, zero)
    for k_ref, s_ref in zip(k_refs, s_refs):
        k = k_ref[0]
        s_ref[0] = _dot_nt(q1, k)
        s_ref[1] = _dot_nt(q2, k)


def _diff_finish(s_refs, v_refs, lamv_ref, g_ref, o_ref, lam_init):
    p1, l1 = _row_softmax([s_ref[0] for s_ref in s_refs])
    p2, l2 = _row_softmax([s_ref[1] for s_ref in s_refs])
    lv = lamv_ref[...]
    lam = (jnp.exp(jnp.sum(lv[0:1] * lv[1:2], axis=-1, keepdims=True))
           - jnp.exp(jnp.sum(lv[2:3] * lv[3:4], axis=-1, keepdims=True)) + lam_init)
    c = lam * l1 / l2
    o = None
    for a, b, v_ref in zip(p1, p2, v_refs):
        part = _dot((a - c * b).astype(bf16), v_ref[0])
        o = part if o is None else o + part
    o = o / l1
    o_ref[0] = (_rms(o, g_ref[...]) * (1 - lam_init)).astype(o_ref.dtype)


def _diff_kernel(*refs, n_seg, nq, lam_init):
    q_ref, qn_ref = refs[:2]
    k_refs = refs[2:2 + 2 * n_seg:2]
    v_refs = refs[3:3 + 2 * n_seg:2]
    lamv_ref, g_ref, o_ref = refs[2 + 2 * n_seg:5 + 2 * n_seg]
    scratch = refs[5 + 2 * n_seg:]
    slots = (scratch[:n_seg], scratch[n_seg:])
    t = pl.program_id(2)

    @pl.when(t == 0)
    def _():
        _diff_scores(q_ref[0], k_refs, slots[0])

    if nq == 1:
        _diff_finish(slots[0], v_refs, lamv_ref, g_ref, o_ref, lam_init)
        return
    for parity in range(2):
        @pl.when((t & 1) == parity)
        def _(parity=parity):
            _diff_scores(qn_ref[0], k_refs, slots[1 - parity])
            _diff_finish(slots[parity], v_refs, lamv_ref, g_ref, o_ref, lam_init)


def _diff_attention(qkv_q, qkv_ctx, qkv_lat, lamv, subln, lam_init):
    B, T, n3 = qkv_q.shape
    D = n3 // 3
    H = D // LANES
    tq = min(_tok_tile(T), 256)
    nq = T // tq
    kv = [qkv_ctx] if qkv_lat is None else [qkv_ctx, qkv_lat]
    args = [qkv_q, qkv_q]
    specs = [pl.BlockSpec((1, tq, LANES), lambda b, h, t: (b, t, h)),
             pl.BlockSpec((1, tq, LANES), lambda b, h, t: (b, jnp.minimum(t + 1, nq - 1), h))]
    for a in kv:
        args += [a, a]
        specs += [pl.BlockSpec((1, a.shape[1], LANES), lambda b, h, t: (b, 0, H + h)),
                  pl.BlockSpec((1, a.shape[1], LANES), lambda b, h, t: (b, 0, 2 * H + h))]
    args += [lamv, subln.reshape(1, LANES)]
    specs += [_full(lamv.shape), _full((1, LANES))]
    scores = [pltpu.VMEM((2, tq, a.shape[1]), f32) for a in kv]
    return pl.pallas_call(
        functools.partial(_diff_kernel, n_seg=len(kv), nq=nq, lam_init=lam_init),
        grid=(B, H, nq),
        in_specs=specs,
        out_specs=pl.BlockSpec((1, tq, LANES), lambda b, h, t: (b, t, h)),
        out_shape=jax.ShapeDtypeStruct((B, T, D), bf16),
        scratch_shapes=scores + scores,
        compiler_params=_cparams("parallel", "parallel", "arbitrary"),
        name="diff_attention",
    )(*args)


def _rope_tables(n_tokens, head_dim):
    n_freq = head_dim // 4
    inv_freq = ROPE_BASE ** (-jnp.arange(n_freq, dtype=f32) / n_freq)
    t = jnp.arange(n_tokens)
    row = (t // GRID_W).astype(f32)
    col = (t % GRID_W).astype(f32)
    ang = jnp.concatenate([row[:, None] * inv_freq, col[:, None] * inv_freq], axis=-1)
    cos = jnp.repeat(jnp.cos(ang), 2, axis=-1)
    sin = jnp.repeat(jnp.sin(ang), 2, axis=-1) * jnp.tile(jnp.array([-1.0, 1.0], f32), head_dim // 2)
    reps = LANES // head_dim
    return jnp.tile(cos, (1, reps)), jnp.tile(sin, (1, reps))


def _na_kernel(*refs, lat, s_tok):
    if lat:
        q_ref, kc_ref, vc_ref, kl_ref, vl_ref, tb_ref, o_ref, bias_ref = refs
    else:
        q_ref, kc_ref, vc_ref, o_ref = refs
    q = q_ref[0]
    lane = lax.broadcasted_iota(jnp.int32, q.shape, 1)
    scale = NA_HEAD_DIM ** -0.5
    qs = (q.astype(f32) * scale).astype(bf16)
    zero = jnp.zeros_like(qs)
    kc, vc = kc_ref[0], vc_ref[0]
    if lat:
        t = pl.program_id(2)
        rows = s_tok // GRID_W
        w0 = jnp.clip(t * NA_TILE_ROWS - (NA_WIN_ROWS - NA_TILE_ROWS) // 2, 0, rows - NA_WIN_ROWS)
        tok0 = pl.multiple_of(w0 * GRID_W, 4 * GRID_W)
        kw = kl_ref[0, pl.ds(tok0, NA_WIN_ROWS * GRID_W), :]
        vw = vl_ref[0, pl.ds(tok0, NA_WIN_ROWS * GRID_W), :]
        for i in range(NA_TILE_ROWS):
            qr = t * NA_TILE_ROWS + i
            r0 = jnp.clip(qr - WIN_H // 2, 0, rows - WIN_H)
            for j in range(NA_WIN_ROWS // 2):
                es = []
                for kr in (w0 + 2 * j, w0 + 2 * j + 1):
                    ok = (kr >= r0) & (kr < r0 + WIN_H)
                    es.append(jnp.where(ok, kr - qr + WIN_H, 0))
                for hh in range(2):
                    bias_ref[hh, i * GRID_W:(i + 1) * GRID_W, j * LANES:(j + 1) * LANES] = (
                        tb_ref[0, hh, 0, es[0]] + tb_ref[0, hh, 1, es[1]])
    outs = []
    for hh in range(2):
        sel = (lane < NA_HEAD_DIM) if hh == 0 else (lane >= NA_HEAD_DIM)
        qh = jnp.where(sel, qs, zero)
        s_c = _dot_nt(qh, kc)
        m = jnp.max(s_c, axis=-1, keepdims=True)
        if lat:
            s_l = _dot_nt(qh, kw) + bias_ref[hh]
            m = jnp.maximum(m, jnp.max(s_l, axis=-1, keepdims=True))
        p_c = jnp.exp(s_c - m)
        l = jnp.sum(p_c, axis=-1, keepdims=True)
        o = _dot(p_c.astype(bf16), vc)
        if lat:
            p_l = jnp.exp(s_l - m)
            l = l + jnp.sum(p_l, axis=-1, keepdims=True)
            o = o + _dot(p_l.astype(bf16), vw)
        outs.append(o / l)
    o_ref[0] = jnp.where(lane < NA_HEAD_DIM, outs[0], outs[1]).astype(o_ref.dtype)


def _na_bias_blocks(rpb):
    H, n_dr, n_dc = rpb.shape
    col = jnp.arange(GRID_W)
    c0 = jnp.clip(col - WIN_W // 2, 0, GRID_W - WIN_W)
    col_ok = (col[None, :] >= c0[:, None]) & (col[None, :] < c0[:, None] + WIN_W)
    dc = jnp.clip(col[None, :] - col[:, None] + WIN_W - 1, 0, n_dc - 1)
    onehot = (dc.reshape(1, -1) == jnp.arange(n_dc)[:, None]).astype(f32)
    blk = jnp.einsum('hrj,jx->hrx', rpb.astype(f32), onehot, precision=lax.Precision.HIGHEST)
    blk = jnp.where(col_ok[None, None], blk.reshape(H, n_dr, GRID_W, GRID_W), NEG_INF)
    blk = jnp.concatenate([jnp.full((H, 1, GRID_W, GRID_W), NEG_INF, f32), blk], axis=1)
    z = jnp.zeros_like(blk)
    tab = jnp.stack([jnp.concatenate([blk, z], axis=-1), jnp.concatenate([z, blk], axis=-1)], axis=1)
    return tab.reshape((H // 2, 2) + tab.shape[1:])


def _na_attention(qkv_q, qkv_ctx, qkv_lat, bias):
    B, T, n3 = qkv_q.shape
    D = n3 // 3
    HP = D // LANES
    L = qkv_ctx.shape[1]
    lat = qkv_lat is not None
    args = [qkv_q, qkv_ctx, qkv_ctx]
    if lat:
        tq = NA_TILE_ROWS * GRID_W
        n_tiles = T // tq
        assert T % tq == 0 and n_tiles >= 2
    else:
        tq = min(_tok_tile(T), 256)
        n_tiles = T // tq
    specs = [pl.BlockSpec((1, tq, LANES), lambda b, h, t: (b, t, h)),
             pl.BlockSpec((1, L, LANES), lambda b, h, t: (b, 0, HP + h)),
             pl.BlockSpec((1, L, LANES), lambda b, h, t: (b, 0, 2 * HP + h))]
    scratch = []
    if lat:
        assert T // GRID_W >= NA_WIN_ROWS and bias.shape[3] == 2 * WIN_H
        args += [qkv_lat, qkv_lat, bias]
        specs += [pl.BlockSpec((1, T, LANES), lambda b, h, t: (b, 0, HP + h)),
                  pl.BlockSpec((1, T, LANES), lambda b, h, t: (b, 0, 2 * HP + h)),
                  pl.BlockSpec((1,) + bias.shape[1:], lambda b, h, t: (h, 0, 0, 0, 0, 0))]
        scratch = [pltpu.VMEM((2, tq, NA_WIN_ROWS * GRID_W), f32)]
    return pl.pallas_call(
        functools.partial(_na_kernel, lat=lat, s_tok=T),
        grid=(B, HP, n_tiles),
        in_specs=specs,
        out_specs=pl.BlockSpec((1, tq, LANES), lambda b, h, t: (b, t, h)),
        out_shape=jax.ShapeDtypeStruct((B, T, D), bf16),
        scratch_shapes=scratch,
        compiler_params=_cparams("parallel", "parallel", "arbitrary"),
        name="na_attention",
    )(*args)


def _chunk_cumsum(x, reverse):
    n = x.shape[0]
    pos = lax.broadcasted_iota(jnp.int32, x.shape, 0) & (HG_CHUNK - 1)
    k = 1
    while k < HG_CHUNK:
        if reverse:
            x = x + jnp.where(pos < HG_CHUNK - k, pltpu.roll(x, n - k, 0), 0.0)
        else:
            x = x + jnp.where(pos >= k, pltpu.roll(x, k, 0), 0.0)
        k *= 2
    return x


def _hg_block(z, v, q, lb, st, reverse, with_out):
    n = z.shape[0]
    nch = n // HG_CHUNK
    f = lb + (1 - lb) * jax.nn.sigmoid(z)
    lf = jnp.log(f)
    kk = 1 - f
    g = _chunk_cumsum(lf, reverse)
    out = None
    if with_out:
        ri = lax.broadcasted_iota(jnp.int32, (n, n), 0)
        ci = lax.broadcasted_iota(jnp.int32, (n, n), 1)
        shift = HG_CHUNK.bit_length() - 1
        tri = ((ri >> shift) == (ci >> shift)) & ((ci >= ri) if reverse else (ci <= ri))
        q_dec = q * jnp.exp(g)
        k_inv = kk * jnp.exp(-g)
        att = jnp.where(tri, _dot_nt(q_dec.astype(bf16), k_inv.astype(bf16)), 0.0)
        out_intra = _dot(att.astype(bf16), v.astype(bf16))
        outs = [None] * nch
    order = range(nch - 1, -1, -1) if reverse else range(nch)
    for c in order:
        sl = slice(c * HG_CHUNK, (c + 1) * HG_CHUNK)
        end = c * HG_CHUNK if reverse else (c + 1) * HG_CHUNK - 1
        g_tot = g[end:end + 1]
        if with_out:
            outs[c] = _dot_nt(q_dec[sl].astype(bf16), st.astype(bf16))
        kdec = kk[sl] * jnp.exp(g_tot - g[sl])
        ds_t = _dot(v[sl].T.astype(bf16), kdec.astype(bf16))
        st = st * jnp.exp(g_tot) + ds_t
    if with_out:
        out = out_intra + jnp.concatenate(outs, axis=0)
    return out, st


def _hg_kernel(q_ref, i_ref, gate_ref, zf_ref, zb_ref, ic_ref, zfc_ref, zbc_ref, lb_ref, gn_ref, o_ref,
               accf_ref, accb_ref, *, s_tok):
    lb = lb_ref[...]
    nblk = s_tok // HG_BLOCK
    zero = jnp.zeros((HG_EXPAND, HG_EXPAND), f32)
    _, st_f = _hg_block(zfc_ref[0], ic_ref[0], None, lb, zero, False, False)
    _, st_b = _hg_block(zbc_ref[0], ic_ref[0], None, lb, zero, True, False)

    def body(n, carry):
        st_f, st_b = carry
        rows_f = pl.ds(pl.multiple_of(n * HG_BLOCK, HG_BLOCK), HG_BLOCK)
        rows_b = pl.ds(pl.multiple_of((nblk - 1 - n) * HG_BLOCK, HG_BLOCK), HG_BLOCK)
        out_f, st_f = _hg_block(zf_ref[0, rows_f, :], i_ref[0, rows_f, :], q_ref[0, rows_f, :], lb, st_f, False, True)
        out_b, st_b = _hg_block(zb_ref[0, rows_b, :], i_ref[0, rows_b, :], q_ref[0, rows_b, :], lb, st_b, True, True)
        accf_ref[rows_f, :] = out_f
        accb_ref[rows_b, :] = out_b
        return st_f, st_b

    lax.fori_loop(0, nblk, body, (st_f, st_b))
    gate = gate_ref[0]
    o = accf_ref[...] + accb_ref[...]
    o_ref[0] = (_rms(o, gn_ref[...]) * (gate * jax.nn.sigmoid(gate))).astype(o_ref.dtype)


def _hgrn2_core(p_lat, p_ctx, lb, gn_g):
    B, S, n5 = p_lat.shape
    D = n5 // 5
    H = D // HG_EXPAND
    L = p_ctx.shape[1]
    assert S % HG_BLOCK == 0 and L % HG_CHUNK == 0

    def lat(k):
        return pl.BlockSpec((1, S, HG_EXPAND), lambda b, h: (b, 0, k * H + h))

    def ctx(k):
        return pl.BlockSpec((1, L, HG_EXPAND), lambda b, h: (b, 0, k * H + h))

    vec = pl.BlockSpec((1, HG_EXPAND), lambda b, h: (0, h))
    return pl.pallas_call(
        functools.partial(_hg_kernel, s_tok=S),
        grid=(B, H),
        in_specs=[lat(0), lat(1), lat(2), lat(3), lat(4), ctx(1), ctx(3), ctx(4), vec, _full((1, HG_EXPAND))],
        out_specs=pl.BlockSpec((1, S, HG_EXPAND), lambda b, h: (b, 0, h)),
        out_shape=jax.ShapeDtypeStruct((B, S, D), bf16),
        scratch_shapes=[pltpu.VMEM((S, HG_EXPAND), f32), pltpu.VMEM((S, HG_EXPAND), f32)],
        compiler_params=_cparams("parallel", "parallel"),
        name="hgrn2_core",
    )(p_lat, p_lat, p_lat, p_lat, p_lat, p_ctx, p_ctx, p_ctx, lb.reshape(1, D), gn_g.reshape(1, HG_EXPAND))


def kernel(x, c, ctx, c_ctx, w_ada, b_ada, g_pre, g_post, w_ff1, w_ff3, w_ff2, s5_a_re, s5_a_im, s5_log_dt, s5_b_re, s5_b_im, s5_c_re, s5_c_im, s5_d, s5_w_glu, s5_b_glu, da_w_qkv, da_w_o, da_lam_q1, da_lam_k1, da_lam_q2, da_lam_k2, da_subln, na_w_qkv, na_w_o, na_rpb, hg_w_qig, hg_w_f, hg_b_f, hg_lb_logits, hg_gnorm, hg_w_o):
    B, S, D = x.shape
    depth = w_ada.shape[0]
    n_mix = 4
    rows_pad = -(-(B + 1) // 8) * 8
    c_all = jnp.concatenate([c, c_ctx[None], jnp.zeros((rows_pad - B - 1, D), f32)], axis=0)
    mods = _ada(c_all, w_ada, b_ada)
    lb_p = jax.nn.softmax(hg_lb_logits.astype(f32), axis=0)
    lower_bounds = jnp.cumsum(lb_p, axis=0) - lb_p[0]
    w1b, w3b, w2b = w_ff1.astype(bf16), w_ff3.astype(bf16), w_ff2.astype(bf16)

    x_lat, x_ctx = x, ctx
    for i in range(depth):
        last = i == depth - 1
        occ, kind = i // n_mix, i % n_mix
        m_lat = mods[i, :B].reshape(B, 3 * N_SUB, D)
        m_ctx = mods[i, B:B + 1].reshape(1, 3 * N_SUB, D)
        streams = [(x_lat, m_lat), (x_ctx, m_ctx)]

        x_lat, x_ctx = [_half_ffn(xs, ms, g_pre[i, 0], g_post[i, 0], w1b[i, 0], w3b[i, 0], w2b[i, 0], 0)
                        for xs, ms in streams]

        if kind == 0:
            h_lat = _prenorm(x_lat, m_lat, g_pre[i, 1], 1)
            h_ctx = _prenorm(x_ctx, m_ctx, g_pre[i, 1], 1)
            y_lat, y_ctx = _s5_scan(h_lat, h_ctx, s5_a_re[occ], s5_a_im[occ], s5_log_dt[occ], s5_b_re[occ],
                                    s5_b_im[occ], s5_c_re[occ], s5_c_im[occ], s5_d[occ])
            wg = s5_w_glu[occ].astype(bf16)
            x_lat_new = _glu_residual(y_lat, x_lat, m_lat, g_post[i, 1], wg, s5_b_glu[occ], 1)
            x_ctx_new = None if last else _glu_residual(y_ctx, x_ctx, m_ctx, g_post[i, 1], wg, s5_b_glu[occ], 1)
        elif kind == 1:
            wq = da_w_qkv[occ].astype(bf16)
            rope = _rope_tables(S, DIFF_HEAD_DIM)
            qkv_lat = _prenorm_proj(x_lat, m_lat, g_pre[i, 1], 1, wq, rope=rope, n_rope=2 * D)
            qkv_ctx = _prenorm_proj(x_ctx, m_ctx, g_pre[i, 1], 1, wq)
            lamv = jnp.stack([da_lam_q1[occ], da_lam_k1[occ], da_lam_q2[occ], da_lam_k2[occ]]).astype(f32)
            lam_init = 0.8 - 0.6 * math.exp(-0.3 * i)
            wo = da_w_o[occ].astype(bf16)
            o_lat = _diff_attention(qkv_lat, qkv_ctx, qkv_lat, lamv, da_subln[occ], lam_init)
            x_lat_new = _outproj_residual(o_lat, x_lat, m_lat, g_post[i, 1], wo, 1)
            x_ctx_new = None
            if not last:
                o_ctx = _diff_attention(qkv_ctx, qkv_ctx, None, lamv, da_subln[occ], lam_init)
                x_ctx_new = _outproj_residual(o_ctx, x_ctx, m_ctx, g_post[i, 1], wo, 1)
        elif kind == 2:
            wq = na_w_qkv[occ].astype(bf16)
            qkv_lat = _prenorm_proj(x_lat, m_lat, g_pre[i, 1], 1, wq)
            qkv_ctx = _prenorm_proj(x_ctx, m_ctx, g_pre[i, 1], 1, wq)
            wo = na_w_o[occ].astype(bf16)
            o_lat = _na_attention(qkv_lat, qkv_ctx, qkv_lat, _na_bias_blocks(na_rpb[occ]))
            x_lat_new = _outproj_residual(o_lat, x_lat, m_lat, g_post[i, 1], wo, 1)
            x_ctx_new = None
            if not last:
                o_ctx = _na_attention(qkv_ctx, qkv_ctx, None, None)
                x_ctx_new = _outproj_residual(o_ctx, x_ctx, m_ctx, g_post[i, 1], wo, 1)
        else:
            assert last, "HGRN2 context outputs are not needed when it is the last layer"
            wp = jnp.concatenate([hg_w_qig[occ], hg_w_f[occ, 0], hg_w_f[occ, 1]], axis=1).astype(bf16)
            bp = jnp.concatenate([jnp.zeros((3 * D,), f32), hg_b_f[occ, 0], hg_b_f[occ, 1]])
            p_lat = _prenorm_proj(x_lat, m_lat, g_pre[i, 1], 1, wp, bias=bp, out_dtype=f32)
            p_ctx = _prenorm_proj(x_ctx, m_ctx, g_pre[i, 1], 1, wp, bias=bp, out_dtype=f32)
            o_lat = _hgrn2_core(p_lat, p_ctx, lower_bounds[i], hg_gnorm[occ])
            x_lat_new = _outproj_residual(o_lat, x_lat, m_lat, g_post[i, 1], hg_w_o[occ].astype(bf16), 1)
            x_ctx_new = None

        x_lat = _half_ffn(x_lat_new, m_lat, g_pre[i, 2], g_post[i, 2], w1b[i, 1], w3b[i, 1], w2b[i, 1], 2)
        if not last:
            x_ctx = _half_ffn(x_ctx_new, m_ctx, g_pre[i, 2], g_post[i, 2], w1b[i, 1], w3b[i, 1], w2b[i, 1], 2)
    return x_lat
```

```python
import functools
import math

import jax
import jax.numpy as jnp
from jax import lax
from jax.experimental import pallas as pl
from jax.experimental.pallas import tpu as pltpu

f32 = jnp.float32
bf16 = jnp.bfloat16

N_SUB = 3
RMS_EPS = 1e-6
NEG_INF = -1e30
ROPE_BASE = 10000.0
GRID_W = 64
S5_GROUP = 16
S5_STATE = 64
S5_CHUNK = 16
DIFF_HEAD_DIM = 64
DIFF_KEY_CHUNK = 1024
NA_HEAD_DIM = 64
WIN_H = 8
WIN_W = 16
NA_TILE_ROWS = 8
NA_WIN_ROWS = 16
HG_EXPAND = 128
HG_CHUNK = 64
HG_BLOCK = 512
LANES = 128
VMEM_LIMIT = 56 * 1024 * 1024


def _cparams(*sem):
    return pltpu.CompilerParams(dimension_semantics=sem, vmem_limit_bytes=VMEM_LIMIT)


def _dot(a, b):
    return jnp.dot(a, b, preferred_element_type=f32)


def _dot_nt(a, b):
    return lax.dot_general(a, b, (((1,), (1,)), ((), ())), preferred_element_type=f32)


def _rms(x, g):
    return x * lax.rsqrt(jnp.mean(jnp.square(x), axis=-1, keepdims=True) + RMS_EPS) * g


def _pre_norm(x, g, mod_ref, j):
    shift = mod_ref[0, 3 * j:3 * j + 1, :]
    scale = mod_ref[0, 3 * j + 1:3 * j + 2, :]
    return _rms(x, g) * (1 + scale) + shift


def _post_residual(x, y, g, mod_ref, j, weight):
    gate = mod_ref[0, 3 * j + 2:3 * j + 3, :]
    return x + weight * gate * _rms(y, g)


def _tok_tile(t):
    for tm in (512, 256, 128):
        if t % tm == 0:
            return tm
    raise ValueError(f"token count {t} is not a multiple of 128")


def _full(shape):
    return pl.BlockSpec(shape, lambda *_: (0,) * len(shape))


def _mod_spec(mod):
    nd = mod.shape[1]
    d = mod.shape[2]
    if mod.shape[0] == 1:
        return pl.BlockSpec((1, nd, d), lambda b, t: (0, 0, 0))
    return pl.BlockSpec((1, nd, d), lambda b, t: (b, 0, 0))


def _ada_kernel(c_ref, w_ref, b_ref, o_ref):
    c = c_ref[...]
    sc = (c * jax.nn.sigmoid(c)).astype(bf16)
    o_ref[0] = _dot(sc, w_ref[0].astype(bf16)) + b_ref[0]


def _ada(c_all, w_ada, b_ada):
    depth, d, n = w_ada.shape
    rows = c_all.shape[0]
    tn = n // 4
    return pl.pallas_call(
        _ada_kernel,
        grid=(depth, n // tn),
        in_specs=[pl.BlockSpec((rows, d), lambda i, t: (0, 0)),
                  pl.BlockSpec((1, d, tn), lambda i, t: (i, 0, t)),
                  pl.BlockSpec((1, 1, tn), lambda i, t: (i, 0, t))],
        out_specs=pl.BlockSpec((1, rows, tn), lambda i, t: (i, 0, t)),
        out_shape=jax.ShapeDtypeStruct((depth, rows, n), f32),
        compiler_params=_cparams("arbitrary", "arbitrary"),
        name="ada",
    )(c_all, w_ada, b_ada.reshape(depth, 1, n))


def _ffn_kernel(x_ref, mod_ref, gin_ref, gout_ref, w1_ref, w3_ref, w2_ref, o_ref, *, j, chunks):
    x = x_ref[0]
    h = _pre_norm(x, gin_ref[...], mod_ref, j).astype(bf16)
    acc = None
    for s, n in chunks:
        a = _dot(h, w1_ref[:, s:s + n])
        b = _dot(h, w3_ref[:, s:s + n])
        g = (a * jax.nn.sigmoid(a) * b).astype(bf16)
        y = _dot(g, w2_ref[s:s + n, :])
        acc = y if acc is None else acc + y
    o_ref[0] = _post_residual(x, acc, gout_ref[...], mod_ref, j, 0.5)


def _ff_chunks(dff, step=1024):
    out, s = [], 0
    while s < dff:
        n = min(step, dff - s)
        out.append((s, n))
        s += n
    return tuple(out)


def _half_ffn(x, mod, g_in, g_out, w1, w3, w2, j):
    b, t, d = x.shape
    dff = w1.shape[1]
    tm = _tok_tile(t)
    return pl.pallas_call(
        functools.partial(_ffn_kernel, j=j, chunks=_ff_chunks(dff)),
        grid=(b, t // tm),
        in_specs=[pl.BlockSpec((1, tm, d), lambda bb, tt: (bb, tt, 0)),
                  _mod_spec(mod), _full((1, d)), _full((1, d)),
                  _full((d, dff)), _full((d, dff)), _full((dff, d))],
        out_specs=pl.BlockSpec((1, tm, d), lambda bb, tt: (bb, tt, 0)),
        out_shape=jax.ShapeDtypeStruct((b, t, d), f32),
        compiler_params=_cparams("parallel", "parallel"),
        name="half_ffn",
    )(x, mod, g_in.reshape(1, d), g_out.reshape(1, d), w1, w3, w2)


def _prenorm_kernel(x_ref, mod_ref, g_ref, o_ref, *, j):
    o_ref[0] = _pre_norm(x_ref[0], g_ref[...], mod_ref, j).astype(o_ref.dtype)


def _prenorm(x, mod, g, j):
    b, t, d = x.shape
    tm = _tok_tile(t)
    return pl.pallas_call(
        functools.partial(_prenorm_kernel, j=j),
        grid=(b, t // tm),
        in_specs=[pl.BlockSpec((1, tm, d), lambda bb, tt: (bb, tt, 0)), _mod_spec(mod), _full((1, d))],
        out_specs=pl.BlockSpec((1, tm, d), lambda bb, tt: (bb, tt, 0)),
        out_shape=jax.ShapeDtypeStruct((b, t, d), bf16),
        compiler_params=_cparams("parallel", "parallel"),
        name="prenorm",
    )(x, mod, g.reshape(1, d))


def _swap_pairs(x):
    lane = lax.broadcasted_iota(jnp.int32, x.shape, 1)
    nxt = pltpu.roll(x, LANES - 1, 1)
    prv = pltpu.roll(x, 1, 1)
    return jnp.where((lane & 1) == 0, nxt, prv)


def _proj_kernel(*refs, j, n_out, n_rope, has_bias, step):
    x_ref, mod_ref, g_ref, w_ref = refs[:4]
    k = 4
    b_ref = None
    if has_bias:
        b_ref = refs[k]
        k += 1
    if n_rope:
        cos_ref, sin_ref = refs[k], refs[k + 1]
        k += 2
    o_ref = refs[k]
    h = _pre_norm(x_ref[0], g_ref[...], mod_ref, j).astype(bf16)
    for s in range(0, n_out, step):
        y = _dot(h, w_ref[:, s:s + step])
        if has_bias:
            y = y + b_ref[:, s:s + step]
        if s < n_rope:
            cos = cos_ref[...]
            sin = sin_ref[...]
            parts = []
            for c in range(0, step, LANES):
                yc = y[:, c:c + LANES]
                parts.append(yc * cos + _swap_pairs(yc) * sin)
            y = jnp.concatenate(parts, axis=1)
        o_ref[0, :, s:s + step] = y.astype(o_ref.dtype)


def _prenorm_proj(x, mod, g, j, w, bias=None, rope=None, n_rope=0, out_dtype=bf16):
    b, t, d = x.shape
    n = w.shape[1]
    tm = min(_tok_tile(t), 256)
    step = 512
    assert n % step == 0 and n_rope % step == 0
    args = [x, mod, g.reshape(1, d), w]
    specs = [pl.BlockSpec((1, tm, d), lambda bb, tt: (bb, tt, 0)), _mod_spec(mod), _full((1, d)), _full((d, n))]
    if bias is not None:
        args.append(bias.reshape(1, n))
        specs.append(_full((1, n)))
    if n_rope:
        args += list(rope)
        specs += [pl.BlockSpec((tm, LANES), lambda bb, tt: (tt, 0))] * 2
    return pl.pallas_call(
        functools.partial(_proj_kernel, j=j, n_out=n, n_rope=n_rope, has_bias=bias is not None, step=step),
        grid=(b, t // tm),
        in_specs=specs,
        out_specs=pl.BlockSpec((1, tm, n), lambda bb, tt: (bb, tt, 0)),
        out_shape=jax.ShapeDtypeStruct((b, t, n), out_dtype),
        compiler_params=_cparams("parallel", "parallel"),
        name="prenorm_proj",
    )(*args)


def _outproj_kernel(y_ref, x_ref, mod_ref, g_ref, w_ref, o_ref, *, j):
    y = _dot(y_ref[0], w_ref[...])
    o_ref[0] = _post_residual(x_ref[0], y, g_ref[...], mod_ref, j, 1.0)


def _outproj_residual(y, x, mod, g, w, j):
    b, t, d = x.shape
    tm = _tok_tile(t)
    tok = pl.BlockSpec((1, tm, d), lambda bb, tt: (bb, tt, 0))
    return pl.pallas_call(
        functools.partial(_outproj_kernel, j=j),
        grid=(b, t // tm),
        in_specs=[tok, tok, _mod_spec(mod), _full((1, d)), _full((d, d))],
        out_specs=tok,
        out_shape=jax.ShapeDtypeStruct((b, t, d), f32),
        compiler_params=_cparams("parallel", "parallel"),
        name="outproj_residual",
    )(y, x, mod, g.reshape(1, d), w)


def _s5_kernel(xc_ref, xl_ref, wb_ref, wk_ref, wc_ref, lam_ref, yc_ref, yl_ref, s_ref, hin_ref, *, nb, nc_ctx, nc):
    rc = nc_ctx * nb
    xc, xl = xc_ref[0], xl_ref[0]
    s_ref[0:rc, :] = _dot(xc, wb_ref[0])
    s_ref[rc:, :] = _dot(xl, wb_ref[0])
    P = S5_STATE
    lam = lam_ref[0]
    a_re, a_im = lam[0:1], lam[1:2]
    fwd_lane = lax.broadcasted_iota(jnp.int32, (nb, 2 * P), 1) < P

    def step(n, carry):
        h_re, h_im = carry
        r_f = pl.multiple_of(n * nb, nb)
        n_b = jnp.where(n < nc_ctx, nc_ctx - 1 - n, nc - 1 - (n - nc_ctx))
        r_b = pl.multiple_of(n_b * nb, nb)
        rows_f, rows_b = pl.ds(r_f, nb), pl.ds(r_b, nb)
        hin_ref[rows_f, 0:P] = h_re[:, 0:P]
        hin_ref[rows_b, P:2 * P] = h_re[:, P:2 * P]
        hin_ref[rows_f, 2 * P:3 * P] = h_im[:, 0:P]
        hin_ref[rows_b, 3 * P:4 * P] = h_im[:, P:2 * P]
        s_re = jnp.where(fwd_lane, s_ref[rows_f, 0:2 * P], s_ref[rows_b, 0:2 * P])
        s_im = jnp.where(fwd_lane, s_ref[rows_f, 2 * P:4 * P], s_ref[rows_b, 2 * P:4 * P])
        return a_re * h_re - a_im * h_im + s_re, a_re * h_im + a_im * h_re + s_im

    zero = jnp.zeros((nb, 2 * P), f32)
    lax.fori_loop(0, nc, step, (zero, zero), unroll=4 if nc % 4 == 0 else 1)
    yc_ref[0] = _dot(xc, wk_ref[0]) + _dot(hin_ref[0:rc, :].astype(bf16), wc_ref[0])
    yl_ref[0] = _dot(xl, wk_ref[0]) + _dot(hin_ref[rc:, :].astype(bf16), wc_ref[0])

def _s5_weights(a_re, a_im, log_dt, b_re, b_im, c_re, c_im):
    hp = lax.Precision.HIGHEST
    C = S5_CHUNK
    G, P = a_re.shape[1:]
    N = b_re.shape[-1]
    j = jnp.arange(C + 1, dtype=f32)
    wbs, wks, wcs, lams = [], [], [], []
    for dirn in range(2):
        are, aim = a_re[dirn].astype(f32), a_im[dirn].astype(f32)
        dt = jnp.exp(log_dt[dirn].astype(f32))[:, None]
        pw_mag = jnp.exp(j[:, None, None] * (are * dt)[None])
        pw_re = pw_mag * jnp.cos(j[:, None, None] * (aim * dt)[None])
        pw_im = pw_mag * jnp.sin(j[:, None, None] * (aim * dt)[None])
        nr, ni = pw_re[1] - 1.0, pw_im[1]
        den = are * are + aim * aim
        fr = (nr * are + ni * aim) / den
        fi = (ni * are - nr * aim) / den
        bre, bim = b_re[dirn].astype(f32), b_im[dirn].astype(f32)
        bbr = fr[..., None] * bre - fi[..., None] * bim
        bbi = fr[..., None] * bim + fi[..., None] * bre
        cre, cim = c_re[dirn].astype(f32), c_im[dirn].astype(f32)
        dr, di = (pw_re[:C][::-1], pw_im[:C][::-1]) if dirn == 0 else (pw_re[:C], pw_im[:C])
        sr = dr[..., None] * bbr[None] - di[..., None] * bbi[None]
        si = dr[..., None] * bbi[None] + di[..., None] * bbr[None]
        wb = jnp.concatenate([sr, si], axis=2)
        wbs.append(wb.transpose(1, 0, 3, 2).reshape(G, C * N, 2 * P))
        pr, pi = pw_re[1:], pw_im[1:]
        if dirn == 1:
            pr, pi = pr[::-1], pi[::-1]
        or_ = cre[None] * pr[:, :, None, :] - cim[None] * pi[:, :, None, :]
        oi_ = -(cre[None] * pi[:, :, None, :] + cim[None] * pr[:, :, None, :])
        wc = jnp.concatenate([or_, oi_], axis=3)
        wcs.append(wc.transpose(1, 3, 0, 2).reshape(G, 2 * P, C * N))
        clr = cre[None] * pw_re[:C, :, None, :] - cim[None] * pw_im[:C, :, None, :]
        cli = cre[None] * pw_im[:C, :, None, :] + cim[None] * pw_re[:C, :, None, :]
        kk = (jnp.einsum('jgnp,gpm->gjnm', clr, bbr, precision=hp)
              - jnp.einsum('jgnp,gpm->gjnm', cli, bbi, precision=hp))
        zeros = jnp.zeros_like(kk)
        if dirn == 0:
            rows = [jnp.concatenate([zeros[:, :s], kk[:, :C - s]], axis=1) for s in range(C)]
        else:
            rows = [jnp.concatenate([kk[:, :s + 1][:, ::-1], zeros[:, :C - 1 - s]], axis=1) for s in range(C)]
        kt = jnp.stack(rows, axis=1)
        wks.append(kt.transpose(0, 1, 4, 2, 3).reshape(G, C * N, C * N))
        lams.append((pw_re[C], pw_im[C]))
    wb = jnp.concatenate([wbs[0][..., :P], wbs[1][..., :P], wbs[0][..., P:], wbs[1][..., P:]], axis=2).astype(bf16)
    wc = jnp.concatenate([wcs[0][:, :P], wcs[1][:, :P], wcs[0][:, P:], wcs[1][:, P:]], axis=1).astype(bf16)
    wk = (wks[0] + wks[1]).astype(bf16)
    lam = jnp.stack([jnp.concatenate([lams[0][0], lams[1][0]], axis=-1),
                     jnp.concatenate([lams[0][1], lams[1][1]], axis=-1)], axis=1)
    return wb, wk, wc, lam


def _s5_scan(h_lat, h_ctx, a_re, a_im, log_dt, b_re, b_im, c_re, c_im):
    B, S, D = h_lat.shape
    L = h_ctx.shape[1]
    C, N, P = S5_CHUNK, S5_GROUP, S5_STATE
    G = D // N
    nc_ctx, nc = L // C, (L + S) // C
    assert 2 * P == LANES and L % C == 0 and S % C == 0
    wb, wk, wc, lam = _s5_weights(a_re, a_im, log_dt, b_re, b_im, c_re, c_im)

    def to_groups(h):
        n = h.shape[1] // C
        return h.reshape(B, n, C, G, N).transpose(3, 1, 0, 2, 4).reshape(G, n * B, C * N)

    def from_groups(y):
        n = y.shape[1] // B
        return y.reshape(G, n, B, C, N).transpose(2, 1, 3, 0, 4).reshape(B, n * C, D)

    rc, rl = nc_ctx * B, (nc - nc_ctx) * B

    def rows(r):
        return pl.BlockSpec((1, r, C * N), lambda g: (g, 0, 0))

    yc, yl = pl.pallas_call(
        functools.partial(_s5_kernel, nb=B, nc_ctx=nc_ctx, nc=nc),
        grid=(G,),
        in_specs=[rows(rc), rows(rl),
                  pl.BlockSpec((1, C * N, 4 * P), lambda g: (g, 0, 0)),
                  pl.BlockSpec((1, C * N, C * N), lambda g: (g, 0, 0)),
                  pl.BlockSpec((1, 4 * P, C * N), lambda g: (g, 0, 0)),
                  pl.BlockSpec((1, 2, 2 * P), lambda g: (g, 0, 0))],
        out_specs=[rows(rc), rows(rl)],
        out_shape=[jax.ShapeDtypeStruct((G, rc, C * N), f32), jax.ShapeDtypeStruct((G, rl, C * N), f32)],
        scratch_shapes=[pltpu.VMEM((rc + rl, 4 * P), f32), pltpu.VMEM((rc + rl, 4 * P), f32)],
        compiler_params=_cparams("parallel"),
        name="s5_scan",
    )(to_groups(h_ctx), to_groups(h_lat), wb, wk, wc, lam)
    return from_groups(yl), from_groups(yc)


def _glu_kernel(y_ref, x_ref, mod_ref, gpre_ref, d_ref, g_ref, w_ref, b_ref, o_ref, *, j):
    x = x_ref[0]
    z = jax.nn.gelu(d_ref[...] * _pre_norm(x, gpre_ref[...], mod_ref, j) + y_ref[0])
    u = _dot(z.astype(bf16), w_ref[...]) + b_ref[...]
    o_ref[0] = _post_residual(x, z * jax.nn.sigmoid(u), g_ref[...], mod_ref, j, 1.0)


def _glu_residual(y, x, mod, g_pre, d_skip, g, w, bias, j):
    b, t, d = x.shape
    tm = _tok_tile(t)
    tok = pl.BlockSpec((1, tm, d), lambda bb, tt: (bb, tt, 0))
    vec = _full((1, d))
    return pl.pallas_call(
        functools.partial(_glu_kernel, j=j),
        grid=(b, t // tm),
        in_specs=[tok, tok, _mod_spec(mod), vec, vec, vec, _full((d, d)), vec],
        out_specs=tok,
        out_shape=jax.ShapeDtypeStruct((b, t, d), f32),
        compiler_params=_cparams("parallel", "parallel"),
        name="glu_residual",
    )(y, x, mod, g_pre.reshape(1, d), d_skip.astype(f32).reshape(1, d), g.reshape(1, d), w, bias.reshape(1, d))


def _diff_maps(q):
    lane = lax.broadcasted_iota(jnp.int32, q.shape, 1)
    qs = (q.astype(f32) * DIFF_HEAD_DIM ** -0.5).astype(bf16)
    zero = jnp.zeros_like(qs)
    return jnp.where(lane < DIFF_HEAD_DIM, qs, zero), jnp.where(lane >= DIFF_HEAD_DIM, qs, zero)


def _key_chunks(k_refs):
    out = []
    for seg, k_ref in enumerate(k_refs):
        n = k_ref.shape[1]
        out += [(seg, c, min(c + DIFF_KEY_CHUNK, n)) for c in range(0, n, DIFF_KEY_CHUNK)]
    return out


def _diff_step(q_next, cur, nxt, k_refs, v_refs, lamv_ref, g_ref, o_ref, lam_init):
    chunks = _key_chunks(k_refs)
    cur_s, cur_m = cur
    tq = cur_m.shape[1]
    m1, m2 = cur_m[0, :, 0:1], cur_m[1, :, 0:1]
    if q_next is not None:
        q1n, q2n = _diff_maps(q_next)
        nxt_s, nxt_m = nxt
    n1 = n2 = l1 = l2 = None
    for seg, a, b in chunks:
        if q_next is not None:
            s1 = _dot_nt(q1n, k_refs[seg][0, a:b, :])
            s2 = _dot_nt(q2n, k_refs[seg][0, a:b, :])
            nxt_s[seg][0, :, a:b] = s1
            nxt_s[seg][1, :, a:b] = s2
            c1 = jnp.max(s1, axis=-1, keepdims=True)
            c2 = jnp.max(s2, axis=-1, keepdims=True)
            n1 = c1 if n1 is None else jnp.maximum(n1, c1)
            n2 = c2 if n2 is None else jnp.maximum(n2, c2)
        p1 = jnp.exp(cur_s[seg][0, :, a:b] - m1)
        p2 = jnp.exp(cur_s[seg][1, :, a:b] - m2)
        cur_s[seg][0, :, a:b] = p1
        cur_s[seg][1, :, a:b] = p2
        r1 = jnp.sum(p1, axis=-1, keepdims=True)
        r2 = jnp.sum(p2, axis=-1, keepdims=True)
        l1 = r1 if l1 is None else l1 + r1
        l2 = r2 if l2 is None else l2 + r2
    if q_next is not None:
        nxt_m[0] = jnp.broadcast_to(n1, (tq, LANES))
        nxt_m[1] = jnp.broadcast_to(n2, (tq, LANES))
    lv = lamv_ref[...]
    lam = (jnp.exp(jnp.sum(lv[0:1] * lv[1:2], axis=-1, keepdims=True))
           - jnp.exp(jnp.sum(lv[2:3] * lv[3:4], axis=-1, keepdims=True)) + lam_init)
    c = lam * l1 / l2
    o = None
    for seg, a, b in chunks:
        w = (cur_s[seg][0, :, a:b] - c * cur_s[seg][1, :, a:b]).astype(bf16)
        part = _dot(w, v_refs[seg][0, a:b, :])
        o = part if o is None else o + part
    o = o / l1
    o_ref[0] = (_rms(o, g_ref[...]) * (1 - lam_init)).astype(o_ref.dtype)


def _diff_kernel(*refs, n_seg, nq, lam_init):
    q_ref, qn_ref = refs[:2]
    k_refs = refs[2:2 + 2 * n_seg:2]
    v_refs = refs[3:3 + 2 * n_seg:2]
    lamv_ref, g_ref, o_ref = refs[2 + 2 * n_seg:5 + 2 * n_seg]
    scratch = refs[5 + 2 * n_seg:]
    slots = tuple((scratch[p * (n_seg + 1):p * (n_seg + 1) + n_seg], scratch[p * (n_seg + 1) + n_seg])
                  for p in range(2))
    t = pl.program_id(2)

    @pl.when(t == 0)
    def _():
        q1, q2 = _diff_maps(q_ref[0])
        s_refs, m_ref = slots[0]
        for mp, qm in enumerate((q1, q2)):
            parts = [_dot_nt(qm, k_ref[0]) for k_ref in k_refs]
            for s_ref, sc in zip(s_refs, parts):
                s_ref[mp] = sc
            mx = functools.reduce(jnp.maximum, [jnp.max(sc, axis=-1, keepdims=True) for sc in parts])
            m_ref[mp] = jnp.broadcast_to(mx, m_ref.shape[1:])

    if nq == 1:
        _diff_step(None, slots[0], None, k_refs, v_refs, lamv_ref, g_ref, o_ref, lam_init)
        return
    for parity in range(2):
        @pl.when((t & 1) == parity)
        def _(parity=parity):
            _diff_step(qn_ref[0], slots[parity], slots[1 - parity], k_refs, v_refs, lamv_ref, g_ref, o_ref,
                       lam_init)


def _diff_attention(qkv_q, qkv_ctx, qkv_lat, lamv, subln, lam_init):
    B, T, n3 = qkv_q.shape
    D = n3 // 3
    H = D // LANES
    tq = min(_tok_tile(T), 256)
    nq = T // tq
    kv = [qkv_ctx] if qkv_lat is None else [qkv_ctx, qkv_lat]
    args = [qkv_q, qkv_q]
    specs = [pl.BlockSpec((1, tq, LANES), lambda b, h, t: (b, t, h)),
             pl.BlockSpec((1, tq, LANES), lambda b, h, t: (b, jnp.minimum(t + 1, nq - 1), h))]
    for a in kv:
        args += [a, a]
        specs += [pl.BlockSpec((1, a.shape[1], LANES), lambda b, h, t: (b, 0, H + h)),
                  pl.BlockSpec((1, a.shape[1], LANES), lambda b, h, t: (b, 0, 2 * H + h))]
    args += [lamv, subln.reshape(1, LANES)]
    specs += [_full(lamv.shape), _full((1, LANES))]
    slot = [pltpu.VMEM((2, tq, a.shape[1]), f32) for a in kv] + [pltpu.VMEM((2, tq, LANES), f32)]
    return pl.pallas_call(
        functools.partial(_diff_kernel, n_seg=len(kv), nq=nq, lam_init=lam_init),
        grid=(B, H, nq),
        in_specs=specs,
        out_specs=pl.BlockSpec((1, tq, LANES), lambda b, h, t: (b, t, h)),
        out_shape=jax.ShapeDtypeStruct((B, T, D), bf16),
        scratch_shapes=slot + slot,
        compiler_params=_cparams("parallel", "parallel", "arbitrary"),
        name="diff_attention",
    )(*args)


def _rope_tables(n_tokens, head_dim):
    n_freq = head_dim // 4
    inv_freq = ROPE_BASE ** (-jnp.arange(n_freq, dtype=f32) / n_freq)
    t = jnp.arange(n_tokens)
    row = (t // GRID_W).astype(f32)
    col = (t % GRID_W).astype(f32)
    ang = jnp.concatenate([row[:, None] * inv_freq, col[:, None] * inv_freq], axis=-1)
    cos = jnp.repeat(jnp.cos(ang), 2, axis=-1)
    sin = jnp.repeat(jnp.sin(ang), 2, axis=-1) * jnp.tile(jnp.array([-1.0, 1.0], f32), head_dim // 2)
    reps = LANES // head_dim
    return jnp.tile(cos, (1, reps)), jnp.tile(sin, (1, reps))


def _na_kernel(*refs, lat, s_tok):
    if lat:
        q_ref, kc_ref, vc_ref, kl_ref, vl_ref, tb_ref, o_ref, bias_ref = refs
    else:
        q_ref, kc_ref, vc_ref, o_ref = refs
    q = q_ref[0]
    lane = lax.broadcasted_iota(jnp.int32, q.shape, 1)
    scale = NA_HEAD_DIM ** -0.5
    qs = (q.astype(f32) * scale).astype(bf16)
    zero = jnp.zeros_like(qs)
    kc, vc = kc_ref[0], vc_ref[0]
    if lat:
        t = pl.program_id(2)
        rows = s_tok // GRID_W
        w0 = jnp.clip(t * NA_TILE_ROWS - (NA_WIN_ROWS - NA_TILE_ROWS) // 2, 0, rows - NA_WIN_ROWS)
        tok0 = pl.multiple_of(w0 * GRID_W, 4 * GRID_W)
        kw = kl_ref[0, pl.ds(tok0, NA_WIN_ROWS * GRID_W), :]
        vw = vl_ref[0, pl.ds(tok0, NA_WIN_ROWS * GRID_W), :]
        for i in range(NA_TILE_ROWS):
            qr = t * NA_TILE_ROWS + i
            r0 = jnp.clip(qr - WIN_H // 2, 0, rows - WIN_H)
            for j in range(NA_WIN_ROWS // 2):
                es = []
                for kr in (w0 + 2 * j, w0 + 2 * j + 1):
                    ok = (kr >= r0) & (kr < r0 + WIN_H)
                    es.append(jnp.where(ok, kr - qr + WIN_H, 0))
                for hh in range(2):
                    bias_ref[hh, i * GRID_W:(i + 1) * GRID_W, j * LANES:(j + 1) * LANES] = (
                        tb_ref[0, hh, 0, es[0]] + tb_ref[0, hh, 1, es[1]])
    outs = []
    for hh in range(2):
        sel = (lane < NA_HEAD_DIM) if hh == 0 else (lane >= NA_HEAD_DIM)
        qh = jnp.where(sel, qs, zero)
        s_c = _dot_nt(qh, kc)
        m = jnp.max(s_c, axis=-1, keepdims=True)
        if lat:
            s_l = _dot_nt(qh, kw) + bias_ref[hh]
            m = jnp.maximum(m, jnp.max(s_l, axis=-1, keepdims=True))
        p_c = jnp.exp(s_c - m)
        l = jnp.sum(p_c, axis=-1, keepdims=True)
        o = _dot(p_c.astype(bf16), vc)
        if lat:
            p_l = jnp.exp(s_l - m)
            l = l + jnp.sum(p_l, axis=-1, keepdims=True)
            o = o + _dot(p_l.astype(bf16), vw)
        outs.append(o / l)
    o_ref[0] = jnp.where(lane < NA_HEAD_DIM, outs[0], outs[1]).astype(o_ref.dtype)


def _na_bias_blocks(rpb):
    H, n_dr, n_dc = rpb.shape
    col = jnp.arange(GRID_W)
    c0 = jnp.clip(col - WIN_W // 2, 0, GRID_W - WIN_W)
    col_ok = (col[None, :] >= c0[:, None]) & (col[None, :] < c0[:, None] + WIN_W)
    dc = jnp.clip(col[None, :] - col[:, None] + WIN_W - 1, 0, n_dc - 1)
    onehot = (dc.reshape(1, -1) == jnp.arange(n_dc)[:, None]).astype(f32)
    blk = jnp.einsum('hrj,jx->hrx', rpb.astype(f32), onehot, precision=lax.Precision.HIGHEST)
    blk = jnp.where(col_ok[None, None], blk.reshape(H, n_dr, GRID_W, GRID_W), NEG_INF)
    blk = jnp.concatenate([jnp.full((H, 1, GRID_W, GRID_W), NEG_INF, f32), blk], axis=1)
    z = jnp.zeros_like(blk)
    tab = jnp.stack([jnp.concatenate([blk, z], axis=-1), jnp.concatenate([z, blk], axis=-1)], axis=1)
    return tab.reshape((H // 2, 2) + tab.shape[1:])


def _na_attention(qkv_q, qkv_ctx, qkv_lat, bias):
    B, T, n3 = qkv_q.shape
    D = n3 // 3
    HP = D // LANES
    L = qkv_ctx.shape[1]
    lat = qkv_lat is not None
    args = [qkv_q, qkv_ctx, qkv_ctx]
    if lat:
        tq = NA_TILE_ROWS * GRID_W
        n_tiles = T // tq
        assert T % tq == 0 and n_tiles >= 2
    else:
        tq = min(_tok_tile(T), 256)
        n_tiles = T // tq
    specs = [pl.BlockSpec((1, tq, LANES), lambda b, h, t: (b, t, h)),
             pl.BlockSpec((1, L, LANES), lambda b, h, t: (b, 0, HP + h)),
             pl.BlockSpec((1, L, LANES), lambda b, h, t: (b, 0, 2 * HP + h))]
    scratch = []
    if lat:
        assert T // GRID_W >= NA_WIN_ROWS and bias.shape[3] == 2 * WIN_H
        args += [qkv_lat, qkv_lat, bias]
        specs += [pl.BlockSpec((1, T, LANES), lambda b, h, t: (b, 0, HP + h)),
                  pl.BlockSpec((1, T, LANES), lambda b, h, t: (b, 0, 2 * HP + h)),
                  pl.BlockSpec((1,) + bias.shape[1:], lambda b, h, t: (h, 0, 0, 0, 0, 0))]
        scratch = [pltpu.VMEM((2, tq, NA_WIN_ROWS * GRID_W), f32)]
    return pl.pallas_call(
        functools.partial(_na_kernel, lat=lat, s_tok=T),
        grid=(B, HP, n_tiles),
        in_specs=specs,
        out_specs=pl.BlockSpec((1, tq, LANES), lambda b, h, t: (b, t, h)),
        out_shape=jax.ShapeDtypeStruct((B, T, D), bf16),
        scratch_shapes=scratch,
        compiler_params=_cparams("parallel", "parallel", "arbitrary"),
        name="na_attention",
    )(*args)


def _chunk_cumsum(x, reverse):
    n = x.shape[0]
    pos = lax.broadcasted_iota(jnp.int32, x.shape, 0) & (HG_CHUNK - 1)
    k = 1
    while k < HG_CHUNK:
        if reverse:
            x = x + jnp.where(pos < HG_CHUNK - k, pltpu.roll(x, n - k, 0), 0.0)
        else:
            x = x + jnp.where(pos >= k, pltpu.roll(x, k, 0), 0.0)
        k *= 2
    return x


def _hg_block(z, v, q, lb, st, reverse, with_out):
    n = z.shape[0]
    nch = n // HG_CHUNK
    f = lb + (1 - lb) * jax.nn.sigmoid(z)
    lf = jnp.log(f)
    kk = 1 - f
    g = _chunk_cumsum(lf, reverse)
    out = None
    if with_out:
        ri = lax.broadcasted_iota(jnp.int32, (n, n), 0)
        ci = lax.broadcasted_iota(jnp.int32, (n, n), 1)
        shift = HG_CHUNK.bit_length() - 1
        tri = ((ri >> shift) == (ci >> shift)) & ((ci >= ri) if reverse else (ci <= ri))
        q_dec = q * jnp.exp(g)
        k_inv = kk * jnp.exp(-g)
        att = jnp.where(tri, _dot_nt(q_dec.astype(bf16), k_inv.astype(bf16)), 0.0)
        out_intra = _dot(att.astype(bf16), v.astype(bf16))
        outs = [None] * nch
    order = range(nch - 1, -1, -1) if reverse else range(nch)
    for c in order:
        sl = slice(c * HG_CHUNK, (c + 1) * HG_CHUNK)
        end = c * HG_CHUNK if reverse else (c + 1) * HG_CHUNK - 1
        g_tot = g[end:end + 1]
        if with_out:
            outs[c] = _dot_nt(q_dec[sl].astype(bf16), st.astype(bf16))
        kdec = kk[sl] * jnp.exp(g_tot - g[sl])
        ds_t = _dot(v[sl].T.astype(bf16), kdec.astype(bf16))
        st = st * jnp.exp(g_tot) + ds_t
    if with_out:
        out = out_intra + jnp.concatenate(outs, axis=0)
    return out, st


def _hg_kernel(q_ref, i_ref, gate_ref, zf_ref, zb_ref, ic_ref, zfc_ref, zbc_ref, lb_ref, gn_ref, o_ref,
               accf_ref, accb_ref, *, s_tok):
    lb = lb_ref[...]
    nblk = s_tok // HG_BLOCK
    zero = jnp.zeros((HG_EXPAND, HG_EXPAND), f32)
    _, st_f = _hg_block(zfc_ref[0], ic_ref[0], None, lb, zero, False, False)
    _, st_b = _hg_block(zbc_ref[0], ic_ref[0], None, lb, zero, True, False)

    def body(n, carry):
        st_f, st_b = carry
        rows_f = pl.ds(pl.multiple_of(n * HG_BLOCK, HG_BLOCK), HG_BLOCK)
        rows_b = pl.ds(pl.multiple_of((nblk - 1 - n) * HG_BLOCK, HG_BLOCK), HG_BLOCK)
        out_f, st_f = _hg_block(zf_ref[0, rows_f, :], i_ref[0, rows_f, :], q_ref[0, rows_f, :], lb, st_f, False, True)
        out_b, st_b = _hg_block(zb_ref[0, rows_b, :], i_ref[0, rows_b, :], q_ref[0, rows_b, :], lb, st_b, True, True)
        accf_ref[rows_f, :] = out_f
        accb_ref[rows_b, :] = out_b
        return st_f, st_b

    lax.fori_loop(0, nblk, body, (st_f, st_b))
    gate = gate_ref[0]
    o = accf_ref[...] + accb_ref[...]
    o_ref[0] = (_rms(o, gn_ref[...]) * (gate * jax.nn.sigmoid(gate))).astype(o_ref.dtype)


def _hgrn2_core(p_lat, p_ctx, lb, gn_g):
    B, S, n5 = p_lat.shape
    D = n5 // 5
    H = D // HG_EXPAND
    L = p_ctx.shape[1]
    assert S % HG_BLOCK == 0 and L % HG_CHUNK == 0

    def lat(k):
        return pl.BlockSpec((1, S, HG_EXPAND), lambda b, h: (b, 0, k * H + h))

    def ctx(k):
        return pl.BlockSpec((1, L, HG_EXPAND), lambda b, h: (b, 0, k * H + h))

    vec = pl.BlockSpec((1, HG_EXPAND), lambda b, h: (0, h))
    return pl.pallas_call(
        functools.partial(_hg_kernel, s_tok=S),
        grid=(B, H),
        in_specs=[lat(0), lat(1), lat(2), lat(3), lat(4), ctx(1), ctx(3), ctx(4), vec, _full((1, HG_EXPAND))],
        out_specs=pl.BlockSpec((1, S, HG_EXPAND), lambda b, h: (b, 0, h)),
        out_shape=jax.ShapeDtypeStruct((B, S, D), bf16),
        scratch_shapes=[pltpu.VMEM((S, HG_EXPAND), f32), pltpu.VMEM((S, HG_EXPAND), f32)],
        compiler_params=_cparams("parallel", "parallel"),
        name="hgrn2_core",
    )(p_lat, p_lat, p_lat, p_lat, p_lat, p_ctx, p_ctx, p_ctx, lb.reshape(1, D), gn_g.reshape(1, HG_EXPAND))


def kernel(x, c, ctx, c_ctx, w_ada, b_ada, g_pre, g_post, w_ff1, w_ff3, w_ff2, s5_a_re, s5_a_im, s5_log_dt, s5_b_re, s5_b_im, s5_c_re, s5_c_im, s5_d, s5_w_glu, s5_b_glu, da_w_qkv, da_w_o, da_lam_q1, da_lam_k1, da_lam_q2, da_lam_k2, da_subln, na_w_qkv, na_w_o, na_rpb, hg_w_qig, hg_w_f, hg_b_f, hg_lb_logits, hg_gnorm, hg_w_o):
    B, S, D = x.shape
    depth = w_ada.shape[0]
    n_mix = 4
    rows_pad = -(-(B + 1) // 8) * 8
    c_all = jnp.concatenate([c, c_ctx[None], jnp.zeros((rows_pad - B - 1, D), f32)], axis=0)
    mods = _ada(c_all, w_ada, b_ada)
    lb_p = jax.nn.softmax(hg_lb_logits.astype(f32), axis=0)
    lower_bounds = jnp.cumsum(lb_p, axis=0) - lb_p[0]
    w1b, w3b, w2b = w_ff1.astype(bf16), w_ff3.astype(bf16), w_ff2.astype(bf16)

    x_lat, x_ctx = x, ctx
    for i in range(depth):
        last = i == depth - 1
        occ, kind = i // n_mix, i % n_mix
        m_lat = mods[i, :B].reshape(B, 3 * N_SUB, D)
        m_ctx = mods[i, B:B + 1].reshape(1, 3 * N_SUB, D)
        streams = [(x_lat, m_lat), (x_ctx, m_ctx)]

        x_lat, x_ctx = [_half_ffn(xs, ms, g_pre[i, 0], g_post[i, 0], w1b[i, 0], w3b[i, 0], w2b[i, 0], 0)
                        for xs, ms in streams]

        if kind == 0:
            h_lat = _prenorm(x_lat, m_lat, g_pre[i, 1], 1)
            h_ctx = _prenorm(x_ctx, m_ctx, g_pre[i, 1], 1)
            y_lat, y_ctx = _s5_scan(h_lat, h_ctx, s5_a_re[occ], s5_a_im[occ], s5_log_dt[occ], s5_b_re[occ],
                                    s5_b_im[occ], s5_c_re[occ], s5_c_im[occ])
            wg = s5_w_glu[occ].astype(bf16)
            glu = functools.partial(_glu_residual, g_pre=g_pre[i, 1], d_skip=s5_d[occ], g=g_post[i, 1], w=wg,
                                    bias=s5_b_glu[occ], j=1)
            x_lat_new = glu(y_lat, x_lat, m_lat)
            x_ctx_new = None if last else glu(y_ctx, x_ctx, m_ctx)
        elif kind == 1:
            wq = da_w_qkv[occ].astype(bf16)
            rope = _rope_tables(S, DIFF_HEAD_DIM)
            qkv_lat = _prenorm_proj(x_lat, m_lat, g_pre[i, 1], 1, wq, rope=rope, n_rope=2 * D)
            qkv_ctx = _prenorm_proj(x_ctx, m_ctx, g_pre[i, 1], 1, wq)
            lamv = jnp.stack([da_lam_q1[occ], da_lam_k1[occ], da_lam_q2[occ], da_lam_k2[occ]]).astype(f32)
            lam_init = 0.8 - 0.6 * math.exp(-0.3 * i)
            wo = da_w_o[occ].astype(bf16)
            o_lat = _diff_attention(qkv_lat, qkv_ctx, qkv_lat, lamv, da_subln[occ], lam_init)
            x_lat_new = _outproj_residual(o_lat, x_lat, m_lat, g_post[i, 1], wo, 1)
            x_ctx_new = None
            if not last:
                o_ctx = _diff_attention(qkv_ctx, qkv_ctx, None, lamv, da_subln[occ], lam_init)
                x_ctx_new = _outproj_residual(o_ctx, x_ctx, m_ctx, g_post[i, 1], wo, 1)
        elif kind == 2:
            wq = na_w_qkv[occ].astype(bf16)
            qkv_lat = _prenorm_proj(x_lat, m_lat, g_pre[i, 1], 1, wq)
            qkv_ctx = _prenorm_proj(x_ctx, m_ctx, g_pre[i, 1], 1, wq)
            wo = na_w_o[occ].astype(bf16)
            o_lat = _na_attention(qkv_lat, qkv_ctx, qkv_lat, _na_bias_blocks(na_rpb[occ]))
            x_lat_new = _outproj_residual(o_lat, x_lat, m_lat, g_post[i, 1], wo, 1)
            x_ctx_new = None
            if not last:
                o_ctx = _na_attention(qkv_ctx, qkv_ctx, None, None)
                x_ctx_new = _outproj_residual(o_ctx, x_ctx, m_ctx, g_post[i, 1], wo, 1)
        else:
            assert last, "HGRN2 context outputs are not needed when it is the last layer"
            wp = jnp.concatenate([hg_w_qig[occ], hg_w_f[occ, 0], hg_w_f[occ, 1]], axis=1).astype(bf16)
            bp = jnp.concatenate([jnp.zeros((3 * D,), f32), hg_b_f[occ, 0], hg_b_f[occ, 1]])
            p_lat = _prenorm_proj(x_lat, m_lat, g_pre[i, 1], 1, wp, bias=bp, out_dtype=f32)
            p_ctx = _prenorm_proj(x_ctx, m_ctx, g_pre[i, 1], 1, wp, bias=bp, out_dtype=f32)
            o_lat = _hgrn2_core(p_lat, p_ctx, lower_bounds[i], hg_gnorm[occ])
            x_lat_new = _outproj_residual(o_lat, x_lat, m_lat, g_post[i, 1], hg_w_o[occ].astype(bf16), 1)
            x_ctx_new = None

        x_lat = _half_ffn(x_lat_new, m_lat, g_pre[i, 2], g_post[i, 2], w1b[i, 1], w3b[i, 1], w2b[i, 1], 2)
        if not last:
            x_ctx = _half_ffn(x_ctx_new, m_ctx, g_pre[i, 2], g_post[i, 2], w1b[i, 1], w3b[i, 1], w2b[i, 1], 2)
    return x_lat
```

```python
import functools
import math

import jax
import jax.numpy as jnp
from jax import lax
from jax.experimental import pallas as pl
from jax.experimental.pallas import tpu as pltpu

f32 = jnp.float32
bf16 = jnp.bfloat16

N_SUB = 3
RMS_EPS = 1e-6
NEG_INF = -1e30
ROPE_BASE = 10000.0
GRID_W = 64
S5_GROUP = 16
S5_STATE = 64
S5_CHUNK = 16
DIFF_HEAD_DIM = 64
DIFF_KEY_CHUNK = 1024
NA_HEAD_DIM = 64
WIN_H = 8
WIN_W = 16
NA_TILE_ROWS = 8
NA_WIN_ROWS = 16
HG_EXPAND = 128
HG_CHUNK = 64
HG_BLOCK = 512
LANES = 128
VMEM_LIMIT = 56 * 1024 * 1024


def _cparams(*sem):
    return pltpu.CompilerParams(dimension_semantics=sem, vmem_limit_bytes=VMEM_LIMIT)


def _dot(a, b):
    return jnp.dot(a, b, preferred_element_type=f32)


def _dot_nt(a, b):
    return lax.dot_general(a, b, (((1,), (1,)), ((), ())), preferred_element_type=f32)


def _rms(x, g):
    return x * lax.rsqrt(jnp.mean(jnp.square(x), axis=-1, keepdims=True) + RMS_EPS) * g


def _pre_norm(x, g, mod_ref, j):
    shift = mod_ref[0, 3 * j:3 * j + 1, :]
    scale = mod_ref[0, 3 * j + 1:3 * j + 2, :]
    return _rms(x, g) * (1 + scale) + shift


def _post_residual(x, y, g, mod_ref, j, weight):
    gate = mod_ref[0, 3 * j + 2:3 * j + 3, :]
    return x + weight * gate * _rms(y, g)


def _tok_tile(t):
    for tm in (512, 256, 128):
        if t % tm == 0:
            return tm
    raise ValueError(f"token count {t} is not a multiple of 128")


def _full(shape):
    return pl.BlockSpec(shape, lambda *_: (0,) * len(shape))


def _mod_spec(mod):
    nd = mod.shape[1]
    d = mod.shape[2]
    if mod.shape[0] == 1:
        return pl.BlockSpec((1, nd, d), lambda b, t: (0, 0, 0))
    return pl.BlockSpec((1, nd, d), lambda b, t: (b, 0, 0))


def _ada_kernel(c_ref, w_ref, b_ref, o_ref):
    c = c_ref[...]
    sc = (c * jax.nn.sigmoid(c)).astype(bf16)
    o_ref[0] = _dot(sc, w_ref[0].astype(bf16)) + b_ref[0]


def _ada(c_all, w_ada, b_ada):
    depth, d, n = w_ada.shape
    rows = c_all.shape[0]
    tn = n // 4
    return pl.pallas_call(
        _ada_kernel,
        grid=(depth, n // tn),
        in_specs=[pl.BlockSpec((rows, d), lambda i, t: (0, 0)),
                  pl.BlockSpec((1, d, tn), lambda i, t: (i, 0, t)),
                  pl.BlockSpec((1, 1, tn), lambda i, t: (i, 0, t))],
        out_specs=pl.BlockSpec((1, rows, tn), lambda i, t: (i, 0, t)),
        out_shape=jax.ShapeDtypeStruct((depth, rows, n), f32),
        compiler_params=_cparams("arbitrary", "arbitrary"),
        name="ada",
    )(c_all, w_ada, b_ada.reshape(depth, 1, n))


def _ffn_kernel(x_ref, mod_ref, gin_ref, gout_ref, w1_ref, w3_ref, w2_ref, o_ref, *, j, chunks):
    x = x_ref[0]
    h = _pre_norm(x, gin_ref[...], mod_ref, j).astype(bf16)
    acc = None
    for s, n in chunks:
        a = _dot(h, w1_ref[:, s:s + n])
        b = _dot(h, w3_ref[:, s:s + n])
        g = (a * jax.nn.sigmoid(a) * b).astype(bf16)
        y = _dot(g, w2_ref[s:s + n, :])
        acc = y if acc is None else acc + y
    o_ref[0] = _post_residual(x, acc, gout_ref[...], mod_ref, j, 0.5)


def _ff_chunks(dff, step=1024):
    out, s = [], 0
    while s < dff:
        n = min(step, dff - s)
        out.append((s, n))
        s += n
    return tuple(out)


def _half_ffn(x, mod, g_in, g_out, w1, w3, w2, j):
    b, t, d = x.shape
    dff = w1.shape[1]
    tm = _tok_tile(t)
    return pl.pallas_call(
        functools.partial(_ffn_kernel, j=j, chunks=_ff_chunks(dff)),
        grid=(b, t // tm),
        in_specs=[pl.BlockSpec((1, tm, d), lambda bb, tt: (bb, tt, 0)),
                  _mod_spec(mod), _full((1, d)), _full((1, d)),
                  _full((d, dff)), _full((d, dff)), _full((dff, d))],
        out_specs=pl.BlockSpec((1, tm, d), lambda bb, tt: (bb, tt, 0)),
        out_shape=jax.ShapeDtypeStruct((b, t, d), f32),
        compiler_params=_cparams("parallel", "parallel"),
        name="half_ffn",
    )(x, mod, g_in.reshape(1, d), g_out.reshape(1, d), w1, w3, w2)


def _prenorm_kernel(x_ref, mod_ref, g_ref, o_ref, *, j):
    o_ref[0] = _pre_norm(x_ref[0], g_ref[...], mod_ref, j).astype(o_ref.dtype)


def _prenorm(x, mod, g, j):
    b, t, d = x.shape
    tm = _tok_tile(t)
    return pl.pallas_call(
        functools.partial(_prenorm_kernel, j=j),
        grid=(b, t // tm),
        in_specs=[pl.BlockSpec((1, tm, d), lambda bb, tt: (bb, tt, 0)), _mod_spec(mod), _full((1, d))],
        out_specs=pl.BlockSpec((1, tm, d), lambda bb, tt: (bb, tt, 0)),
        out_shape=jax.ShapeDtypeStruct((b, t, d), bf16),
        compiler_params=_cparams("parallel", "parallel"),
        name="prenorm",
    )(x, mod, g.reshape(1, d))


def _swap_pairs(x):
    lane = lax.broadcasted_iota(jnp.int32, x.shape, 1)
    nxt = pltpu.roll(x, LANES - 1, 1)
    prv = pltpu.roll(x, 1, 1)
    return jnp.where((lane & 1) == 0, nxt, prv)


def _proj_kernel(*refs, j, n_out, n_rope, has_bias, step):
    x_ref, mod_ref, g_ref, w_ref = refs[:4]
    k = 4
    b_ref = None
    if has_bias:
        b_ref = refs[k]
        k += 1
    if n_rope:
        cos_ref, sin_ref = refs[k], refs[k + 1]
        k += 2
    o_ref = refs[k]
    h = _pre_norm(x_ref[0], g_ref[...], mod_ref, j).astype(bf16)
    for s in range(0, n_out, step):
        y = _dot(h, w_ref[:, s:s + step])
        if has_bias:
            y = y + b_ref[:, s:s + step]
        if s < n_rope:
            cos = cos_ref[...]
            sin = sin_ref[...]
            parts = []
            for c in range(0, step, LANES):
                yc = y[:, c:c + LANES]
                parts.append(yc * cos + _swap_pairs(yc) * sin)
            y = jnp.concatenate(parts, axis=1)
        o_ref[0, :, s:s + step] = y.astype(o_ref.dtype)


def _prenorm_proj(x, mod, g, j, w, bias=None, rope=None, n_rope=0, out_dtype=bf16):
    b, t, d = x.shape
    n = w.shape[1]
    tm = min(_tok_tile(t), 256)
    step = 512
    assert n % step == 0 and n_rope % step == 0
    args = [x, mod, g.reshape(1, d), w]
    specs = [pl.BlockSpec((1, tm, d), lambda bb, tt: (bb, tt, 0)), _mod_spec(mod), _full((1, d)), _full((d, n))]
    if bias is not None:
        args.append(bias.reshape(1, n))
        specs.append(_full((1, n)))
    if n_rope:
        args += list(rope)
        specs += [pl.BlockSpec((tm, LANES), lambda bb, tt: (tt, 0))] * 2
    return pl.pallas_call(
        functools.partial(_proj_kernel, j=j, n_out=n, n_rope=n_rope, has_bias=bias is not None, step=step),
        grid=(b, t // tm),
        in_specs=specs,
        out_specs=pl.BlockSpec((1, tm, n), lambda bb, tt: (bb, tt, 0)),
        out_shape=jax.ShapeDtypeStruct((b, t, n), out_dtype),
        compiler_params=_cparams("parallel", "parallel"),
        name="prenorm_proj",
    )(*args)


def _outproj_kernel(y_ref, x_ref, mod_ref, g_ref, w_ref, o_ref, *, j):
    y = _dot(y_ref[0], w_ref[...])
    o_ref[0] = _post_residual(x_ref[0], y, g_ref[...], mod_ref, j, 1.0)


def _outproj_residual(y, x, mod, g, w, j):
    b, t, d = x.shape
    tm = _tok_tile(t)
    tok = pl.BlockSpec((1, tm, d), lambda bb, tt: (bb, tt, 0))
    return pl.pallas_call(
        functools.partial(_outproj_kernel, j=j),
        grid=(b, t // tm),
        in_specs=[tok, tok, _mod_spec(mod), _full((1, d)), _full((d, d))],
        out_specs=tok,
        out_shape=jax.ShapeDtypeStruct((b, t, d), f32),
        compiler_params=_cparams("parallel", "parallel"),
        name="outproj_residual",
    )(y, x, mod, g.reshape(1, d), w)


def _s5_kernel(xc_ref, xl_ref, wb_ref, wk_ref, wc_ref, lam_ref, yc_ref, yl_ref, s_ref, hin_ref, *, nb, nc_ctx, nc):
    rc = nc_ctx * nb
    xc, xl = xc_ref[0], xl_ref[0]
    s_ref[0:rc, :] = _dot(xc, wb_ref[0])
    s_ref[rc:, :] = _dot(xl, wb_ref[0])
    P = S5_STATE
    lam = lam_ref[0]
    a_re, a_im = lam[0:1], lam[1:2]
    fwd_lane = lax.broadcasted_iota(jnp.int32, (nb, 2 * P), 1) < P

    def step(n, carry):
        h_re, h_im = carry
        r_f = pl.multiple_of(n * nb, nb)
        n_b = jnp.where(n < nc_ctx, nc_ctx - 1 - n, nc - 1 - (n - nc_ctx))
        r_b = pl.multiple_of(n_b * nb, nb)
        rows_f, rows_b = pl.ds(r_f, nb), pl.ds(r_b, nb)
        hin_ref[rows_f, 0:P] = h_re[:, 0:P]
        hin_ref[rows_b, P:2 * P] = h_re[:, P:2 * P]
        hin_ref[rows_f, 2 * P:3 * P] = h_im[:, 0:P]
        hin_ref[rows_b, 3 * P:4 * P] = h_im[:, P:2 * P]
        s_re = jnp.where(fwd_lane, s_ref[rows_f, 0:2 * P], s_ref[rows_b, 0:2 * P])
        s_im = jnp.where(fwd_lane, s_ref[rows_f, 2 * P:4 * P], s_ref[rows_b, 2 * P:4 * P])
        return a_re * h_re - a_im * h_im + s_re, a_re * h_im + a_im * h_re + s_im

    zero = jnp.zeros((nb, 2 * P), f32)
    lax.fori_loop(0, nc, step, (zero, zero), unroll=4 if nc % 4 == 0 else 1)
    yc_ref[0] = _dot(xc, wk_ref[0]) + _dot(hin_ref[0:rc, :].astype(bf16), wc_ref[0])
    yl_ref[0] = _dot(xl, wk_ref[0]) + _dot(hin_ref[rc:, :].astype(bf16), wc_ref[0])

def _s5_weights(a_re, a_im, log_dt, b_re, b_im, c_re, c_im):
    hp = lax.Precision.HIGHEST
    C = S5_CHUNK
    G, P = a_re.shape[1:]
    N = b_re.shape[-1]
    j = jnp.arange(C + 1, dtype=f32)
    wbs, wks, wcs, lams = [], [], [], []
    for dirn in range(2):
        are, aim = a_re[dirn].astype(f32), a_im[dirn].astype(f32)
        dt = jnp.exp(log_dt[dirn].astype(f32))[:, None]
        pw_mag = jnp.exp(j[:, None, None] * (are * dt)[None])
        pw_re = pw_mag * jnp.cos(j[:, None, None] * (aim * dt)[None])
        pw_im = pw_mag * jnp.sin(j[:, None, None] * (aim * dt)[None])
        nr, ni = pw_re[1] - 1.0, pw_im[1]
        den = are * are + aim * aim
        fr = (nr * are + ni * aim) / den
        fi = (ni * are - nr * aim) / den
        bre, bim = b_re[dirn].astype(f32), b_im[dirn].astype(f32)
        bbr = fr[..., None] * bre - fi[..., None] * bim
        bbi = fr[..., None] * bim + fi[..., None] * bre
        cre, cim = c_re[dirn].astype(f32), c_im[dirn].astype(f32)
        dr, di = (pw_re[:C][::-1], pw_im[:C][::-1]) if dirn == 0 else (pw_re[:C], pw_im[:C])
        sr = dr[..., None] * bbr[None] - di[..., None] * bbi[None]
        si = dr[..., None] * bbi[None] + di[..., None] * bbr[None]
        wb = jnp.concatenate([sr, si], axis=2)
        wbs.append(wb.transpose(1, 0, 3, 2).reshape(G, C * N, 2 * P))
        pr, pi = pw_re[1:], pw_im[1:]
        if dirn == 1:
            pr, pi = pr[::-1], pi[::-1]
        or_ = cre[None] * pr[:, :, None, :] - cim[None] * pi[:, :, None, :]
        oi_ = -(cre[None] * pi[:, :, None, :] + cim[None] * pr[:, :, None, :])
        wc = jnp.concatenate([or_, oi_], axis=3)
        wcs.append(wc.transpose(1, 3, 0, 2).reshape(G, 2 * P, C * N))
        clr = cre[None] * pw_re[:C, :, None, :] - cim[None] * pw_im[:C, :, None, :]
        cli = cre[None] * pw_im[:C, :, None, :] + cim[None] * pw_re[:C, :, None, :]
        kk = (jnp.einsum('jgnp,gpm->gjnm', clr, bbr, precision=hp)
              - jnp.einsum('jgnp,gpm->gjnm', cli, bbi, precision=hp))
        s_idx = jnp.arange(C)[:, None]
        t_idx = jnp.arange(C)[None, :]
        lag = (t_idx - s_idx) if dirn == 0 else (s_idx - t_idx)
        onehot = (lag[:, :, None] == jnp.arange(C)[None, None, :]).astype(f32)
        kt = jnp.einsum('stj,gjnm->gsmtn', onehot, kk, precision=hp)
        wks.append(kt.reshape(G, C * N, C * N))
        lams.append((pw_re[C], pw_im[C]))
    wb = jnp.concatenate([wbs[0][..., :P], wbs[1][..., :P], wbs[0][..., P:], wbs[1][..., P:]], axis=2).astype(bf16)
    wc = jnp.concatenate([wcs[0][:, :P], wcs[1][:, :P], wcs[0][:, P:], wcs[1][:, P:]], axis=1).astype(bf16)
    wk = (wks[0] + wks[1]).astype(bf16)
    lam = jnp.stack([jnp.concatenate([lams[0][0], lams[1][0]], axis=-1),
                     jnp.concatenate([lams[0][1], lams[1][1]], axis=-1)], axis=1)
    return wb, wk, wc, lam


def _s5_scan(h_lat, h_ctx, a_re, a_im, log_dt, b_re, b_im, c_re, c_im):
    B, S, D = h_lat.shape
    L = h_ctx.shape[1]
    C, N, P = S5_CHUNK, S5_GROUP, S5_STATE
    G = D // N
    nc_ctx, nc = L // C, (L + S) // C
    assert 2 * P == LANES and L % C == 0 and S % C == 0
    wb, wk, wc, lam = _s5_weights(a_re, a_im, log_dt, b_re, b_im, c_re, c_im)

    def to_groups(h):
        n = h.shape[1] // C
        return h.reshape(B, n, C, G, N).transpose(3, 1, 0, 2, 4).reshape(G, n * B, C * N)

    def from_groups(y):
        n = y.shape[1] // B
        return y.reshape(G, n, B, C, N).transpose(2, 1, 3, 0, 4).reshape(B, n * C, D)

    rc, rl = nc_ctx * B, (nc - nc_ctx) * B

    def rows(r):
        return pl.BlockSpec((1, r, C * N), lambda g: (g, 0, 0))

    yc, yl = pl.pallas_call(
        functools.partial(_s5_kernel, nb=B, nc_ctx=nc_ctx, nc=nc),
        grid=(G,),
        in_specs=[rows(rc), rows(rl),
                  pl.BlockSpec((1, C * N, 4 * P), lambda g: (g, 0, 0)),
                  pl.BlockSpec((1, C * N, C * N), lambda g: (g, 0, 0)),
                  pl.BlockSpec((1, 4 * P, C * N), lambda g: (g, 0, 0)),
                  pl.BlockSpec((1, 2, 2 * P), lambda g: (g, 0, 0))],
        out_specs=[rows(rc), rows(rl)],
        out_shape=[jax.ShapeDtypeStruct((G, rc, C * N), f32), jax.ShapeDtypeStruct((G, rl, C * N), f32)],
        scratch_shapes=[pltpu.VMEM((rc + rl, 4 * P), f32), pltpu.VMEM((rc + rl, 4 * P), f32)],
        compiler_params=_cparams("parallel"),
        name="s5_scan",
    )(to_groups(h_ctx), to_groups(h_lat), wb, wk, wc, lam)
    return from_groups(yl), from_groups(yc)


def _glu_kernel(y_ref, x_ref, mod_ref, gpre_ref, d_ref, g_ref, w_ref, b_ref, o_ref, *, j):
    x = x_ref[0]
    z = jax.nn.gelu(d_ref[...] * _pre_norm(x, gpre_ref[...], mod_ref, j) + y_ref[0])
    u = _dot(z.astype(bf16), w_ref[...]) + b_ref[...]
    o_ref[0] = _post_residual(x, z * jax.nn.sigmoid(u), g_ref[...], mod_ref, j, 1.0)


def _glu_residual(y, x, mod, g_pre, d_skip, g, w, bias, j):
    b, t, d = x.shape
    tm = _tok_tile(t)
    tok = pl.BlockSpec((1, tm, d), lambda bb, tt: (bb, tt, 0))
    vec = _full((1, d))
    return pl.pallas_call(
        functools.partial(_glu_kernel, j=j),
        grid=(b, t // tm),
        in_specs=[tok, tok, _mod_spec(mod), vec, vec, vec, _full((d, d)), vec],
        out_specs=tok,
        out_shape=jax.ShapeDtypeStruct((b, t, d), f32),
        compiler_params=_cparams("parallel", "parallel"),
        name="glu_residual",
    )(y, x, mod, g_pre.reshape(1, d), d_skip.astype(f32).reshape(1, d), g.reshape(1, d), w, bias.reshape(1, d))


def _diff_maps(q):
    lane = lax.broadcasted_iota(jnp.int32, q.shape, 1)
    qs = (q.astype(f32) * (DIFF_HEAD_DIM ** -0.5 * math.log2(math.e))).astype(bf16)
    zero = jnp.zeros_like(qs)
    return jnp.where(lane < DIFF_HEAD_DIM, qs, zero), jnp.where(lane >= DIFF_HEAD_DIM, qs, zero)


def _key_chunks(k_refs):
    out = []
    for seg, k_ref in enumerate(k_refs):
        n = k_ref.shape[1]
        out += [(seg, c, min(c + DIFF_KEY_CHUNK, n)) for c in range(0, n, DIFF_KEY_CHUNK)]
    return out


def _diff_step(q_next, cur, nxt, k_refs, v_refs, lamv_ref, g_ref, o_ref, lam_init):
    chunks = _key_chunks(k_refs)
    cur_s, cur_m = cur
    tq = cur_m.shape[1]
    m1, m2 = cur_m[0, :, 0:1], cur_m[1, :, 0:1]
    if q_next is not None:
        q1n, q2n = _diff_maps(q_next)
        nxt_s, nxt_m = nxt
    n1 = n2 = l1 = l2 = None
    for seg, a, b in chunks:
        if q_next is not None:
            s1 = _dot_nt(q1n, k_refs[seg][0, a:b, :])
            s2 = _dot_nt(q2n, k_refs[seg][0, a:b, :])
            nxt_s[seg][0, :, a:b] = s1
            nxt_s[seg][1, :, a:b] = s2
            c1 = jnp.max(s1, axis=-1, keepdims=True)
            c2 = jnp.max(s2, axis=-1, keepdims=True)
            n1 = c1 if n1 is None else jnp.maximum(n1, c1)
            n2 = c2 if n2 is None else jnp.maximum(n2, c2)
        p1 = jnp.exp2(cur_s[seg][0, :, a:b] - m1)
        p2 = jnp.exp2(cur_s[seg][1, :, a:b] - m2)
        cur_s[seg][0, :, a:b] = p1
        cur_s[seg][1, :, a:b] = p2
        r1 = jnp.sum(p1, axis=-1, keepdims=True)
        r2 = jnp.sum(p2, axis=-1, keepdims=True)
        l1 = r1 if l1 is None else l1 + r1
        l2 = r2 if l2 is None else l2 + r2
    if q_next is not None:
        nxt_m[0] = jnp.broadcast_to(n1, (tq, LANES))
        nxt_m[1] = jnp.broadcast_to(n2, (tq, LANES))
    lv = lamv_ref[...]
    lam = (jnp.exp(jnp.sum(lv[0:1] * lv[1:2], axis=-1, keepdims=True))
           - jnp.exp(jnp.sum(lv[2:3] * lv[3:4], axis=-1, keepdims=True)) + lam_init)
    c = lam * l1 / l2
    o = None
    for seg, a, b in chunks:
        w = (cur_s[seg][0, :, a:b] - c * cur_s[seg][1, :, a:b]).astype(bf16)
        part = _dot(w, v_refs[seg][0, a:b, :])
        o = part if o is None else o + part
    o = o / l1
    o_ref[0] = (_rms(o, g_ref[...]) * (1 - lam_init)).astype(o_ref.dtype)


def _diff_kernel(*refs, n_seg, nq, lam_init):
    q_ref, qn_ref = refs[:2]
    k_refs = refs[2:2 + 2 * n_seg:2]
    v_refs = refs[3:3 + 2 * n_seg:2]
    lamv_ref, g_ref, o_ref = refs[2 + 2 * n_seg:5 + 2 * n_seg]
    scratch = refs[5 + 2 * n_seg:]
    slots = tuple((scratch[p * (n_seg + 1):p * (n_seg + 1) + n_seg], scratch[p * (n_seg + 1) + n_seg])
                  for p in range(2))
    t = pl.program_id(2)

    @pl.when(t == 0)
    def _():
        q1, q2 = _diff_maps(q_ref[0])
        s_refs, m_ref = slots[0]
        for mp, qm in enumerate((q1, q2)):
            parts = [_dot_nt(qm, k_ref[0]) for k_ref in k_refs]
            for s_ref, sc in zip(s_refs, parts):
                s_ref[mp] = sc
            mx = functools.reduce(jnp.maximum, [jnp.max(sc, axis=-1, keepdims=True) for sc in parts])
            m_ref[mp] = jnp.broadcast_to(mx, m_ref.shape[1:])

    if nq == 1:
        _diff_step(None, slots[0], None, k_refs, v_refs, lamv_ref, g_ref, o_ref, lam_init)
        return
    for parity in range(2):
        @pl.when((t & 1) == parity)
        def _(parity=parity):
            _diff_step(qn_ref[0], slots[parity], slots[1 - parity], k_refs, v_refs, lamv_ref, g_ref, o_ref,
                       lam_init)


def _diff_attention(qkv_q, qkv_ctx, qkv_lat, lamv, subln, lam_init):
    B, T, n3 = qkv_q.shape
    D = n3 // 3
    H = D // LANES
    tq = min(_tok_tile(T), 256)
    nq = T // tq
    kv = [qkv_ctx] if qkv_lat is None else [qkv_ctx, qkv_lat]
    args = [qkv_q, qkv_q]
    specs = [pl.BlockSpec((1, tq, LANES), lambda b, h, t: (b, t, h)),
             pl.BlockSpec((1, tq, LANES), lambda b, h, t: (b, jnp.minimum(t + 1, nq - 1), h))]
    for a in kv:
        args += [a, a]
        specs += [pl.BlockSpec((1, a.shape[1], LANES), lambda b, h, t: (b, 0, H + h)),
                  pl.BlockSpec((1, a.shape[1], LANES), lambda b, h, t: (b, 0, 2 * H + h))]
    args += [lamv, subln.reshape(1, LANES)]
    specs += [_full(lamv.shape), _full((1, LANES))]
    slot = [pltpu.VMEM((2, tq, a.shape[1]), f32) for a in kv] + [pltpu.VMEM((2, tq, LANES), f32)]
    return pl.pallas_call(
        functools.partial(_diff_kernel, n_seg=len(kv), nq=nq, lam_init=lam_init),
        grid=(B, H, nq),
        in_specs=specs,
        out_specs=pl.BlockSpec((1, tq, LANES), lambda b, h, t: (b, t, h)),
        out_shape=jax.ShapeDtypeStruct((B, T, D), bf16),
        scratch_shapes=slot + slot,
        compiler_params=_cparams("parallel", "parallel", "arbitrary"),
        name="diff_attention",
    )(*args)


def _rope_tables(n_tokens, head_dim):
    n_freq = head_dim // 4
    inv_freq = ROPE_BASE ** (-jnp.arange(n_freq, dtype=f32) / n_freq)
    t = jnp.arange(n_tokens)
    row = (t // GRID_W).astype(f32)
    col = (t % GRID_W).astype(f32)
    ang = jnp.concatenate([row[:, None] * inv_freq, col[:, None] * inv_freq], axis=-1)
    cos = jnp.repeat(jnp.cos(ang), 2, axis=-1)
    sin = jnp.repeat(jnp.sin(ang), 2, axis=-1) * jnp.tile(jnp.array([-1.0, 1.0], f32), head_dim // 2)
    reps = LANES // head_dim
    return jnp.tile(cos, (1, reps)), jnp.tile(sin, (1, reps))


def _na_kernel(*refs, lat, s_tok):
    if lat:
        q_ref, kc_ref, vc_ref, kl_ref, vl_ref, tb_ref, o_ref, bias_ref = refs
    else:
        q_ref, kc_ref, vc_ref, o_ref = refs
    q = q_ref[0]
    lane = lax.broadcasted_iota(jnp.int32, q.shape, 1)
    scale = NA_HEAD_DIM ** -0.5
    qs = (q.astype(f32) * scale).astype(bf16)
    zero = jnp.zeros_like(qs)
    kc, vc = kc_ref[0], vc_ref[0]
    if lat:
        t = pl.program_id(2)
        rows = s_tok // GRID_W
        w0 = jnp.clip(t * NA_TILE_ROWS - (NA_WIN_ROWS - NA_TILE_ROWS) // 2, 0, rows - NA_WIN_ROWS)
        tok0 = pl.multiple_of(w0 * GRID_W, 4 * GRID_W)
        kw = kl_ref[0, pl.ds(tok0, NA_WIN_ROWS * GRID_W), :]
        vw = vl_ref[0, pl.ds(tok0, NA_WIN_ROWS * GRID_W), :]
        for i in range(NA_TILE_ROWS):
            qr = t * NA_TILE_ROWS + i
            r0 = jnp.clip(qr - WIN_H // 2, 0, rows - WIN_H)
            for j in range(NA_WIN_ROWS // 2):
                es = []
                for kr in (w0 + 2 * j, w0 + 2 * j + 1):
                    ok = (kr >= r0) & (kr < r0 + WIN_H)
                    es.append(jnp.where(ok, kr - qr + WIN_H, 0))
                for hh in range(2):
                    bias_ref[hh, i * GRID_W:(i + 1) * GRID_W, j * LANES:(j + 1) * LANES] = (
                        tb_ref[0, hh, 0, es[0]] + tb_ref[0, hh, 1, es[1]])
    outs = []
    for hh in range(2):
        sel = (lane < NA_HEAD_DIM) if hh == 0 else (lane >= NA_HEAD_DIM)
        qh = jnp.where(sel, qs, zero)
        s_c = _dot_nt(qh, kc)
        m = jnp.max(s_c, axis=-1, keepdims=True)
        if lat:
            s_l = _dot_nt(qh, kw) + bias_ref[hh]
            m = jnp.maximum(m, jnp.max(s_l, axis=-1, keepdims=True))
        p_c = jnp.exp(s_c - m)
        l = jnp.sum(p_c, axis=-1, keepdims=True)
        o = _dot(p_c.astype(bf16), vc)
        if lat:
            p_l = jnp.exp(s_l - m)
            l = l + jnp.sum(p_l, axis=-1, keepdims=True)
            o = o + _dot(p_l.astype(bf16), vw)
        outs.append(o / l)
    o_ref[0] = jnp.where(lane < NA_HEAD_DIM, outs[0], outs[1]).astype(o_ref.dtype)


def _na_bias_blocks(rpb):
    H, n_dr, n_dc = rpb.shape
    col = jnp.arange(GRID_W)
    c0 = jnp.clip(col - WIN_W // 2, 0, GRID_W - WIN_W)
    col_ok = (col[None, :] >= c0[:, None]) & (col[None, :] < c0[:, None] + WIN_W)
    dc = jnp.clip(col[None, :] - col[:, None] + WIN_W - 1, 0, n_dc - 1)
    onehot = (dc.reshape(1, -1) == jnp.arange(n_dc)[:, None]).astype(f32)
    blk = jnp.einsum('hrj,jx->hrx', rpb.astype(f32), onehot, precision=lax.Precision.HIGHEST)
    blk = jnp.where(col_ok[None, None], blk.reshape(H, n_dr, GRID_W, GRID_W), NEG_INF)
    blk = jnp.concatenate([jnp.full((H, 1, GRID_W, GRID_W), NEG_INF, f32), blk], axis=1)
    z = jnp.zeros_like(blk)
    tab = jnp.stack([jnp.concatenate([blk, z], axis=-1), jnp.concatenate([z, blk], axis=-1)], axis=1)
    return tab.reshape((H // 2, 2) + tab.shape[1:])


def _na_attention(qkv_q, qkv_ctx, qkv_lat, bias):
    B, T, n3 = qkv_q.shape
    D = n3 // 3
    HP = D // LANES
    L = qkv_ctx.shape[1]
    lat = qkv_lat is not None
    args = [qkv_q, qkv_ctx, qkv_ctx]
    if lat:
        tq = NA_TILE_ROWS * GRID_W
        n_tiles = T // tq
        assert T % tq == 0 and n_tiles >= 2
    else:
        tq = min(_tok_tile(T), 256)
        n_tiles = T // tq
    specs = [pl.BlockSpec((1, tq, LANES), lambda b, h, t: (b, t, h)),
             pl.BlockSpec((1, L, LANES), lambda b, h, t: (b, 0, HP + h)),
             pl.BlockSpec((1, L, LANES), lambda b, h, t: (b, 0, 2 * HP + h))]
    scratch = []
    if lat:
        assert T // GRID_W >= NA_WIN_ROWS and bias.shape[3] == 2 * WIN_H
        args += [qkv_lat, qkv_lat, bias]
        specs += [pl.BlockSpec((1, T, LANES), lambda b, h, t: (b, 0, HP + h)),
                  pl.BlockSpec((1, T, LANES), lambda b, h, t: (b, 0, 2 * HP + h)),
                  pl.BlockSpec((1,) + bias.shape[1:], lambda b, h, t: (h, 0, 0, 0, 0, 0))]
        scratch = [pltpu.VMEM((2, tq, NA_WIN_ROWS * GRID_W), f32)]
    return pl.pallas_call(
        functools.partial(_na_kernel, lat=lat, s_tok=T),
        grid=(B, HP, n_tiles),
        in_specs=specs,
        out_specs=pl.BlockSpec((1, tq, LANES), lambda b, h, t: (b, t, h)),
        out_shape=jax.ShapeDtypeStruct((B, T, D), bf16),
        scratch_shapes=scratch,
        compiler_params=_cparams("parallel", "parallel", "arbitrary"),
        name="na_attention",
    )(*args)


def _chunk_cumsum(x, reverse):
    n = x.shape[0]
    pos = lax.broadcasted_iota(jnp.int32, x.shape, 0) & (HG_CHUNK - 1)
    k = 1
    while k < HG_CHUNK:
        if reverse:
            x = x + jnp.where(pos < HG_CHUNK - k, pltpu.roll(x, n - k, 0), 0.0)
        else:
            x = x + jnp.where(pos >= k, pltpu.roll(x, k, 0), 0.0)
        k *= 2
    return x


def _hg_block(z, v, q, lb, st, reverse, with_out):
    n = z.shape[0]
    nch = n // HG_CHUNK
    f = lb + (1 - lb) * jax.nn.sigmoid(z)
    lf = jnp.log(f)
    kk = 1 - f
    g = _chunk_cumsum(lf, reverse)
    out = None
    if with_out:
        ri = lax.broadcasted_iota(jnp.int32, (n, n), 0)
        ci = lax.broadcasted_iota(jnp.int32, (n, n), 1)
        shift = HG_CHUNK.bit_length() - 1
        tri = ((ri >> shift) == (ci >> shift)) & ((ci >= ri) if reverse else (ci <= ri))
        q_dec = q * jnp.exp(g)
        k_inv = kk * jnp.exp(-g)
        att = jnp.where(tri, _dot_nt(q_dec.astype(bf16), k_inv.astype(bf16)), 0.0)
        out_intra = _dot(att.astype(bf16), v.astype(bf16))
        outs = [None] * nch
    order = range(nch - 1, -1, -1) if reverse else range(nch)
    for c in order:
        sl = slice(c * HG_CHUNK, (c + 1) * HG_CHUNK)
        end = c * HG_CHUNK if reverse else (c + 1) * HG_CHUNK - 1
        g_tot = g[end:end + 1]
        if with_out:
            outs[c] = _dot_nt(q_dec[sl].astype(bf16), st.astype(bf16))
        kdec = kk[sl] * jnp.exp(g_tot - g[sl])
        ds_t = _dot(v[sl].T.astype(bf16), kdec.astype(bf16))
        st = st * jnp.exp(g_tot) + ds_t
    if with_out:
        out = out_intra + jnp.concatenate(outs, axis=0)
    return out, st


def _hg_kernel(q_ref, i_ref, gate_ref, zf_ref, zb_ref, ic_ref, zfc_ref, zbc_ref, lb_ref, gn_ref, o_ref,
               accf_ref, accb_ref, *, s_tok):
    lb = lb_ref[...]
    nblk = s_tok // HG_BLOCK
    zero = jnp.zeros((HG_EXPAND, HG_EXPAND), f32)
    _, st_f = _hg_block(zfc_ref[0], ic_ref[0], None, lb, zero, False, False)
    _, st_b = _hg_block(zbc_ref[0], ic_ref[0], None, lb, zero, True, False)

    def body(n, carry):
        st_f, st_b = carry
        rows_f = pl.ds(pl.multiple_of(n * HG_BLOCK, HG_BLOCK), HG_BLOCK)
        rows_b = pl.ds(pl.multiple_of((nblk - 1 - n) * HG_BLOCK, HG_BLOCK), HG_BLOCK)
        out_f, st_f = _hg_block(zf_ref[0, rows_f, :], i_ref[0, rows_f, :], q_ref[0, rows_f, :], lb, st_f, False, True)
        out_b, st_b = _hg_block(zb_ref[0, rows_b, :], i_ref[0, rows_b, :], q_ref[0, rows_b, :], lb, st_b, True, True)
        accf_ref[rows_f, :] = out_f
        accb_ref[rows_b, :] = out_b
        return st_f, st_b

    lax.fori_loop(0, nblk, body, (st_f, st_b), unroll=2 if nblk % 2 == 0 else 1)
    gate = gate_ref[0]
    o = accf_ref[...] + accb_ref[...]
    o_ref[0] = (_rms(o, gn_ref[...]) * (gate * jax.nn.sigmoid(gate))).astype(o_ref.dtype)


def _hgrn2_core(p_lat, p_ctx, lb, gn_g):
    B, S, n5 = p_lat.shape
    D = n5 // 5
    H = D // HG_EXPAND
    L = p_ctx.shape[1]
    assert S % HG_BLOCK == 0 and L % HG_CHUNK == 0

    def lat(k):
        return pl.BlockSpec((1, S, HG_EXPAND), lambda b, h: (b, 0, k * H + h))

    def ctx(k):
        return pl.BlockSpec((1, L, HG_EXPAND), lambda b, h: (b, 0, k * H + h))

    vec = pl.BlockSpec((1, HG_EXPAND), lambda b, h: (0, h))
    return pl.pallas_call(
        functools.partial(_hg_kernel, s_tok=S),
        grid=(B, H),
        in_specs=[lat(0), lat(1), lat(2), lat(3), lat(4), ctx(1), ctx(3), ctx(4), vec, _full((1, HG_EXPAND))],
        out_specs=pl.BlockSpec((1, S, HG_EXPAND), lambda b, h: (b, 0, h)),
        out_shape=jax.ShapeDtypeStruct((B, S, D), bf16),
        scratch_shapes=[pltpu.VMEM((S, HG_EXPAND), f32), pltpu.VMEM((S, HG_EXPAND), f32)],
        compiler_params=_cparams("parallel", "parallel"),
        name="hgrn2_core",
    )(p_lat, p_lat, p_lat, p_lat, p_lat, p_ctx, p_ctx, p_ctx, lb.reshape(1, D), gn_g.reshape(1, HG_EXPAND))


def kernel(x, c, ctx, c_ctx, w_ada, b_ada, g_pre, g_post, w_ff1, w_ff3, w_ff2, s5_a_re, s5_a_im, s5_log_dt, s5_b_re, s5_b_im, s5_c_re, s5_c_im, s5_d, s5_w_glu, s5_b_glu, da_w_qkv, da_w_o, da_lam_q1, da_lam_k1, da_lam_q2, da_lam_k2, da_subln, na_w_qkv, na_w_o, na_rpb, hg_w_qig, hg_w_f, hg_b_f, hg_lb_logits, hg_gnorm, hg_w_o):
    B, S, D = x.shape
    depth = w_ada.shape[0]
    n_mix = 4
    rows_pad = -(-(B + 1) // 8) * 8
    c_all = jnp.concatenate([c, c_ctx[None], jnp.zeros((rows_pad - B - 1, D), f32)], axis=0)
    mods = _ada(c_all, w_ada, b_ada)
    lb_p = jax.nn.softmax(hg_lb_logits.astype(f32), axis=0)
    lower_bounds = jnp.cumsum(lb_p, axis=0) - lb_p[0]
    w1b, w3b, w2b = w_ff1.astype(bf16), w_ff3.astype(bf16), w_ff2.astype(bf16)

    x_lat, x_ctx = x, ctx
    for i in range(depth):
        last = i == depth - 1
        occ, kind = i // n_mix, i % n_mix
        m_lat = mods[i, :B].reshape(B, 3 * N_SUB, D)
        m_ctx = mods[i, B:B + 1].reshape(1, 3 * N_SUB, D)
        streams = [(x_lat, m_lat), (x_ctx, m_ctx)]

        x_lat, x_ctx = [_half_ffn(xs, ms, g_pre[i, 0], g_post[i, 0], w1b[i, 0], w3b[i, 0], w2b[i, 0], 0)
                        for xs, ms in streams]

        if kind == 0:
            h_lat = _prenorm(x_lat, m_lat, g_pre[i, 1], 1)
            h_ctx = _prenorm(x_ctx, m_ctx, g_pre[i, 1], 1)
            y_lat, y_ctx = _s5_scan(h_lat, h_ctx, s5_a_re[occ], s5_a_im[occ], s5_log_dt[occ], s5_b_re[occ],
                                    s5_b_im[occ], s5_c_re[occ], s5_c_im[occ])
            wg = s5_w_glu[occ].astype(bf16)
            glu = functools.partial(_glu_residual, g_pre=g_pre[i, 1], d_skip=s5_d[occ], g=g_post[i, 1], w=wg,
                                    bias=s5_b_glu[occ], j=1)
            x_lat_new = glu(y_lat, x_lat, m_lat)
            x_ctx_new = None if last else glu(y_ctx, x_ctx, m_ctx)
        elif kind == 1:
            wq = da_w_qkv[occ].astype(bf16)
            rope = _rope_tables(S, DIFF_HEAD_DIM)
            qkv_lat = _prenorm_proj(x_lat, m_lat, g_pre[i, 1], 1, wq, rope=rope, n_rope=2 * D)
            qkv_ctx = _prenorm_proj(x_ctx, m_ctx, g_pre[i, 1], 1, wq)
            lamv = jnp.stack([da_lam_q1[occ], da_lam_k1[occ], da_lam_q2[occ], da_lam_k2[occ]]).astype(f32)
            lam_init = 0.8 - 0.6 * math.exp(-0.3 * i)
            wo = da_w_o[occ].astype(bf16)
            o_lat = _diff_attention(qkv_lat, qkv_ctx, qkv_lat, lamv, da_subln[occ], lam_init)
            x_lat_new = _outproj_residual(o_lat, x_lat, m_lat, g_post[i, 1], wo, 1)
            x_ctx_new = None
            if not last:
                o_ctx = _diff_attention(qkv_ctx, qkv_ctx, None, lamv, da_subln[occ], lam_init)
                x_ctx_new = _outproj_residual(o_ctx, x_ctx, m_ctx, g_post[i, 1], wo, 1)
        elif kind == 2:
            wq = na_w_qkv[occ].astype(bf16)
            qkv_lat = _prenorm_proj(x_lat, m_lat, g_pre[i, 1], 1, wq)
            qkv_ctx = _prenorm_proj(x_ctx, m_ctx, g_pre[i, 1], 1, wq)
            wo = na_w_o[occ].astype(bf16)
            o_lat = _na_attention(qkv_lat, qkv_ctx, qkv_lat, _na_bias_blocks(na_rpb[occ]))
            x_lat_new = _outproj_residual(o_lat, x_lat, m_lat, g_post[i, 1], wo, 1)
            x_ctx_new = None
            if not last:
                o_ctx = _na_attention(qkv_ctx, qkv_ctx, None, None)
                x_ctx_new = _outproj_residual(o_ctx, x_ctx, m_ctx, g_post[i, 1], wo, 1)
        else:
            assert last, "HGRN2 context outputs are not needed when it is the last layer"
            wp = jnp.concatenate([hg_w_qig[occ], hg_w_f[occ, 0], hg_w_f[occ, 1]], axis=1).astype(bf16)
            bp = jnp.concatenate([jnp.zeros((3 * D,), f32), hg_b_f[occ, 0], hg_b_f[occ, 1]])
            p_lat = _prenorm_proj(x_lat, m_lat, g_pre[i, 1], 1, wp, bias=bp, out_dtype=f32)
            p_ctx = _prenorm_proj(x_ctx, m_ctx, g_pre[i, 1], 1, wp, bias=bp, out_dtype=f32)
            o_lat = _hgrn2_core(p_lat, p_ctx, lower_bounds[i], hg_gnorm[occ])
            x_lat_new = _outproj_residual(o_lat, x_lat, m_lat, g_post[i, 1], hg_w_o[occ].astype(bf16), 1)
            x_ctx_new = None

        x_lat = _half_ffn(x_lat_new, m_lat, g_pre[i, 2], g_post[i, 2], w1b[i, 1], w3b[i, 1], w2b[i, 1], 2)
        if not last:
            x_ctx = _half_ffn(x_ctx_new, m_ctx, g_pre[i, 2], g_post[i, 2], w1b[i, 1], w3b[i, 1], w2b[i, 1], 2)
    return x_lat
```

```python
import functools
import math

import jax
import jax.numpy as jnp
from jax import lax
from jax.experimental import pallas as pl
from jax.experimental.pallas import tpu as pltpu

f32 = jnp.float32
bf16 = jnp.bfloat16

N_SUB = 3
RMS_EPS = 1e-6
NEG_INF = -1e30
ROPE_BASE = 10000.0
GRID_W = 64
S5_GROUP = 16
S5_STATE = 64
S5_CHUNK = 16
DIFF_HEAD_DIM = 64
DIFF_KEY_CHUNK = 1024
NA_HEAD_DIM = 64
WIN_H = 8
WIN_W = 16
NA_TILE_ROWS = 8
NA_WIN_ROWS = 16
NA_KEY_CHUNK = 512
HG_EXPAND = 128
HG_CHUNK = 64
HG_BLOCK = 512
LANES = 128
VMEM_LIMIT = 56 * 1024 * 1024


def _cparams(*sem):
    return pltpu.CompilerParams(dimension_semantics=sem, vmem_limit_bytes=VMEM_LIMIT)


def _dot(a, b):
    return jnp.dot(a, b, preferred_element_type=f32)


def _dot_nt(a, b):
    return lax.dot_general(a, b, (((1,), (1,)), ((), ())), preferred_element_type=f32)


def _rms(x, g):
    return x * lax.rsqrt(jnp.mean(jnp.square(x), axis=-1, keepdims=True) + RMS_EPS) * g


def _pre_norm(x, g, mod_ref, j):
    shift = mod_ref[0, 3 * j:3 * j + 1, :]
    scale = mod_ref[0, 3 * j + 1:3 * j + 2, :]
    return _rms(x, g) * (1 + scale) + shift


def _post_residual(x, y, g, mod_ref, j, weight):
    gate = mod_ref[0, 3 * j + 2:3 * j + 3, :]
    return x + weight * gate * _rms(y, g)


def _tok_tile(t):
    for tm in (512, 256, 128):
        if t % tm == 0:
            return tm
    raise ValueError(f"token count {t} is not a multiple of 128")


def _full(shape):
    return pl.BlockSpec(shape, lambda *_: (0,) * len(shape))


def _mod_spec(mod):
    nd = mod.shape[1]
    d = mod.shape[2]
    if mod.shape[0] == 1:
        return pl.BlockSpec((1, nd, d), lambda b, t: (0, 0, 0))
    return pl.BlockSpec((1, nd, d), lambda b, t: (b, 0, 0))


def _ada_kernel(c_ref, w_ref, b_ref, o_ref):
    c = c_ref[...]
    sc = (c * jax.nn.sigmoid(c)).astype(bf16)
    o_ref[0] = _dot(sc, w_ref[0].astype(bf16)) + b_ref[0]


def _ada(c_all, w_ada, b_ada):
    depth, d, n = w_ada.shape
    rows = c_all.shape[0]
    tn = n // 4
    return pl.pallas_call(
        _ada_kernel,
        grid=(depth, n // tn),
        in_specs=[pl.BlockSpec((rows, d), lambda i, t: (0, 0)),
                  pl.BlockSpec((1, d, tn), lambda i, t: (i, 0, t)),
                  pl.BlockSpec((1, 1, tn), lambda i, t: (i, 0, t))],
        out_specs=pl.BlockSpec((1, rows, tn), lambda i, t: (i, 0, t)),
        out_shape=jax.ShapeDtypeStruct((depth, rows, n), f32),
        compiler_params=_cparams("arbitrary", "arbitrary"),
        name="ada",
    )(c_all, w_ada, b_ada.reshape(depth, 1, n))


def _ffn_kernel(x_ref, mod_ref, gin_ref, gout_ref, w1_ref, w3_ref, w2_ref, *rest, j, chunks, mix):
    o_ref = rest[-1]
    x = x_ref[0]
    if mix == "outproj":
        y_ref, gmix_ref, wo_ref = rest[:-1]
        x = _post_residual(x, _dot(y_ref[0], wo_ref[...]), gmix_ref[...], mod_ref, 1, 1.0)
    elif mix == "glu":
        y_ref, gpre_ref, d_ref, gmix_ref, wg_ref, bg_ref = rest[:-1]
        z = jax.nn.gelu(d_ref[...] * _pre_norm(x, gpre_ref[...], mod_ref, 1) + y_ref[0])
        u = _dot(z.astype(bf16), wg_ref[...]) + bg_ref[...]
        x = _post_residual(x, z * jax.nn.sigmoid(u), gmix_ref[...], mod_ref, 1, 1.0)
    h = _pre_norm(x, gin_ref[...], mod_ref, j).astype(bf16)
    acc = None
    for s, n in chunks:
        a = _dot(h, w1_ref[:, s:s + n])
        b = _dot(h, w3_ref[:, s:s + n])
        g = (a * jax.nn.sigmoid(a) * b).astype(bf16)
        y = _dot(g, w2_ref[s:s + n, :])
        acc = y if acc is None else acc + y
    o_ref[0] = _post_residual(x, acc, gout_ref[...], mod_ref, j, 0.5)


def _ff_chunks(dff, step=1024):
    out, s = [], 0
    while s < dff:
        n = min(step, dff - s)
        out.append((s, n))
        s += n
    return tuple(out)


def _half_ffn(x, mod, g_in, g_out, w1, w3, w2, j, mix=None, mix_args=()):
    b, t, d = x.shape
    dff = w1.shape[1]
    tm = _tok_tile(t)
    tok = pl.BlockSpec((1, tm, d), lambda bb, tt: (bb, tt, 0))
    vec = _full((1, d))
    args = [x, mod, g_in.reshape(1, d), g_out.reshape(1, d), w1, w3, w2]
    specs = [tok, _mod_spec(mod), vec, vec, _full((d, dff)), _full((d, dff)), _full((dff, d))]
    if mix == "outproj":
        y, g_mix, w_o = mix_args
        args += [y, g_mix.reshape(1, d), w_o]
        specs += [tok, vec, _full((d, d))]
    elif mix == "glu":
        y, g_pre, d_skip, g_mix, w_g, b_g = mix_args
        args += [y, g_pre.reshape(1, d), d_skip.astype(f32).reshape(1, d), g_mix.reshape(1, d), w_g, b_g.reshape(1, d)]
        specs += [tok, vec, vec, vec, _full((d, d)), vec]
    return pl.pallas_call(
        functools.partial(_ffn_kernel, j=j, chunks=_ff_chunks(dff), mix=mix),
        grid=(b, t // tm),
        in_specs=specs,
        out_specs=tok,
        out_shape=jax.ShapeDtypeStruct((b, t, d), f32),
        compiler_params=_cparams("parallel", "parallel"),
        name="half_ffn",
    )(*args)


def _prenorm_kernel(x_ref, mod_ref, g_ref, o_ref, *, j):
    o_ref[0] = _pre_norm(x_ref[0], g_ref[...], mod_ref, j).astype(o_ref.dtype)


def _prenorm(x, mod, g, j):
    b, t, d = x.shape
    tm = _tok_tile(t)
    return pl.pallas_call(
        functools.partial(_prenorm_kernel, j=j),
        grid=(b, t // tm),
        in_specs=[pl.BlockSpec((1, tm, d), lambda bb, tt: (bb, tt, 0)), _mod_spec(mod), _full((1, d))],
        out_specs=pl.BlockSpec((1, tm, d), lambda bb, tt: (bb, tt, 0)),
        out_shape=jax.ShapeDtypeStruct((b, t, d), bf16),
        compiler_params=_cparams("parallel", "parallel"),
        name="prenorm",
    )(x, mod, g.reshape(1, d))


def _swap_pairs(x):
    lane = lax.broadcasted_iota(jnp.int32, x.shape, 1)
    nxt = pltpu.roll(x, LANES - 1, 1)
    prv = pltpu.roll(x, 1, 1)
    return jnp.where((lane & 1) == 0, nxt, prv)


def _proj_kernel(*refs, j, n_out, n_rope, has_bias, step):
    x_ref, mod_ref, g_ref, w_ref = refs[:4]
    k = 4
    b_ref = None
    if has_bias:
        b_ref = refs[k]
        k += 1
    if n_rope:
        cos_ref, sin_ref = refs[k], refs[k + 1]
        k += 2
    o_ref = refs[k]
    h = _pre_norm(x_ref[0], g_ref[...], mod_ref, j).astype(bf16)
    for s in range(0, n_out, step):
        y = _dot(h, w_ref[:, s:s + step])
        if has_bias:
            y = y + b_ref[:, s:s + step]
        if s < n_rope:
            cos = cos_ref[...]
            sin = sin_ref[...]
            parts = []
            for c in range(0, step, LANES):
                yc = y[:, c:c + LANES]
                parts.append(yc * cos + _swap_pairs(yc) * sin)
            y = jnp.concatenate(parts, axis=1)
        o_ref[0, :, s:s + step] = y.astype(o_ref.dtype)


def _prenorm_proj(x, mod, g, j, w, bias=None, rope=None, n_rope=0, out_dtype=bf16):
    b, t, d = x.shape
    n = w.shape[1]
    tm = min(_tok_tile(t), 256)
    step = 512
    assert n % step == 0 and n_rope % step == 0
    args = [x, mod, g.reshape(1, d), w]
    specs = [pl.BlockSpec((1, tm, d), lambda bb, tt: (bb, tt, 0)), _mod_spec(mod), _full((1, d)), _full((d, n))]
    if bias is not None:
        args.append(bias.reshape(1, n))
        specs.append(_full((1, n)))
    if n_rope:
        args += list(rope)
        specs += [pl.BlockSpec((tm, LANES), lambda bb, tt: (tt, 0))] * 2
    return pl.pallas_call(
        functools.partial(_proj_kernel, j=j, n_out=n, n_rope=n_rope, has_bias=bias is not None, step=step),
        grid=(b, t // tm),
        in_specs=specs,
        out_specs=pl.BlockSpec((1, tm, n), lambda bb, tt: (bb, tt, 0)),
        out_shape=jax.ShapeDtypeStruct((b, t, n), out_dtype),
        compiler_params=_cparams("parallel", "parallel"),
        name="prenorm_proj",
    )(*args)


def _s5_kernel(xc_ref, xl_ref, wb_ref, wk_ref, wc_ref, lam_ref, yc_ref, yl_ref, s_ref, hin_ref, *, nb, nc_ctx, nc):
    rc = nc_ctx * nb
    xc, xl = xc_ref[0], xl_ref[0]
    s_ref[0:rc, :] = _dot(xc, wb_ref[0])
    s_ref[rc:, :] = _dot(xl, wb_ref[0])
    P = S5_STATE
    lam = lam_ref[0]
    a_re, a_im = lam[0:1], lam[1:2]
    fwd_lane = lax.broadcasted_iota(jnp.int32, (nb, 2 * P), 1) < P

    def step(n, carry):
        h_re, h_im = carry
        r_f = pl.multiple_of(n * nb, nb)
        n_b = jnp.where(n < nc_ctx, nc_ctx - 1 - n, nc - 1 - (n - nc_ctx))
        r_b = pl.multiple_of(n_b * nb, nb)
        rows_f, rows_b = pl.ds(r_f, nb), pl.ds(r_b, nb)
        hin_ref[rows_f, 0:P] = h_re[:, 0:P]
        hin_ref[rows_b, P:2 * P] = h_re[:, P:2 * P]
        hin_ref[rows_f, 2 * P:3 * P] = h_im[:, 0:P]
        hin_ref[rows_b, 3 * P:4 * P] = h_im[:, P:2 * P]
        s_re = jnp.where(fwd_lane, s_ref[rows_f, 0:2 * P], s_ref[rows_b, 0:2 * P])
        s_im = jnp.where(fwd_lane, s_ref[rows_f, 2 * P:4 * P], s_ref[rows_b, 2 * P:4 * P])
        return a_re * h_re - a_im * h_im + s_re, a_re * h_im + a_im * h_re + s_im

    zero = jnp.zeros((nb, 2 * P), f32)
    lax.fori_loop(0, nc, step, (zero, zero), unroll=4 if nc % 4 == 0 else 1)
    yc_ref[0] = _dot(xc, wk_ref[0]) + _dot(hin_ref[0:rc, :].astype(bf16), wc_ref[0])
    yl_ref[0] = _dot(xl, wk_ref[0]) + _dot(hin_ref[rc:, :].astype(bf16), wc_ref[0])

def _s5_weights(a_re, a_im, log_dt, b_re, b_im, c_re, c_im):
    hp = lax.Precision.HIGHEST
    C = S5_CHUNK
    G, P = a_re.shape[1:]
    N = b_re.shape[-1]
    j = jnp.arange(C + 1, dtype=f32)
    wbs, wks, wcs, lams = [], [], [], []
    for dirn in range(2):
        are, aim = a_re[dirn].astype(f32), a_im[dirn].astype(f32)
        dt = jnp.exp(log_dt[dirn].astype(f32))[:, None]
        pw_mag = jnp.exp(j[:, None, None] * (are * dt)[None])
        pw_re = pw_mag * jnp.cos(j[:, None, None] * (aim * dt)[None])
        pw_im = pw_mag * jnp.sin(j[:, None, None] * (aim * dt)[None])
        nr, ni = pw_re[1] - 1.0, pw_im[1]
        den = are * are + aim * aim
        fr = (nr * are + ni * aim) / den
        fi = (ni * are - nr * aim) / den
        bre, bim = b_re[dirn].astype(f32), b_im[dirn].astype(f32)
        bbr = fr[..., None] * bre - fi[..., None] * bim
        bbi = fr[..., None] * bim + fi[..., None] * bre
        cre, cim = c_re[dirn].astype(f32), c_im[dirn].astype(f32)
        dr, di = (pw_re[:C][::-1], pw_im[:C][::-1]) if dirn == 0 else (pw_re[:C], pw_im[:C])
        sr = dr[..., None] * bbr[None] - di[..., None] * bbi[None]
        si = dr[..., None] * bbi[None] + di[..., None] * bbr[None]
        wb = jnp.concatenate([sr, si], axis=2)
        wbs.append(wb.transpose(1, 0, 3, 2).reshape(G, C * N, 2 * P))
        pr, pi = pw_re[1:], pw_im[1:]
        if dirn == 1:
            pr, pi = pr[::-1], pi[::-1]
        or_ = cre[None] * pr[:, :, None, :] - cim[None] * pi[:, :, None, :]
        oi_ = -(cre[None] * pi[:, :, None, :] + cim[None] * pr[:, :, None, :])
        wc = jnp.concatenate([or_, oi_], axis=3)
        wcs.append(wc.transpose(1, 3, 0, 2).reshape(G, 2 * P, C * N))
        clr = cre[None] * pw_re[:C, :, None, :] - cim[None] * pw_im[:C, :, None, :]
        cli = cre[None] * pw_im[:C, :, None, :] + cim[None] * pw_re[:C, :, None, :]
        kk = (jnp.einsum('jgnp,gpm->gjnm', clr, bbr, precision=hp)
              - jnp.einsum('jgnp,gpm->gjnm', cli, bbi, precision=hp))
        s_idx = jnp.arange(C)[:, None]
        t_idx = jnp.arange(C)[None, :]
        lag = (t_idx - s_idx) if dirn == 0 else (s_idx - t_idx)
        onehot = (lag[:, :, None] == jnp.arange(C)[None, None, :]).astype(f32)
        kt = jnp.einsum('stj,gjnm->gsmtn', onehot, kk, precision=hp)
        wks.append(kt.reshape(G, C * N, C * N))
        lams.append((pw_re[C], pw_im[C]))
    wb = jnp.concatenate([wbs[0][..., :P], wbs[1][..., :P], wbs[0][..., P:], wbs[1][..., P:]], axis=2).astype(bf16)
    wc = jnp.concatenate([wcs[0][:, :P], wcs[1][:, :P], wcs[0][:, P:], wcs[1][:, P:]], axis=1).astype(bf16)
    wk = (wks[0] + wks[1]).astype(bf16)
    lam = jnp.stack([jnp.concatenate([lams[0][0], lams[1][0]], axis=-1),
                     jnp.concatenate([lams[0][1], lams[1][1]], axis=-1)], axis=1)
    return wb, wk, wc, lam


def _s5_scan(h_lat, h_ctx, a_re, a_im, log_dt, b_re, b_im, c_re, c_im):
    B, S, D = h_lat.shape
    L = h_ctx.shape[1]
    C, N, P = S5_CHUNK, S5_GROUP, S5_STATE
    G = D // N
    nc_ctx, nc = L // C, (L + S) // C
    assert 2 * P == LANES and L % C == 0 and S % C == 0
    wb, wk, wc, lam = _s5_weights(a_re, a_im, log_dt, b_re, b_im, c_re, c_im)

    def to_groups(h):
        n = h.shape[1] // C
        return h.reshape(B, n, C, G, N).transpose(3, 1, 0, 2, 4).reshape(G, n * B, C * N)

    def from_groups(y):
        n = y.shape[1] // B
        return y.reshape(G, n, B, C, N).transpose(2, 1, 3, 0, 4).reshape(B, n * C, D)

    rc, rl = nc_ctx * B, (nc - nc_ctx) * B

    def rows(r):
        return pl.BlockSpec((1, r, C * N), lambda g: (g, 0, 0))

    yc, yl = pl.pallas_call(
        functools.partial(_s5_kernel, nb=B, nc_ctx=nc_ctx, nc=nc),
        grid=(G,),
        in_specs=[rows(rc), rows(rl),
                  pl.BlockSpec((1, C * N, 4 * P), lambda g: (g, 0, 0)),
                  pl.BlockSpec((1, C * N, C * N), lambda g: (g, 0, 0)),
                  pl.BlockSpec((1, 4 * P, C * N), lambda g: (g, 0, 0)),
                  pl.BlockSpec((1, 2, 2 * P), lambda g: (g, 0, 0))],
        out_specs=[rows(rc), rows(rl)],
        out_shape=[jax.ShapeDtypeStruct((G, rc, C * N), f32), jax.ShapeDtypeStruct((G, rl, C * N), f32)],
        scratch_shapes=[pltpu.VMEM((rc + rl, 4 * P), f32), pltpu.VMEM((rc + rl, 4 * P), f32)],
        compiler_params=_cparams("parallel"),
        name="s5_scan",
    )(to_groups(h_ctx), to_groups(h_lat), wb, wk, wc, lam)
    return from_groups(yl), from_groups(yc)


def _diff_maps(q):
    lane = lax.broadcasted_iota(jnp.int32, q.shape, 1)
    qs = (q.astype(f32) * (DIFF_HEAD_DIM ** -0.5 * math.log2(math.e))).astype(bf16)
    zero = jnp.zeros_like(qs)
    return jnp.where(lane < DIFF_HEAD_DIM, qs, zero), jnp.where(lane >= DIFF_HEAD_DIM, qs, zero)


def _key_chunks(k_refs):
    out = []
    for seg, k_ref in enumerate(k_refs):
        n = k_ref.shape[1]
        out += [(seg, c, min(c + DIFF_KEY_CHUNK, n)) for c in range(0, n, DIFF_KEY_CHUNK)]
    return out


def _diff_step(q_next, cur, nxt, k_refs, v_refs, lamv_ref, g_ref, o_ref, lam_init):
    chunks = _key_chunks(k_refs)
    cur_s, cur_m = cur
    tq = cur_m.shape[1]
    m1, m2 = cur_m[0, :, 0:1], cur_m[1, :, 0:1]
    if q_next is not None:
        q1n, q2n = _diff_maps(q_next)
        nxt_s, nxt_m = nxt
    n1 = n2 = l1 = l2 = None
    for seg, a, b in chunks:
        if q_next is not None:
            s1 = _dot_nt(q1n, k_refs[seg][0, a:b, :])
            s2 = _dot_nt(q2n, k_refs[seg][0, a:b, :])
            nxt_s[seg][0, :, a:b] = s1
            nxt_s[seg][1, :, a:b] = s2
            c1 = _lane_fold(s1, jnp.maximum)
            c2 = _lane_fold(s2, jnp.maximum)
            n1 = c1 if n1 is None else jnp.maximum(n1, c1)
            n2 = c2 if n2 is None else jnp.maximum(n2, c2)
        p1 = jnp.exp2(cur_s[seg][0, :, a:b] - m1)
        p2 = jnp.exp2(cur_s[seg][1, :, a:b] - m2)
        cur_s[seg][0, :, a:b] = p1
        cur_s[seg][1, :, a:b] = p2
        r1 = _lane_fold(p1, jnp.add)
        r2 = _lane_fold(p2, jnp.add)
        l1 = r1 if l1 is None else l1 + r1
        l2 = r2 if l2 is None else l2 + r2
    l1 = jnp.sum(l1, axis=-1, keepdims=True)
    l2 = jnp.sum(l2, axis=-1, keepdims=True)
    if q_next is not None:
        nxt_m[0] = jnp.broadcast_to(jnp.max(n1, axis=-1, keepdims=True), (tq, LANES))
        nxt_m[1] = jnp.broadcast_to(jnp.max(n2, axis=-1, keepdims=True), (tq, LANES))
    lv = lamv_ref[...]
    lam = (jnp.exp(jnp.sum(lv[0:1] * lv[1:2], axis=-1, keepdims=True))
           - jnp.exp(jnp.sum(lv[2:3] * lv[3:4], axis=-1, keepdims=True)) + lam_init)
    c = lam * l1 / l2
    o = None
    for seg, a, b in chunks:
        w = (cur_s[seg][0, :, a:b] - c * cur_s[seg][1, :, a:b]).astype(bf16)
        part = _dot(w, v_refs[seg][0, a:b, :])
        o = part if o is None else o + part
    o = o / l1
    o_ref[0] = (_rms(o, g_ref[...]) * (1 - lam_init)).astype(o_ref.dtype)


def _diff_kernel(*refs, n_seg, nq, lam_init):
    q_ref, qn_ref = refs[:2]
    k_refs = refs[2:2 + 2 * n_seg:2]
    v_refs = refs[3:3 + 2 * n_seg:2]
    lamv_ref, g_ref, o_ref = refs[2 + 2 * n_seg:5 + 2 * n_seg]
    scratch = refs[5 + 2 * n_seg:]
    slots = tuple((scratch[p * (n_seg + 1):p * (n_seg + 1) + n_seg], scratch[p * (n_seg + 1) + n_seg])
                  for p in range(2))
    t = pl.program_id(2)

    @pl.when(t == 0)
    def _():
        q1, q2 = _diff_maps(q_ref[0])
        s_refs, m_ref = slots[0]
        for mp, qm in enumerate((q1, q2)):
            parts = [_dot_nt(qm, k_ref[0]) for k_ref in k_refs]
            for s_ref, sc in zip(s_refs, parts):
                s_ref[mp] = sc
            mx = functools.reduce(jnp.maximum, [jnp.max(sc, axis=-1, keepdims=True) for sc in parts])
            m_ref[mp] = jnp.broadcast_to(mx, m_ref.shape[1:])

    if nq == 1:
        _diff_step(None, slots[0], None, k_refs, v_refs, lamv_ref, g_ref, o_ref, lam_init)
        return
    for parity in range(2):
        @pl.when((t & 1) == parity)
        def _(parity=parity):
            _diff_step(qn_ref[0], slots[parity], slots[1 - parity], k_refs, v_refs, lamv_ref, g_ref, o_ref,
                       lam_init)


def _diff_attention(qkv_q, qkv_ctx, qkv_lat, lamv, subln, lam_init):
    B, T, n3 = qkv_q.shape
    D = n3 // 3
    H = D // LANES
    tq = min(_tok_tile(T), 256)
    nq = T // tq
    kv = [qkv_ctx] if qkv_lat is None else [qkv_ctx, qkv_lat]
    args = [qkv_q, qkv_q]
    specs = [pl.BlockSpec((1, tq, LANES), lambda b, h, t: (b, t, h)),
             pl.BlockSpec((1, tq, LANES), lambda b, h, t: (b, jnp.minimum(t + 1, nq - 1), h))]
    for a in kv:
        args += [a, a]
        specs += [pl.BlockSpec((1, a.shape[1], LANES), lambda b, h, t: (b, 0, H + h)),
                  pl.BlockSpec((1, a.shape[1], LANES), lambda b, h, t: (b, 0, 2 * H + h))]
    args += [lamv, subln.reshape(1, LANES)]
    specs += [_full(lamv.shape), _full((1, LANES))]
    slot = [pltpu.VMEM((2, tq, a.shape[1]), f32) for a in kv] + [pltpu.VMEM((2, tq, LANES), f32)]
    return pl.pallas_call(
        functools.partial(_diff_kernel, n_seg=len(kv), nq=nq, lam_init=lam_init),
        grid=(B, H, nq),
        in_specs=specs,
        out_specs=pl.BlockSpec((1, tq, LANES), lambda b, h, t: (b, t, h)),
        out_shape=jax.ShapeDtypeStruct((B, T, D), bf16),
        scratch_shapes=slot + slot,
        compiler_params=_cparams("parallel", "parallel", "arbitrary"),
        name="diff_attention",
    )(*args)


def _rope_tables(n_tokens, head_dim):
    n_freq = head_dim // 4
    inv_freq = ROPE_BASE ** (-jnp.arange(n_freq, dtype=f32) / n_freq)
    t = jnp.arange(n_tokens)
    row = (t // GRID_W).astype(f32)
    col = (t % GRID_W).astype(f32)
    ang = jnp.concatenate([row[:, None] * inv_freq, col[:, None] * inv_freq], axis=-1)
    cos = jnp.repeat(jnp.cos(ang), 2, axis=-1)
    sin = jnp.repeat(jnp.sin(ang), 2, axis=-1) * jnp.tile(jnp.array([-1.0, 1.0], f32), head_dim // 2)
    reps = LANES // head_dim
    return jnp.tile(cos, (1, reps)), jnp.tile(sin, (1, reps))


def _na_head(q, hh):
    lane = lax.broadcasted_iota(jnp.int32, q.shape, 1)
    qs = (q.astype(f32) * (NA_HEAD_DIM ** -0.5 * math.log2(math.e))).astype(bf16)
    sel = (lane < NA_HEAD_DIM) if hh == 0 else (lane >= NA_HEAD_DIM)
    return jnp.where(sel, qs, jnp.zeros_like(qs))


def _na_ctx_kernel(q_ref, kc_ref, vc_ref, o_ref):
    q = q_ref[0]
    lane = lax.broadcasted_iota(jnp.int32, q.shape, 1)
    outs = []
    for hh in range(2):
        s = _dot_nt(_na_head(q, hh), kc_ref[0])
        p = jnp.exp2(s - jnp.max(s, axis=-1, keepdims=True))
        outs.append(_dot(p.astype(bf16), vc_ref[0]) / jnp.sum(p, axis=-1, keepdims=True))
    o_ref[0] = jnp.where(lane < NA_HEAD_DIM, outs[0], outs[1]).astype(o_ref.dtype)


def _na_window_table(rows):
    table = []
    for t in range(rows // NA_TILE_ROWS):
        w0 = min(max(t * NA_TILE_ROWS - (NA_WIN_ROWS - NA_TILE_ROWS) // 2, 0), rows - NA_WIN_ROWS)
        line = [w0 * GRID_W]
        for i in range(NA_TILE_ROWS):
            qr = t * NA_TILE_ROWS + i
            r0 = min(max(qr - WIN_H // 2, 0), rows - WIN_H)
            for j in range(NA_WIN_ROWS // 2):
                kr = w0 + 2 * j
                e = kr - qr + WIN_H
                ok0 = r0 <= kr < r0 + WIN_H
                ok1 = r0 <= kr + 1 < r0 + WIN_H
                line.append(e if ok0 and ok1 else 2 * WIN_H - 1 + e if ok0 else 4 * WIN_H - 1 + e if ok1 else 0)
        table.append(line)
    return jnp.array(table, jnp.int32)


def _na_window(win_ref, t):
    n_pairs = NA_WIN_ROWS // 2
    entry = [[win_ref[t, 1 + i * n_pairs + j] for j in range(n_pairs)] for i in range(NA_TILE_ROWS)]
    return pl.multiple_of(win_ref[t, 0], 4 * GRID_W), entry


def _lane_fold(x, op):
    return functools.reduce(op, [x[:, c:c + LANES] for c in range(0, x.shape[1], LANES)])


def _na_biased(sl, hh, entry, j0, tb_ref, s_ref, c0):
    tiles = []
    for i in range(NA_TILE_ROWS):
        r = slice(i * GRID_W, (i + 1) * GRID_W)
        mx = None
        for jj in range(sl.shape[1] // LANES):
            blk = sl[r, jj * LANES:(jj + 1) * LANES] + tb_ref[0, hh, entry[i][j0 + jj]]
            s_ref[r, c0 + jj * LANES:c0 + (jj + 1) * LANES] = blk
            mx = blk if mx is None else jnp.maximum(mx, blk)
        tiles.append(mx)
    return jnp.concatenate(tiles, axis=0)


def _na_unit(make, use, kc_ref, vc_ref, kl_ref, vl_ref, tb_ref):
    q, hh_m, tok_m, entry, (mc_ref, ml_ref, mm_ref) = make
    hh_u, tok_u, (uc_ref, ul_ref, um_ref) = use
    qh = _na_head(q, hh_m)
    m = um_ref[:, 0:1]
    run_max = l = o = None
    for c0 in [None] + list(range(0, NA_WIN_ROWS * GRID_W, NA_KEY_CHUNK)):
        if c0 is None:
            sc = _dot_nt(qh, kc_ref[0])
            mc_ref[...] = sc
            run_max = _lane_fold(sc, jnp.maximum)
            p = jnp.exp2(uc_ref[...] - m)
            v = vc_ref[0]
        else:
            sl = _dot_nt(qh, kl_ref[0, pl.ds(tok_m + c0, NA_KEY_CHUNK), :])
            run_max = jnp.maximum(run_max, _na_biased(sl, hh_m, entry, c0 // LANES, tb_ref, ml_ref, c0))
            p = jnp.exp2(ul_ref[:, c0:c0 + NA_KEY_CHUNK] - m)
            v = vl_ref[0, pl.ds(tok_u + c0, NA_KEY_CHUNK), :]
        r = _lane_fold(p, jnp.add)
        part = _dot(p.astype(bf16), v)
        l = r if l is None else l + r
        o = part if o is None else o + part
    mm_ref[...] = jnp.broadcast_to(jnp.max(run_max, axis=-1, keepdims=True), mm_ref.shape)
    return o / jnp.sum(l, axis=-1, keepdims=True)


def _na_kernel(win_ref, q_ref, qn_ref, kc_ref, vc_ref, kl_ref, vl_ref, tb_ref, o_ref, *scratch, n_tiles):
    slot_a, slot_b = scratch[:3], scratch[3:]
    t = pl.program_id(2)
    tok_t, entry_t = _na_window(win_ref, t)
    tok_n, entry_n = _na_window(win_ref, jnp.minimum(t + 1, n_tiles - 1))
    shared = (kc_ref, vc_ref, kl_ref, vl_ref, tb_ref)

    @pl.when(t == 0)
    def _():
        qh = _na_head(q_ref[0], 0)
        sc = _dot_nt(qh, kc_ref[0])
        slot_a[0][...] = sc
        sl = _dot_nt(qh, kl_ref[0, pl.ds(tok_t, NA_WIN_ROWS * GRID_W), :])
        run_max = jnp.maximum(_lane_fold(sc, jnp.maximum), _na_biased(sl, 0, entry_t, 0, tb_ref, slot_a[1], 0))
        slot_a[2][...] = jnp.broadcast_to(jnp.max(run_max, axis=-1, keepdims=True), slot_a[2].shape)

    o0 = _na_unit((q_ref[0], 1, tok_t, entry_t, slot_b), (0, tok_t, slot_a), *shared)
    o1 = _na_unit((qn_ref[0], 0, tok_n, entry_n, slot_a), (1, tok_t, slot_b), *shared)
    lane = lax.broadcasted_iota(jnp.int32, o0.shape, 1)
    o_ref[0] = jnp.where(lane < NA_HEAD_DIM, o0, o1).astype(o_ref.dtype)


def _na_bias_blocks(rpb):
    H, n_dr, n_dc = rpb.shape
    col = jnp.arange(GRID_W)
    c0 = jnp.clip(col - WIN_W // 2, 0, GRID_W - WIN_W)
    col_ok = (col[None, :] >= c0[:, None]) & (col[None, :] < c0[:, None] + WIN_W)
    dc = jnp.clip(col[None, :] - col[:, None] + WIN_W - 1, 0, n_dc - 1)
    onehot = (dc.reshape(1, -1) == jnp.arange(n_dc)[:, None]).astype(f32)
    blk = jnp.einsum('hrj,jx->hrx', rpb.astype(f32) * math.log2(math.e), onehot, precision=lax.Precision.HIGHEST)
    blk = jnp.where(col_ok[None, None], blk.reshape(H, n_dr, GRID_W, GRID_W), NEG_INF)
    blk = jnp.concatenate([jnp.full((H, 1, GRID_W, GRID_W), NEG_INF, f32), blk], axis=1)
    n_e = 2 * WIN_H - 1
    left = [0] + list(range(1, n_e)) + [0] + list(range(1, n_e + 1)) + [0] * n_e
    right = [0] + list(range(2, n_e + 1)) + [0] + [0] * n_e + list(range(1, n_e + 1))
    tab = jnp.concatenate([jnp.take(blk, jnp.array(left), axis=1), jnp.take(blk, jnp.array(right), axis=1)], axis=-1)
    return tab.reshape((H // 2, 2) + tab.shape[1:])


def _na_attention(qkv_q, qkv_ctx, qkv_lat, bias):
    B, T, n3 = qkv_q.shape
    D = n3 // 3
    HP = D // LANES
    L = qkv_ctx.shape[1]
    ctx_k = pl.BlockSpec((1, L, LANES), lambda b, h, t: (b, 0, HP + h))
    ctx_v = pl.BlockSpec((1, L, LANES), lambda b, h, t: (b, 0, 2 * HP + h))
    if qkv_lat is None:
        tq = min(_tok_tile(T), 256)
        tile = pl.BlockSpec((1, tq, LANES), lambda b, h, t: (b, t, h))
        return pl.pallas_call(
            _na_ctx_kernel,
            grid=(B, HP, T // tq),
            in_specs=[tile, ctx_k, ctx_v],
            out_specs=tile,
            out_shape=jax.ShapeDtypeStruct((B, T, D), bf16),
            compiler_params=_cparams("parallel", "parallel", "parallel"),
            name="na_attention_ctx",
        )(qkv_q, qkv_ctx, qkv_ctx)
    tq = NA_TILE_ROWS * GRID_W
    n_tiles = T // tq
    n_win = NA_WIN_ROWS * GRID_W
    assert T % tq == 0 and T // GRID_W >= NA_WIN_ROWS and bias.shape[2] == 6 * WIN_H - 2
    tile = pl.BlockSpec((1, tq, LANES), lambda b, h, t: (b, t, h))
    slot = [pltpu.VMEM((tq, L), f32), pltpu.VMEM((tq, n_win), f32), pltpu.VMEM((tq, LANES), f32)]
    return pl.pallas_call(
        functools.partial(_na_kernel, n_tiles=n_tiles),
        grid=(B, HP, n_tiles),
        in_specs=[pl.BlockSpec(memory_space=pltpu.SMEM), tile,
                  pl.BlockSpec((1, tq, LANES), lambda b, h, t: (b, jnp.minimum(t + 1, n_tiles - 1), h)),
                  ctx_k, ctx_v,
                  pl.BlockSpec((1, T, LANES), lambda b, h, t: (b, 0, HP + h)),
                  pl.BlockSpec((1, T, LANES), lambda b, h, t: (b, 0, 2 * HP + h)),
                  pl.BlockSpec((1,) + bias.shape[1:], lambda b, h, t: (h, 0, 0, 0, 0))],
        out_specs=tile,
        out_shape=jax.ShapeDtypeStruct((B, T, D), bf16),
        scratch_shapes=slot + slot,
        compiler_params=_cparams("parallel", "parallel", "arbitrary"),
        name="na_attention",
    )(_na_window_table(T // GRID_W), qkv_q, qkv_q, qkv_ctx, qkv_ctx, qkv_lat, qkv_lat, bias)


def _chunk_cumsum(x, reverse):
    n = x.shape[0]
    pos = lax.broadcasted_iota(jnp.int32, x.shape, 0) & (HG_CHUNK - 1)
    k = 1
    while k < HG_CHUNK:
        if reverse:
            x = x + jnp.where(pos < HG_CHUNK - k, pltpu.roll(x, n - k, 0), 0.0)
        else:
            x = x + jnp.where(pos >= k, pltpu.roll(x, k, 0), 0.0)
        k *= 2
    return x


def _hg_block(z, v, q, lb, st, reverse, with_out):
    n = z.shape[0]
    nch = n // HG_CHUNK
    f = lb + (1 - lb) * jax.nn.sigmoid(z)
    lf = jnp.log(f)
    kk = 1 - f
    g = _chunk_cumsum(lf, reverse)
    out = None
    if with_out:
        ri = lax.broadcasted_iota(jnp.int32, (n, n), 0)
        ci = lax.broadcasted_iota(jnp.int32, (n, n), 1)
        shift = HG_CHUNK.bit_length() - 1
        tri = ((ri >> shift) == (ci >> shift)) & ((ci >= ri) if reverse else (ci <= ri))
        q_dec = q * jnp.exp(g)
        k_inv = kk * jnp.exp(-g)
        att = jnp.where(tri, _dot_nt(q_dec.astype(bf16), k_inv.astype(bf16)), 0.0)
        out_intra = _dot(att.astype(bf16), v.astype(bf16))
        outs = [None] * nch
    order = range(nch - 1, -1, -1) if reverse else range(nch)
    for c in order:
        sl = slice(c * HG_CHUNK, (c + 1) * HG_CHUNK)
        end = c * HG_CHUNK if reverse else (c + 1) * HG_CHUNK - 1
        g_tot = g[end:end + 1]
        if with_out:
            outs[c] = _dot_nt(q_dec[sl].astype(bf16), st.astype(bf16))
        kdec = kk[sl] * jnp.exp(g_tot - g[sl])
        ds_t = _dot(v[sl].T.astype(bf16), kdec.astype(bf16))
        st = st * jnp.exp(g_tot) + ds_t
    if with_out:
        out = out_intra + jnp.concatenate(outs, axis=0)
    return out, st


def _hg_kernel(q_ref, i_ref, gate_ref, zf_ref, zb_ref, ic_ref, zfc_ref, zbc_ref, lb_ref, gn_ref, o_ref,
               accf_ref, accb_ref, *, s_tok):
    lb = lb_ref[...]
    nblk = s_tok // HG_BLOCK
    zero = jnp.zeros((HG_EXPAND, HG_EXPAND), f32)
    _, st_f = _hg_block(zfc_ref[0], ic_ref[0], None, lb, zero, False, False)
    _, st_b = _hg_block(zbc_ref[0], ic_ref[0], None, lb, zero, True, False)

    def body(n, carry):
        st_f, st_b = carry
        rows_f = pl.ds(pl.multiple_of(n * HG_BLOCK, HG_BLOCK), HG_BLOCK)
        rows_b = pl.ds(pl.multiple_of((nblk - 1 - n) * HG_BLOCK, HG_BLOCK), HG_BLOCK)
        out_f, st_f = _hg_block(zf_ref[0, rows_f, :], i_ref[0, rows_f, :], q_ref[0, rows_f, :], lb, st_f, False, True)
        out_b, st_b = _hg_block(zb_ref[0, rows_b, :], i_ref[0, rows_b, :], q_ref[0, rows_b, :], lb, st_b, True, True)
        accf_ref[rows_f, :] = out_f
        accb_ref[rows_b, :] = out_b
        return st_f, st_b

    lax.fori_loop(0, nblk, body, (st_f, st_b), unroll=2 if nblk % 2 == 0 else 1)
    gate = gate_ref[0]
    o = accf_ref[...] + accb_ref[...]
    o_ref[0] = (_rms(o, gn_ref[...]) * (gate * jax.nn.sigmoid(gate))).astype(o_ref.dtype)


def _hgrn2_core(p_lat, p_ctx, lb, gn_g):
    B, S, n5 = p_lat.shape
    D = n5 // 5
    H = D // HG_EXPAND
    L = p_ctx.shape[1]
    assert S % HG_BLOCK == 0 and L % HG_CHUNK == 0

    def lat(k):
        return pl.BlockSpec((1, S, HG_EXPAND), lambda b, h: (b, 0, k * H + h))

    def ctx(k):
        return pl.BlockSpec((1, L, HG_EXPAND), lambda b, h: (b, 0, k * H + h))

    vec = pl.BlockSpec((1, HG_EXPAND), lambda b, h: (0, h))
    return pl.pallas_call(
        functools.partial(_hg_kernel, s_tok=S),
        grid=(B, H),
        in_specs=[lat(0), lat(1), lat(2), lat(3), lat(4), ctx(1), ctx(3), ctx(4), vec, _full((1, HG_EXPAND))],
        out_specs=pl.BlockSpec((1, S, HG_EXPAND), lambda b, h: (b, 0, h)),
        out_shape=jax.ShapeDtypeStruct((B, S, D), bf16),
        scratch_shapes=[pltpu.VMEM((S, HG_EXPAND), f32), pltpu.VMEM((S, HG_EXPAND), f32)],
        compiler_params=_cparams("parallel", "parallel"),
        name="hgrn2_core",
    )(p_lat, p_lat, p_lat, p_lat, p_lat, p_ctx, p_ctx, p_ctx, lb.reshape(1, D), gn_g.reshape(1, HG_EXPAND))


def kernel(x, c, ctx, c_ctx, w_ada, b_ada, g_pre, g_post, w_ff1, w_ff3, w_ff2, s5_a_re, s5_a_im, s5_log_dt, s5_b_re, s5_b_im, s5_c_re, s5_c_im, s5_d, s5_w_glu, s5_b_glu, da_w_qkv, da_w_o, da_lam_q1, da_lam_k1, da_lam_q2, da_lam_k2, da_subln, na_w_qkv, na_w_o, na_rpb, hg_w_qig, hg_w_f, hg_b_f, hg_lb_logits, hg_gnorm, hg_w_o):
    B, S, D = x.shape
    depth = w_ada.shape[0]
    n_mix = 4
    rows_pad = -(-(B + 1) // 8) * 8
    c_all = jnp.concatenate([c, c_ctx[None], jnp.zeros((rows_pad - B - 1, D), f32)], axis=0)
    mods = _ada(c_all, w_ada, b_ada)
    lb_p = jax.nn.softmax(hg_lb_logits.astype(f32), axis=0)
    lower_bounds = jnp.cumsum(lb_p, axis=0) - lb_p[0]
    w1b, w3b, w2b = w_ff1.astype(bf16), w_ff3.astype(bf16), w_ff2.astype(bf16)

    x_lat, x_ctx = x, ctx
    for i in range(depth):
        last = i == depth - 1
        occ, kind = i // n_mix, i % n_mix
        m_lat = mods[i, :B].reshape(B, 3 * N_SUB, D)
        m_ctx = mods[i, B:B + 1].reshape(1, 3 * N_SUB, D)
        ffn1 = functools.partial(_half_ffn, g_in=g_pre[i, 0], g_out=g_post[i, 0], w1=w1b[i, 0], w3=w3b[i, 0],
                                 w2=w2b[i, 0], j=0)
        ffn2 = functools.partial(_half_ffn, g_in=g_pre[i, 2], g_out=g_post[i, 2], w1=w1b[i, 1], w3=w3b[i, 1],
                                 w2=w2b[i, 1], j=2)
        x_lat, x_ctx = ffn1(x_lat, m_lat), ffn1(x_ctx, m_ctx)

        y_ctx = None
        if kind == 0:
            h_lat = _prenorm(x_lat, m_lat, g_pre[i, 1], 1)
            h_ctx = _prenorm(x_ctx, m_ctx, g_pre[i, 1], 1)
            y_lat, y_ctx = _s5_scan(h_lat, h_ctx, s5_a_re[occ], s5_a_im[occ], s5_log_dt[occ], s5_b_re[occ],
                                    s5_b_im[occ], s5_c_re[occ], s5_c_im[occ])
            mix = "glu"
            mix_head = (g_pre[i, 1], s5_d[occ])
            mix_tail = (g_post[i, 1], s5_w_glu[occ].astype(bf16), s5_b_glu[occ])
        elif kind == 1:
            wq = da_w_qkv[occ].astype(bf16)
            rope = _rope_tables(S, DIFF_HEAD_DIM)
            qkv_lat = _prenorm_proj(x_lat, m_lat, g_pre[i, 1], 1, wq, rope=rope, n_rope=2 * D)
            qkv_ctx = _prenorm_proj(x_ctx, m_ctx, g_pre[i, 1], 1, wq)
            lamv = jnp.stack([da_lam_q1[occ], da_lam_k1[occ], da_lam_q2[occ], da_lam_k2[occ]]).astype(f32)
            lam_init = 0.8 - 0.6 * math.exp(-0.3 * i)
            y_lat = _diff_attention(qkv_lat, qkv_ctx, qkv_lat, lamv, da_subln[occ], lam_init)
            if not last:
                y_ctx = _diff_attention(qkv_ctx, qkv_ctx, None, lamv, da_subln[occ], lam_init)
            mix, mix_head, mix_tail = "outproj", (), (g_post[i, 1], da_w_o[occ].astype(bf16))
        elif kind == 2:
            wq = na_w_qkv[occ].astype(bf16)
            qkv_lat = _prenorm_proj(x_lat, m_lat, g_pre[i, 1], 1, wq)
            qkv_ctx = _prenorm_proj(x_ctx, m_ctx, g_pre[i, 1], 1, wq)
            y_lat = _na_attention(qkv_lat, qkv_ctx, qkv_lat, _na_bias_blocks(na_rpb[occ]))
            if not last:
                y_ctx = _na_attention(qkv_ctx, qkv_ctx, None, None)
            mix, mix_head, mix_tail = "outproj", (), (g_post[i, 1], na_w_o[occ].astype(bf16))
        else:
            assert last, "HGRN2 context outputs are not needed when it is the last layer"
            wp = jnp.concatenate([hg_w_qig[occ], hg_w_f[occ, 0], hg_w_f[occ, 1]], axis=1).astype(bf16)
            bp = jnp.concatenate([jnp.zeros((3 * D,), f32), hg_b_f[occ, 0], hg_b_f[occ, 1]])
            p_lat = _prenorm_proj(x_lat, m_lat, g_pre[i, 1], 1, wp, bias=bp, out_dtype=f32)
            p_ctx = _prenorm_proj(x_ctx, m_ctx, g_pre[i, 1], 1, wp, bias=bp, out_dtype=f32)
            y_lat = _hgrn2_core(p_lat, p_ctx, lower_bounds[i], hg_gnorm[occ])
            mix, mix_head, mix_tail = "outproj", (), (g_post[i, 1], hg_w_o[occ].astype(bf16))

        x_lat = ffn2(x_lat, m_lat, mix=mix, mix_args=(y_lat,) + mix_head + mix_tail)
        if not last:
            x_ctx = ffn2(x_ctx, m_ctx, mix=mix, mix_args=(y_ctx,) + mix_head + mix_tail)
    return x_lat
```

```python
import functools
import math

import jax
import jax.numpy as jnp
from jax import lax
from jax.experimental import pallas as pl
from jax.experimental.pallas import tpu as pltpu

f32 = jnp.float32
bf16 = jnp.bfloat16

N_SUB = 3
RMS_EPS = 1e-6
NEG_INF = -1e30
ROPE_BASE = 10000.0
GRID_W = 64
S5_GROUP = 16
S5_STATE = 64
S5_CHUNK = 16
DIFF_HEAD_DIM = 64
DIFF_KEY_CHUNK = 1024
NA_HEAD_DIM = 64
WIN_H = 8
WIN_W = 16
NA_TILE_ROWS = 8
NA_WIN_ROWS = 16
NA_KEY_CHUNK = 512
HG_EXPAND = 128
HG_CHUNK = 64
HG_BLOCK = 256
LANES = 128
VMEM_LIMIT = 56 * 1024 * 1024


def _cparams(*sem):
    return pltpu.CompilerParams(dimension_semantics=sem, vmem_limit_bytes=VMEM_LIMIT)


def _dot(a, b):
    return jnp.dot(a, b, preferred_element_type=f32)


def _dot_nt(a, b):
    return lax.dot_general(a, b, (((1,), (1,)), ((), ())), preferred_element_type=f32)


def _rms(x, g):
    return x * lax.rsqrt(jnp.mean(jnp.square(x), axis=-1, keepdims=True) + RMS_EPS) * g


def _pre_norm(x, g, mod_ref, j):
    shift = mod_ref[0, 3 * j:3 * j + 1, :]
    scale = mod_ref[0, 3 * j + 1:3 * j + 2, :]
    return _rms(x, g) * (1 + scale) + shift


def _post_residual(x, y, g, mod_ref, j, weight):
    gate = mod_ref[0, 3 * j + 2:3 * j + 3, :]
    return x + weight * gate * _rms(y, g)


def _tok_tile(t):
    for tm in (512, 256, 128):
        if t % tm == 0:
            return tm
    raise ValueError(f"token count {t} is not a multiple of 128")


def _full(shape):
    return pl.BlockSpec(shape, lambda *_: (0,) * len(shape))


def _mod_spec(mod):
    nd = mod.shape[1]
    d = mod.shape[2]
    if mod.shape[0] == 1:
        return pl.BlockSpec((1, nd, d), lambda b, t: (0, 0, 0))
    return pl.BlockSpec((1, nd, d), lambda b, t: (b, 0, 0))


def _ada_kernel(c_ref, w_ref, b_ref, o_ref):
    c = c_ref[...]
    sc = (c * jax.nn.sigmoid(c)).astype(bf16)
    o_ref[0] = _dot(sc, w_ref[0].astype(bf16)) + b_ref[0]


def _ada(c_all, w_ada, b_ada):
    depth, d, n = w_ada.shape
    rows = c_all.shape[0]
    tn = n // 4
    return pl.pallas_call(
        _ada_kernel,
        grid=(depth, n // tn),
        in_specs=[pl.BlockSpec((rows, d), lambda i, t: (0, 0)),
                  pl.BlockSpec((1, d, tn), lambda i, t: (i, 0, t)),
                  pl.BlockSpec((1, 1, tn), lambda i, t: (i, 0, t))],
        out_specs=pl.BlockSpec((1, rows, tn), lambda i, t: (i, 0, t)),
        out_shape=jax.ShapeDtypeStruct((depth, rows, n), f32),
        compiler_params=_cparams("arbitrary", "arbitrary"),
        name="ada",
    )(c_all, w_ada, b_ada.reshape(depth, 1, n))


def _ffn_kernel(x_ref, mod_ref, gin_ref, gout_ref, w1_ref, w3_ref, w2_ref, *rest, j, chunks, mix):
    o_ref = rest[-1]
    x = x_ref[0]
    if mix == "outproj":
        y_ref, gmix_ref, wo_ref = rest[:-1]
        x = _post_residual(x, _dot(y_ref[0], wo_ref[...]), gmix_ref[...], mod_ref, 1, 1.0)
    elif mix == "glu":
        y_ref, gpre_ref, d_ref, gmix_ref, wg_ref, bg_ref = rest[:-1]
        z = jax.nn.gelu(d_ref[...] * _pre_norm(x, gpre_ref[...], mod_ref, 1) + y_ref[0])
        u = _dot(z.astype(bf16), wg_ref[...]) + bg_ref[...]
        x = _post_residual(x, z * jax.nn.sigmoid(u), gmix_ref[...], mod_ref, 1, 1.0)
    h = _pre_norm(x, gin_ref[...], mod_ref, j).astype(bf16)
    acc = None
    for s, n in chunks:
        a = _dot(h, w1_ref[:, s:s + n])
        b = _dot(h, w3_ref[:, s:s + n])
        g = (a * jax.nn.sigmoid(a) * b).astype(bf16)
        y = _dot(g, w2_ref[s:s + n, :])
        acc = y if acc is None else acc + y
    o_ref[0] = _post_residual(x, acc, gout_ref[...], mod_ref, j, 0.5)


def _ff_chunks(dff, step=1024):
    out, s = [], 0
    while s < dff:
        n = min(step, dff - s)
        out.append((s, n))
        s += n
    return tuple(out)


def _half_ffn(x, mod, g_in, g_out, w1, w3, w2, j, mix=None, mix_args=()):
    b, t, d = x.shape
    dff = w1.shape[1]
    tm = _tok_tile(t)
    tok = pl.BlockSpec((1, tm, d), lambda bb, tt: (bb, tt, 0))
    vec = _full((1, d))
    args = [x, mod, g_in.reshape(1, d), g_out.reshape(1, d), w1, w3, w2]
    specs = [tok, _mod_spec(mod), vec, vec, _full((d, dff)), _full((d, dff)), _full((dff, d))]
    if mix == "outproj":
        y, g_mix, w_o = mix_args
        args += [y, g_mix.reshape(1, d), w_o]
        specs += [tok, vec, _full((d, d))]
    elif mix == "glu":
        y, g_pre, d_skip, g_mix, w_g, b_g = mix_args
        args += [y, g_pre.reshape(1, d), d_skip.astype(f32).reshape(1, d), g_mix.reshape(1, d), w_g, b_g.reshape(1, d)]
        specs += [tok, vec, vec, vec, _full((d, d)), vec]
    return pl.pallas_call(
        functools.partial(_ffn_kernel, j=j, chunks=_ff_chunks(dff), mix=mix),
        grid=(b, t // tm),
        in_specs=specs,
        out_specs=tok,
        out_shape=jax.ShapeDtypeStruct((b, t, d), f32),
        compiler_params=_cparams("parallel", "parallel"),
        name="half_ffn",
    )(*args)


def _prenorm_kernel(x_ref, mod_ref, g_ref, o_ref, *, j):
    o_ref[0] = _pre_norm(x_ref[0], g_ref[...], mod_ref, j).astype(o_ref.dtype)


def _prenorm(x, mod, g, j):
    b, t, d = x.shape
    tm = _tok_tile(t)
    return pl.pallas_call(
        functools.partial(_prenorm_kernel, j=j),
        grid=(b, t // tm),
        in_specs=[pl.BlockSpec((1, tm, d), lambda bb, tt: (bb, tt, 0)), _mod_spec(mod), _full((1, d))],
        out_specs=pl.BlockSpec((1, tm, d), lambda bb, tt: (bb, tt, 0)),
        out_shape=jax.ShapeDtypeStruct((b, t, d), bf16),
        compiler_params=_cparams("parallel", "parallel"),
        name="prenorm",
    )(x, mod, g.reshape(1, d))


def _swap_pairs(x):
    lane = lax.broadcasted_iota(jnp.int32, x.shape, 1)
    nxt = pltpu.roll(x, LANES - 1, 1)
    prv = pltpu.roll(x, 1, 1)
    return jnp.where((lane & 1) == 0, nxt, prv)


def _proj_kernel(*refs, j, n_out, n_rope, has_bias, step):
    x_ref, mod_ref, g_ref, w_ref = refs[:4]
    k = 4
    b_ref = None
    if has_bias:
        b_ref = refs[k]
        k += 1
    if n_rope:
        cos_ref, sin_ref = refs[k], refs[k + 1]
        k += 2
    o_ref = refs[k]
    h = _pre_norm(x_ref[0], g_ref[...], mod_ref, j).astype(bf16)
    for s in range(0, n_out, step):
        y = _dot(h, w_ref[:, s:s + step])
        if has_bias:
            y = y + b_ref[:, s:s + step]
        if s < n_rope:
            cos = cos_ref[...]
            sin = sin_ref[...]
            parts = []
            for c in range(0, step, LANES):
                yc = y[:, c:c + LANES]
                parts.append(yc * cos + _swap_pairs(yc) * sin)
            y = jnp.concatenate(parts, axis=1)
        o_ref[0, :, s:s + step] = y.astype(o_ref.dtype)


def _prenorm_proj(x, mod, g, j, w, bias=None, rope=None, n_rope=0, out_dtype=bf16):
    b, t, d = x.shape
    n = w.shape[1]
    tm = min(_tok_tile(t), 256)
    step = 512
    assert n % step == 0 and n_rope % step == 0
    args = [x, mod, g.reshape(1, d), w]
    specs = [pl.BlockSpec((1, tm, d), lambda bb, tt: (bb, tt, 0)), _mod_spec(mod), _full((1, d)), _full((d, n))]
    if bias is not None:
        args.append(bias.reshape(1, n))
        specs.append(_full((1, n)))
    if n_rope:
        args += list(rope)
        specs += [pl.BlockSpec((tm, LANES), lambda bb, tt: (tt, 0))] * 2
    return pl.pallas_call(
        functools.partial(_proj_kernel, j=j, n_out=n, n_rope=n_rope, has_bias=bias is not None, step=step),
        grid=(b, t // tm),
        in_specs=specs,
        out_specs=pl.BlockSpec((1, tm, n), lambda bb, tt: (bb, tt, 0)),
        out_shape=jax.ShapeDtypeStruct((b, t, n), out_dtype),
        compiler_params=_cparams("parallel", "parallel"),
        name="prenorm_proj",
    )(*args)


def _s5_kernel(xc_ref, xl_ref, wb_ref, wk_ref, wc_ref, lam_ref, yc_ref, yl_ref, s_ref, hin_ref, *, nb, nc_ctx, nc):
    rc = nc_ctx * nb
    xc, xl = xc_ref[0], xl_ref[0]
    s_ref[0:rc, :] = _dot(xc, wb_ref[0])
    s_ref[rc:, :] = _dot(xl, wb_ref[0])
    P = S5_STATE
    lam = lam_ref[0]
    a_re, a_im = lam[0:1], lam[1:2]
    fwd_lane = lax.broadcasted_iota(jnp.int32, (nb, 2 * P), 1) < P

    def step(n, carry):
        h_re, h_im = carry
        r_f = pl.multiple_of(n * nb, nb)
        n_b = jnp.where(n < nc_ctx, nc_ctx - 1 - n, nc - 1 - (n - nc_ctx))
        r_b = pl.multiple_of(n_b * nb, nb)
        rows_f, rows_b = pl.ds(r_f, nb), pl.ds(r_b, nb)
        hin_ref[rows_f, 0:P] = h_re[:, 0:P]
        hin_ref[rows_b, P:2 * P] = h_re[:, P:2 * P]
        hin_ref[rows_f, 2 * P:3 * P] = h_im[:, 0:P]
        hin_ref[rows_b, 3 * P:4 * P] = h_im[:, P:2 * P]
        s_re = jnp.where(fwd_lane, s_ref[rows_f, 0:2 * P], s_ref[rows_b, 0:2 * P])
        s_im = jnp.where(fwd_lane, s_ref[rows_f, 2 * P:4 * P], s_ref[rows_b, 2 * P:4 * P])
        return a_re * h_re - a_im * h_im + s_re, a_re * h_im + a_im * h_re + s_im

    zero = jnp.zeros((nb, 2 * P), f32)
    lax.fori_loop(0, nc, step, (zero, zero), unroll=4 if nc % 4 == 0 else 1)
    yc_ref[0] = _dot(xc, wk_ref[0]) + _dot(hin_ref[0:rc, :].astype(bf16), wc_ref[0])
    yl_ref[0] = _dot(xl, wk_ref[0]) + _dot(hin_ref[rc:, :].astype(bf16), wc_ref[0])

def _s5_weights(a_re, a_im, log_dt, b_re, b_im, c_re, c_im):
    hp = lax.Precision.HIGHEST
    C = S5_CHUNK
    G, P = a_re.shape[1:]
    N = b_re.shape[-1]
    j = jnp.arange(C + 1, dtype=f32)
    wbs, wks, wcs, lams = [], [], [], []
    for dirn in range(2):
        are, aim = a_re[dirn].astype(f32), a_im[dirn].astype(f32)
        dt = jnp.exp(log_dt[dirn].astype(f32))[:, None]
        pw_mag = jnp.exp(j[:, None, None] * (are * dt)[None])
        pw_re = pw_mag * jnp.cos(j[:, None, None] * (aim * dt)[None])
        pw_im = pw_mag * jnp.sin(j[:, None, None] * (aim * dt)[None])
        nr, ni = pw_re[1] - 1.0, pw_im[1]
        den = are * are + aim * aim
        fr = (nr * are + ni * aim) / den
        fi = (ni * are - nr * aim) / den
        bre, bim = b_re[dirn].astype(f32), b_im[dirn].astype(f32)
        bbr = fr[..., None] * bre - fi[..., None] * bim
        bbi = fr[..., None] * bim + fi[..., None] * bre
        cre, cim = c_re[dirn].astype(f32), c_im[dirn].astype(f32)
        dr, di = (pw_re[:C][::-1], pw_im[:C][::-1]) if dirn == 0 else (pw_re[:C], pw_im[:C])
        sr = dr[..., None] * bbr[None] - di[..., None] * bbi[None]
        si = dr[..., None] * bbi[None] + di[..., None] * bbr[None]
        wb = jnp.concatenate([sr, si], axis=2)
        wbs.append(wb.transpose(1, 0, 3, 2).reshape(G, C * N, 2 * P))
        pr, pi = pw_re[1:], pw_im[1:]
        if dirn == 1:
            pr, pi = pr[::-1], pi[::-1]
        or_ = cre[None] * pr[:, :, None, :] - cim[None] * pi[:, :, None, :]
        oi_ = -(cre[None] * pi[:, :, None, :] + cim[None] * pr[:, :, None, :])
        wc = jnp.concatenate([or_, oi_], axis=3)
        wcs.append(wc.transpose(1, 3, 0, 2).reshape(G, 2 * P, C * N))
        clr = cre[None] * pw_re[:C, :, None, :] - cim[None] * pw_im[:C, :, None, :]
        cli = cre[None] * pw_im[:C, :, None, :] + cim[None] * pw_re[:C, :, None, :]
        kk = (jnp.einsum('jgnp,gpm->gjnm', clr, bbr, precision=hp)
              - jnp.einsum('jgnp,gpm->gjnm', cli, bbi, precision=hp))
        s_idx = jnp.arange(C)[:, None]
        t_idx = jnp.arange(C)[None, :]
        lag = (t_idx - s_idx) if dirn == 0 else (s_idx - t_idx)
        onehot = (lag[:, :, None] == jnp.arange(C)[None, None, :]).astype(f32)
        kt = jnp.einsum('stj,gjnm->gsmtn', onehot, kk, precision=hp)
        wks.append(kt.reshape(G, C * N, C * N))
        lams.append((pw_re[C], pw_im[C]))
    wb = jnp.concatenate([wbs[0][..., :P], wbs[1][..., :P], wbs[0][..., P:], wbs[1][..., P:]], axis=2).astype(bf16)
    wc = jnp.concatenate([wcs[0][:, :P], wcs[1][:, :P], wcs[0][:, P:], wcs[1][:, P:]], axis=1).astype(bf16)
    wk = (wks[0] + wks[1]).astype(bf16)
    lam = jnp.stack([jnp.concatenate([lams[0][0], lams[1][0]], axis=-1),
                     jnp.concatenate([lams[0][1], lams[1][1]], axis=-1)], axis=1)
    return wb, wk, wc, lam


def _s5_scan(h_lat, h_ctx, a_re, a_im, log_dt, b_re, b_im, c_re, c_im):
    B, S, D = h_lat.shape
    L = h_ctx.shape[1]
    C, N, P = S5_CHUNK, S5_GROUP, S5_STATE
    G = D // N
    nc_ctx, nc = L // C, (L + S) // C
    assert 2 * P == LANES and L % C == 0 and S % C == 0
    wb, wk, wc, lam = _s5_weights(a_re, a_im, log_dt, b_re, b_im, c_re, c_im)

    def to_groups(h):
        n = h.shape[1] // C
        return h.reshape(B, n, C, G, N).transpose(3, 1, 0, 2, 4).reshape(G, n * B, C * N)

    def from_groups(y):
        n = y.shape[1] // B
        return y.reshape(G, n, B, C, N).transpose(2, 1, 3, 0, 4).reshape(B, n * C, D)

    rc, rl = nc_ctx * B, (nc - nc_ctx) * B

    def rows(r):
        return pl.BlockSpec((1, r, C * N), lambda g: (g, 0, 0))

    yc, yl = pl.pallas_call(
        functools.partial(_s5_kernel, nb=B, nc_ctx=nc_ctx, nc=nc),
        grid=(G,),
        in_specs=[rows(rc), rows(rl),
                  pl.BlockSpec((1, C * N, 4 * P), lambda g: (g, 0, 0)),
                  pl.BlockSpec((1, C * N, C * N), lambda g: (g, 0, 0)),
                  pl.BlockSpec((1, 4 * P, C * N), lambda g: (g, 0, 0)),
                  pl.BlockSpec((1, 2, 2 * P), lambda g: (g, 0, 0))],
        out_specs=[rows(rc), rows(rl)],
        out_shape=[jax.ShapeDtypeStruct((G, rc, C * N), f32), jax.ShapeDtypeStruct((G, rl, C * N), f32)],
        scratch_shapes=[pltpu.VMEM((rc + rl, 4 * P), f32), pltpu.VMEM((rc + rl, 4 * P), f32)],
        compiler_params=_cparams("parallel"),
        name="s5_scan",
    )(to_groups(h_ctx), to_groups(h_lat), wb, wk, wc, lam)
    return from_groups(yl), from_groups(yc)


def _diff_maps(q):
    lane = lax.broadcasted_iota(jnp.int32, q.shape, 1)
    qs = (q.astype(f32) * (DIFF_HEAD_DIM ** -0.5 * math.log2(math.e))).astype(bf16)
    zero = jnp.zeros_like(qs)
    return jnp.where(lane < DIFF_HEAD_DIM, qs, zero), jnp.where(lane >= DIFF_HEAD_DIM, qs, zero)


def _key_chunks(k_refs):
    out = []
    for seg, k_ref in enumerate(k_refs):
        n = k_ref.shape[1]
        out += [(seg, c, min(c + DIFF_KEY_CHUNK, n)) for c in range(0, n, DIFF_KEY_CHUNK)]
    return out


def _diff_step(q_next, cur, nxt, k_refs, v_refs, lamv_ref, g_ref, o_ref, lam_init):
    chunks = _key_chunks(k_refs)
    cur_s, cur_m = cur
    tq = cur_m.shape[1]
    m1, m2 = cur_m[0, :, 0:1], cur_m[1, :, 0:1]
    if q_next is not None:
        q1n, q2n = _diff_maps(q_next)
        nxt_s, nxt_m = nxt
    n1 = n2 = l1 = l2 = None
    for seg, a, b in chunks:
        if q_next is not None:
            s1 = _dot_nt(q1n, k_refs[seg][0, a:b, :])
            s2 = _dot_nt(q2n, k_refs[seg][0, a:b, :])
            nxt_s[seg][0, :, a:b] = s1
            nxt_s[seg][1, :, a:b] = s2
            c1 = jnp.max(s1, axis=-1, keepdims=True)
            c2 = jnp.max(s2, axis=-1, keepdims=True)
            n1 = c1 if n1 is None else jnp.maximum(n1, c1)
            n2 = c2 if n2 is None else jnp.maximum(n2, c2)
        p1 = jnp.exp2(cur_s[seg][0, :, a:b] - m1)
        p2 = jnp.exp2(cur_s[seg][1, :, a:b] - m2)
        cur_s[seg][0, :, a:b] = p1
        cur_s[seg][1, :, a:b] = p2
        r1 = jnp.sum(p1, axis=-1, keepdims=True)
        r2 = jnp.sum(p2, axis=-1, keepdims=True)
        l1 = r1 if l1 is None else l1 + r1
        l2 = r2 if l2 is None else l2 + r2
    if q_next is not None:
        nxt_m[0] = jnp.broadcast_to(n1, (tq, LANES))
        nxt_m[1] = jnp.broadcast_to(n2, (tq, LANES))
    lv = lamv_ref[...]
    lam = (jnp.exp(jnp.sum(lv[0:1] * lv[1:2], axis=-1, keepdims=True))
           - jnp.exp(jnp.sum(lv[2:3] * lv[3:4], axis=-1, keepdims=True)) + lam_init)
    c = lam * l1 / l2
    o = None
    for seg, a, b in chunks:
        w = (cur_s[seg][0, :, a:b] - c * cur_s[seg][1, :, a:b]).astype(bf16)
        part = _dot(w, v_refs[seg][0, a:b, :])
        o = part if o is None else o + part
    o = o / l1
    o_ref[0] = (_rms(o, g_ref[...]) * (1 - lam_init)).astype(o_ref.dtype)


def _diff_kernel(*refs, n_seg, nq, lam_init):
    q_ref, qn_ref = refs[:2]
    k_refs = refs[2:2 + 2 * n_seg:2]
    v_refs = refs[3:3 + 2 * n_seg:2]
    lamv_ref, g_ref, o_ref = refs[2 + 2 * n_seg:5 + 2 * n_seg]
    scratch = refs[5 + 2 * n_seg:]
    slots = tuple((scratch[p * (n_seg + 1):p * (n_seg + 1) + n_seg], scratch[p * (n_seg + 1) + n_seg])
                  for p in range(2))
    t = pl.program_id(2)

    @pl.when(t == 0)
    def _():
        q1, q2 = _diff_maps(q_ref[0])
        s_refs, m_ref = slots[0]
        for mp, qm in enumerate((q1, q2)):
            parts = [_dot_nt(qm, k_ref[0]) for k_ref in k_refs]
            for s_ref, sc in zip(s_refs, parts):
                s_ref[mp] = sc
            mx = functools.reduce(jnp.maximum, [jnp.max(sc, axis=-1, keepdims=True) for sc in parts])
            m_ref[mp] = jnp.broadcast_to(mx, m_ref.shape[1:])

    if nq == 1:
        _diff_step(None, slots[0], None, k_refs, v_refs, lamv_ref, g_ref, o_ref, lam_init)
        return
    for parity in range(2):
        @pl.when((t & 1) == parity)
        def _(parity=parity):
            _diff_step(qn_ref[0], slots[parity], slots[1 - parity], k_refs, v_refs, lamv_ref, g_ref, o_ref,
                       lam_init)


def _diff_attention(qkv_q, qkv_ctx, qkv_lat, lamv, subln, lam_init):
    B, T, n3 = qkv_q.shape
    D = n3 // 3
    H = D // LANES
    tq = min(_tok_tile(T), 256)
    nq = T // tq
    kv = [qkv_ctx] if qkv_lat is None else [qkv_ctx, qkv_lat]
    args = [qkv_q, qkv_q]
    specs = [pl.BlockSpec((1, tq, LANES), lambda b, h, t: (b, t, h)),
             pl.BlockSpec((1, tq, LANES), lambda b, h, t: (b, jnp.minimum(t + 1, nq - 1), h))]
    for a in kv:
        args += [a, a]
        specs += [pl.BlockSpec((1, a.shape[1], LANES), lambda b, h, t: (b, 0, H + h)),
                  pl.BlockSpec((1, a.shape[1], LANES), lambda b, h, t: (b, 0, 2 * H + h))]
    args += [lamv, subln.reshape(1, LANES)]
    specs += [_full(lamv.shape), _full((1, LANES))]
    slot = [pltpu.VMEM((2, tq, a.shape[1]), f32) for a in kv] + [pltpu.VMEM((2, tq, LANES), f32)]
    return pl.pallas_call(
        functools.partial(_diff_kernel, n_seg=len(kv), nq=nq, lam_init=lam_init),
        grid=(B, H, nq),
        in_specs=specs,
        out_specs=pl.BlockSpec((1, tq, LANES), lambda b, h, t: (b, t, h)),
        out_shape=jax.ShapeDtypeStruct((B, T, D), bf16),
        scratch_shapes=slot + slot,
        compiler_params=_cparams("parallel", "parallel", "arbitrary"),
        name="diff_attention",
    )(*args)


def _rope_tables(n_tokens, head_dim):
    n_freq = head_dim // 4
    inv_freq = ROPE_BASE ** (-jnp.arange(n_freq, dtype=f32) / n_freq)
    t = jnp.arange(n_tokens)
    row = (t // GRID_W).astype(f32)
    col = (t % GRID_W).astype(f32)
    ang = jnp.concatenate([row[:, None] * inv_freq, col[:, None] * inv_freq], axis=-1)
    cos = jnp.repeat(jnp.cos(ang), 2, axis=-1)
    sin = jnp.repeat(jnp.sin(ang), 2, axis=-1) * jnp.tile(jnp.array([-1.0, 1.0], f32), head_dim // 2)
    reps = LANES // head_dim
    return jnp.tile(cos, (1, reps)), jnp.tile(sin, (1, reps))


def _na_head(q, hh):
    lane = lax.broadcasted_iota(jnp.int32, q.shape, 1)
    qs = (q.astype(f32) * (NA_HEAD_DIM ** -0.5 * math.log2(math.e))).astype(bf16)
    sel = (lane < NA_HEAD_DIM) if hh == 0 else (lane >= NA_HEAD_DIM)
    return jnp.where(sel, qs, jnp.zeros_like(qs))


def _na_ctx_kernel(q_ref, kc_ref, vc_ref, o_ref):
    q = q_ref[0]
    lane = lax.broadcasted_iota(jnp.int32, q.shape, 1)
    outs = []
    for hh in range(2):
        s = _dot_nt(_na_head(q, hh), kc_ref[0])
        p = jnp.exp2(s - jnp.max(s, axis=-1, keepdims=True))
        outs.append(_dot(p.astype(bf16), vc_ref[0]) / jnp.sum(p, axis=-1, keepdims=True))
    o_ref[0] = jnp.where(lane < NA_HEAD_DIM, outs[0], outs[1]).astype(o_ref.dtype)


def _na_window_table(rows):
    table = []
    for t in range(rows // NA_TILE_ROWS):
        w0 = min(max(t * NA_TILE_ROWS - (NA_WIN_ROWS - NA_TILE_ROWS) // 2, 0), rows - NA_WIN_ROWS)
        line = [w0 * GRID_W]
        for i in range(NA_TILE_ROWS):
            qr = t * NA_TILE_ROWS + i
            r0 = min(max(qr - WIN_H // 2, 0), rows - WIN_H)
            for j in range(NA_WIN_ROWS // 2):
                kr = w0 + 2 * j
                e = kr - qr + WIN_H
                ok0 = r0 <= kr < r0 + WIN_H
                ok1 = r0 <= kr + 1 < r0 + WIN_H
                line.append(e if ok0 and ok1 else 2 * WIN_H - 1 + e if ok0 else 4 * WIN_H - 1 + e if ok1 else 0)
        table.append(line)
    return jnp.array(table, jnp.int32)


def _na_window(win_ref, t):
    n_pairs = NA_WIN_ROWS // 2
    entry = [[win_ref[t, 1 + i * n_pairs + j] for j in range(n_pairs)] for i in range(NA_TILE_ROWS)]
    return pl.multiple_of(win_ref[t, 0], 4 * GRID_W), entry


def _lane_fold(x, op):
    return functools.reduce(op, [x[:, c:c + LANES] for c in range(0, x.shape[1], LANES)])


def _na_biased(sl, hh, entry, j0, tb_ref, s_ref, c0):
    tiles = []
    for i in range(NA_TILE_ROWS):
        r = slice(i * GRID_W, (i + 1) * GRID_W)
        mx = None
        for jj in range(sl.shape[1] // LANES):
            blk = sl[r, jj * LANES:(jj + 1) * LANES] + tb_ref[0, hh, entry[i][j0 + jj]]
            s_ref[r, c0 + jj * LANES:c0 + (jj + 1) * LANES] = blk
            mx = blk if mx is None else jnp.maximum(mx, blk)
        tiles.append(mx)
    return jnp.concatenate(tiles, axis=0)


def _na_unit(make, use, kc_ref, vc_ref, kl_ref, vl_ref, tb_ref):
    q, hh_m, tok_m, entry, (mc_ref, ml_ref, mm_ref) = make
    hh_u, tok_u, (uc_ref, ul_ref, um_ref) = use
    qh = _na_head(q, hh_m)
    m = um_ref[:, 0:1]
    run_max = l = o = None
    for c0 in [None] + list(range(0, NA_WIN_ROWS * GRID_W, NA_KEY_CHUNK)):
        if c0 is None:
            sc = _dot_nt(qh, kc_ref[0])
            mc_ref[...] = sc
            run_max = _lane_fold(sc, jnp.maximum)
            p = jnp.exp2(uc_ref[...] - m)
            v = vc_ref[0]
        else:
            sl = _dot_nt(qh, kl_ref[0, pl.ds(tok_m + c0, NA_KEY_CHUNK), :])
            run_max = jnp.maximum(run_max, _na_biased(sl, hh_m, entry, c0 // LANES, tb_ref, ml_ref, c0))
            p = jnp.exp2(ul_ref[:, c0:c0 + NA_KEY_CHUNK] - m)
            v = vl_ref[0, pl.ds(tok_u + c0, NA_KEY_CHUNK), :]
        r = _lane_fold(p, jnp.add)
        part = _dot(p.astype(bf16), v)
        l = r if l is None else l + r
        o = part if o is None else o + part
    mm_ref[...] = jnp.broadcast_to(jnp.max(run_max, axis=-1, keepdims=True), mm_ref.shape)
    return o / jnp.sum(l, axis=-1, keepdims=True)


def _na_kernel(win_ref, q_ref, qn_ref, kc_ref, vc_ref, kl_ref, vl_ref, tb_ref, o_ref, *scratch, n_tiles):
    slot_a, slot_b = scratch[:3], scratch[3:]
    t = pl.program_id(2)
    tok_t, entry_t = _na_window(win_ref, t)
    tok_n, entry_n = _na_window(win_ref, jnp.minimum(t + 1, n_tiles - 1))
    shared = (kc_ref, vc_ref, kl_ref, vl_ref, tb_ref)

    @pl.when(t == 0)
    def _():
        qh = _na_head(q_ref[0], 0)
        sc = _dot_nt(qh, kc_ref[0])
        slot_a[0][...] = sc
        sl = _dot_nt(qh, kl_ref[0, pl.ds(tok_t, NA_WIN_ROWS * GRID_W), :])
        run_max = jnp.maximum(_lane_fold(sc, jnp.maximum), _na_biased(sl, 0, entry_t, 0, tb_ref, slot_a[1], 0))
        slot_a[2][...] = jnp.broadcast_to(jnp.max(run_max, axis=-1, keepdims=True), slot_a[2].shape)

    o0 = _na_unit((q_ref[0], 1, tok_t, entry_t, slot_b), (0, tok_t, slot_a), *shared)
    o1 = _na_unit((qn_ref[0], 0, tok_n, entry_n, slot_a), (1, tok_t, slot_b), *shared)
    lane = lax.broadcasted_iota(jnp.int32, o0.shape, 1)
    o_ref[0] = jnp.where(lane < NA_HEAD_DIM, o0, o1).astype(o_ref.dtype)


def _na_bias_blocks(rpb):
    H, n_dr, n_dc = rpb.shape
    col = jnp.arange(GRID_W)
    c0 = jnp.clip(col - WIN_W // 2, 0, GRID_W - WIN_W)
    col_ok = (col[None, :] >= c0[:, None]) & (col[None, :] < c0[:, None] + WIN_W)
    dc = jnp.clip(col[None, :] - col[:, None] + WIN_W - 1, 0, n_dc - 1)
    onehot = (dc.reshape(1, -1) == jnp.arange(n_dc)[:, None]).astype(f32)
    blk = jnp.einsum('hrj,jx->hrx', rpb.astype(f32) * math.log2(math.e), onehot, precision=lax.Precision.HIGHEST)
    blk = jnp.where(col_ok[None, None], blk.reshape(H, n_dr, GRID_W, GRID_W), NEG_INF)
    blk = jnp.concatenate([jnp.full((H, 1, GRID_W, GRID_W), NEG_INF, f32), blk], axis=1)
    n_e = 2 * WIN_H - 1
    left = [0] + list(range(1, n_e)) + [0] + list(range(1, n_e + 1)) + [0] * n_e
    right = [0] + list(range(2, n_e + 1)) + [0] + [0] * n_e + list(range(1, n_e + 1))
    tab = jnp.concatenate([jnp.take(blk, jnp.array(left), axis=1), jnp.take(blk, jnp.array(right), axis=1)], axis=-1)
    return tab.reshape((H // 2, 2) + tab.shape[1:])


def _na_attention(qkv_q, qkv_ctx, qkv_lat, bias):
    B, T, n3 = qkv_q.shape
    D = n3 // 3
    HP = D // LANES
    L = qkv_ctx.shape[1]
    ctx_k = pl.BlockSpec((1, L, LANES), lambda b, h, t: (b, 0, HP + h))
    ctx_v = pl.BlockSpec((1, L, LANES), lambda b, h, t: (b, 0, 2 * HP + h))
    if qkv_lat is None:
        tq = min(_tok_tile(T), 256)
        tile = pl.BlockSpec((1, tq, LANES), lambda b, h, t: (b, t, h))
        return pl.pallas_call(
            _na_ctx_kernel,
            grid=(B, HP, T // tq),
            in_specs=[tile, ctx_k, ctx_v],
            out_specs=tile,
            out_shape=jax.ShapeDtypeStruct((B, T, D), bf16),
            compiler_params=_cparams("parallel", "parallel", "parallel"),
            name="na_attention_ctx",
        )(qkv_q, qkv_ctx, qkv_ctx)
    tq = NA_TILE_ROWS * GRID_W
    n_tiles = T // tq
    n_win = NA_WIN_ROWS * GRID_W
    assert T % tq == 0 and T // GRID_W >= NA_WIN_ROWS and bias.shape[2] == 6 * WIN_H - 2
    tile = pl.BlockSpec((1, tq, LANES), lambda b, h, t: (b, t, h))
    slot = [pltpu.VMEM((tq, L), f32), pltpu.VMEM((tq, n_win), f32), pltpu.VMEM((tq, LANES), f32)]
    return pl.pallas_call(
        functools.partial(_na_kernel, n_tiles=n_tiles),
        grid=(B, HP, n_tiles),
        in_specs=[pl.BlockSpec(memory_space=pltpu.SMEM), tile,
                  pl.BlockSpec((1, tq, LANES), lambda b, h, t: (b, jnp.minimum(t + 1, n_tiles - 1), h)),
                  ctx_k, ctx_v,
                  pl.BlockSpec((1, T, LANES), lambda b, h, t: (b, 0, HP + h)),
                  pl.BlockSpec((1, T, LANES), lambda b, h, t: (b, 0, 2 * HP + h)),
                  pl.BlockSpec((1,) + bias.shape[1:], lambda b, h, t: (h, 0, 0, 0, 0))],
        out_specs=tile,
        out_shape=jax.ShapeDtypeStruct((B, T, D), bf16),
        scratch_shapes=slot + slot,
        compiler_params=_cparams("parallel", "parallel", "arbitrary"),
        name="na_attention",
    )(_na_window_table(T // GRID_W), qkv_q, qkv_q, qkv_ctx, qkv_ctx, qkv_lat, qkv_lat, bias)


def _chunk_cumsum(x, reverse):
    n = x.shape[0]
    pos = lax.broadcasted_iota(jnp.int32, x.shape, 0) & (HG_CHUNK - 1)
    k = 1
    while k < HG_CHUNK:
        if reverse:
            x = x + jnp.where(pos < HG_CHUNK - k, pltpu.roll(x, n - k, 0), 0.0)
        else:
            x = x + jnp.where(pos >= k, pltpu.roll(x, k, 0), 0.0)
        k *= 2
    return x


def _hg_mask(n, reverse):
    ri = lax.broadcasted_iota(jnp.int32, (n, n), 0)
    ci = lax.broadcasted_iota(jnp.int32, (n, n), 1)
    shift = HG_CHUNK.bit_length() - 1
    return ((ri >> shift) == (ci >> shift)) & ((ci >= ri) if reverse else (ci <= ri))


def _hg_block(z, v, q, lb, st, reverse, tri):
    with_out = q is not None
    n = z.shape[0]
    nch = n // HG_CHUNK
    f = lb + (1 - lb) * jax.nn.sigmoid(z)
    lf = jnp.log(f)
    kk = 1 - f
    g = _chunk_cumsum(lf, reverse)
    out = None
    if with_out:
        q_dec = q * jnp.exp(g)
        k_inv = kk * jnp.exp(-g)
        att = jnp.where(tri, _dot_nt(q_dec.astype(bf16), k_inv.astype(bf16)), 0.0)
        out_intra = _dot(att.astype(bf16), v.astype(bf16))
        outs = [None] * nch
    order = range(nch - 1, -1, -1) if reverse else range(nch)
    for c in order:
        sl = slice(c * HG_CHUNK, (c + 1) * HG_CHUNK)
        end = c * HG_CHUNK if reverse else (c + 1) * HG_CHUNK - 1
        g_tot = g[end:end + 1]
        if with_out:
            outs[c] = _dot_nt(q_dec[sl].astype(bf16), st.astype(bf16))
        kdec = kk[sl] * jnp.exp(g_tot - g[sl])
        ds_t = _dot(v[sl].T.astype(bf16), kdec.astype(bf16))
        st = st * jnp.exp(g_tot) + ds_t
    if with_out:
        out = out_intra + jnp.concatenate(outs, axis=0)
    return out, st


def _hg_kernel(q_ref, i_ref, gate_ref, zf_ref, zb_ref, ic_ref, zfc_ref, zbc_ref, lb_ref, gn_ref, o_ref,
               accf_ref, accb_ref, *, s_tok):
    lb = lb_ref[...]
    nblk = s_tok // HG_BLOCK
    zero = jnp.zeros((HG_EXPAND, HG_EXPAND), f32)
    _, st_f = _hg_block(zfc_ref[0], ic_ref[0], None, lb, zero, False, None)
    _, st_b = _hg_block(zbc_ref[0], ic_ref[0], None, lb, zero, True, None)
    tri_f, tri_b = _hg_mask(HG_BLOCK, False), _hg_mask(HG_BLOCK, True)

    def body(n, carry):
        st_f, st_b = carry
        rows_f = pl.ds(pl.multiple_of(n * HG_BLOCK, HG_BLOCK), HG_BLOCK)
        rows_b = pl.ds(pl.multiple_of((nblk - 1 - n) * HG_BLOCK, HG_BLOCK), HG_BLOCK)
        out_f, st_f = _hg_block(zf_ref[0, rows_f, :], i_ref[0, rows_f, :], q_ref[0, rows_f, :], lb, st_f, False, tri_f)
        out_b, st_b = _hg_block(zb_ref[0, rows_b, :], i_ref[0, rows_b, :], q_ref[0, rows_b, :], lb, st_b, True, tri_b)
        accf_ref[rows_f, :] = out_f
        accb_ref[rows_b, :] = out_b
        return st_f, st_b

    lax.fori_loop(0, nblk, body, (st_f, st_b), unroll=8 if nblk % 8 == 0 else 1)
    gate = gate_ref[0]
    o = accf_ref[...] + accb_ref[...]
    o_ref[0] = (_rms(o, gn_ref[...]) * (gate * jax.nn.sigmoid(gate))).astype(o_ref.dtype)


def _hgrn2_core(p_lat, p_ctx, lb, gn_g):
    B, S, n5 = p_lat.shape
    D = n5 // 5
    H = D // HG_EXPAND
    L = p_ctx.shape[1]
    assert S % HG_BLOCK == 0 and L % HG_CHUNK == 0

    def lat(k):
        return pl.BlockSpec((1, S, HG_EXPAND), lambda b, h: (b, 0, k * H + h))

    def ctx(k):
        return pl.BlockSpec((1, L, HG_EXPAND), lambda b, h: (b, 0, k * H + h))

    vec = pl.BlockSpec((1, HG_EXPAND), lambda b, h: (0, h))
    return pl.pallas_call(
        functools.partial(_hg_kernel, s_tok=S),
        grid=(B, H),
        in_specs=[lat(0), lat(1), lat(2), lat(3), lat(4), ctx(1), ctx(3), ctx(4), vec, _full((1, HG_EXPAND))],
        out_specs=pl.BlockSpec((1, S, HG_EXPAND), lambda b, h: (b, 0, h)),
        out_shape=jax.ShapeDtypeStruct((B, S, D), bf16),
        scratch_shapes=[pltpu.VMEM((S, HG_EXPAND), f32), pltpu.VMEM((S, HG_EXPAND), f32)],
        compiler_params=_cparams("parallel", "parallel"),
        name="hgrn2_core",
    )(p_lat, p_lat, p_lat, p_lat, p_lat, p_ctx, p_ctx, p_ctx, lb.reshape(1, D), gn_g.reshape(1, HG_EXPAND))


def kernel(x, c, ctx, c_ctx, w_ada, b_ada, g_pre, g_post, w_ff1, w_ff3, w_ff2, s5_a_re, s5_a_im, s5_log_dt, s5_b_re, s5_b_im, s5_c_re, s5_c_im, s5_d, s5_w_glu, s5_b_glu, da_w_qkv, da_w_o, da_lam_q1, da_lam_k1, da_lam_q2, da_lam_k2, da_subln, na_w_qkv, na_w_o, na_rpb, hg_w_qig, hg_w_f, hg_b_f, hg_lb_logits, hg_gnorm, hg_w_o):
    B, S, D = x.shape
    depth = w_ada.shape[0]
    n_mix = 4
    rows_pad = -(-(B + 1) // 8) * 8
    c_all = jnp.concatenate([c, c_ctx[None], jnp.zeros((rows_pad - B - 1, D), f32)], axis=0)
    mods = _ada(c_all, w_ada, b_ada)
    lb_p = jax.nn.softmax(hg_lb_logits.astype(f32), axis=0)
    lower_bounds = jnp.cumsum(lb_p, axis=0) - lb_p[0]
    w1b, w3b, w2b = w_ff1.astype(bf16), w_ff3.astype(bf16), w_ff2.astype(bf16)

    x_lat, x_ctx = x, ctx
    for i in range(depth):
        last = i == depth - 1
        occ, kind = i // n_mix, i % n_mix
        m_lat = mods[i, :B].reshape(B, 3 * N_SUB, D)
        m_ctx = mods[i, B:B + 1].reshape(1, 3 * N_SUB, D)
        ffn1 = functools.partial(_half_ffn, g_in=g_pre[i, 0], g_out=g_post[i, 0], w1=w1b[i, 0], w3=w3b[i, 0],
                                 w2=w2b[i, 0], j=0)
        ffn2 = functools.partial(_half_ffn, g_in=g_pre[i, 2], g_out=g_post[i, 2], w1=w1b[i, 1], w3=w3b[i, 1],
                                 w2=w2b[i, 1], j=2)
        x_lat, x_ctx = ffn1(x_lat, m_lat), ffn1(x_ctx, m_ctx)

        y_ctx = None
        if kind == 0:
            h_lat = _prenorm(x_lat, m_lat, g_pre[i, 1], 1)
            h_ctx = _prenorm(x_ctx, m_ctx, g_pre[i, 1], 1)
            y_lat, y_ctx = _s5_scan(h_lat, h_ctx, s5_a_re[occ], s5_a_im[occ], s5_log_dt[occ], s5_b_re[occ],
                                    s5_b_im[occ], s5_c_re[occ], s5_c_im[occ])
            mix = "glu"
            mix_head = (g_pre[i, 1], s5_d[occ])
            mix_tail = (g_post[i, 1], s5_w_glu[occ].astype(bf16), s5_b_glu[occ])
        elif kind == 1:
            wq = da_w_qkv[occ].astype(bf16)
            rope = _rope_tables(S, DIFF_HEAD_DIM)
            qkv_lat = _prenorm_proj(x_lat, m_lat, g_pre[i, 1], 1, wq, rope=rope, n_rope=2 * D)
            qkv_ctx = _prenorm_proj(x_ctx, m_ctx, g_pre[i, 1], 1, wq)
            lamv = jnp.stack([da_lam_q1[occ], da_lam_k1[occ], da_lam_q2[occ], da_lam_k2[occ]]).astype(f32)
            lam_init = 0.8 - 0.6 * math.exp(-0.3 * i)
            y_lat = _diff_attention(qkv_lat, qkv_ctx, qkv_lat, lamv, da_subln[occ], lam_init)
            if not last:
                y_ctx = _diff_attention(qkv_ctx, qkv_ctx, None, lamv, da_subln[occ], lam_init)
            mix, mix_head, mix_tail = "outproj", (), (g_post[i, 1], da_w_o[occ].astype(bf16))
        elif kind == 2:
            wq = na_w_qkv[occ].astype(bf16)
            qkv_lat = _prenorm_proj(x_lat, m_lat, g_pre[i, 1], 1, wq)
            qkv_ctx = _prenorm_proj(x_ctx, m_ctx, g_pre[i, 1], 1, wq)
            y_lat = _na_attention(qkv_lat, qkv_ctx, qkv_lat, _na_bias_blocks(na_rpb[occ]))
            if not last:
                y_ctx = _na_attention(qkv_ctx, qkv_ctx, None, None)
            mix, mix_head, mix_tail = "outproj", (), (g_post[i, 1], na_w_o[occ].astype(bf16))
        else:
            assert last, "HGRN2 context outputs are not needed when it is the last layer"
            wp = jnp.concatenate([hg_w_qig[occ], hg_w_f[occ, 0], hg_w_f[occ, 1]], axis=1).astype(bf16)
            bp = jnp.concatenate([jnp.zeros((3 * D,), f32), hg_b_f[occ, 0], hg_b_f[occ, 1]])
            p_lat = _prenorm_proj(x_lat, m_lat, g_pre[i, 1], 1, wp, bias=bp, out_dtype=f32)
            p_ctx = _prenorm_proj(x_ctx, m_ctx, g_pre[i, 1], 1, wp, bias=bp, out_dtype=f32)
            y_lat = _hgrn2_core(p_lat, p_ctx, lower_bounds[i], hg_gnorm[occ])
            mix, mix_head, mix_tail = "outproj", (), (g_post[i, 1], hg_w_o[occ].astype(bf16))

        x_lat = ffn2(x_lat, m_lat, mix=mix, mix_args=(y_lat,) + mix_head + mix_tail)
        if not last:
            x_ctx = ffn2(x_ctx, m_ctx, mix=mix, mix_args=(y_ctx,) + mix_head + mix_tail)
    return x_lat
```

```python
import functools
import math

import jax
import jax.numpy as jnp
from jax import lax
from jax.experimental import pallas as pl
from jax.experimental.pallas import tpu as pltpu

f32 = jnp.float32
bf16 = jnp.bfloat16

N_SUB = 3
RMS_EPS = 1e-6
NEG_INF = -1e30
ROPE_BASE = 10000.0
GRID_W = 64
S5_GROUP = 16
S5_STATE = 64
S5_CHUNK = 16
DIFF_HEAD_DIM = 64
DIFF_KEY_CHUNK = 1024
NA_HEAD_DIM = 64
WIN_H = 8
WIN_W = 16
NA_TILE_ROWS = 8
NA_WIN_ROWS = 16
NA_KEY_CHUNK = 512
HG_EXPAND = 128
HG_CHUNK = 64
HG_BLOCK = 256
LANES = 128
VMEM_LIMIT = 56 * 1024 * 1024


def _cparams(*sem):
    return pltpu.CompilerParams(dimension_semantics=sem, vmem_limit_bytes=VMEM_LIMIT)


def _dot(a, b):
    return jnp.dot(a, b, preferred_element_type=f32)


def _dot_nt(a, b):
    return lax.dot_general(a, b, (((1,), (1,)), ((), ())), preferred_element_type=f32)


def _rms(x, g):
    return x * lax.rsqrt(jnp.mean(jnp.square(x), axis=-1, keepdims=True) + RMS_EPS) * g


def _pre_norm(x, g, mod_ref, j):
    shift = mod_ref[0, 3 * j:3 * j + 1, :]
    scale = mod_ref[0, 3 * j + 1:3 * j + 2, :]
    return _rms(x, g) * (1 + scale) + shift


def _post_residual(x, y, g, mod_ref, j, weight):
    gate = mod_ref[0, 3 * j + 2:3 * j + 3, :]
    return x + weight * gate * _rms(y, g)


def _tok_tile(t):
    for tm in (512, 256, 128):
        if t % tm == 0:
            return tm
    raise ValueError(f"token count {t} is not a multiple of 128")


def _full(shape):
    return pl.BlockSpec(shape, lambda *_: (0,) * len(shape))


def _mod_spec(mod):
    nd = mod.shape[1]
    d = mod.shape[2]
    if mod.shape[0] == 1:
        return pl.BlockSpec((1, nd, d), lambda b, t: (0, 0, 0))
    return pl.BlockSpec((1, nd, d), lambda b, t: (b, 0, 0))


def _ada_kernel(c_ref, w_ref, b_ref, o_ref):
    c = c_ref[...]
    sc = (c * jax.nn.sigmoid(c)).astype(bf16)
    o_ref[0] = _dot(sc, w_ref[0].astype(bf16)) + b_ref[0]


def _ada(c_all, w_ada, b_ada):
    depth, d, n = w_ada.shape
    rows = c_all.shape[0]
    tn = n // 4
    return pl.pallas_call(
        _ada_kernel,
        grid=(depth, n // tn),
        in_specs=[pl.BlockSpec((rows, d), lambda i, t: (0, 0)),
                  pl.BlockSpec((1, d, tn), lambda i, t: (i, 0, t)),
                  pl.BlockSpec((1, 1, tn), lambda i, t: (i, 0, t))],
        out_specs=pl.BlockSpec((1, rows, tn), lambda i, t: (i, 0, t)),
        out_shape=jax.ShapeDtypeStruct((depth, rows, n), f32),
        compiler_params=_cparams("arbitrary", "arbitrary"),
        name="ada",
    )(c_all, w_ada, b_ada.reshape(depth, 1, n))


def _ffn_kernel(x_ref, mod_ref, gin_ref, gout_ref, w1_ref, w3_ref, w2_ref, *rest, j, chunks, mix):
    o_ref = rest[-1]
    x = x_ref[0]
    if mix == "outproj":
        y_ref, gmix_ref, wo_ref = rest[:-1]
        x = _post_residual(x, _dot(y_ref[0], wo_ref[...]), gmix_ref[...], mod_ref, 1, 1.0)
    elif mix == "glu":
        y_ref, gpre_ref, d_ref, gmix_ref, wg_ref, bg_ref = rest[:-1]
        z = jax.nn.gelu(d_ref[...] * _pre_norm(x, gpre_ref[...], mod_ref, 1) + y_ref[0])
        u = _dot(z.astype(bf16), wg_ref[...]) + bg_ref[...]
        x = _post_residual(x, z * jax.nn.sigmoid(u), gmix_ref[...], mod_ref, 1, 1.0)
    h = _pre_norm(x, gin_ref[...], mod_ref, j).astype(bf16)
    acc = None
    for s, n in chunks:
        a = _dot(h, w1_ref[:, s:s + n])
        b = _dot(h, w3_ref[:, s:s + n])
        g = (a * jax.nn.sigmoid(a) * b).astype(bf16)
        y = _dot(g, w2_ref[s:s + n, :])
        acc = y if acc is None else acc + y
    o_ref[0] = _post_residual(x, acc, gout_ref[...], mod_ref, j, 0.5)


def _ff_chunks(dff, step=1024):
    out, s = [], 0
    while s < dff:
        n = min(step, dff - s)
        out.append((s, n))
        s += n
    return tuple(out)


def _half_ffn(x, mod, g_in, g_out, w1, w3, w2, j, mix=None, mix_args=()):
    b, t, d = x.shape
    dff = w1.shape[1]
    tm = _tok_tile(t)
    tok = pl.BlockSpec((1, tm, d), lambda bb, tt: (bb, tt, 0))
    vec = _full((1, d))
    args = [x, mod, g_in.reshape(1, d), g_out.reshape(1, d), w1, w3, w2]
    specs = [tok, _mod_spec(mod), vec, vec, _full((d, dff)), _full((d, dff)), _full((dff, d))]
    if mix == "outproj":
        y, g_mix, w_o = mix_args
        args += [y, g_mix.reshape(1, d), w_o]
        specs += [tok, vec, _full((d, d))]
    elif mix == "glu":
        y, g_pre, d_skip, g_mix, w_g, b_g = mix_args
        args += [y, g_pre.reshape(1, d), d_skip.astype(f32).reshape(1, d), g_mix.reshape(1, d), w_g, b_g.reshape(1, d)]
        specs += [tok, vec, vec, vec, _full((d, d)), vec]
    return pl.pallas_call(
        functools.partial(_ffn_kernel, j=j, chunks=_ff_chunks(dff), mix=mix),
        grid=(b, t // tm),
        in_specs=specs,
        out_specs=tok,
        out_shape=jax.ShapeDtypeStruct((b, t, d), f32),
        compiler_params=_cparams("parallel", "parallel"),
        name="half_ffn",
    )(*args)


def _prenorm_kernel(x_ref, mod_ref, g_ref, o_ref, *, j):
    o_ref[0] = _pre_norm(x_ref[0], g_ref[...], mod_ref, j).astype(o_ref.dtype)


def _prenorm(x, mod, g, j):
    b, t, d = x.shape
    tm = _tok_tile(t)
    return pl.pallas_call(
        functools.partial(_prenorm_kernel, j=j),
        grid=(b, t // tm),
        in_specs=[pl.BlockSpec((1, tm, d), lambda bb, tt: (bb, tt, 0)), _mod_spec(mod), _full((1, d))],
        out_specs=pl.BlockSpec((1, tm, d), lambda bb, tt: (bb, tt, 0)),
        out_shape=jax.ShapeDtypeStruct((b, t, d), bf16),
        compiler_params=_cparams("parallel", "parallel"),
        name="prenorm",
    )(x, mod, g.reshape(1, d))


def _swap_pairs(x):
    lane = lax.broadcasted_iota(jnp.int32, x.shape, 1)
    nxt = pltpu.roll(x, LANES - 1, 1)
    prv = pltpu.roll(x, 1, 1)
    return jnp.where((lane & 1) == 0, nxt, prv)


def _proj_kernel(*refs, j, n_out, n_rope, has_bias, step):
    x_ref, mod_ref, g_ref, w_ref = refs[:4]
    k = 4
    b_ref = None
    if has_bias:
        b_ref = refs[k]
        k += 1
    if n_rope:
        cos_ref, sin_ref = refs[k], refs[k + 1]
        k += 2
    o_ref = refs[k]
    h = _pre_norm(x_ref[0], g_ref[...], mod_ref, j).astype(bf16)
    for s in range(0, n_out, step):
        y = _dot(h, w_ref[:, s:s + step])
        if has_bias:
            y = y + b_ref[:, s:s + step]
        if s < n_rope:
            cos = cos_ref[...]
            sin = sin_ref[...]
            parts = []
            for c in range(0, step, LANES):
                yc = y[:, c:c + LANES]
                parts.append(yc * cos + _swap_pairs(yc) * sin)
            y = jnp.concatenate(parts, axis=1)
        o_ref[0, :, s:s + step] = y.astype(o_ref.dtype)


def _prenorm_proj(x, mod, g, j, w, bias=None, rope=None, n_rope=0, out_dtype=bf16):
    b, t, d = x.shape
    n = w.shape[1]
    tm = min(_tok_tile(t), 256)
    step = 512
    assert n % step == 0 and n_rope % step == 0
    args = [x, mod, g.reshape(1, d), w]
    specs = [pl.BlockSpec((1, tm, d), lambda bb, tt: (bb, tt, 0)), _mod_spec(mod), _full((1, d)), _full((d, n))]
    if bias is not None:
        args.append(bias.reshape(1, n))
        specs.append(_full((1, n)))
    if n_rope:
        args += list(rope)
        specs += [pl.BlockSpec((tm, LANES), lambda bb, tt: (tt, 0))] * 2
    return pl.pallas_call(
        functools.partial(_proj_kernel, j=j, n_out=n, n_rope=n_rope, has_bias=bias is not None, step=step),
        grid=(b, t // tm),
        in_specs=specs,
        out_specs=pl.BlockSpec((1, tm, n), lambda bb, tt: (bb, tt, 0)),
        out_shape=jax.ShapeDtypeStruct((b, t, n), out_dtype),
        compiler_params=_cparams("parallel", "parallel"),
        name="prenorm_proj",
    )(*args)


def _s5_kernel(xc_ref, xl_ref, wb_ref, wk_ref, wc_ref, lam_ref, yc_ref, yl_ref, s_ref, hin_ref, *, nb, nc_ctx, nc):
    rc = nc_ctx * nb
    xc, xl = xc_ref[0], xl_ref[0]
    s_ref[0:rc, :] = _dot(xc, wb_ref[0])
    s_ref[rc:, :] = _dot(xl, wb_ref[0])
    P = S5_STATE
    lam = lam_ref[0]
    a_re, a_im = lam[0:1], lam[1:2]
    fwd_lane = lax.broadcasted_iota(jnp.int32, (nb, 2 * P), 1) < P

    def step(n, carry):
        h_re, h_im = carry
        r_f = pl.multiple_of(n * nb, nb)
        n_b = jnp.where(n < nc_ctx, nc_ctx - 1 - n, nc - 1 - (n - nc_ctx))
        r_b = pl.multiple_of(n_b * nb, nb)
        rows_f, rows_b = pl.ds(r_f, nb), pl.ds(r_b, nb)
        hin_ref[rows_f, 0:P] = h_re[:, 0:P]
        hin_ref[rows_b, P:2 * P] = h_re[:, P:2 * P]
        hin_ref[rows_f, 2 * P:3 * P] = h_im[:, 0:P]
        hin_ref[rows_b, 3 * P:4 * P] = h_im[:, P:2 * P]
        s_re = jnp.where(fwd_lane, s_ref[rows_f, 0:2 * P], s_ref[rows_b, 0:2 * P])
        s_im = jnp.where(fwd_lane, s_ref[rows_f, 2 * P:4 * P], s_ref[rows_b, 2 * P:4 * P])
        return a_re * h_re - a_im * h_im + s_re, a_re * h_im + a_im * h_re + s_im

    zero = jnp.zeros((nb, 2 * P), f32)
    lax.fori_loop(0, nc, step, (zero, zero), unroll=4 if nc % 4 == 0 else 1)
    yc_ref[0] = _dot(xc, wk_ref[0]) + _dot(hin_ref[0:rc, :].astype(bf16), wc_ref[0])
    yl_ref[0] = _dot(xl, wk_ref[0]) + _dot(hin_ref[rc:, :].astype(bf16), wc_ref[0])

def _s5_weights(a_re, a_im, log_dt, b_re, b_im, c_re, c_im):
    hp = lax.Precision.HIGHEST
    C = S5_CHUNK
    G, P = a_re.shape[1:]
    N = b_re.shape[-1]
    j = jnp.arange(C + 1, dtype=f32)
    wbs, wks, wcs, lams = [], [], [], []
    for dirn in range(2):
        are, aim = a_re[dirn].astype(f32), a_im[dirn].astype(f32)
        dt = jnp.exp(log_dt[dirn].astype(f32))[:, None]
        pw_mag = jnp.exp(j[:, None, None] * (are * dt)[None])
        pw_re = pw_mag * jnp.cos(j[:, None, None] * (aim * dt)[None])
        pw_im = pw_mag * jnp.sin(j[:, None, None] * (aim * dt)[None])
        nr, ni = pw_re[1] - 1.0, pw_im[1]
        den = are * are + aim * aim
        fr = (nr * are + ni * aim) / den
        fi = (ni * are - nr * aim) / den
        bre, bim = b_re[dirn].astype(f32), b_im[dirn].astype(f32)
        bbr = fr[..., None] * bre - fi[..., None] * bim
        bbi = fr[..., None] * bim + fi[..., None] * bre
        cre, cim = c_re[dirn].astype(f32), c_im[dirn].astype(f32)
        dr, di = (pw_re[:C][::-1], pw_im[:C][::-1]) if dirn == 0 else (pw_re[:C], pw_im[:C])
        sr = dr[..., None] * bbr[None] - di[..., None] * bbi[None]
        si = dr[..., None] * bbi[None] + di[..., None] * bbr[None]
        wb = jnp.concatenate([sr, si], axis=2)
        wbs.append(wb.transpose(1, 0, 3, 2).reshape(G, C * N, 2 * P))
        pr, pi = pw_re[1:], pw_im[1:]
        if dirn == 1:
            pr, pi = pr[::-1], pi[::-1]
        or_ = cre[None] * pr[:, :, None, :] - cim[None] * pi[:, :, None, :]
        oi_ = -(cre[None] * pi[:, :, None, :] + cim[None] * pr[:, :, None, :])
        wc = jnp.concatenate([or_, oi_], axis=3)
        wcs.append(wc.transpose(1, 3, 0, 2).reshape(G, 2 * P, C * N))
        clr = cre[None] * pw_re[:C, :, None, :] - cim[None] * pw_im[:C, :, None, :]
        cli = cre[None] * pw_im[:C, :, None, :] + cim[None] * pw_re[:C, :, None, :]
        kk = (jnp.einsum('jgnp,gpm->gjnm', clr, bbr, precision=hp)
              - jnp.einsum('jgnp,gpm->gjnm', cli, bbi, precision=hp))
        s_idx = jnp.arange(C)[:, None]
        t_idx = jnp.arange(C)[None, :]
        lag = (t_idx - s_idx) if dirn == 0 else (s_idx - t_idx)
        onehot = (lag[:, :, None] == jnp.arange(C)[None, None, :]).astype(f32)
        kt = jnp.einsum('stj,gjnm->gsmtn', onehot, kk, precision=hp)
        wks.append(kt.reshape(G, C * N, C * N))
        lams.append((pw_re[C], pw_im[C]))
    wb = jnp.concatenate([wbs[0][..., :P], wbs[1][..., :P], wbs[0][..., P:], wbs[1][..., P:]], axis=2).astype(bf16)
    wc = jnp.concatenate([wcs[0][:, :P], wcs[1][:, :P], wcs[0][:, P:], wcs[1][:, P:]], axis=1).astype(bf16)
    wk = (wks[0] + wks[1]).astype(bf16)
    lam = jnp.stack([jnp.concatenate([lams[0][0], lams[1][0]], axis=-1),
                     jnp.concatenate([lams[0][1], lams[1][1]], axis=-1)], axis=1)
    return wb, wk, wc, lam


def _s5_scan(h_lat, h_ctx, a_re, a_im, log_dt, b_re, b_im, c_re, c_im):
    B, S, D = h_lat.shape
    L = h_ctx.shape[1]
    C, N, P = S5_CHUNK, S5_GROUP, S5_STATE
    G = D // N
    nc_ctx, nc = L // C, (L + S) // C
    assert 2 * P == LANES and L % C == 0 and S % C == 0
    wb, wk, wc, lam = _s5_weights(a_re, a_im, log_dt, b_re, b_im, c_re, c_im)

    def to_groups(h):
        n = h.shape[1] // C
        return h.reshape(B, n, C, G, N).transpose(3, 1, 0, 2, 4).reshape(G, n * B, C * N)

    def from_groups(y):
        n = y.shape[1] // B
        return y.reshape(G, n, B, C, N).transpose(2, 1, 3, 0, 4).reshape(B, n * C, D)

    rc, rl = nc_ctx * B, (nc - nc_ctx) * B

    def rows(r):
        return pl.BlockSpec((1, r, C * N), lambda g: (g, 0, 0))

    yc, yl = pl.pallas_call(
        functools.partial(_s5_kernel, nb=B, nc_ctx=nc_ctx, nc=nc),
        grid=(G,),
        in_specs=[rows(rc), rows(rl),
                  pl.BlockSpec((1, C * N, 4 * P), lambda g: (g, 0, 0)),
                  pl.BlockSpec((1, C * N, C * N), lambda g: (g, 0, 0)),
                  pl.BlockSpec((1, 4 * P, C * N), lambda g: (g, 0, 0)),
                  pl.BlockSpec((1, 2, 2 * P), lambda g: (g, 0, 0))],
        out_specs=[rows(rc), rows(rl)],
        out_shape=[jax.ShapeDtypeStruct((G, rc, C * N), f32), jax.ShapeDtypeStruct((G, rl, C * N), f32)],
        scratch_shapes=[pltpu.VMEM((rc + rl, 4 * P), f32), pltpu.VMEM((rc + rl, 4 * P), f32)],
        compiler_params=_cparams("parallel"),
        name="s5_scan",
    )(to_groups(h_ctx), to_groups(h_lat), wb, wk, wc, lam)
    return from_groups(yl), from_groups(yc)


def _diff_maps(q):
    lane = lax.broadcasted_iota(jnp.int32, q.shape, 1)
    qs = (q.astype(f32) * (DIFF_HEAD_DIM ** -0.5 * math.log2(math.e))).astype(bf16)
    zero = jnp.zeros_like(qs)
    return jnp.where(lane < DIFF_HEAD_DIM, qs, zero), jnp.where(lane >= DIFF_HEAD_DIM, qs, zero)


def _key_chunks(k_refs):
    out = []
    for seg, k_ref in enumerate(k_refs):
        n = k_ref.shape[1]
        out += [(seg, c, min(c + DIFF_KEY_CHUNK, n)) for c in range(0, n, DIFF_KEY_CHUNK)]
    return out


def _diff_step(q_next, cur, nxt, k_refs, v_refs, lamv_ref, g_ref, o_ref, lam_init):
    chunks = _key_chunks(k_refs)
    cur_s, cur_m = cur
    tq = cur_m.shape[1]
    m1, m2 = cur_m[0, :, 0:1], cur_m[1, :, 0:1]
    if q_next is not None:
        q1n, q2n = _diff_maps(q_next)
        nxt_s, nxt_m = nxt
    n1 = n2 = l1 = l2 = None
    for seg, a, b in chunks:
        if q_next is not None:
            s1 = _dot_nt(q1n, k_refs[seg][0, a:b, :])
            s2 = _dot_nt(q2n, k_refs[seg][0, a:b, :])
            nxt_s[seg][0, :, a:b] = s1
            nxt_s[seg][1, :, a:b] = s2
            c1 = jnp.max(s1, axis=-1, keepdims=True)
            c2 = jnp.max(s2, axis=-1, keepdims=True)
            n1 = c1 if n1 is None else jnp.maximum(n1, c1)
            n2 = c2 if n2 is None else jnp.maximum(n2, c2)
        p1 = jnp.exp2(cur_s[seg][0, :, a:b] - m1)
        p2 = jnp.exp2(cur_s[seg][1, :, a:b] - m2)
        cur_s[seg][0, :, a:b] = p1
        cur_s[seg][1, :, a:b] = p2
        r1 = jnp.sum(p1, axis=-1, keepdims=True)
        r2 = jnp.sum(p2, axis=-1, keepdims=True)
        l1 = r1 if l1 is None else l1 + r1
        l2 = r2 if l2 is None else l2 + r2
    if q_next is not None:
        nxt_m[0] = jnp.broadcast_to(n1, (tq, LANES))
        nxt_m[1] = jnp.broadcast_to(n2, (tq, LANES))
    lv = lamv_ref[...]
    lam = (jnp.exp(jnp.sum(lv[0:1] * lv[1:2], axis=-1, keepdims=True))
           - jnp.exp(jnp.sum(lv[2:3] * lv[3:4], axis=-1, keepdims=True)) + lam_init)
    c = lam * l1 / l2
    o = None
    for seg, a, b in chunks:
        w = (cur_s[seg][0, :, a:b] - c * cur_s[seg][1, :, a:b]).astype(bf16)
        part = _dot(w, v_refs[seg][0, a:b, :])
        o = part if o is None else o + part
    o = o / l1
    o_ref[0] = (_rms(o, g_ref[...]) * (1 - lam_init)).astype(o_ref.dtype)


def _diff_kernel(*refs, n_seg, nq, lam_init):
    q_ref, qn_ref = refs[:2]
    k_refs = refs[2:2 + 2 * n_seg:2]
    v_refs = refs[3:3 + 2 * n_seg:2]
    lamv_ref, g_ref, o_ref = refs[2 + 2 * n_seg:5 + 2 * n_seg]
    scratch = refs[5 + 2 * n_seg:]
    slots = tuple((scratch[p * (n_seg + 1):p * (n_seg + 1) + n_seg], scratch[p * (n_seg + 1) + n_seg])
                  for p in range(2))
    t = pl.program_id(2)

    @pl.when(t == 0)
    def _():
        q1, q2 = _diff_maps(q_ref[0])
        s_refs, m_ref = slots[0]
        for mp, qm in enumerate((q1, q2)):
            parts = [_dot_nt(qm, k_ref[0]) for k_ref in k_refs]
            for s_ref, sc in zip(s_refs, parts):
                s_ref[mp] = sc
            mx = functools.reduce(jnp.maximum, [jnp.max(sc, axis=-1, keepdims=True) for sc in parts])
            m_ref[mp] = jnp.broadcast_to(mx, m_ref.shape[1:])

    if nq == 1:
        _diff_step(None, slots[0], None, k_refs, v_refs, lamv_ref, g_ref, o_ref, lam_init)
        return
    for parity in range(2):
        @pl.when((t & 1) == parity)
        def _(parity=parity):
            _diff_step(qn_ref[0], slots[parity], slots[1 - parity], k_refs, v_refs, lamv_ref, g_ref, o_ref,
                       lam_init)


def _diff_attention(qkv_q, qkv_ctx, qkv_lat, lamv, subln, lam_init):
    B, T, n3 = qkv_q.shape
    D = n3 // 3
    H = D // LANES
    tq = min(_tok_tile(T), 512)
    nq = T // tq
    kv = [qkv_ctx] if qkv_lat is None else [qkv_ctx, qkv_lat]
    args = [qkv_q, qkv_q]
    specs = [pl.BlockSpec((1, tq, LANES), lambda b, h, t: (b, t, h)),
             pl.BlockSpec((1, tq, LANES), lambda b, h, t: (b, jnp.minimum(t + 1, nq - 1), h))]
    for a in kv:
        args += [a, a]
        specs += [pl.BlockSpec((1, a.shape[1], LANES), lambda b, h, t: (b, 0, H + h)),
                  pl.BlockSpec((1, a.shape[1], LANES), lambda b, h, t: (b, 0, 2 * H + h))]
    args += [lamv, subln.reshape(1, LANES)]
    specs += [_full(lamv.shape), _full((1, LANES))]
    slot = [pltpu.VMEM((2, tq, a.shape[1]), f32) for a in kv] + [pltpu.VMEM((2, tq, LANES), f32)]
    return pl.pallas_call(
        functools.partial(_diff_kernel, n_seg=len(kv), nq=nq, lam_init=lam_init),
        grid=(B, H, nq),
        in_specs=specs,
        out_specs=pl.BlockSpec((1, tq, LANES), lambda b, h, t: (b, t, h)),
        out_shape=jax.ShapeDtypeStruct((B, T, D), bf16),
        scratch_shapes=slot + slot,
        compiler_params=_cparams("parallel", "parallel", "arbitrary"),
        name="diff_attention",
    )(*args)


def _rope_tables(n_tokens, head_dim):
    n_freq = head_dim // 4
    inv_freq = ROPE_BASE ** (-jnp.arange(n_freq, dtype=f32) / n_freq)
    t = jnp.arange(n_tokens)
    row = (t // GRID_W).astype(f32)
    col = (t % GRID_W).astype(f32)
    ang = jnp.concatenate([row[:, None] * inv_freq, col[:, None] * inv_freq], axis=-1)
    cos = jnp.repeat(jnp.cos(ang), 2, axis=-1)
    sin = jnp.repeat(jnp.sin(ang), 2, axis=-1) * jnp.tile(jnp.array([-1.0, 1.0], f32), head_dim // 2)
    reps = LANES // head_dim
    return jnp.tile(cos, (1, reps)), jnp.tile(sin, (1, reps))


def _na_head(q, hh):
    lane = lax.broadcasted_iota(jnp.int32, q.shape, 1)
    qs = (q.astype(f32) * (NA_HEAD_DIM ** -0.5 * math.log2(math.e))).astype(bf16)
    sel = (lane < NA_HEAD_DIM) if hh == 0 else (lane >= NA_HEAD_DIM)
    return jnp.where(sel, qs, jnp.zeros_like(qs))


def _na_ctx_kernel(q_ref, kc_ref, vc_ref, o_ref):
    q = q_ref[0]
    lane = lax.broadcasted_iota(jnp.int32, q.shape, 1)
    outs = []
    for hh in range(2):
        s = _dot_nt(_na_head(q, hh), kc_ref[0])
        p = jnp.exp2(s - jnp.max(s, axis=-1, keepdims=True))
        outs.append(_dot(p.astype(bf16), vc_ref[0]) / jnp.sum(p, axis=-1, keepdims=True))
    o_ref[0] = jnp.where(lane < NA_HEAD_DIM, outs[0], outs[1]).astype(o_ref.dtype)


def _na_window_table(rows):
    table = []
    for t in range(rows // NA_TILE_ROWS):
        w0 = min(max(t * NA_TILE_ROWS - (NA_WIN_ROWS - NA_TILE_ROWS) // 2, 0), rows - NA_WIN_ROWS)
        line = [w0 * GRID_W]
        for i in range(NA_TILE_ROWS):
            qr = t * NA_TILE_ROWS + i
            r0 = min(max(qr - WIN_H // 2, 0), rows - WIN_H)
            for j in range(NA_WIN_ROWS // 2):
                kr = w0 + 2 * j
                e = kr - qr + WIN_H
                ok0 = r0 <= kr < r0 + WIN_H
                ok1 = r0 <= kr + 1 < r0 + WIN_H
                line.append(e if ok0 and ok1 else 2 * WIN_H - 1 + e if ok0 else 4 * WIN_H - 1 + e if ok1 else 0)
        table.append(line)
    return jnp.array(table, jnp.int32)


def _na_window(win_ref, t):
    n_pairs = NA_WIN_ROWS // 2
    entry = [[win_ref[t, 1 + i * n_pairs + j] for j in range(n_pairs)] for i in range(NA_TILE_ROWS)]
    return pl.multiple_of(win_ref[t, 0], 4 * GRID_W), entry


def _lane_fold(x, op):
    return functools.reduce(op, [x[:, c:c + LANES] for c in range(0, x.shape[1], LANES)])


def _na_biased(sl, hh, entry, j0, tb_ref, s_ref, c0):
    tiles = []
    for i in range(NA_TILE_ROWS):
        r = slice(i * GRID_W, (i + 1) * GRID_W)
        mx = None
        for jj in range(sl.shape[1] // LANES):
            blk = sl[r, jj * LANES:(jj + 1) * LANES] + tb_ref[0, hh, entry[i][j0 + jj]]
            s_ref[r, c0 + jj * LANES:c0 + (jj + 1) * LANES] = blk
            mx = blk if mx is None else jnp.maximum(mx, blk)
        tiles.append(mx)
    return jnp.concatenate(tiles, axis=0)


def _na_unit(make, use, kc_ref, vc_ref, kl_ref, vl_ref, tb_ref):
    q, hh_m, tok_m, entry, (mc_ref, ml_ref, mm_ref) = make
    hh_u, tok_u, (uc_ref, ul_ref, um_ref) = use
    qh = _na_head(q, hh_m)
    m = um_ref[:, 0:1]
    run_max = l = o = None
    for c0 in [None] + list(range(0, NA_WIN_ROWS * GRID_W, NA_KEY_CHUNK)):
        if c0 is None:
            sc = _dot_nt(qh, kc_ref[0])
            mc_ref[...] = sc
            run_max = _lane_fold(sc, jnp.maximum)
            p = jnp.exp2(uc_ref[...] - m)
            v = vc_ref[0]
        else:
            sl = _dot_nt(qh, kl_ref[0, pl.ds(tok_m + c0, NA_KEY_CHUNK), :])
            run_max = jnp.maximum(run_max, _na_biased(sl, hh_m, entry, c0 // LANES, tb_ref, ml_ref, c0))
            p = jnp.exp2(ul_ref[:, c0:c0 + NA_KEY_CHUNK] - m)
            v = vl_ref[0, pl.ds(tok_u + c0, NA_KEY_CHUNK), :]
        r = _lane_fold(p, jnp.add)
        part = _dot(p.astype(bf16), v)
        l = r if l is None else l + r
        o = part if o is None else o + part
    mm_ref[...] = jnp.broadcast_to(jnp.max(run_max, axis=-1, keepdims=True), mm_ref.shape)
    return o / jnp.sum(l, axis=-1, keepdims=True)


def _na_kernel(win_ref, q_ref, qn_ref, kc_ref, vc_ref, kl_ref, vl_ref, tb_ref, o_ref, *scratch, n_tiles):
    slot_a, slot_b = scratch[:3], scratch[3:]
    t = pl.program_id(2)
    tok_t, entry_t = _na_window(win_ref, t)
    tok_n, entry_n = _na_window(win_ref, jnp.minimum(t + 1, n_tiles - 1))
    shared = (kc_ref, vc_ref, kl_ref, vl_ref, tb_ref)

    @pl.when(t == 0)
    def _():
        qh = _na_head(q_ref[0], 0)
        sc = _dot_nt(qh, kc_ref[0])
        slot_a[0][...] = sc
        sl = _dot_nt(qh, kl_ref[0, pl.ds(tok_t, NA_WIN_ROWS * GRID_W), :])
        run_max = jnp.maximum(_lane_fold(sc, jnp.maximum), _na_biased(sl, 0, entry_t, 0, tb_ref, slot_a[1], 0))
        slot_a[2][...] = jnp.broadcast_to(jnp.max(run_max, axis=-1, keepdims=True), slot_a[2].shape)

    o0 = _na_unit((q_ref[0], 1, tok_t, entry_t, slot_b), (0, tok_t, slot_a), *shared)
    o1 = _na_unit((qn_ref[0], 0, tok_n, entry_n, slot_a), (1, tok_t, slot_b), *shared)
    lane = lax.broadcasted_iota(jnp.int32, o0.shape, 1)
    o_ref[0] = jnp.where(lane < NA_HEAD_DIM, o0, o1).astype(o_ref.dtype)


def _na_bias_blocks(rpb):
    H, n_dr, n_dc = rpb.shape
    col = jnp.arange(GRID_W)
    c0 = jnp.clip(col - WIN_W // 2, 0, GRID_W - WIN_W)
    col_ok = (col[None, :] >= c0[:, None]) & (col[None, :] < c0[:, None] + WIN_W)
    dc = jnp.clip(col[None, :] - col[:, None] + WIN_W - 1, 0, n_dc - 1)
    onehot = (dc.reshape(1, -1) == jnp.arange(n_dc)[:, None]).astype(f32)
    blk = jnp.einsum('hrj,jx->hrx', rpb.astype(f32) * math.log2(math.e), onehot, precision=lax.Precision.HIGHEST)
    blk = jnp.where(col_ok[None, None], blk.reshape(H, n_dr, GRID_W, GRID_W), NEG_INF)
    blk = jnp.concatenate([jnp.full((H, 1, GRID_W, GRID_W), NEG_INF, f32), blk], axis=1)
    n_e = 2 * WIN_H - 1
    left = [0] + list(range(1, n_e)) + [0] + list(range(1, n_e + 1)) + [0] * n_e
    right = [0] + list(range(2, n_e + 1)) + [0] + [0] * n_e + list(range(1, n_e + 1))
    tab = jnp.concatenate([jnp.take(blk, jnp.array(left), axis=1), jnp.take(blk, jnp.array(right), axis=1)], axis=-1)
    return tab.reshape((H // 2, 2) + tab.shape[1:])


def _na_attention(qkv_q, qkv_ctx, qkv_lat, bias):
    B, T, n3 = qkv_q.shape
    D = n3 // 3
    HP = D // LANES
    L = qkv_ctx.shape[1]
    ctx_k = pl.BlockSpec((1, L, LANES), lambda b, h, t: (b, 0, HP + h))
    ctx_v = pl.BlockSpec((1, L, LANES), lambda b, h, t: (b, 0, 2 * HP + h))
    if qkv_lat is None:
        tq = min(_tok_tile(T), 256)
        tile = pl.BlockSpec((1, tq, LANES), lambda b, h, t: (b, t, h))
        return pl.pallas_call(
            _na_ctx_kernel,
            grid=(B, HP, T // tq),
            in_specs=[tile, ctx_k, ctx_v],
            out_specs=tile,
            out_shape=jax.ShapeDtypeStruct((B, T, D), bf16),
            compiler_params=_cparams("parallel", "parallel", "parallel"),
            name="na_attention_ctx",
        )(qkv_q, qkv_ctx, qkv_ctx)
    tq = NA_TILE_ROWS * GRID_W
    n_tiles = T // tq
    n_win = NA_WIN_ROWS * GRID_W
    assert T % tq == 0 and T // GRID_W >= NA_WIN_ROWS and bias.shape[2] == 6 * WIN_H - 2
    tile = pl.BlockSpec((1, tq, LANES), lambda b, h, t: (b, t, h))
    slot = [pltpu.VMEM((tq, L), f32), pltpu.VMEM((tq, n_win), f32), pltpu.VMEM((tq, LANES), f32)]
    return pl.pallas_call(
        functools.partial(_na_kernel, n_tiles=n_tiles),
        grid=(B, HP, n_tiles),
        in_specs=[pl.BlockSpec(memory_space=pltpu.SMEM), tile,
                  pl.BlockSpec((1, tq, LANES), lambda b, h, t: (b, jnp.minimum(t + 1, n_tiles - 1), h)),
                  ctx_k, ctx_v,
                  pl.BlockSpec((1, T, LANES), lambda b, h, t: (b, 0, HP + h)),
                  pl.BlockSpec((1, T, LANES), lambda b, h, t: (b, 0, 2 * HP + h)),
                  pl.BlockSpec((1,) + bias.shape[1:], lambda b, h, t: (h, 0, 0, 0, 0))],
        out_specs=tile,
        out_shape=jax.ShapeDtypeStruct((B, T, D), bf16),
        scratch_shapes=slot + slot,
        compiler_params=_cparams("parallel", "parallel", "arbitrary"),
        name="na_attention",
    )(_na_window_table(T // GRID_W), qkv_q, qkv_q, qkv_ctx, qkv_ctx, qkv_lat, qkv_lat, bias)


def _chunk_cumsum(x, reverse):
    n = x.shape[0]
    pos = lax.broadcasted_iota(jnp.int32, x.shape, 0) & (HG_CHUNK - 1)
    k = 1
    while k < HG_CHUNK:
        if reverse:
            x = x + jnp.where(pos < HG_CHUNK - k, pltpu.roll(x, n - k, 0), 0.0)
        else:
            x = x + jnp.where(pos >= k, pltpu.roll(x, k, 0), 0.0)
        k *= 2
    return x


def _hg_mask(n, reverse):
    ri = lax.broadcasted_iota(jnp.int32, (n, n), 0)
    ci = lax.broadcasted_iota(jnp.int32, (n, n), 1)
    shift = HG_CHUNK.bit_length() - 1
    return ((ri >> shift) == (ci >> shift)) & ((ci >= ri) if reverse else (ci <= ri))


def _hg_block(z, v, q, lb, st, reverse, tri):
    with_out = q is not None
    n = z.shape[0]
    nch = n // HG_CHUNK
    f = lb + (1 - lb) * jax.nn.sigmoid(z)
    lf = jnp.log(f)
    kk = 1 - f
    g = _chunk_cumsum(lf, reverse)
    out = None
    if with_out:
        q_dec = q * jnp.exp(g)
        k_inv = kk * jnp.exp(-g)
        att = jnp.where(tri, _dot_nt(q_dec.astype(bf16), k_inv.astype(bf16)), 0.0)
        out_intra = _dot(att.astype(bf16), v.astype(bf16))
        outs = [None] * nch
    order = range(nch - 1, -1, -1) if reverse else range(nch)
    for c in order:
        sl = slice(c * HG_CHUNK, (c + 1) * HG_CHUNK)
        end = c * HG_CHUNK if reverse else (c + 1) * HG_CHUNK - 1
        g_tot = g[end:end + 1]
        if with_out:
            outs[c] = _dot_nt(q_dec[sl].astype(bf16), st.astype(bf16))
        kdec = kk[sl] * jnp.exp(g_tot - g[sl])
        ds_t = _dot(v[sl].T.astype(bf16), kdec.astype(bf16))
        st = st * jnp.exp(g_tot) + ds_t
    if with_out:
        out = out_intra + jnp.concatenate(outs, axis=0)
    return out, st


def _hg_kernel(q_ref, i_ref, gate_ref, zf_ref, zb_ref, ic_ref, zfc_ref, zbc_ref, lb_ref, gn_ref, o_ref,
               accf_ref, accb_ref, *, s_tok):
    lb = lb_ref[...]
    nblk = s_tok // HG_BLOCK
    zero = jnp.zeros((HG_EXPAND, HG_EXPAND), f32)
    _, st_f = _hg_block(zfc_ref[0], ic_ref[0], None, lb, zero, False, None)
    _, st_b = _hg_block(zbc_ref[0], ic_ref[0], None, lb, zero, True, None)
    tri_f, tri_b = _hg_mask(HG_BLOCK, False), _hg_mask(HG_BLOCK, True)

    def body(n, carry):
        st_f, st_b = carry
        rows_f = pl.ds(pl.multiple_of(n * HG_BLOCK, HG_BLOCK), HG_BLOCK)
        rows_b = pl.ds(pl.multiple_of((nblk - 1 - n) * HG_BLOCK, HG_BLOCK), HG_BLOCK)
        out_f, st_f = _hg_block(zf_ref[0, rows_f, :], i_ref[0, rows_f, :], q_ref[0, rows_f, :], lb, st_f, False, tri_f)
        out_b, st_b = _hg_block(zb_ref[0, rows_b, :], i_ref[0, rows_b, :], q_ref[0, rows_b, :], lb, st_b, True, tri_b)
        accf_ref[rows_f, :] = out_f
        accb_ref[rows_b, :] = out_b
        return st_f, st_b

    lax.fori_loop(0, nblk, body, (st_f, st_b), unroll=8 if nblk % 8 == 0 else 1)
    gate = gate_ref[0]
    o = accf_ref[...] + accb_ref[...]
    o_ref[0] = (_rms(o, gn_ref[...]) * (gate * jax.nn.sigmoid(gate))).astype(o_ref.dtype)


def _hgrn2_core(p_lat, p_ctx, lb, gn_g):
    B, S, n5 = p_lat.shape
    D = n5 // 5
    H = D // HG_EXPAND
    L = p_ctx.shape[1]
    assert S % HG_BLOCK == 0 and L % HG_CHUNK == 0

    def lat(k):
        return pl.BlockSpec((1, S, HG_EXPAND), lambda b, h: (b, 0, k * H + h))

    def ctx(k):
        return pl.BlockSpec((1, L, HG_EXPAND), lambda b, h: (b, 0, k * H + h))

    vec = pl.BlockSpec((1, HG_EXPAND), lambda b, h: (0, h))
    return pl.pallas_call(
        functools.partial(_hg_kernel, s_tok=S),
        grid=(B, H),
        in_specs=[lat(0), lat(1), lat(2), lat(3), lat(4), ctx(1), ctx(3), ctx(4), vec, _full((1, HG_EXPAND))],
        out_specs=pl.BlockSpec((1, S, HG_EXPAND), lambda b, h: (b, 0, h)),
        out_shape=jax.ShapeDtypeStruct((B, S, D), bf16),
        scratch_shapes=[pltpu.VMEM((S, HG_EXPAND), f32), pltpu.VMEM((S, HG_EXPAND), f32)],
        compiler_params=_cparams("parallel", "parallel"),
        name="hgrn2_core",
    )(p_lat, p_lat, p_lat, p_lat, p_lat, p_ctx, p_ctx, p_ctx, lb.reshape(1, D), gn_g.reshape(1, HG_EXPAND))


def kernel(x, c, ctx, c_ctx, w_ada, b_ada, g_pre, g_post, w_ff1, w_ff3, w_ff2, s5_a_re, s5_a_im, s5_log_dt, s5_b_re, s5_b_im, s5_c_re, s5_c_im, s5_d, s5_w_glu, s5_b_glu, da_w_qkv, da_w_o, da_lam_q1, da_lam_k1, da_lam_q2, da_lam_k2, da_subln, na_w_qkv, na_w_o, na_rpb, hg_w_qig, hg_w_f, hg_b_f, hg_lb_logits, hg_gnorm, hg_w_o):
    B, S, D = x.shape
    depth = w_ada.shape[0]
    n_mix = 4
    rows_pad = -(-(B + 1) // 8) * 8
    c_all = jnp.concatenate([c, c_ctx[None], jnp.zeros((rows_pad - B - 1, D), f32)], axis=0)
    mods = _ada(c_all, w_ada, b_ada)
    lb_p = jax.nn.softmax(hg_lb_logits.astype(f32), axis=0)
    lower_bounds = jnp.cumsum(lb_p, axis=0) - lb_p[0]
    w1b, w3b, w2b = w_ff1.astype(bf16), w_ff3.astype(bf16), w_ff2.astype(bf16)

    x_lat, x_ctx = x, ctx
    for i in range(depth):
        last = i == depth - 1
        occ, kind = i // n_mix, i % n_mix
        m_lat = mods[i, :B].reshape(B, 3 * N_SUB, D)
        m_ctx = mods[i, B:B + 1].reshape(1, 3 * N_SUB, D)
        ffn1 = functools.partial(_half_ffn, g_in=g_pre[i, 0], g_out=g_post[i, 0], w1=w1b[i, 0], w3=w3b[i, 0],
                                 w2=w2b[i, 0], j=0)
        ffn2 = functools.partial(_half_ffn, g_in=g_pre[i, 2], g_out=g_post[i, 2], w1=w1b[i, 1], w3=w3b[i, 1],
                                 w2=w2b[i, 1], j=2)
        x_lat, x_ctx = ffn1(x_lat, m_lat), ffn1(x_ctx, m_ctx)

        y_ctx = None
        if kind == 0:
            h_lat = _prenorm(x_lat, m_lat, g_pre[i, 1], 1)
            h_ctx = _prenorm(x_ctx, m_ctx, g_pre[i, 1], 1)
            y_lat, y_ctx = _s5_scan(h_lat, h_ctx, s5_a_re[occ], s5_a_im[occ], s5_log_dt[occ], s5_b_re[occ],
                                    s5_b_im[occ], s5_c_re[occ], s5_c_im[occ])
            mix = "glu"
            mix_head = (g_pre[i, 1], s5_d[occ])
            mix_tail = (g_post[i, 1], s5_w_glu[occ].astype(bf16), s5_b_glu[occ])
        elif kind == 1:
            wq = da_w_qkv[occ].astype(bf16)
            rope = _rope_tables(S, DIFF_HEAD_DIM)
            qkv_lat = _prenorm_proj(x_lat, m_lat, g_pre[i, 1], 1, wq, rope=rope, n_rope=2 * D)
            qkv_ctx = _prenorm_proj(x_ctx, m_ctx, g_pre[i, 1], 1, wq)
            lamv = jnp.stack([da_lam_q1[occ], da_lam_k1[occ], da_lam_q2[occ], da_lam_k2[occ]]).astype(f32)
            lam_init = 0.8 - 0.6 * math.exp(-0.3 * i)
            y_lat = _diff_attention(qkv_lat, qkv_ctx, qkv_lat, lamv, da_subln[occ], lam_init)
            if not last:
                y_ctx = _diff_attention(qkv_ctx, qkv_ctx, None, lamv, da_subln[occ], lam_init)
            mix, mix_head, mix_tail = "outproj", (), (g_post[i, 1], da_w_o[occ].astype(bf16))
        elif kind == 2:
            wq = na_w_qkv[occ].astype(bf16)
            qkv_lat = _prenorm_proj(x_lat, m_lat, g_pre[i, 1], 1, wq)
            qkv_ctx = _prenorm_proj(x_ctx, m_ctx, g_pre[i, 1], 1, wq)
            y_lat = _na_attention(qkv_lat, qkv_ctx, qkv_lat, _na_bias_blocks(na_rpb[occ]))
            if not last:
                y_ctx = _na_attention(qkv_ctx, qkv_ctx, None, None)
            mix, mix_head, mix_tail = "outproj", (), (g_post[i, 1], na_w_o[occ].astype(bf16))
        else:
            assert last, "HGRN2 context outputs are not needed when it is the last layer"
            wp = jnp.concatenate([hg_w_qig[occ], hg_w_f[occ, 0], hg_w_f[occ, 1]], axis=1).astype(bf16)
            bp = jnp.concatenate([jnp.zeros((3 * D,), f32), hg_b_f[occ, 0], hg_b_f[occ, 1]])
            p_lat = _prenorm_proj(x_lat, m_lat, g_pre[i, 1], 1, wp, bias=bp, out_dtype=f32)
            p_ctx = _prenorm_proj(x_ctx, m_ctx, g_pre[i, 1], 1, wp, bias=bp, out_dtype=f32)
            y_lat = _hgrn2_core(p_lat, p_ctx, lower_bounds[i], hg_gnorm[occ])
            mix, mix_head, mix_tail = "outproj", (), (g_post[i, 1], hg_w_o[occ].astype(bf16))

        x_lat = ffn2(x_lat, m_lat, mix=mix, mix_args=(y_lat,) + mix_head + mix_tail)
        if not last:
            x_ctx = ffn2(x_ctx, m_ctx, mix=mix, mix_args=(y_ctx,) + mix_head + mix_tail)
    return x_lat
```

```python
import functools
import math

import jax
import jax.numpy as jnp
from jax import lax
from jax.experimental import pallas as pl
from jax.experimental.pallas import tpu as pltpu

f32 = jnp.float32
bf16 = jnp.bfloat16

N_SUB = 3
RMS_EPS = 1e-6
NEG_INF = -1e30
ROPE_BASE = 10000.0
GRID_W = 64
S5_GROUP = 16
S5_STATE = 64
S5_CHUNK = 16
DIFF_HEAD_DIM = 64
DIFF_KEY_CHUNK = 1024
NA_HEAD_DIM = 64
WIN_H = 8
WIN_W = 16
NA_TILE_ROWS = 8
NA_WIN_ROWS = 16
NA_KEY_CHUNK = 512
HG_EXPAND = 128
HG_CHUNK = 64
HG_BLOCK = 256
LANES = 128
VMEM_LIMIT = 56 * 1024 * 1024


def _cparams(*sem):
    return pltpu.CompilerParams(dimension_semantics=sem, vmem_limit_bytes=VMEM_LIMIT)


def _dot(a, b):
    return jnp.dot(a, b, preferred_element_type=f32)


def _dot_nt(a, b):
    return lax.dot_general(a, b, (((1,), (1,)), ((), ())), preferred_element_type=f32)


def _rms(x, g):
    return x * lax.rsqrt(jnp.mean(jnp.square(x), axis=-1, keepdims=True) + RMS_EPS) * g


def _pre_norm(x, g, mod_ref, j):
    shift = mod_ref[0, 3 * j:3 * j + 1, :]
    scale = mod_ref[0, 3 * j + 1:3 * j + 2, :]
    return _rms(x, g) * (1 + scale) + shift


def _post_residual(x, y, g, mod_ref, j, weight):
    gate = mod_ref[0, 3 * j + 2:3 * j + 3, :]
    return x + weight * gate * _rms(y, g)


def _tok_tile(t):
    for tm in (512, 256, 128):
        if t % tm == 0:
            return tm
    raise ValueError(f"token count {t} is not a multiple of 128")


def _full(shape):
    return pl.BlockSpec(shape, lambda *_: (0,) * len(shape))


def _mod_spec(mod):
    nd = mod.shape[1]
    d = mod.shape[2]
    if mod.shape[0] == 1:
        return pl.BlockSpec((1, nd, d), lambda b, t: (0, 0, 0))
    return pl.BlockSpec((1, nd, d), lambda b, t: (b, 0, 0))


def _ada_kernel(c_ref, w_ref, b_ref, o_ref):
    c = c_ref[...]
    sc = (c * jax.nn.sigmoid(c)).astype(bf16)
    o_ref[0] = _dot(sc, w_ref[0].astype(bf16)) + b_ref[0]


def _ada(c_all, w_ada, b_ada):
    depth, d, n = w_ada.shape
    rows = c_all.shape[0]
    tn = n // 4
    return pl.pallas_call(
        _ada_kernel,
        grid=(depth, n // tn),
        in_specs=[pl.BlockSpec((rows, d), lambda i, t: (0, 0)),
                  pl.BlockSpec((1, d, tn), lambda i, t: (i, 0, t)),
                  pl.BlockSpec((1, 1, tn), lambda i, t: (i, 0, t))],
        out_specs=pl.BlockSpec((1, rows, tn), lambda i, t: (i, 0, t)),
        out_shape=jax.ShapeDtypeStruct((depth, rows, n), f32),
        compiler_params=_cparams("arbitrary", "arbitrary"),
        name="ada",
    )(c_all, w_ada, b_ada.reshape(depth, 1, n))


def _ffn_kernel(x_ref, mod_ref, gin_ref, gout_ref, w1_ref, w3_ref, w2_ref, *rest, j, chunks, mix):
    o_ref = rest[-1]
    x = x_ref[0]
    if mix == "outproj":
        y_ref, gmix_ref, wo_ref = rest[:-1]
        x = _post_residual(x, _dot(y_ref[0], wo_ref[...]), gmix_ref[...], mod_ref, 1, 1.0)
    elif mix == "glu":
        y_ref, gpre_ref, d_ref, gmix_ref, wg_ref, bg_ref = rest[:-1]
        z = jax.nn.gelu(d_ref[...] * _pre_norm(x, gpre_ref[...], mod_ref, 1) + y_ref[0])
        u = _dot(z.astype(bf16), wg_ref[...]) + bg_ref[...]
        x = _post_residual(x, z * jax.nn.sigmoid(u), gmix_ref[...], mod_ref, 1, 1.0)
    h = _pre_norm(x, gin_ref[...], mod_ref, j).astype(bf16)
    acc = None
    for s, n in chunks:
        a = _dot(h, w1_ref[:, s:s + n])
        b = _dot(h, w3_ref[:, s:s + n])
        g = (a * jax.nn.sigmoid(a) * b).astype(bf16)
        y = _dot(g, w2_ref[s:s + n, :])
        acc = y if acc is None else acc + y
    o_ref[0] = _post_residual(x, acc, gout_ref[...], mod_ref, j, 0.5)


def _ff_chunks(dff, step=1024):
    out, s = [], 0
    while s < dff:
        n = min(step, dff - s)
        out.append((s, n))
        s += n
    return tuple(out)


def _half_ffn(x, mod, g_in, g_out, w1, w3, w2, j, mix=None, mix_args=()):
    b, t, d = x.shape
    dff = w1.shape[1]
    tm = _tok_tile(t)
    tok = pl.BlockSpec((1, tm, d), lambda bb, tt: (bb, tt, 0))
    vec = _full((1, d))
    args = [x, mod, g_in.reshape(1, d), g_out.reshape(1, d), w1, w3, w2]
    specs = [tok, _mod_spec(mod), vec, vec, _full((d, dff)), _full((d, dff)), _full((dff, d))]
    if mix == "outproj":
        y, g_mix, w_o = mix_args
        args += [y, g_mix.reshape(1, d), w_o]
        specs += [tok, vec, _full((d, d))]
    elif mix == "glu":
        y, g_pre, d_skip, g_mix, w_g, b_g = mix_args
        args += [y, g_pre.reshape(1, d), d_skip.astype(f32).reshape(1, d), g_mix.reshape(1, d), w_g, b_g.reshape(1, d)]
        specs += [tok, vec, vec, vec, _full((d, d)), vec]
    return pl.pallas_call(
        functools.partial(_ffn_kernel, j=j, chunks=_ff_chunks(dff), mix=mix),
        grid=(b, t // tm),
        in_specs=specs,
        out_specs=tok,
        out_shape=jax.ShapeDtypeStruct((b, t, d), f32),
        compiler_params=_cparams("parallel", "parallel"),
        name="half_ffn",
    )(*args)


def _prenorm_kernel(x_ref, mod_ref, g_ref, o_ref, *, j):
    o_ref[0] = _pre_norm(x_ref[0], g_ref[...], mod_ref, j).astype(o_ref.dtype)


def _prenorm(x, mod, g, j):
    b, t, d = x.shape
    tm = _tok_tile(t)
    return pl.pallas_call(
        functools.partial(_prenorm_kernel, j=j),
        grid=(b, t // tm),
        in_specs=[pl.BlockSpec((1, tm, d), lambda bb, tt: (bb, tt, 0)), _mod_spec(mod), _full((1, d))],
        out_specs=pl.BlockSpec((1, tm, d), lambda bb, tt: (bb, tt, 0)),
        out_shape=jax.ShapeDtypeStruct((b, t, d), bf16),
        compiler_params=_cparams("parallel", "parallel"),
        name="prenorm",
    )(x, mod, g.reshape(1, d))


def _swap_pairs(x):
    lane = lax.broadcasted_iota(jnp.int32, x.shape, 1)
    nxt = pltpu.roll(x, LANES - 1, 1)
    prv = pltpu.roll(x, 1, 1)
    return jnp.where((lane & 1) == 0, nxt, prv)


def _proj_kernel(*refs, j, n_out, n_rope, has_bias, step):
    x_ref, mod_ref, g_ref, w_ref = refs[:4]
    k = 4
    b_ref = None
    if has_bias:
        b_ref = refs[k]
        k += 1
    if n_rope:
        cos_ref, sin_ref = refs[k], refs[k + 1]
        k += 2
    o_ref = refs[k]
    h = _pre_norm(x_ref[0], g_ref[...], mod_ref, j).astype(bf16)
    for s in range(0, n_out, step):
        y = _dot(h, w_ref[:, s:s + step])
        if has_bias:
            y = y + b_ref[:, s:s + step]
        if s < n_rope:
            cos = cos_ref[...]
            sin = sin_ref[...]
            parts = []
            for c in range(0, step, LANES):
                yc = y[:, c:c + LANES]
                parts.append(yc * cos + _swap_pairs(yc) * sin)
            y = jnp.concatenate(parts, axis=1)
        o_ref[0, :, s:s + step] = y.astype(o_ref.dtype)


def _prenorm_proj(x, mod, g, j, w, bias=None, rope=None, n_rope=0, out_dtype=bf16):
    b, t, d = x.shape
    n = w.shape[1]
    tm = min(_tok_tile(t), 256)
    step = 512
    assert n % step == 0 and n_rope % step == 0
    args = [x, mod, g.reshape(1, d), w]
    specs = [pl.BlockSpec((1, tm, d), lambda bb, tt: (bb, tt, 0)), _mod_spec(mod), _full((1, d)), _full((d, n))]
    if bias is not None:
        args.append(bias.reshape(1, n))
        specs.append(_full((1, n)))
    if n_rope:
        args += list(rope)
        specs += [pl.BlockSpec((tm, LANES), lambda bb, tt: (tt, 0))] * 2
    return pl.pallas_call(
        functools.partial(_proj_kernel, j=j, n_out=n, n_rope=n_rope, has_bias=bias is not None, step=step),
        grid=(b, t // tm),
        in_specs=specs,
        out_specs=pl.BlockSpec((1, tm, n), lambda bb, tt: (bb, tt, 0)),
        out_shape=jax.ShapeDtypeStruct((b, t, n), out_dtype),
        compiler_params=_cparams("parallel", "parallel"),
        name="prenorm_proj",
    )(*args)


def _s5_kernel(xc_ref, xl_ref, wb_ref, wk_ref, wc_ref, lam_ref, yc_ref, yl_ref, s_ref, hin_ref, *, nb, nc_ctx, nc):
    rc = nc_ctx * nb
    xc, xl = xc_ref[0], xl_ref[0]
    s_ref[0:rc, :] = _dot(xc, wb_ref[0])
    s_ref[rc:, :] = _dot(xl, wb_ref[0])
    P = S5_STATE
    lam = lam_ref[0]
    a_re, a_im = lam[0:1], lam[1:2]
    fwd_lane = lax.broadcasted_iota(jnp.int32, (nb, 2 * P), 1) < P

    def step(n, carry):
        h_re, h_im = carry
        r_f = pl.multiple_of(n * nb, nb)
        n_b = jnp.where(n < nc_ctx, nc_ctx - 1 - n, nc - 1 - (n - nc_ctx))
        r_b = pl.multiple_of(n_b * nb, nb)
        rows_f, rows_b = pl.ds(r_f, nb), pl.ds(r_b, nb)
        hin_ref[rows_f, 0:P] = h_re[:, 0:P]
        hin_ref[rows_b, P:2 * P] = h_re[:, P:2 * P]
        hin_ref[rows_f, 2 * P:3 * P] = h_im[:, 0:P]
        hin_ref[rows_b, 3 * P:4 * P] = h_im[:, P:2 * P]
        s_re = jnp.where(fwd_lane, s_ref[rows_f, 0:2 * P], s_ref[rows_b, 0:2 * P])
        s_im = jnp.where(fwd_lane, s_ref[rows_f, 2 * P:4 * P], s_ref[rows_b, 2 * P:4 * P])
        return a_re * h_re - a_im * h_im + s_re, a_re * h_im + a_im * h_re + s_im

    zero = jnp.zeros((nb, 2 * P), f32)
    lax.fori_loop(0, nc, step, (zero, zero), unroll=4 if nc % 4 == 0 else 1)
    yc_ref[0] = _dot(xc, wk_ref[0]) + _dot(hin_ref[0:rc, :].astype(bf16), wc_ref[0])
    yl_ref[0] = _dot(xl, wk_ref[0]) + _dot(hin_ref[rc:, :].astype(bf16), wc_ref[0])

def _s5_weights(a_re, a_im, log_dt, b_re, b_im, c_re, c_im):
    hp = lax.Precision.HIGHEST
    C = S5_CHUNK
    G, P = a_re.shape[1:]
    N = b_re.shape[-1]
    j = jnp.arange(C + 1, dtype=f32)
    wbs, wks, wcs, lams = [], [], [], []
    for dirn in range(2):
        are, aim = a_re[dirn].astype(f32), a_im[dirn].astype(f32)
        dt = jnp.exp(log_dt[dirn].astype(f32))[:, None]
        pw_mag = jnp.exp(j[:, None, None] * (are * dt)[None])
        pw_re = pw_mag * jnp.cos(j[:, None, None] * (aim * dt)[None])
        pw_im = pw_mag * jnp.sin(j[:, None, None] * (aim * dt)[None])
        nr, ni = pw_re[1] - 1.0, pw_im[1]
        den = are * are + aim * aim
        fr = (nr * are + ni * aim) / den
        fi = (ni * are - nr * aim) / den
        bre, bim = b_re[dirn].astype(f32), b_im[dirn].astype(f32)
        bbr = fr[..., None] * bre - fi[..., None] * bim
        bbi = fr[..., None] * bim + fi[..., None] * bre
        cre, cim = c_re[dirn].astype(f32), c_im[dirn].astype(f32)
        dr, di = (pw_re[:C][::-1], pw_im[:C][::-1]) if dirn == 0 else (pw_re[:C], pw_im[:C])
        sr = dr[..., None] * bbr[None] - di[..., None] * bbi[None]
        si = dr[..., None] * bbi[None] + di[..., None] * bbr[None]
        wb = jnp.concatenate([sr, si], axis=2)
        wbs.append(wb.transpose(1, 0, 3, 2).reshape(G, C * N, 2 * P))
        pr, pi = pw_re[1:], pw_im[1:]
        if dirn == 1:
            pr, pi = pr[::-1], pi[::-1]
        or_ = cre[None] * pr[:, :, None, :] - cim[None] * pi[:, :, None, :]
        oi_ = -(cre[None] * pi[:, :, None, :] + cim[None] * pr[:, :, None, :])
        wc = jnp.concatenate([or_, oi_], axis=3)
        wcs.append(wc.transpose(1, 3, 0, 2).reshape(G, 2 * P, C * N))
        clr = cre[None] * pw_re[:C, :, None, :] - cim[None] * pw_im[:C, :, None, :]
        cli = cre[None] * pw_im[:C, :, None, :] + cim[None] * pw_re[:C, :, None, :]
        kk = (jnp.einsum('jgnp,gpm->gjnm', clr, bbr, precision=hp)
              - jnp.einsum('jgnp,gpm->gjnm', cli, bbi, precision=hp))
        s_idx = jnp.arange(C)[:, None]
        t_idx = jnp.arange(C)[None, :]
        lag = (t_idx - s_idx) if dirn == 0 else (s_idx - t_idx)
        onehot = (lag[:, :, None] == jnp.arange(C)[None, None, :]).astype(f32)
        kt = jnp.einsum('stj,gjnm->gsmtn', onehot, kk, precision=hp)
        wks.append(kt.reshape(G, C * N, C * N))
        lams.append((pw_re[C], pw_im[C]))
    wb = jnp.concatenate([wbs[0][..., :P], wbs[1][..., :P], wbs[0][..., P:], wbs[1][..., P:]], axis=2).astype(bf16)
    wc = jnp.concatenate([wcs[0][:, :P], wcs[1][:, :P], wcs[0][:, P:], wcs[1][:, P:]], axis=1).astype(bf16)
    wk = (wks[0] + wks[1]).astype(bf16)
    lam = jnp.stack([jnp.concatenate([lams[0][0], lams[1][0]], axis=-1),
                     jnp.concatenate([lams[0][1], lams[1][1]], axis=-1)], axis=1)
    return wb, wk, wc, lam


def _s5_scan(h_lat, h_ctx, a_re, a_im, log_dt, b_re, b_im, c_re, c_im):
    B, S, D = h_lat.shape
    L = h_ctx.shape[1]
    C, N, P = S5_CHUNK, S5_GROUP, S5_STATE
    G = D // N
    nc_ctx, nc = L // C, (L + S) // C
    assert 2 * P == LANES and L % C == 0 and S % C == 0
    wb, wk, wc, lam = _s5_weights(a_re, a_im, log_dt, b_re, b_im, c_re, c_im)

    def to_groups(h):
        n = h.shape[1] // C
        return h.reshape(B, n, C, G, N).transpose(3, 1, 0, 2, 4).reshape(G, n * B, C * N)

    def from_groups(y):
        n = y.shape[1] // B
        return y.reshape(G, n, B, C, N).transpose(2, 1, 3, 0, 4).reshape(B, n * C, D)

    rc, rl = nc_ctx * B, (nc - nc_ctx) * B

    def rows(r):
        return pl.BlockSpec((1, r, C * N), lambda g: (g, 0, 0))

    yc, yl = pl.pallas_call(
        functools.partial(_s5_kernel, nb=B, nc_ctx=nc_ctx, nc=nc),
        grid=(G,),
        in_specs=[rows(rc), rows(rl),
                  pl.BlockSpec((1, C * N, 4 * P), lambda g: (g, 0, 0)),
                  pl.BlockSpec((1, C * N, C * N), lambda g: (g, 0, 0)),
                  pl.BlockSpec((1, 4 * P, C * N), lambda g: (g, 0, 0)),
                  pl.BlockSpec((1, 2, 2 * P), lambda g: (g, 0, 0))],
        out_specs=[rows(rc), rows(rl)],
        out_shape=[jax.ShapeDtypeStruct((G, rc, C * N), f32), jax.ShapeDtypeStruct((G, rl, C * N), f32)],
        scratch_shapes=[pltpu.VMEM((rc + rl, 4 * P), f32), pltpu.VMEM((rc + rl, 4 * P), f32)],
        compiler_params=_cparams("parallel"),
        name="s5_scan",
    )(to_groups(h_ctx), to_groups(h_lat), wb, wk, wc, lam)
    return from_groups(yl), from_groups(yc)


def _diff_maps(q):
    lane = lax.broadcasted_iota(jnp.int32, q.shape, 1)
    qs = (q.astype(f32) * (DIFF_HEAD_DIM ** -0.5 * math.log2(math.e))).astype(bf16)
    zero = jnp.zeros_like(qs)
    return jnp.where(lane < DIFF_HEAD_DIM, qs, zero), jnp.where(lane >= DIFF_HEAD_DIM, qs, zero)


def _key_chunks(k_refs):
    out = []
    for seg, k_ref in enumerate(k_refs):
        n = k_ref.shape[1]
        out += [(seg, c, min(c + DIFF_KEY_CHUNK, n)) for c in range(0, n, DIFF_KEY_CHUNK)]
    return out


def _diff_step(q_next, cur, nxt, k_refs, v_refs, lamv_ref, g_ref, o_ref, lam_init):
    chunks = _key_chunks(k_refs)
    cur_s, cur_m = cur
    nxt_s, nxt_m = nxt
    tq = cur_m.shape[1]
    m1, m2 = cur_m[0, :, 0:1], cur_m[1, :, 0:1]
    q1n, q2n = _diff_maps(q_next)
    n1 = n2 = l1 = l2 = None
    for seg, a, b in chunks:
        s1 = _dot_nt(q1n, k_refs[seg][0, a:b, :])
        s2 = _dot_nt(q2n, k_refs[seg][0, a:b, :])
        nxt_s[seg][0, :, a:b] = s1
        nxt_s[seg][1, :, a:b] = s2
        c1 = jnp.max(s1, axis=-1, keepdims=True)
        c2 = jnp.max(s2, axis=-1, keepdims=True)
        n1 = c1 if n1 is None else jnp.maximum(n1, c1)
        n2 = c2 if n2 is None else jnp.maximum(n2, c2)
        p1 = jnp.exp2(cur_s[seg][0, :, a:b] - m1)
        p2 = jnp.exp2(cur_s[seg][1, :, a:b] - m2)
        cur_s[seg][0, :, a:b] = p1
        cur_s[seg][1, :, a:b] = p2
        r1 = jnp.sum(p1, axis=-1, keepdims=True)
        r2 = jnp.sum(p2, axis=-1, keepdims=True)
        l1 = r1 if l1 is None else l1 + r1
        l2 = r2 if l2 is None else l2 + r2
    nxt_m[0] = jnp.broadcast_to(n1, (tq, LANES))
    nxt_m[1] = jnp.broadcast_to(n2, (tq, LANES))
    lv = lamv_ref[...]
    lam = (jnp.exp(jnp.sum(lv[0:1] * lv[1:2], axis=-1, keepdims=True))
           - jnp.exp(jnp.sum(lv[2:3] * lv[3:4], axis=-1, keepdims=True)) + lam_init)
    c = lam * l1 / l2
    o = None
    for seg, a, b in chunks:
        w = (cur_s[seg][0, :, a:b] - c * cur_s[seg][1, :, a:b]).astype(bf16)
        part = _dot(w, v_refs[seg][0, a:b, :])
        o = part if o is None else o + part
    o = o / l1
    o_ref[0] = (_rms(o, g_ref[...]) * (1 - lam_init)).astype(o_ref.dtype)


def _diff_kernel(*refs, n_seg, lam_init):
    q_ref, qn_ref = refs[:2]
    k_refs = refs[2:2 + 3 * n_seg:3]
    v_refs = refs[3:3 + 3 * n_seg:3]
    kn_refs = refs[4:4 + 3 * n_seg:3]
    lamv_ref, g_ref, o_ref = refs[2 + 3 * n_seg:5 + 3 * n_seg]
    scratch = refs[5 + 3 * n_seg:]
    slots = tuple((scratch[p * (n_seg + 1):p * (n_seg + 1) + n_seg], scratch[p * (n_seg + 1) + n_seg])
                  for p in range(2))
    unit = (pl.program_id(0) * pl.num_programs(1) + pl.program_id(1)) * pl.num_programs(2) + pl.program_id(2)

    @pl.when(unit == 0)
    def _():
        q1, q2 = _diff_maps(q_ref[0])
        s_refs, m_ref = slots[0]
        for mp, qm in enumerate((q1, q2)):
            parts = [_dot_nt(qm, k_ref[0]) for k_ref in k_refs]
            for s_ref, sc in zip(s_refs, parts):
                s_ref[mp] = sc
            mx = functools.reduce(jnp.maximum, [jnp.max(sc, axis=-1, keepdims=True) for sc in parts])
            m_ref[mp] = jnp.broadcast_to(mx, m_ref.shape[1:])

    for parity in range(2):
        @pl.when((unit & 1) == parity)
        def _(parity=parity):
            _diff_step(qn_ref[0], slots[parity], slots[1 - parity], kn_refs, v_refs, lamv_ref, g_ref, o_ref,
                       lam_init)


def _diff_attention(qkv_q, qkv_ctx, qkv_lat, lamv, subln, lam_init):
    B, T, n3 = qkv_q.shape
    D = n3 // 3
    H = D // LANES
    tq = min(_tok_tile(T), 512)
    nq = T // tq
    kv = [qkv_ctx] if qkv_lat is None else [qkv_ctx, qkv_lat]

    def following(b, h, t):
        u = jnp.minimum((b * H + h) * nq + t + 1, B * H * nq - 1)
        return u // (H * nq), (u // nq) % H, u % nq

    def q_next(b, h, t):
        nb, nh, nt = following(b, h, t)
        return nb, nt, nh

    def k_next(b, h, t):
        nb, nh, _ = following(b, h, t)
        return nb, 0, H + nh

    args = [qkv_q, qkv_q]
    specs = [pl.BlockSpec((1, tq, LANES), lambda b, h, t: (b, t, h)), pl.BlockSpec((1, tq, LANES), q_next)]
    for a in kv:
        args += [a, a, a]
        specs += [pl.BlockSpec((1, a.shape[1], LANES), lambda b, h, t: (b, 0, H + h)),
                  pl.BlockSpec((1, a.shape[1], LANES), lambda b, h, t: (b, 0, 2 * H + h)),
                  pl.BlockSpec((1, a.shape[1], LANES), k_next)]
    args += [lamv, subln.reshape(1, LANES)]
    specs += [_full(lamv.shape), _full((1, LANES))]
    slot = [pltpu.VMEM((2, tq, a.shape[1]), f32) for a in kv] + [pltpu.VMEM((2, tq, LANES), f32)]
    return pl.pallas_call(
        functools.partial(_diff_kernel, n_seg=len(kv), lam_init=lam_init),
        grid=(B, H, nq),
        in_specs=specs,
        out_specs=pl.BlockSpec((1, tq, LANES), lambda b, h, t: (b, t, h)),
        out_shape=jax.ShapeDtypeStruct((B, T, D), bf16),
        scratch_shapes=slot + slot,
        compiler_params=_cparams("arbitrary", "arbitrary", "arbitrary"),
        name="diff_attention",
    )(*args)


def _rope_tables(n_tokens, head_dim):
    n_freq = head_dim // 4
    inv_freq = ROPE_BASE ** (-jnp.arange(n_freq, dtype=f32) / n_freq)
    t = jnp.arange(n_tokens)
    row = (t // GRID_W).astype(f32)
    col = (t % GRID_W).astype(f32)
    ang = jnp.concatenate([row[:, None] * inv_freq, col[:, None] * inv_freq], axis=-1)
    cos = jnp.repeat(jnp.cos(ang), 2, axis=-1)
    sin = jnp.repeat(jnp.sin(ang), 2, axis=-1) * jnp.tile(jnp.array([-1.0, 1.0], f32), head_dim // 2)
    reps = LANES // head_dim
    return jnp.tile(cos, (1, reps)), jnp.tile(sin, (1, reps))


def _na_head(q, hh):
    lane = lax.broadcasted_iota(jnp.int32, q.shape, 1)
    qs = (q.astype(f32) * (NA_HEAD_DIM ** -0.5 * math.log2(math.e))).astype(bf16)
    sel = (lane < NA_HEAD_DIM) if hh == 0 else (lane >= NA_HEAD_DIM)
    return jnp.where(sel, qs, jnp.zeros_like(qs))


def _na_ctx_kernel(q_ref, kc_ref, vc_ref, o_ref):
    q = q_ref[0]
    lane = lax.broadcasted_iota(jnp.int32, q.shape, 1)
    outs = []
    for hh in range(2):
        s = _dot_nt(_na_head(q, hh), kc_ref[0])
        p = jnp.exp2(s - jnp.max(s, axis=-1, keepdims=True))
        outs.append(_dot(p.astype(bf16), vc_ref[0]) / jnp.sum(p, axis=-1, keepdims=True))
    o_ref[0] = jnp.where(lane < NA_HEAD_DIM, outs[0], outs[1]).astype(o_ref.dtype)


def _na_window_table(rows):
    table = []
    for t in range(rows // NA_TILE_ROWS):
        w0 = min(max(t * NA_TILE_ROWS - (NA_WIN_ROWS - NA_TILE_ROWS) // 2, 0), rows - NA_WIN_ROWS)
        line = [w0 * GRID_W]
        for i in range(NA_TILE_ROWS):
            qr = t * NA_TILE_ROWS + i
            r0 = min(max(qr - WIN_H // 2, 0), rows - WIN_H)
            for j in range(NA_WIN_ROWS // 2):
                kr = w0 + 2 * j
                e = kr - qr + WIN_H
                ok0 = r0 <= kr < r0 + WIN_H
                ok1 = r0 <= kr + 1 < r0 + WIN_H
                line.append(e if ok0 and ok1 else 2 * WIN_H - 1 + e if ok0 else 4 * WIN_H - 1 + e if ok1 else 0)
        table.append(line)
    return jnp.array(table, jnp.int32)


def _na_window(win_ref, t):
    n_pairs = NA_WIN_ROWS // 2
    entry = [[win_ref[t, 1 + i * n_pairs + j] for j in range(n_pairs)] for i in range(NA_TILE_ROWS)]
    return pl.multiple_of(win_ref[t, 0], 4 * GRID_W), entry


def _lane_fold(x, op):
    return functools.reduce(op, [x[:, c:c + LANES] for c in range(0, x.shape[1], LANES)])


def _na_biased(sl, hh, entry, j0, tb_ref, s_ref, c0):
    tiles = []
    for i in range(NA_TILE_ROWS):
        r = slice(i * GRID_W, (i + 1) * GRID_W)
        mx = None
        for jj in range(sl.shape[1] // LANES):
            blk = sl[r, jj * LANES:(jj + 1) * LANES] + tb_ref[0, hh, entry[i][j0 + jj]]
            s_ref[r, c0 + jj * LANES:c0 + (jj + 1) * LANES] = blk
            mx = blk if mx is None else jnp.maximum(mx, blk)
        tiles.append(mx)
    return jnp.concatenate(tiles, axis=0)


def _na_unit(make, use, kc_ref, vc_ref, kl_ref, vl_ref, tb_ref):
    q, hh_m, tok_m, entry, (mc_ref, ml_ref, mm_ref) = make
    hh_u, tok_u, (uc_ref, ul_ref, um_ref) = use
    qh = _na_head(q, hh_m)
    m = um_ref[:, 0:1]
    run_max = l = o = None
    for c0 in [None] + list(range(0, NA_WIN_ROWS * GRID_W, NA_KEY_CHUNK)):
        if c0 is None:
            sc = _dot_nt(qh, kc_ref[0])
            mc_ref[...] = sc
            run_max = _lane_fold(sc, jnp.maximum)
            p = jnp.exp2(uc_ref[...] - m)
            v = vc_ref[0]
        else:
            sl = _dot_nt(qh, kl_ref[0, pl.ds(tok_m + c0, NA_KEY_CHUNK), :])
            run_max = jnp.maximum(run_max, _na_biased(sl, hh_m, entry, c0 // LANES, tb_ref, ml_ref, c0))
            p = jnp.exp2(ul_ref[:, c0:c0 + NA_KEY_CHUNK] - m)
            v = vl_ref[0, pl.ds(tok_u + c0, NA_KEY_CHUNK), :]
        r = _lane_fold(p, jnp.add)
        part = _dot(p.astype(bf16), v)
        l = r if l is None else l + r
        o = part if o is None else o + part
    mm_ref[...] = jnp.broadcast_to(jnp.max(run_max, axis=-1, keepdims=True), mm_ref.shape)
    return o / jnp.sum(l, axis=-1, keepdims=True)


def _na_kernel(win_ref, q_ref, qn_ref, kc_ref, vc_ref, kl_ref, vl_ref, tb_ref, o_ref, *scratch, n_tiles):
    slot_a, slot_b = scratch[:3], scratch[3:]
    t = pl.program_id(2)
    tok_t, entry_t = _na_window(win_ref, t)
    tok_n, entry_n = _na_window(win_ref, jnp.minimum(t + 1, n_tiles - 1))
    shared = (kc_ref, vc_ref, kl_ref, vl_ref, tb_ref)

    @pl.when(t == 0)
    def _():
        qh = _na_head(q_ref[0], 0)
        sc = _dot_nt(qh, kc_ref[0])
        slot_a[0][...] = sc
        sl = _dot_nt(qh, kl_ref[0, pl.ds(tok_t, NA_WIN_ROWS * GRID_W), :])
        run_max = jnp.maximum(_lane_fold(sc, jnp.maximum), _na_biased(sl, 0, entry_t, 0, tb_ref, slot_a[1], 0))
        slot_a[2][...] = jnp.broadcast_to(jnp.max(run_max, axis=-1, keepdims=True), slot_a[2].shape)

    o0 = _na_unit((q_ref[0], 1, tok_t, entry_t, slot_b), (0, tok_t, slot_a), *shared)
    o1 = _na_unit((qn_ref[0], 0, tok_n, entry_n, slot_a), (1, tok_t, slot_b), *shared)
    lane = lax.broadcasted_iota(jnp.int32, o0.shape, 1)
    o_ref[0] = jnp.where(lane < NA_HEAD_DIM, o0, o1).astype(o_ref.dtype)


def _na_bias_blocks(rpb):
    H, n_dr, n_dc = rpb.shape
    col = jnp.arange(GRID_W)
    c0 = jnp.clip(col - WIN_W // 2, 0, GRID_W - WIN_W)
    col_ok = (col[None, :] >= c0[:, None]) & (col[None, :] < c0[:, None] + WIN_W)
    dc = jnp.clip(col[None, :] - col[:, None] + WIN_W - 1, 0, n_dc - 1)
    onehot = (dc.reshape(1, -1) == jnp.arange(n_dc)[:, None]).astype(f32)
    blk = jnp.einsum('hrj,jx->hrx', rpb.astype(f32) * math.log2(math.e), onehot, precision=lax.Precision.HIGHEST)
    blk = jnp.where(col_ok[None, None], blk.reshape(H, n_dr, GRID_W, GRID_W), NEG_INF)
    blk = jnp.concatenate([jnp.full((H, 1, GRID_W, GRID_W), NEG_INF, f32), blk], axis=1)
    n_e = 2 * WIN_H - 1
    left = [0] + list(range(1, n_e)) + [0] + list(range(1, n_e + 1)) + [0] * n_e
    right = [0] + list(range(2, n_e + 1)) + [0] + [0] * n_e + list(range(1, n_e + 1))
    tab = jnp.concatenate([jnp.take(blk, jnp.array(left), axis=1), jnp.take(blk, jnp.array(right), axis=1)], axis=-1)
    return tab.reshape((H // 2, 2) + tab.shape[1:])


def _na_attention(qkv_q, qkv_ctx, qkv_lat, bias):
    B, T, n3 = qkv_q.shape
    D = n3 // 3
    HP = D // LANES
    L = qkv_ctx.shape[1]
    ctx_k = pl.BlockSpec((1, L, LANES), lambda b, h, t: (b, 0, HP + h))
    ctx_v = pl.BlockSpec((1, L, LANES), lambda b, h, t: (b, 0, 2 * HP + h))
    if qkv_lat is None:
        tq = min(_tok_tile(T), 256)
        tile = pl.BlockSpec((1, tq, LANES), lambda b, h, t: (b, t, h))
        return pl.pallas_call(
            _na_ctx_kernel,
            grid=(B, HP, T // tq),
            in_specs=[tile, ctx_k, ctx_v],
            out_specs=tile,
            out_shape=jax.ShapeDtypeStruct((B, T, D), bf16),
            compiler_params=_cparams("parallel", "parallel", "parallel"),
            name="na_attention_ctx",
        )(qkv_q, qkv_ctx, qkv_ctx)
    tq = NA_TILE_ROWS * GRID_W
    n_tiles = T // tq
    n_win = NA_WIN_ROWS * GRID_W
    assert T % tq == 0 and T // GRID_W >= NA_WIN_ROWS and bias.shape[2] == 6 * WIN_H - 2
    tile = pl.BlockSpec((1, tq, LANES), lambda b, h, t: (b, t, h))
    slot = [pltpu.VMEM((tq, L), f32), pltpu.VMEM((tq, n_win), f32), pltpu.VMEM((tq, LANES), f32)]
    return pl.pallas_call(
        functools.partial(_na_kernel, n_tiles=n_tiles),
        grid=(B, HP, n_tiles),
        in_specs=[pl.BlockSpec(memory_space=pltpu.SMEM), tile,
                  pl.BlockSpec((1, tq, LANES), lambda b, h, t: (b, jnp.minimum(t + 1, n_tiles - 1), h)),
                  ctx_k, ctx_v,
                  pl.BlockSpec((1, T, LANES), lambda b, h, t: (b, 0, HP + h)),
                  pl.BlockSpec((1, T, LANES), lambda b, h, t: (b, 0, 2 * HP + h)),
                  pl.BlockSpec((1,) + bias.shape[1:], lambda b, h, t: (h, 0, 0, 0, 0))],
        out_specs=tile,
        out_shape=jax.ShapeDtypeStruct((B, T, D), bf16),
        scratch_shapes=slot + slot,
        compiler_params=_cparams("parallel", "parallel", "arbitrary"),
        name="na_attention",
    )(_na_window_table(T // GRID_W), qkv_q, qkv_q, qkv_ctx, qkv_ctx, qkv_lat, qkv_lat, bias)


def _chunk_cumsum(x, reverse):
    n = x.shape[0]
    pos = lax.broadcasted_iota(jnp.int32, x.shape, 0) & (HG_CHUNK - 1)
    k = 1
    while k < HG_CHUNK:
        if reverse:
            x = x + jnp.where(pos < HG_CHUNK - k, pltpu.roll(x, n - k, 0), 0.0)
        else:
            x = x + jnp.where(pos >= k, pltpu.roll(x, k, 0), 0.0)
        k *= 2
    return x


def _hg_mask(n, reverse):
    ri = lax.broadcasted_iota(jnp.int32, (n, n), 0)
    ci = lax.broadcasted_iota(jnp.int32, (n, n), 1)
    shift = HG_CHUNK.bit_length() - 1
    return ((ri >> shift) == (ci >> shift)) & ((ci >= ri) if reverse else (ci <= ri))


def _hg_block(z, v, q, lb, st, reverse, tri):
    with_out = q is not None
    n = z.shape[0]
    nch = n // HG_CHUNK
    f = lb + (1 - lb) * jax.nn.sigmoid(z)
    lf = jnp.log(f)
    kk = 1 - f
    g = _chunk_cumsum(lf, reverse)
    out = None
    if with_out:
        q_dec = q * jnp.exp(g)
        k_inv = kk * jnp.exp(-g)
        att = jnp.where(tri, _dot_nt(q_dec.astype(bf16), k_inv.astype(bf16)), 0.0)
        out_intra = _dot(att.astype(bf16), v.astype(bf16))
        outs = [None] * nch
    order = range(nch - 1, -1, -1) if reverse else range(nch)
    for c in order:
        sl = slice(c * HG_CHUNK, (c + 1) * HG_CHUNK)
        end = c * HG_CHUNK if reverse else (c + 1) * HG_CHUNK - 1
        g_tot = g[end:end + 1]
        if with_out:
            outs[c] = _dot_nt(q_dec[sl].astype(bf16), st.astype(bf16))
        kdec = kk[sl] * jnp.exp(g_tot - g[sl])
        ds_t = _dot(v[sl].T.astype(bf16), kdec.astype(bf16))
        st = st * jnp.exp(g_tot) + ds_t
    if with_out:
        out = out_intra + jnp.concatenate(outs, axis=0)
    return out, st


def _hg_kernel(q_ref, i_ref, gate_ref, zf_ref, zb_ref, ic_ref, zfc_ref, zbc_ref, lb_ref, gn_ref, o_ref,
               accf_ref, accb_ref, *, s_tok):
    lb = lb_ref[...]
    nblk = s_tok // HG_BLOCK
    zero = jnp.zeros((HG_EXPAND, HG_EXPAND), f32)
    _, st_f = _hg_block(zfc_ref[0], ic_ref[0], None, lb, zero, False, None)
    _, st_b = _hg_block(zbc_ref[0], ic_ref[0], None, lb, zero, True, None)
    tri_f, tri_b = _hg_mask(HG_BLOCK, False), _hg_mask(HG_BLOCK, True)

    def body(n, carry):
        st_f, st_b = carry
        rows_f = pl.ds(pl.multiple_of(n * HG_BLOCK, HG_BLOCK), HG_BLOCK)
        rows_b = pl.ds(pl.multiple_of((nblk - 1 - n) * HG_BLOCK, HG_BLOCK), HG_BLOCK)
        out_f, st_f = _hg_block(zf_ref[0, rows_f, :], i_ref[0, rows_f, :], q_ref[0, rows_f, :], lb, st_f, False, tri_f)
        out_b, st_b = _hg_block(zb_ref[0, rows_b, :], i_ref[0, rows_b, :], q_ref[0, rows_b, :], lb, st_b, True, tri_b)
        accf_ref[rows_f, :] = out_f
        accb_ref[rows_b, :] = out_b
        return st_f, st_b

    lax.fori_loop(0, nblk, body, (st_f, st_b), unroll=16 if nblk % 16 == 0 else 1)
    gate = gate_ref[0]
    o = accf_ref[...] + accb_ref[...]
    o_ref[0] = (_rms(o, gn_ref[...]) * (gate * jax.nn.sigmoid(gate))).astype(o_ref.dtype)


def _hgrn2_core(p_lat, p_ctx, lb, gn_g):
    B, S, n5 = p_lat.shape
    D = n5 // 5
    H = D // HG_EXPAND
    L = p_ctx.shape[1]
    assert S % HG_BLOCK == 0 and L % HG_CHUNK == 0

    def lat(k):
        return pl.BlockSpec((1, S, HG_EXPAND), lambda b, h: (b, 0, k * H + h))

    def ctx(k):
        return pl.BlockSpec((1, L, HG_EXPAND), lambda b, h: (b, 0, k * H + h))

    vec = pl.BlockSpec((1, HG_EXPAND), lambda b, h: (0, h))
    return pl.pallas_call(
        functools.partial(_hg_kernel, s_tok=S),
        grid=(B, H),
        in_specs=[lat(0), lat(1), lat(2), lat(3), lat(4), ctx(1), ctx(3), ctx(4), vec, _full((1, HG_EXPAND))],
        out_specs=pl.BlockSpec((1, S, HG_EXPAND), lambda b, h: (b, 0, h)),
        out_shape=jax.ShapeDtypeStruct((B, S, D), bf16),
        scratch_shapes=[pltpu.VMEM((S, HG_EXPAND), f32), pltpu.VMEM((S, HG_EXPAND), f32)],
        compiler_params=_cparams("parallel", "parallel"),
        name="hgrn2_core",
    )(p_lat, p_lat, p_lat, p_lat, p_lat, p_ctx, p_ctx, p_ctx, lb.reshape(1, D), gn_g.reshape(1, HG_EXPAND))


def kernel(x, c, ctx, c_ctx, w_ada, b_ada, g_pre, g_post, w_ff1, w_ff3, w_ff2, s5_a_re, s5_a_im, s5_log_dt, s5_b_re, s5_b_im, s5_c_re, s5_c_im, s5_d, s5_w_glu, s5_b_glu, da_w_qkv, da_w_o, da_lam_q1, da_lam_k1, da_lam_q2, da_lam_k2, da_subln, na_w_qkv, na_w_o, na_rpb, hg_w_qig, hg_w_f, hg_b_f, hg_lb_logits, hg_gnorm, hg_w_o):
    B, S, D = x.shape
    depth = w_ada.shape[0]
    n_mix = 4
    rows_pad = -(-(B + 1) // 8) * 8
    c_all = jnp.concatenate([c, c_ctx[None], jnp.zeros((rows_pad - B - 1, D), f32)], axis=0)
    mods = _ada(c_all, w_ada, b_ada)
    lb_p = jax.nn.softmax(hg_lb_logits.astype(f32), axis=0)
    lower_bounds = jnp.cumsum(lb_p, axis=0) - lb_p[0]
    w1b, w3b, w2b = w_ff1.astype(bf16), w_ff3.astype(bf16), w_ff2.astype(bf16)

    x_lat, x_ctx = x, ctx
    for i in range(depth):
        last = i == depth - 1
        occ, kind = i // n_mix, i % n_mix
        m_lat = mods[i, :B].reshape(B, 3 * N_SUB, D)
        m_ctx = mods[i, B:B + 1].reshape(1, 3 * N_SUB, D)
        ffn1 = functools.partial(_half_ffn, g_in=g_pre[i, 0], g_out=g_post[i, 0], w1=w1b[i, 0], w3=w3b[i, 0],
                                 w2=w2b[i, 0], j=0)
        ffn2 = functools.partial(_half_ffn, g_in=g_pre[i, 2], g_out=g_post[i, 2], w1=w1b[i, 1], w3=w3b[i, 1],
                                 w2=w2b[i, 1], j=2)
        x_lat, x_ctx = ffn1(x_lat, m_lat), ffn1(x_ctx, m_ctx)

        y_ctx = None
        if kind == 0:
            h_lat = _prenorm(x_lat, m_lat, g_pre[i, 1], 1)
            h_ctx = _prenorm(x_ctx, m_ctx, g_pre[i, 1], 1)
            y_lat, y_ctx = _s5_scan(h_lat, h_ctx, s5_a_re[occ], s5_a_im[occ], s5_log_dt[occ], s5_b_re[occ],
                                    s5_b_im[occ], s5_c_re[occ], s5_c_im[occ])
            mix = "glu"
            mix_head = (g_pre[i, 1], s5_d[occ])
            mix_tail = (g_post[i, 1], s5_w_glu[occ].astype(bf16), s5_b_glu[occ])
        elif kind == 1:
            wq = da_w_qkv[occ].astype(bf16)
            rope = _rope_tables(S, DIFF_HEAD_DIM)
            qkv_lat = _prenorm_proj(x_lat, m_lat, g_pre[i, 1], 1, wq, rope=rope, n_rope=2 * D)
            qkv_ctx = _prenorm_proj(x_ctx, m_ctx, g_pre[i, 1], 1, wq)
            lamv = jnp.stack([da_lam_q1[occ], da_lam_k1[occ], da_lam_q2[occ], da_lam_k2[occ]]).astype(f32)
            lam_init = 0.8 - 0.6 * math.exp(-0.3 * i)
            y_lat = _diff_attention(qkv_lat, qkv_ctx, qkv_lat, lamv, da_subln[occ], lam_init)
            if not last:
                y_ctx = _diff_attention(qkv_ctx, qkv_ctx, None, lamv, da_subln[occ], lam_init)
            mix, mix_head, mix_tail = "outproj", (), (g_post[i, 1], da_w_o[occ].astype(bf16))
        elif kind == 2:
            wq = na_w_qkv[occ].astype(bf16)
            qkv_lat = _prenorm_proj(x_lat, m_lat, g_pre[i, 1], 1, wq)
            qkv_ctx = _prenorm_proj(x_ctx, m_ctx, g_pre[i, 1], 1, wq)
            y_lat = _na_attention(qkv_lat, qkv_ctx, qkv_lat, _na_bias_blocks(na_rpb[occ]))
            if not last:
                y_ctx = _na_attention(qkv_ctx, qkv_ctx, None, None)
            mix, mix_head, mix_tail = "outproj", (), (g_post[i, 1], na_w_o[occ].astype(bf16))
        else:
            assert last, "HGRN2 context outputs are not needed when it is the last layer"
            wp = jnp.concatenate([hg_w_qig[occ], hg_w_f[occ, 0], hg_w_f[occ, 1]], axis=1).astype(bf16)
            bp = jnp.concatenate([jnp.zeros((3 * D,), f32), hg_b_f[occ, 0], hg_b_f[occ, 1]])
            p_lat = _prenorm_proj(x_lat, m_lat, g_pre[i, 1], 1, wp, bias=bp, out_dtype=f32)
            p_ctx = _prenorm_proj(x_ctx, m_ctx, g_pre[i, 1], 1, wp, bias=bp, out_dtype=f32)
            y_lat = _hgrn2_core(p_lat, p_ctx, lower_bounds[i], hg_gnorm[occ])
            mix, mix_head, mix_tail = "outproj", (), (g_post[i, 1], hg_w_o[occ].astype(bf16))

        x_lat = ffn2(x_lat, m_lat, mix=mix, mix_args=(y_lat,) + mix_head + mix_tail)
        if not last:
            x_ctx = ffn2(x_ctx, m_ctx, mix=mix, mix_args=(y_ctx,) + mix_head + mix_tail)
    return x_lat
```

```python
import functools
import math

import jax
import jax.numpy as jnp
from jax import lax
from jax.experimental import pallas as pl
from jax.experimental.pallas import tpu as pltpu

f32 = jnp.float32
bf16 = jnp.bfloat16

N_SUB = 3
RMS_EPS = 1e-6
NEG_INF = -1e30
ROPE_BASE = 10000.0
GRID_W = 64
S5_GROUP = 16
S5_STATE = 64
S5_CHUNK = 16
DIFF_HEAD_DIM = 64
DIFF_KEY_CHUNK = 1024
NA_HEAD_DIM = 64
WIN_H = 8
WIN_W = 16
NA_TILE_ROWS = 8
NA_WIN_ROWS = 16
NA_KEY_CHUNK = 512
HG_EXPAND = 128
HG_CHUNK = 64
HG_BLOCK = 256
LANES = 128
SUBLANES = 8
VMEM_LIMIT = 56 * 1024 * 1024
TOKEN_TILES = (512, 256, 128)
PROJ_TOKEN_TILE = 256
PROJ_COL_STEP = 512
FFN_COL_STEP = 1024
ATTN_QUERY_TILE = 512
ADA_COL_TILES = 4


def _cparams(*sem):
    return pltpu.CompilerParams(dimension_semantics=sem, vmem_limit_bytes=VMEM_LIMIT)


def _dot(a, b):
    return jnp.dot(a, b, preferred_element_type=f32)


def _dot_nt(a, b):
    return lax.dot_general(a, b, (((1,), (1,)), ((), ())), preferred_element_type=f32)


def _rms(x, g):
    return x * lax.rsqrt(jnp.mean(jnp.square(x), axis=-1, keepdims=True) + RMS_EPS) * g


def _pre_norm(x, g, mod_ref, j):
    shift = mod_ref[0, 3 * j:3 * j + 1, :]
    scale = mod_ref[0, 3 * j + 1:3 * j + 2, :]
    return _rms(x, g) * (1 + scale) + shift


def _post_residual(x, y, g, mod_ref, j, weight):
    gate = mod_ref[0, 3 * j + 2:3 * j + 3, :]
    return x + weight * gate * _rms(y, g)


def _tok_tile(t):
    for tm in TOKEN_TILES:
        if t % tm == 0:
            return tm
    raise ValueError(f"token count {t} is not a multiple of {TOKEN_TILES[-1]}")


def _full(shape):
    return pl.BlockSpec(shape, lambda *_: (0,) * len(shape))


def _mod_spec(mod):
    nd = mod.shape[1]
    d = mod.shape[2]
    if mod.shape[0] == 1:
        return pl.BlockSpec((1, nd, d), lambda b, t: (0, 0, 0))
    return pl.BlockSpec((1, nd, d), lambda b, t: (b, 0, 0))


def _ada_kernel(c_ref, w_ref, b_ref, o_ref):
    c = c_ref[...]
    sc = (c * jax.nn.sigmoid(c)).astype(bf16)
    o_ref[0] = _dot(sc, w_ref[0].astype(bf16)) + b_ref[0]


def _ada(c_all, w_ada, b_ada):
    depth, d, n = w_ada.shape
    rows = c_all.shape[0]
    tn = n // ADA_COL_TILES
    return pl.pallas_call(
        _ada_kernel,
        grid=(depth, n // tn),
        in_specs=[pl.BlockSpec((rows, d), lambda i, t: (0, 0)),
                  pl.BlockSpec((1, d, tn), lambda i, t: (i, 0, t)),
                  pl.BlockSpec((1, 1, tn), lambda i, t: (i, 0, t))],
        out_specs=pl.BlockSpec((1, rows, tn), lambda i, t: (i, 0, t)),
        out_shape=jax.ShapeDtypeStruct((depth, rows, n), f32),
        compiler_params=_cparams("arbitrary", "arbitrary"),
        name="ada",
    )(c_all, w_ada, b_ada.reshape(depth, 1, n))


def _ffn_kernel(x_ref, mod_ref, gin_ref, gout_ref, w1_ref, w3_ref, w2_ref, *rest, j, chunks, mix):
    o_ref = rest[-1]
    x = x_ref[0]
    if mix == "outproj":
        y_ref, gmix_ref, wo_ref = rest[:-1]
        x = _post_residual(x, _dot(y_ref[0], wo_ref[...]), gmix_ref[...], mod_ref, 1, 1.0)
    elif mix == "glu":
        y_ref, gpre_ref, d_ref, gmix_ref, wg_ref, bg_ref = rest[:-1]
        z = jax.nn.gelu(d_ref[...] * _pre_norm(x, gpre_ref[...], mod_ref, 1) + y_ref[0])
        u = _dot(z.astype(bf16), wg_ref[...]) + bg_ref[...]
        x = _post_residual(x, z * jax.nn.sigmoid(u), gmix_ref[...], mod_ref, 1, 1.0)
    h = _pre_norm(x, gin_ref[...], mod_ref, j).astype(bf16)
    acc = None
    for s, n in chunks:
        a = _dot(h, w1_ref[:, s:s + n])
        b = _dot(h, w3_ref[:, s:s + n])
        g = (a * jax.nn.sigmoid(a) * b).astype(bf16)
        y = _dot(g, w2_ref[s:s + n, :])
        acc = y if acc is None else acc + y
    o_ref[0] = _post_residual(x, acc, gout_ref[...], mod_ref, j, 0.5)


def _ff_chunks(dff, step=FFN_COL_STEP):
    out, s = [], 0
    while s < dff:
        n = min(step, dff - s)
        out.append((s, n))
        s += n
    return tuple(out)


def _half_ffn(x, mod, g_in, g_out, w1, w3, w2, j, mix=None, mix_args=()):
    b, t, d = x.shape
    dff = w1.shape[1]
    tm = _tok_tile(t)
    tok = pl.BlockSpec((1, tm, d), lambda bb, tt: (bb, tt, 0))
    vec = _full((1, d))
    args = [x, mod, g_in.reshape(1, d), g_out.reshape(1, d), w1, w3, w2]
    specs = [tok, _mod_spec(mod), vec, vec, _full((d, dff)), _full((d, dff)), _full((dff, d))]
    if mix == "outproj":
        y, g_mix, w_o = mix_args
        args += [y, g_mix.reshape(1, d), w_o]
        specs += [tok, vec, _full((d, d))]
    elif mix == "glu":
        y, g_pre, d_skip, g_mix, w_g, b_g = mix_args
        args += [y, g_pre.reshape(1, d), d_skip.astype(f32).reshape(1, d), g_mix.reshape(1, d), w_g, b_g.reshape(1, d)]
        specs += [tok, vec, vec, vec, _full((d, d)), vec]
    return pl.pallas_call(
        functools.partial(_ffn_kernel, j=j, chunks=_ff_chunks(dff), mix=mix),
        grid=(b, t // tm),
        in_specs=specs,
        out_specs=tok,
        out_shape=jax.ShapeDtypeStruct((b, t, d), f32),
        compiler_params=_cparams("parallel", "parallel"),
        name="half_ffn",
    )(*args)


def _prenorm_kernel(x_ref, mod_ref, g_ref, o_ref, *, j):
    o_ref[0] = _pre_norm(x_ref[0], g_ref[...], mod_ref, j).astype(o_ref.dtype)


def _prenorm(x, mod, g, j):
    b, t, d = x.shape
    tm = _tok_tile(t)
    return pl.pallas_call(
        functools.partial(_prenorm_kernel, j=j),
        grid=(b, t // tm),
        in_specs=[pl.BlockSpec((1, tm, d), lambda bb, tt: (bb, tt, 0)), _mod_spec(mod), _full((1, d))],
        out_specs=pl.BlockSpec((1, tm, d), lambda bb, tt: (bb, tt, 0)),
        out_shape=jax.ShapeDtypeStruct((b, t, d), bf16),
        compiler_params=_cparams("parallel", "parallel"),
        name="prenorm",
    )(x, mod, g.reshape(1, d))


def _swap_pairs(x):
    lane = lax.broadcasted_iota(jnp.int32, x.shape, 1)
    nxt = pltpu.roll(x, LANES - 1, 1)
    prv = pltpu.roll(x, 1, 1)
    return jnp.where((lane & 1) == 0, nxt, prv)


def _proj_kernel(*refs, j, n_out, n_rope, has_bias, step):
    x_ref, mod_ref, g_ref, w_ref = refs[:4]
    k = 4
    b_ref = None
    if has_bias:
        b_ref = refs[k]
        k += 1
    if n_rope:
        cos_ref, sin_ref = refs[k], refs[k + 1]
        k += 2
    o_ref = refs[k]
    h = _pre_norm(x_ref[0], g_ref[...], mod_ref, j).astype(bf16)
    for s in range(0, n_out, step):
        y = _dot(h, w_ref[:, s:s + step])
        if has_bias:
            y = y + b_ref[:, s:s + step]
        if s < n_rope:
            cos = cos_ref[...]
            sin = sin_ref[...]
            parts = []
            for c in range(0, step, LANES):
                yc = y[:, c:c + LANES]
                parts.append(yc * cos + _swap_pairs(yc) * sin)
            y = jnp.concatenate(parts, axis=1)
        o_ref[0, :, s:s + step] = y.astype(o_ref.dtype)


def _prenorm_proj(x, mod, g, j, w, bias=None, rope=None, n_rope=0, out_dtype=bf16):
    b, t, d = x.shape
    n = w.shape[1]
    tm = min(_tok_tile(t), PROJ_TOKEN_TILE)
    step = PROJ_COL_STEP
    assert n % step == 0 and n_rope % step == 0
    args = [x, mod, g.reshape(1, d), w]
    specs = [pl.BlockSpec((1, tm, d), lambda bb, tt: (bb, tt, 0)), _mod_spec(mod), _full((1, d)), _full((d, n))]
    if bias is not None:
        args.append(bias.reshape(1, n))
        specs.append(_full((1, n)))
    if n_rope:
        args += list(rope)
        specs += [pl.BlockSpec((tm, LANES), lambda bb, tt: (tt, 0))] * 2
    return pl.pallas_call(
        functools.partial(_proj_kernel, j=j, n_out=n, n_rope=n_rope, has_bias=bias is not None, step=step),
        grid=(b, t // tm),
        in_specs=specs,
        out_specs=pl.BlockSpec((1, tm, n), lambda bb, tt: (bb, tt, 0)),
        out_shape=jax.ShapeDtypeStruct((b, t, n), out_dtype),
        compiler_params=_cparams("parallel", "parallel"),
        name="prenorm_proj",
    )(*args)


def _s5_kernel(xc_ref, xl_ref, wb_ref, wk_ref, wc_ref, lam_ref, yc_ref, yl_ref, s_ref, hin_ref, *, nb, nc_ctx, nc):
    rc = nc_ctx * nb
    xc, xl = xc_ref[0], xl_ref[0]
    s_ref[0:rc, :] = _dot(xc, wb_ref[0])
    s_ref[rc:, :] = _dot(xl, wb_ref[0])
    P = S5_STATE
    lam = lam_ref[0]
    a_re, a_im = lam[0:1], lam[1:2]
    fwd_lane = lax.broadcasted_iota(jnp.int32, (nb, 2 * P), 1) < P

    def step(n, carry):
        h_re, h_im = carry
        r_f = pl.multiple_of(n * nb, nb)
        n_b = jnp.where(n < nc_ctx, nc_ctx - 1 - n, nc - 1 - (n - nc_ctx))
        r_b = pl.multiple_of(n_b * nb, nb)
        rows_f, rows_b = pl.ds(r_f, nb), pl.ds(r_b, nb)
        hin_ref[rows_f, 0:P] = h_re[:, 0:P]
        hin_ref[rows_b, P:2 * P] = h_re[:, P:2 * P]
        hin_ref[rows_f, 2 * P:3 * P] = h_im[:, 0:P]
        hin_ref[rows_b, 3 * P:4 * P] = h_im[:, P:2 * P]
        s_re = jnp.where(fwd_lane, s_ref[rows_f, 0:2 * P], s_ref[rows_b, 0:2 * P])
        s_im = jnp.where(fwd_lane, s_ref[rows_f, 2 * P:4 * P], s_ref[rows_b, 2 * P:4 * P])
        return a_re * h_re - a_im * h_im + s_re, a_re * h_im + a_im * h_re + s_im

    zero = jnp.zeros((nb, 2 * P), f32)
    lax.fori_loop(0, nc, step, (zero, zero), unroll=4 if nc % 4 == 0 else 1)
    yc_ref[0] = _dot(xc, wk_ref[0]) + _dot(hin_ref[0:rc, :].astype(bf16), wc_ref[0])
    yl_ref[0] = _dot(xl, wk_ref[0]) + _dot(hin_ref[rc:, :].astype(bf16), wc_ref[0])

def _s5_weights(a_re, a_im, log_dt, b_re, b_im, c_re, c_im):
    hp = lax.Precision.HIGHEST
    C = S5_CHUNK
    G, P = a_re.shape[1:]
    N = b_re.shape[-1]
    j = jnp.arange(C + 1, dtype=f32)
    wbs, wks, wcs, lams = [], [], [], []
    for dirn in range(2):
        are, aim = a_re[dirn].astype(f32), a_im[dirn].astype(f32)
        dt = jnp.exp(log_dt[dirn].astype(f32))[:, None]
        pw_mag = jnp.exp(j[:, None, None] * (are * dt)[None])
        pw_re = pw_mag * jnp.cos(j[:, None, None] * (aim * dt)[None])
        pw_im = pw_mag * jnp.sin(j[:, None, None] * (aim * dt)[None])
        nr, ni = pw_re[1] - 1.0, pw_im[1]
        den = are * are + aim * aim
        fr = (nr * are + ni * aim) / den
        fi = (ni * are - nr * aim) / den
        bre, bim = b_re[dirn].astype(f32), b_im[dirn].astype(f32)
        bbr = fr[..., None] * bre - fi[..., None] * bim
        bbi = fr[..., None] * bim + fi[..., None] * bre
        cre, cim = c_re[dirn].astype(f32), c_im[dirn].astype(f32)
        dr, di = (pw_re[:C][::-1], pw_im[:C][::-1]) if dirn == 0 else (pw_re[:C], pw_im[:C])
        sr = dr[..., None] * bbr[None] - di[..., None] * bbi[None]
        si = dr[..., None] * bbi[None] + di[..., None] * bbr[None]
        wb = jnp.concatenate([sr, si], axis=2)
        wbs.append(wb.transpose(1, 0, 3, 2).reshape(G, C * N, 2 * P))
        pr, pi = pw_re[1:], pw_im[1:]
        if dirn == 1:
            pr, pi = pr[::-1], pi[::-1]
        or_ = cre[None] * pr[:, :, None, :] - cim[None] * pi[:, :, None, :]
        oi_ = -(cre[None] * pi[:, :, None, :] + cim[None] * pr[:, :, None, :])
        wc = jnp.concatenate([or_, oi_], axis=3)
        wcs.append(wc.transpose(1, 3, 0, 2).reshape(G, 2 * P, C * N))
        clr = cre[None] * pw_re[:C, :, None, :] - cim[None] * pw_im[:C, :, None, :]
        cli = cre[None] * pw_im[:C, :, None, :] + cim[None] * pw_re[:C, :, None, :]
        kk = (jnp.einsum('jgnp,gpm->gjnm', clr, bbr, precision=hp)
              - jnp.einsum('jgnp,gpm->gjnm', cli, bbi, precision=hp))
        s_idx = jnp.arange(C)[:, None]
        t_idx = jnp.arange(C)[None, :]
        lag = (t_idx - s_idx) if dirn == 0 else (s_idx - t_idx)
        onehot = (lag[:, :, None] == jnp.arange(C)[None, None, :]).astype(f32)
        kt = jnp.einsum('stj,gjnm->gsmtn', onehot, kk, precision=hp)
        wks.append(kt.reshape(G, C * N, C * N))
        lams.append((pw_re[C], pw_im[C]))
    wb = jnp.concatenate([wbs[0][..., :P], wbs[1][..., :P], wbs[0][..., P:], wbs[1][..., P:]], axis=2).astype(bf16)
    wc = jnp.concatenate([wcs[0][:, :P], wcs[1][:, :P], wcs[0][:, P:], wcs[1][:, P:]], axis=1).astype(bf16)
    wk = (wks[0] + wks[1]).astype(bf16)
    lam = jnp.stack([jnp.concatenate([lams[0][0], lams[1][0]], axis=-1),
                     jnp.concatenate([lams[0][1], lams[1][1]], axis=-1)], axis=1)
    return wb, wk, wc, lam


def _s5_scan(h_lat, h_ctx, a_re, a_im, log_dt, b_re, b_im, c_re, c_im):
    B, S, D = h_lat.shape
    L = h_ctx.shape[1]
    C, N, P = S5_CHUNK, S5_GROUP, S5_STATE
    G = D // N
    nc_ctx, nc = L // C, (L + S) // C
    assert 2 * P == LANES and L % C == 0 and S % C == 0
    wb, wk, wc, lam = _s5_weights(a_re, a_im, log_dt, b_re, b_im, c_re, c_im)

    def to_groups(h):
        n = h.shape[1] // C
        return h.reshape(B, n, C, G, N).transpose(3, 1, 0, 2, 4).reshape(G, n * B, C * N)

    def from_groups(y):
        n = y.shape[1] // B
        return y.reshape(G, n, B, C, N).transpose(2, 1, 3, 0, 4).reshape(B, n * C, D)

    rc, rl = nc_ctx * B, (nc - nc_ctx) * B

    def rows(r):
        return pl.BlockSpec((1, r, C * N), lambda g: (g, 0, 0))

    yc, yl = pl.pallas_call(
        functools.partial(_s5_kernel, nb=B, nc_ctx=nc_ctx, nc=nc),
        grid=(G,),
        in_specs=[rows(rc), rows(rl),
                  pl.BlockSpec((1, C * N, 4 * P), lambda g: (g, 0, 0)),
                  pl.BlockSpec((1, C * N, C * N), lambda g: (g, 0, 0)),
                  pl.BlockSpec((1, 4 * P, C * N), lambda g: (g, 0, 0)),
                  pl.BlockSpec((1, 2, 2 * P), lambda g: (g, 0, 0))],
        out_specs=[rows(rc), rows(rl)],
        out_shape=[jax.ShapeDtypeStruct((G, rc, C * N), f32), jax.ShapeDtypeStruct((G, rl, C * N), f32)],
        scratch_shapes=[pltpu.VMEM((rc + rl, 4 * P), f32), pltpu.VMEM((rc + rl, 4 * P), f32)],
        compiler_params=_cparams("parallel"),
        name="s5_scan",
    )(to_groups(h_ctx), to_groups(h_lat), wb, wk, wc, lam)
    return from_groups(yl), from_groups(yc)


def _diff_maps(q):
    lane = lax.broadcasted_iota(jnp.int32, q.shape, 1)
    qs = (q.astype(f32) * (DIFF_HEAD_DIM ** -0.5 * math.log2(math.e))).astype(bf16)
    zero = jnp.zeros_like(qs)
    return jnp.where(lane < DIFF_HEAD_DIM, qs, zero), jnp.where(lane >= DIFF_HEAD_DIM, qs, zero)


def _key_chunks(k_refs):
    out = []
    for seg, k_ref in enumerate(k_refs):
        n = k_ref.shape[1]
        out += [(seg, c, min(c + DIFF_KEY_CHUNK, n)) for c in range(0, n, DIFF_KEY_CHUNK)]
    return out


def _diff_step(q_next, cur, nxt, k_refs, v_refs, lamv_ref, g_ref, o_ref, lam_init):
    chunks = _key_chunks(k_refs)
    cur_s, cur_m = cur
    nxt_s, nxt_m = nxt
    tq = cur_m.shape[1]
    m1, m2 = cur_m[0, :, 0:1], cur_m[1, :, 0:1]
    q1n, q2n = _diff_maps(q_next)
    n1 = n2 = l1 = l2 = None
    for seg, a, b in chunks:
        s1 = _dot_nt(q1n, k_refs[seg][0, a:b, :])
        s2 = _dot_nt(q2n, k_refs[seg][0, a:b, :])
        nxt_s[seg][0, :, a:b] = s1
        nxt_s[seg][1, :, a:b] = s2
        c1 = jnp.max(s1, axis=-1, keepdims=True)
        c2 = jnp.max(s2, axis=-1, keepdims=True)
        n1 = c1 if n1 is None else jnp.maximum(n1, c1)
        n2 = c2 if n2 is None else jnp.maximum(n2, c2)
        p1 = jnp.exp2(cur_s[seg][0, :, a:b] - m1)
        p2 = jnp.exp2(cur_s[seg][1, :, a:b] - m2)
        cur_s[seg][0, :, a:b] = p1
        cur_s[seg][1, :, a:b] = p2
        r1 = jnp.sum(p1, axis=-1, keepdims=True)
        r2 = jnp.sum(p2, axis=-1, keepdims=True)
        l1 = r1 if l1 is None else l1 + r1
        l2 = r2 if l2 is None else l2 + r2
    nxt_m[0] = jnp.broadcast_to(n1, (tq, LANES))
    nxt_m[1] = jnp.broadcast_to(n2, (tq, LANES))
    lv = lamv_ref[...]
    lam = (jnp.exp(jnp.sum(lv[0:1] * lv[1:2], axis=-1, keepdims=True))
           - jnp.exp(jnp.sum(lv[2:3] * lv[3:4], axis=-1, keepdims=True)) + lam_init)
    c = lam * l1 / l2
    o = None
    for seg, a, b in chunks:
        w = (cur_s[seg][0, :, a:b] - c * cur_s[seg][1, :, a:b]).astype(bf16)
        part = _dot(w, v_refs[seg][0, a:b, :])
        o = part if o is None else o + part
    o = o / l1
    o_ref[0] = (_rms(o, g_ref[...]) * (1 - lam_init)).astype(o_ref.dtype)


def _diff_kernel(*refs, n_seg, lam_init):
    q_ref, qn_ref = refs[:2]
    k_refs = refs[2:2 + 3 * n_seg:3]
    v_refs = refs[3:3 + 3 * n_seg:3]
    kn_refs = refs[4:4 + 3 * n_seg:3]
    lamv_ref, g_ref, o_ref = refs[2 + 3 * n_seg:5 + 3 * n_seg]
    scratch = refs[5 + 3 * n_seg:]
    slots = tuple((scratch[p * (n_seg + 1):p * (n_seg + 1) + n_seg], scratch[p * (n_seg + 1) + n_seg])
                  for p in range(2))
    unit = (pl.program_id(0) * pl.num_programs(1) + pl.program_id(1)) * pl.num_programs(2) + pl.program_id(2)

    @pl.when(unit == 0)
    def _():
        q1, q2 = _diff_maps(q_ref[0])
        s_refs, m_ref = slots[0]
        for mp, qm in enumerate((q1, q2)):
            parts = [_dot_nt(qm, k_ref[0]) for k_ref in k_refs]
            for s_ref, sc in zip(s_refs, parts):
                s_ref[mp] = sc
            mx = functools.reduce(jnp.maximum, [jnp.max(sc, axis=-1, keepdims=True) for sc in parts])
            m_ref[mp] = jnp.broadcast_to(mx, m_ref.shape[1:])

    for parity in range(2):
        @pl.when((unit & 1) == parity)
        def _(parity=parity):
            _diff_step(qn_ref[0], slots[parity], slots[1 - parity], kn_refs, v_refs, lamv_ref, g_ref, o_ref,
                       lam_init)


def _diff_attention(qkv_q, qkv_ctx, qkv_lat, lamv, subln, lam_init):
    B, T, n3 = qkv_q.shape
    D = n3 // 3
    H = D // LANES
    tq = min(_tok_tile(T), ATTN_QUERY_TILE)
    nq = T // tq
    kv = [qkv_ctx] if qkv_lat is None else [qkv_ctx, qkv_lat]

    def following(b, h, t):
        u = jnp.minimum((b * H + h) * nq + t + 1, B * H * nq - 1)
        return u // (H * nq), (u // nq) % H, u % nq

    def q_next(b, h, t):
        nb, nh, nt = following(b, h, t)
        return nb, nt, nh

    def k_next(b, h, t):
        nb, nh, _ = following(b, h, t)
        return nb, 0, H + nh

    args = [qkv_q, qkv_q]
    specs = [pl.BlockSpec((1, tq, LANES), lambda b, h, t: (b, t, h)), pl.BlockSpec((1, tq, LANES), q_next)]
    for a in kv:
        args += [a, a, a]
        specs += [pl.BlockSpec((1, a.shape[1], LANES), lambda b, h, t: (b, 0, H + h)),
                  pl.BlockSpec((1, a.shape[1], LANES), lambda b, h, t: (b, 0, 2 * H + h)),
                  pl.BlockSpec((1, a.shape[1], LANES), k_next)]
    args += [lamv, subln.reshape(1, LANES)]
    specs += [_full(lamv.shape), _full((1, LANES))]
    slot = [pltpu.VMEM((2, tq, a.shape[1]), f32) for a in kv] + [pltpu.VMEM((2, tq, LANES), f32)]
    return pl.pallas_call(
        functools.partial(_diff_kernel, n_seg=len(kv), lam_init=lam_init),
        grid=(B, H, nq),
        in_specs=specs,
        out_specs=pl.BlockSpec((1, tq, LANES), lambda b, h, t: (b, t, h)),
        out_shape=jax.ShapeDtypeStruct((B, T, D), bf16),
        scratch_shapes=slot + slot,
        compiler_params=_cparams("arbitrary", "arbitrary", "arbitrary"),
        name="diff_attention",
    )(*args)


def _rope_tables(n_tokens, head_dim):
    n_freq = head_dim // 4
    inv_freq = ROPE_BASE ** (-jnp.arange(n_freq, dtype=f32) / n_freq)
    t = jnp.arange(n_tokens)
    row = (t // GRID_W).astype(f32)
    col = (t % GRID_W).astype(f32)
    ang = jnp.concatenate([row[:, None] * inv_freq, col[:, None] * inv_freq], axis=-1)
    cos = jnp.repeat(jnp.cos(ang), 2, axis=-1)
    sin = jnp.repeat(jnp.sin(ang), 2, axis=-1) * jnp.tile(jnp.array([-1.0, 1.0], f32), head_dim // 2)
    reps = LANES // head_dim
    return jnp.tile(cos, (1, reps)), jnp.tile(sin, (1, reps))


def _na_head(q, hh):
    lane = lax.broadcasted_iota(jnp.int32, q.shape, 1)
    qs = (q.astype(f32) * (NA_HEAD_DIM ** -0.5 * math.log2(math.e))).astype(bf16)
    sel = (lane < NA_HEAD_DIM) if hh == 0 else (lane >= NA_HEAD_DIM)
    return jnp.where(sel, qs, jnp.zeros_like(qs))


def _na_ctx_kernel(q_ref, kc_ref, vc_ref, o_ref):
    q = q_ref[0]
    lane = lax.broadcasted_iota(jnp.int32, q.shape, 1)
    outs = []
    for hh in range(2):
        s = _dot_nt(_na_head(q, hh), kc_ref[0])
        p = jnp.exp2(s - jnp.max(s, axis=-1, keepdims=True))
        outs.append(_dot(p.astype(bf16), vc_ref[0]) / jnp.sum(p, axis=-1, keepdims=True))
    o_ref[0] = jnp.where(lane < NA_HEAD_DIM, outs[0], outs[1]).astype(o_ref.dtype)


def _na_window_table(rows):
    table = []
    for t in range(rows // NA_TILE_ROWS):
        w0 = min(max(t * NA_TILE_ROWS - (NA_WIN_ROWS - NA_TILE_ROWS) // 2, 0), rows - NA_WIN_ROWS)
        line = [w0 * GRID_W]
        for i in range(NA_TILE_ROWS):
            qr = t * NA_TILE_ROWS + i
            r0 = min(max(qr - WIN_H // 2, 0), rows - WIN_H)
            for j in range(NA_WIN_ROWS // 2):
                kr = w0 + 2 * j
                e = kr - qr + WIN_H
                ok0 = r0 <= kr < r0 + WIN_H
                ok1 = r0 <= kr + 1 < r0 + WIN_H
                line.append(e if ok0 and ok1 else 2 * WIN_H - 1 + e if ok0 else 4 * WIN_H - 1 + e if ok1 else 0)
        table.append(line)
    return jnp.array(table, jnp.int32)


def _na_window(win_ref, t):
    n_pairs = NA_WIN_ROWS // 2
    entry = [[win_ref[t, 1 + i * n_pairs + j] for j in range(n_pairs)] for i in range(NA_TILE_ROWS)]
    return pl.multiple_of(win_ref[t, 0], 4 * GRID_W), entry


def _lane_fold(x, op):
    return functools.reduce(op, [x[:, c:c + LANES] for c in range(0, x.shape[1], LANES)])


def _na_biased(sl, hh, entry, j0, tb_ref, s_ref, c0):
    tiles = []
    for i in range(NA_TILE_ROWS):
        r = slice(i * GRID_W, (i + 1) * GRID_W)
        mx = None
        for jj in range(sl.shape[1] // LANES):
            blk = sl[r, jj * LANES:(jj + 1) * LANES] + tb_ref[0, hh, entry[i][j0 + jj]]
            s_ref[r, c0 + jj * LANES:c0 + (jj + 1) * LANES] = blk
            mx = blk if mx is None else jnp.maximum(mx, blk)
        tiles.append(mx)
    return jnp.concatenate(tiles, axis=0)


def _na_unit(make, use, kc_ref, vc_ref, kl_ref, vl_ref, tb_ref):
    q, hh_m, tok_m, entry, (mc_ref, ml_ref, mm_ref) = make
    hh_u, tok_u, (uc_ref, ul_ref, um_ref) = use
    qh = _na_head(q, hh_m)
    m = um_ref[:, 0:1]
    run_max = l = o = None
    for c0 in [None] + list(range(0, NA_WIN_ROWS * GRID_W, NA_KEY_CHUNK)):
        if c0 is None:
            sc = _dot_nt(qh, kc_ref[0])
            mc_ref[...] = sc
            run_max = _lane_fold(sc, jnp.maximum)
            p = jnp.exp2(uc_ref[...] - m)
            v = vc_ref[0]
        else:
            sl = _dot_nt(qh, kl_ref[0, pl.ds(tok_m + c0, NA_KEY_CHUNK), :])
            run_max = jnp.maximum(run_max, _na_biased(sl, hh_m, entry, c0 // LANES, tb_ref, ml_ref, c0))
            p = jnp.exp2(ul_ref[:, c0:c0 + NA_KEY_CHUNK] - m)
            v = vl_ref[0, pl.ds(tok_u + c0, NA_KEY_CHUNK), :]
        r = _lane_fold(p, jnp.add)
        part = _dot(p.astype(bf16), v)
        l = r if l is None else l + r
        o = part if o is None else o + part
    mm_ref[...] = jnp.broadcast_to(jnp.max(run_max, axis=-1, keepdims=True), mm_ref.shape)
    return o / jnp.sum(l, axis=-1, keepdims=True)


def _na_kernel(win_ref, q_ref, qn_ref, kc_ref, vc_ref, kl_ref, vl_ref, tb_ref, o_ref, *scratch, n_tiles):
    slot_a, slot_b = scratch[:3], scratch[3:]
    t = pl.program_id(2)
    tok_t, entry_t = _na_window(win_ref, t)
    tok_n, entry_n = _na_window(win_ref, jnp.minimum(t + 1, n_tiles - 1))
    shared = (kc_ref, vc_ref, kl_ref, vl_ref, tb_ref)

    @pl.when(t == 0)
    def _():
        qh = _na_head(q_ref[0], 0)
        sc = _dot_nt(qh, kc_ref[0])
        slot_a[0][...] = sc
        sl = _dot_nt(qh, kl_ref[0, pl.ds(tok_t, NA_WIN_ROWS * GRID_W), :])
        run_max = jnp.maximum(_lane_fold(sc, jnp.maximum), _na_biased(sl, 0, entry_t, 0, tb_ref, slot_a[1], 0))
        slot_a[2][...] = jnp.broadcast_to(jnp.max(run_max, axis=-1, keepdims=True), slot_a[2].shape)

    o0 = _na_unit((q_ref[0], 1, tok_t, entry_t, slot_b), (0, tok_t, slot_a), *shared)
    o1 = _na_unit((qn_ref[0], 0, tok_n, entry_n, slot_a), (1, tok_t, slot_b), *shared)
    lane = lax.broadcasted_iota(jnp.int32, o0.shape, 1)
    o_ref[0] = jnp.where(lane < NA_HEAD_DIM, o0, o1).astype(o_ref.dtype)


def _na_bias_blocks(rpb):
    H, n_dr, n_dc = rpb.shape
    col = jnp.arange(GRID_W)
    c0 = jnp.clip(col - WIN_W // 2, 0, GRID_W - WIN_W)
    col_ok = (col[None, :] >= c0[:, None]) & (col[None, :] < c0[:, None] + WIN_W)
    dc = jnp.clip(col[None, :] - col[:, None] + WIN_W - 1, 0, n_dc - 1)
    onehot = (dc.reshape(1, -1) == jnp.arange(n_dc)[:, None]).astype(f32)
    blk = jnp.einsum('hrj,jx->hrx', rpb.astype(f32) * math.log2(math.e), onehot, precision=lax.Precision.HIGHEST)
    blk = jnp.where(col_ok[None, None], blk.reshape(H, n_dr, GRID_W, GRID_W), NEG_INF)
    blk = jnp.concatenate([jnp.full((H, 1, GRID_W, GRID_W), NEG_INF, f32), blk], axis=1)
    n_e = 2 * WIN_H - 1
    left = [0] + list(range(1, n_e)) + [0] + list(range(1, n_e + 1)) + [0] * n_e
    right = [0] + list(range(2, n_e + 1)) + [0] + [0] * n_e + list(range(1, n_e + 1))
    tab = jnp.concatenate([jnp.take(blk, jnp.array(left), axis=1), jnp.take(blk, jnp.array(right), axis=1)], axis=-1)
    return tab.reshape((H // 2, 2) + tab.shape[1:])


def _na_attention(qkv_q, qkv_ctx, qkv_lat, bias):
    B, T, n3 = qkv_q.shape
    D = n3 // 3
    HP = D // LANES
    L = qkv_ctx.shape[1]
    ctx_k = pl.BlockSpec((1, L, LANES), lambda b, h, t: (b, 0, HP + h))
    ctx_v = pl.BlockSpec((1, L, LANES), lambda b, h, t: (b, 0, 2 * HP + h))
    if qkv_lat is None:
        tq = _tok_tile(T)
        tile = pl.BlockSpec((1, tq, LANES), lambda b, h, t: (b, t, h))
        return pl.pallas_call(
            _na_ctx_kernel,
            grid=(B, HP, T // tq),
            in_specs=[tile, ctx_k, ctx_v],
            out_specs=tile,
            out_shape=jax.ShapeDtypeStruct((B, T, D), bf16),
            compiler_params=_cparams("parallel", "parallel", "parallel"),
            name="na_attention_ctx",
        )(qkv_q, qkv_ctx, qkv_ctx)
    tq = NA_TILE_ROWS * GRID_W
    n_tiles = T // tq
    n_win = NA_WIN_ROWS * GRID_W
    assert T % tq == 0 and T // GRID_W >= NA_WIN_ROWS and bias.shape[2] == 6 * WIN_H - 2
    tile = pl.BlockSpec((1, tq, LANES), lambda b, h, t: (b, t, h))
    slot = [pltpu.VMEM((tq, L), f32), pltpu.VMEM((tq, n_win), f32), pltpu.VMEM((tq, LANES), f32)]
    return pl.pallas_call(
        functools.partial(_na_kernel, n_tiles=n_tiles),
        grid=(B, HP, n_tiles),
        in_specs=[pl.BlockSpec(memory_space=pltpu.SMEM), tile,
                  pl.BlockSpec((1, tq, LANES), lambda b, h, t: (b, jnp.minimum(t + 1, n_tiles - 1), h)),
                  ctx_k, ctx_v,
                  pl.BlockSpec((1, T, LANES), lambda b, h, t: (b, 0, HP + h)),
                  pl.BlockSpec((1, T, LANES), lambda b, h, t: (b, 0, 2 * HP + h)),
                  pl.BlockSpec((1,) + bias.shape[1:], lambda b, h, t: (h, 0, 0, 0, 0))],
        out_specs=tile,
        out_shape=jax.ShapeDtypeStruct((B, T, D), bf16),
        scratch_shapes=slot + slot,
        compiler_params=_cparams("parallel", "parallel", "arbitrary"),
        name="na_attention",
    )(_na_window_table(T // GRID_W), qkv_q, qkv_q, qkv_ctx, qkv_ctx, qkv_lat, qkv_lat, bias)


def _chunk_cumsum(x, reverse):
    n = x.shape[0]
    pos = lax.broadcasted_iota(jnp.int32, x.shape, 0) & (HG_CHUNK - 1)
    k = 1
    while k < HG_CHUNK:
        if reverse:
            x = x + jnp.where(pos < HG_CHUNK - k, pltpu.roll(x, n - k, 0), 0.0)
        else:
            x = x + jnp.where(pos >= k, pltpu.roll(x, k, 0), 0.0)
        k *= 2
    return x


def _hg_mask(n, reverse):
    ri = lax.broadcasted_iota(jnp.int32, (n, n), 0)
    ci = lax.broadcasted_iota(jnp.int32, (n, n), 1)
    shift = HG_CHUNK.bit_length() - 1
    return ((ri >> shift) == (ci >> shift)) & ((ci >= ri) if reverse else (ci <= ri))


def _hg_block(z, v, q, lb, st, reverse, tri):
    with_out = q is not None
    n = z.shape[0]
    nch = n // HG_CHUNK
    f = lb + (1 - lb) * jax.nn.sigmoid(z)
    lf = jnp.log(f)
    kk = 1 - f
    g = _chunk_cumsum(lf, reverse)
    out = None
    if with_out:
        q_dec = q * jnp.exp(g)
        k_inv = kk * jnp.exp(-g)
        att = jnp.where(tri, _dot_nt(q_dec.astype(bf16), k_inv.astype(bf16)), 0.0)
        out_intra = _dot(att.astype(bf16), v.astype(bf16))
        outs = [None] * nch
    order = range(nch - 1, -1, -1) if reverse else range(nch)
    for c in order:
        sl = slice(c * HG_CHUNK, (c + 1) * HG_CHUNK)
        end = c * HG_CHUNK if reverse else (c + 1) * HG_CHUNK - 1
        g_tot = g[end:end + 1]
        if with_out:
            outs[c] = _dot_nt(q_dec[sl].astype(bf16), st.astype(bf16))
        kdec = kk[sl] * jnp.exp(g_tot - g[sl])
        ds_t = _dot(v[sl].T.astype(bf16), kdec.astype(bf16))
        st = st * jnp.exp(g_tot) + ds_t
    if with_out:
        out = out_intra + jnp.concatenate(outs, axis=0)
    return out, st


def _hg_kernel(q_ref, i_ref, gate_ref, zf_ref, zb_ref, ic_ref, zfc_ref, zbc_ref, lb_ref, gn_ref, o_ref,
               accf_ref, accb_ref, *, s_tok):
    lb = lb_ref[...]
    nblk = s_tok // HG_BLOCK
    zero = jnp.zeros((HG_EXPAND, HG_EXPAND), f32)
    _, st_f = _hg_block(zfc_ref[0], ic_ref[0], None, lb, zero, False, None)
    _, st_b = _hg_block(zbc_ref[0], ic_ref[0], None, lb, zero, True, None)
    tri_f, tri_b = _hg_mask(HG_BLOCK, False), _hg_mask(HG_BLOCK, True)

    def body(n, carry):
        st_f, st_b = carry
        rows_f = pl.ds(pl.multiple_of(n * HG_BLOCK, HG_BLOCK), HG_BLOCK)
        rows_b = pl.ds(pl.multiple_of((nblk - 1 - n) * HG_BLOCK, HG_BLOCK), HG_BLOCK)
        out_f, st_f = _hg_block(zf_ref[0, rows_f, :], i_ref[0, rows_f, :], q_ref[0, rows_f, :], lb, st_f, False, tri_f)
        out_b, st_b = _hg_block(zb_ref[0, rows_b, :], i_ref[0, rows_b, :], q_ref[0, rows_b, :], lb, st_b, True, tri_b)
        accf_ref[rows_f, :] = out_f
        accb_ref[rows_b, :] = out_b
        return st_f, st_b

    lax.fori_loop(0, nblk, body, (st_f, st_b), unroll=16 if nblk % 16 == 0 else 1)
    gate = gate_ref[0]
    o = accf_ref[...] + accb_ref[...]
    o_ref[0] = (_rms(o, gn_ref[...]) * (gate * jax.nn.sigmoid(gate))).astype(o_ref.dtype)


def _hgrn2_core(p_lat, p_ctx, lb, gn_g):
    B, S, n5 = p_lat.shape
    D = n5 // 5
    H = D // HG_EXPAND
    L = p_ctx.shape[1]
    assert S % HG_BLOCK == 0 and L % HG_CHUNK == 0

    def lat(k):
        return pl.BlockSpec((1, S, HG_EXPAND), lambda b, h: (b, 0, k * H + h))

    def ctx(k):
        return pl.BlockSpec((1, L, HG_EXPAND), lambda b, h: (b, 0, k * H + h))

    vec = pl.BlockSpec((1, HG_EXPAND), lambda b, h: (0, h))
    return pl.pallas_call(
        functools.partial(_hg_kernel, s_tok=S),
        grid=(B, H),
        in_specs=[lat(0), lat(1), lat(2), lat(3), lat(4), ctx(1), ctx(3), ctx(4), vec, _full((1, HG_EXPAND))],
        out_specs=pl.BlockSpec((1, S, HG_EXPAND), lambda b, h: (b, 0, h)),
        out_shape=jax.ShapeDtypeStruct((B, S, D), bf16),
        scratch_shapes=[pltpu.VMEM((S, HG_EXPAND), f32), pltpu.VMEM((S, HG_EXPAND), f32)],
        compiler_params=_cparams("parallel", "parallel"),
        name="hgrn2_core",
    )(p_lat, p_lat, p_lat, p_lat, p_lat, p_ctx, p_ctx, p_ctx, lb.reshape(1, D), gn_g.reshape(1, HG_EXPAND))


def kernel(x, c, ctx, c_ctx, w_ada, b_ada, g_pre, g_post, w_ff1, w_ff3, w_ff2, s5_a_re, s5_a_im, s5_log_dt, s5_b_re, s5_b_im, s5_c_re, s5_c_im, s5_d, s5_w_glu, s5_b_glu, da_w_qkv, da_w_o, da_lam_q1, da_lam_k1, da_lam_q2, da_lam_k2, da_subln, na_w_qkv, na_w_o, na_rpb, hg_w_qig, hg_w_f, hg_b_f, hg_lb_logits, hg_gnorm, hg_w_o):
    B, S, D = x.shape
    depth = w_ada.shape[0]
    n_mix = 4
    rows_pad = -(-(B + 1) // SUBLANES) * SUBLANES
    c_all = jnp.concatenate([c, c_ctx[None], jnp.zeros((rows_pad - B - 1, D), f32)], axis=0)
    mods = _ada(c_all, w_ada, b_ada)
    lb_p = jax.nn.softmax(hg_lb_logits.astype(f32), axis=0)
    lower_bounds = jnp.cumsum(lb_p, axis=0) - lb_p[0]
    w1b, w3b, w2b = w_ff1.astype(bf16), w_ff3.astype(bf16), w_ff2.astype(bf16)

    x_lat, x_ctx = x, ctx
    for i in range(depth):
        last = i == depth - 1
        occ, kind = i // n_mix, i % n_mix
        m_lat = mods[i, :B].reshape(B, 3 * N_SUB, D)
        m_ctx = mods[i, B:B + 1].reshape(1, 3 * N_SUB, D)
        ffn1 = functools.partial(_half_ffn, g_in=g_pre[i, 0], g_out=g_post[i, 0], w1=w1b[i, 0], w3=w3b[i, 0],
                                 w2=w2b[i, 0], j=0)
        ffn2 = functools.partial(_half_ffn, g_in=g_pre[i, 2], g_out=g_post[i, 2], w1=w1b[i, 1], w3=w3b[i, 1],
                                 w2=w2b[i, 1], j=2)
        x_lat, x_ctx = ffn1(x_lat, m_lat), ffn1(x_ctx, m_ctx)

        y_ctx = None
        if kind == 0:
            h_lat = _prenorm(x_lat, m_lat, g_pre[i, 1], 1)
            h_ctx = _prenorm(x_ctx, m_ctx, g_pre[i, 1], 1)
            y_lat, y_ctx = _s5_scan(h_lat, h_ctx, s5_a_re[occ], s5_a_im[occ], s5_log_dt[occ], s5_b_re[occ],
                                    s5_b_im[occ], s5_c_re[occ], s5_c_im[occ])
            mix = "glu"
            mix_head = (g_pre[i, 1], s5_d[occ])
            mix_tail = (g_post[i, 1], s5_w_glu[occ].astype(bf16), s5_b_glu[occ])
        elif kind == 1:
            wq = da_w_qkv[occ].astype(bf16)
            rope = _rope_tables(S, DIFF_HEAD_DIM)
            qkv_lat = _prenorm_proj(x_lat, m_lat, g_pre[i, 1], 1, wq, rope=rope, n_rope=2 * D)
            qkv_ctx = _prenorm_proj(x_ctx, m_ctx, g_pre[i, 1], 1, wq)
            lamv = jnp.stack([da_lam_q1[occ], da_lam_k1[occ], da_lam_q2[occ], da_lam_k2[occ]]).astype(f32)
            lam_init = 0.8 - 0.6 * math.exp(-0.3 * i)
            y_lat = _diff_attention(qkv_lat, qkv_ctx, qkv_lat, lamv, da_subln[occ], lam_init)
            if not last:
                y_ctx = _diff_attention(qkv_ctx, qkv_ctx, None, lamv, da_subln[occ], lam_init)
            mix, mix_head, mix_tail = "outproj", (), (g_post[i, 1], da_w_o[occ].astype(bf16))
        elif kind == 2:
            wq = na_w_qkv[occ].astype(bf16)
            qkv_lat = _prenorm_proj(x_lat, m_lat, g_pre[i, 1], 1, wq)
            qkv_ctx = _prenorm_proj(x_ctx, m_ctx, g_pre[i, 1], 1, wq)
            y_lat = _na_attention(qkv_lat, qkv_ctx, qkv_lat, _na_bias_blocks(na_rpb[occ]))
            if not last:
                y_ctx = _na_attention(qkv_ctx, qkv_ctx, None, None)
            mix, mix_head, mix_tail = "outproj", (), (g_post[i, 1], na_w_o[occ].astype(bf16))
        else:
            assert last, "HGRN2 context outputs are not needed when it is the last layer"
            wp = jnp.concatenate([hg_w_qig[occ], hg_w_f[occ, 0], hg_w_f[occ, 1]], axis=1).astype(bf16)
            bp = jnp.concatenate([jnp.zeros((3 * D,), f32), hg_b_f[occ, 0], hg_b_f[occ, 1]])
            p_lat = _prenorm_proj(x_lat, m_lat, g_pre[i, 1], 1, wp, bias=bp, out_dtype=f32)
            p_ctx = _prenorm_proj(x_ctx, m_ctx, g_pre[i, 1], 1, wp, bias=bp, out_dtype=f32)
            y_lat = _hgrn2_core(p_lat, p_ctx, lower_bounds[i], hg_gnorm[occ])
            mix, mix_head, mix_tail = "outproj", (), (g_post[i, 1], hg_w_o[occ].astype(bf16))

        x_lat = ffn2(x_lat, m_lat, mix=mix, mix_args=(y_lat,) + mix_head + mix_tail)
        if not last:
            x_ctx = ffn2(x_ctx, m_ctx, mix=mix, mix_args=(y_ctx,) + mix_head + mix_tail)
    return x_lat
```

```python
import functools
import math

import jax
import jax.numpy as jnp
from jax import lax
from jax.experimental import pallas as pl
from jax.experimental.pallas import tpu as pltpu

f32 = jnp.float32
bf16 = jnp.bfloat16

N_SUB = 3
RMS_EPS = 1e-6
NEG_INF = -1e30
ROPE_BASE = 10000.0
GRID_W = 64
S5_GROUP = 16
S5_STATE = 64
S5_CHUNK = 16
DIFF_HEAD_DIM = 64
DIFF_KEY_CHUNK = 1024
NA_HEAD_DIM = 64
WIN_H = 8
WIN_W = 16
NA_TILE_ROWS = 8
NA_WIN_ROWS = 16
NA_KEY_CHUNK = 512
NA_TILES_PER_STEP = 8
HG_EXPAND = 128
HG_CHUNK = 64
HG_BLOCK = 256
LANES = 128
SUBLANES = 8
VMEM_LIMIT = 56 * 1024 * 1024
TOKEN_TILES = (512, 256, 128)
PROJ_TOKEN_TILE = 256
PROJ_COL_STEP = 512
FFN_COL_STEP = 1024
ATTN_QUERY_TILE = 512
ADA_COL_TILES = 4


def _cparams(*sem):
    return pltpu.CompilerParams(dimension_semantics=sem, vmem_limit_bytes=VMEM_LIMIT)


def _dot(a, b):
    return jnp.dot(a, b, preferred_element_type=f32)


def _dot_nt(a, b):
    return lax.dot_general(a, b, (((1,), (1,)), ((), ())), preferred_element_type=f32)


def _rms(x, g):
    return x * lax.rsqrt(jnp.mean(jnp.square(x), axis=-1, keepdims=True) + RMS_EPS) * g


def _pre_norm(x, g, mod_ref, j):
    shift = mod_ref[0, 3 * j:3 * j + 1, :]
    scale = mod_ref[0, 3 * j + 1:3 * j + 2, :]
    return _rms(x, g) * (1 + scale) + shift


def _post_residual(x, y, g, mod_ref, j, weight):
    gate = mod_ref[0, 3 * j + 2:3 * j + 3, :]
    return x + weight * gate * _rms(y, g)


def _tok_tile(t):
    for tm in TOKEN_TILES:
        if t % tm == 0:
            return tm
    raise ValueError(f"token count {t} is not a multiple of {TOKEN_TILES[-1]}")


def _full(shape):
    return pl.BlockSpec(shape, lambda *_: (0,) * len(shape))


def _mod_spec(mod):
    nd = mod.shape[1]
    d = mod.shape[2]
    if mod.shape[0] == 1:
        return pl.BlockSpec((1, nd, d), lambda b, t: (0, 0, 0))
    return pl.BlockSpec((1, nd, d), lambda b, t: (b, 0, 0))


def _ada_kernel(c_ref, w_ref, b_ref, o_ref):
    c = c_ref[...]
    sc = (c * jax.nn.sigmoid(c)).astype(bf16)
    o_ref[0] = _dot(sc, w_ref[0].astype(bf16)) + b_ref[0]


def _ada(c_all, w_ada, b_ada):
    depth, d, n = w_ada.shape
    rows = c_all.shape[0]
    tn = n // ADA_COL_TILES
    return pl.pallas_call(
        _ada_kernel,
        grid=(depth, n // tn),
        in_specs=[pl.BlockSpec((rows, d), lambda i, t: (0, 0)),
                  pl.BlockSpec((1, d, tn), lambda i, t: (i, 0, t)),
                  pl.BlockSpec((1, 1, tn), lambda i, t: (i, 0, t))],
        out_specs=pl.BlockSpec((1, rows, tn), lambda i, t: (i, 0, t)),
        out_shape=jax.ShapeDtypeStruct((depth, rows, n), f32),
        compiler_params=_cparams("arbitrary", "arbitrary"),
        name="ada",
    )(c_all, w_ada, b_ada.reshape(depth, 1, n))


def _ffn_kernel(x_ref, mod_ref, gin_ref, gout_ref, w1_ref, w3_ref, w2_ref, *rest, j, chunks, mix):
    o_ref = rest[-1]
    x = x_ref[0]
    if mix == "outproj":
        y_ref, gmix_ref, wo_ref = rest[:-1]
        x = _post_residual(x, _dot(y_ref[0], wo_ref[...]), gmix_ref[...], mod_ref, 1, 1.0)
    elif mix == "glu":
        y_ref, gpre_ref, d_ref, gmix_ref, wg_ref, bg_ref = rest[:-1]
        z = jax.nn.gelu(d_ref[...] * _pre_norm(x, gpre_ref[...], mod_ref, 1) + y_ref[0])
        u = _dot(z.astype(bf16), wg_ref[...]) + bg_ref[...]
        x = _post_residual(x, z * jax.nn.sigmoid(u), gmix_ref[...], mod_ref, 1, 1.0)
    h = _pre_norm(x, gin_ref[...], mod_ref, j).astype(bf16)
    acc = None
    for s, n in chunks:
        a = _dot(h, w1_ref[:, s:s + n])
        b = _dot(h, w3_ref[:, s:s + n])
        g = (a * jax.nn.sigmoid(a) * b).astype(bf16)
        y = _dot(g, w2_ref[s:s + n, :])
        acc = y if acc is None else acc + y
    o_ref[0] = _post_residual(x, acc, gout_ref[...], mod_ref, j, 0.5)


def _ff_chunks(dff, step=FFN_COL_STEP):
    out, s = [], 0
    while s < dff:
        n = min(step, dff - s)
        out.append((s, n))
        s += n
    return tuple(out)


def _half_ffn(x, mod, g_in, g_out, w1, w3, w2, j, mix=None, mix_args=()):
    b, t, d = x.shape
    dff = w1.shape[1]
    tm = _tok_tile(t)
    tok = pl.BlockSpec((1, tm, d), lambda bb, tt: (bb, tt, 0))
    vec = _full((1, d))
    args = [x, mod, g_in.reshape(1, d), g_out.reshape(1, d), w1, w3, w2]
    specs = [tok, _mod_spec(mod), vec, vec, _full((d, dff)), _full((d, dff)), _full((dff, d))]
    if mix == "outproj":
        y, g_mix, w_o = mix_args
        args += [y, g_mix.reshape(1, d), w_o]
        specs += [tok, vec, _full((d, d))]
    elif mix == "glu":
        y, g_pre, d_skip, g_mix, w_g, b_g = mix_args
        args += [y, g_pre.reshape(1, d), d_skip.astype(f32).reshape(1, d), g_mix.reshape(1, d), w_g, b_g.reshape(1, d)]
        specs += [tok, vec, vec, vec, _full((d, d)), vec]
    return pl.pallas_call(
        functools.partial(_ffn_kernel, j=j, chunks=_ff_chunks(dff), mix=mix),
        grid=(b, t // tm),
        in_specs=specs,
        out_specs=tok,
        out_shape=jax.ShapeDtypeStruct((b, t, d), f32),
        compiler_params=_cparams("parallel", "parallel"),
        name="half_ffn",
    )(*args)


def _prenorm_kernel(x_ref, mod_ref, g_ref, o_ref, *, j):
    o_ref[0] = _pre_norm(x_ref[0], g_ref[...], mod_ref, j).astype(o_ref.dtype)


def _prenorm(x, mod, g, j):
    b, t, d = x.shape
    tm = _tok_tile(t)
    return pl.pallas_call(
        functools.partial(_prenorm_kernel, j=j),
        grid=(b, t // tm),
        in_specs=[pl.BlockSpec((1, tm, d), lambda bb, tt: (bb, tt, 0)), _mod_spec(mod), _full((1, d))],
        out_specs=pl.BlockSpec((1, tm, d), lambda bb, tt: (bb, tt, 0)),
        out_shape=jax.ShapeDtypeStruct((b, t, d), bf16),
        compiler_params=_cparams("parallel", "parallel"),
        name="prenorm",
    )(x, mod, g.reshape(1, d))


def _swap_pairs(x):
    lane = lax.broadcasted_iota(jnp.int32, x.shape, 1)
    nxt = pltpu.roll(x, LANES - 1, 1)
    prv = pltpu.roll(x, 1, 1)
    return jnp.where((lane & 1) == 0, nxt, prv)


def _proj_kernel(*refs, j, n_out, n_rope, has_bias, step):
    x_ref, mod_ref, g_ref, w_ref = refs[:4]
    k = 4
    b_ref = None
    if has_bias:
        b_ref = refs[k]
        k += 1
    if n_rope:
        cos_ref, sin_ref = refs[k], refs[k + 1]
        k += 2
    o_ref = refs[k]
    h = _pre_norm(x_ref[0], g_ref[...], mod_ref, j).astype(bf16)
    for s in range(0, n_out, step):
        y = _dot(h, w_ref[:, s:s + step])
        if has_bias:
            y = y + b_ref[:, s:s + step]
        if s < n_rope:
            cos = cos_ref[...]
            sin = sin_ref[...]
            parts = []
            for c in range(0, step, LANES):
                yc = y[:, c:c + LANES]
                parts.append(yc * cos + _swap_pairs(yc) * sin)
            y = jnp.concatenate(parts, axis=1)
        o_ref[0, :, s:s + step] = y.astype(o_ref.dtype)


def _prenorm_proj(x, mod, g, j, w, bias=None, rope=None, n_rope=0, out_dtype=bf16):
    b, t, d = x.shape
    n = w.shape[1]
    tm = min(_tok_tile(t), PROJ_TOKEN_TILE)
    step = PROJ_COL_STEP
    assert n % step == 0 and n_rope % step == 0
    args = [x, mod, g.reshape(1, d), w]
    specs = [pl.BlockSpec((1, tm, d), lambda bb, tt: (bb, tt, 0)), _mod_spec(mod), _full((1, d)), _full((d, n))]
    if bias is not None:
        args.append(bias.reshape(1, n))
        specs.append(_full((1, n)))
    if n_rope:
        args += list(rope)
        specs += [pl.BlockSpec((tm, LANES), lambda bb, tt: (tt, 0))] * 2
    return pl.pallas_call(
        functools.partial(_proj_kernel, j=j, n_out=n, n_rope=n_rope, has_bias=bias is not None, step=step),
        grid=(b, t // tm),
        in_specs=specs,
        out_specs=pl.BlockSpec((1, tm, n), lambda bb, tt: (bb, tt, 0)),
        out_shape=jax.ShapeDtypeStruct((b, t, n), out_dtype),
        compiler_params=_cparams("parallel", "parallel"),
        name="prenorm_proj",
    )(*args)


def _s5_kernel(xc_ref, xl_ref, wb_ref, wk_ref, wc_ref, lam_ref, yc_ref, yl_ref, s_ref, hin_ref, *, nb, nc_ctx, nc):
    rc = nc_ctx * nb
    xc, xl = xc_ref[0], xl_ref[0]
    s_ref[0:rc, :] = _dot(xc, wb_ref[0])
    s_ref[rc:, :] = _dot(xl, wb_ref[0])
    P = S5_STATE
    lam = lam_ref[0]
    a_re, a_im = lam[0:1], lam[1:2]
    fwd_lane = lax.broadcasted_iota(jnp.int32, (nb, 2 * P), 1) < P

    def step(n, carry):
        h_re, h_im = carry
        r_f = pl.multiple_of(n * nb, nb)
        n_b = jnp.where(n < nc_ctx, nc_ctx - 1 - n, nc - 1 - (n - nc_ctx))
        r_b = pl.multiple_of(n_b * nb, nb)
        rows_f, rows_b = pl.ds(r_f, nb), pl.ds(r_b, nb)
        hin_ref[rows_f, 0:P] = h_re[:, 0:P]
        hin_ref[rows_b, P:2 * P] = h_re[:, P:2 * P]
        hin_ref[rows_f, 2 * P:3 * P] = h_im[:, 0:P]
        hin_ref[rows_b, 3 * P:4 * P] = h_im[:, P:2 * P]
        s_re = jnp.where(fwd_lane, s_ref[rows_f, 0:2 * P], s_ref[rows_b, 0:2 * P])
        s_im = jnp.where(fwd_lane, s_ref[rows_f, 2 * P:4 * P], s_ref[rows_b, 2 * P:4 * P])
        return a_re * h_re - a_im * h_im + s_re, a_re * h_im + a_im * h_re + s_im

    zero = jnp.zeros((nb, 2 * P), f32)
    lax.fori_loop(0, nc, step, (zero, zero), unroll=4 if nc % 4 == 0 else 1)
    yc_ref[0] = _dot(xc, wk_ref[0]) + _dot(hin_ref[0:rc, :].astype(bf16), wc_ref[0])
    yl_ref[0] = _dot(xl, wk_ref[0]) + _dot(hin_ref[rc:, :].astype(bf16), wc_ref[0])

def _s5_weights(a_re, a_im, log_dt, b_re, b_im, c_re, c_im):
    hp = lax.Precision.HIGHEST
    C = S5_CHUNK
    G, P = a_re.shape[1:]
    N = b_re.shape[-1]
    j = jnp.arange(C + 1, dtype=f32)
    wbs, wks, wcs, lams = [], [], [], []
    for dirn in range(2):
        are, aim = a_re[dirn].astype(f32), a_im[dirn].astype(f32)
        dt = jnp.exp(log_dt[dirn].astype(f32))[:, None]
        pw_mag = jnp.exp(j[:, None, None] * (are * dt)[None])
        pw_re = pw_mag * jnp.cos(j[:, None, None] * (aim * dt)[None])
        pw_im = pw_mag * jnp.sin(j[:, None, None] * (aim * dt)[None])
        nr, ni = pw_re[1] - 1.0, pw_im[1]
        den = are * are + aim * aim
        fr = (nr * are + ni * aim) / den
        fi = (ni * are - nr * aim) / den
        bre, bim = b_re[dirn].astype(f32), b_im[dirn].astype(f32)
        bbr = fr[..., None] * bre - fi[..., None] * bim
        bbi = fr[..., None] * bim + fi[..., None] * bre
        cre, cim = c_re[dirn].astype(f32), c_im[dirn].astype(f32)
        dr, di = (pw_re[:C][::-1], pw_im[:C][::-1]) if dirn == 0 else (pw_re[:C], pw_im[:C])
        sr = dr[..., None] * bbr[None] - di[..., None] * bbi[None]
        si = dr[..., None] * bbi[None] + di[..., None] * bbr[None]
        wb = jnp.concatenate([sr, si], axis=2)
        wbs.append(wb.transpose(1, 0, 3, 2).reshape(G, C * N, 2 * P))
        pr, pi = pw_re[1:], pw_im[1:]
        if dirn == 1:
            pr, pi = pr[::-1], pi[::-1]
        or_ = cre[None] * pr[:, :, None, :] - cim[None] * pi[:, :, None, :]
        oi_ = -(cre[None] * pi[:, :, None, :] + cim[None] * pr[:, :, None, :])
        wc = jnp.concatenate([or_, oi_], axis=3)
        wcs.append(wc.transpose(1, 3, 0, 2).reshape(G, 2 * P, C * N))
        clr = cre[None] * pw_re[:C, :, None, :] - cim[None] * pw_im[:C, :, None, :]
        cli = cre[None] * pw_im[:C, :, None, :] + cim[None] * pw_re[:C, :, None, :]
        kk = (jnp.einsum('jgnp,gpm->gjnm', clr, bbr, precision=hp)
              - jnp.einsum('jgnp,gpm->gjnm', cli, bbi, precision=hp))
        s_idx = jnp.arange(C)[:, None]
        t_idx = jnp.arange(C)[None, :]
        lag = (t_idx - s_idx) if dirn == 0 else (s_idx - t_idx)
        onehot = (lag[:, :, None] == jnp.arange(C)[None, None, :]).astype(f32)
        kt = jnp.einsum('stj,gjnm->gsmtn', onehot, kk, precision=hp)
        wks.append(kt.reshape(G, C * N, C * N))
        lams.append((pw_re[C], pw_im[C]))
    wb = jnp.concatenate([wbs[0][..., :P], wbs[1][..., :P], wbs[0][..., P:], wbs[1][..., P:]], axis=2).astype(bf16)
    wc = jnp.concatenate([wcs[0][:, :P], wcs[1][:, :P], wcs[0][:, P:], wcs[1][:, P:]], axis=1).astype(bf16)
    wk = (wks[0] + wks[1]).astype(bf16)
    lam = jnp.stack([jnp.concatenate([lams[0][0], lams[1][0]], axis=-1),
                     jnp.concatenate([lams[0][1], lams[1][1]], axis=-1)], axis=1)
    return wb, wk, wc, lam


def _s5_scan(h_lat, h_ctx, a_re, a_im, log_dt, b_re, b_im, c_re, c_im):
    B, S, D = h_lat.shape
    L = h_ctx.shape[1]
    C, N, P = S5_CHUNK, S5_GROUP, S5_STATE
    G = D // N
    nc_ctx, nc = L // C, (L + S) // C
    assert 2 * P == LANES and L % C == 0 and S % C == 0
    wb, wk, wc, lam = _s5_weights(a_re, a_im, log_dt, b_re, b_im, c_re, c_im)

    def to_groups(h):
        n = h.shape[1] // C
        return h.reshape(B, n, C, G, N).transpose(3, 1, 0, 2, 4).reshape(G, n * B, C * N)

    def from_groups(y):
        n = y.shape[1] // B
        return y.reshape(G, n, B, C, N).transpose(2, 1, 3, 0, 4).reshape(B, n * C, D)

    rc, rl = nc_ctx * B, (nc - nc_ctx) * B

    def rows(r):
        return pl.BlockSpec((1, r, C * N), lambda g: (g, 0, 0))

    yc, yl = pl.pallas_call(
        functools.partial(_s5_kernel, nb=B, nc_ctx=nc_ctx, nc=nc),
        grid=(G,),
        in_specs=[rows(rc), rows(rl),
                  pl.BlockSpec((1, C * N, 4 * P), lambda g: (g, 0, 0)),
                  pl.BlockSpec((1, C * N, C * N), lambda g: (g, 0, 0)),
                  pl.BlockSpec((1, 4 * P, C * N), lambda g: (g, 0, 0)),
                  pl.BlockSpec((1, 2, 2 * P), lambda g: (g, 0, 0))],
        out_specs=[rows(rc), rows(rl)],
        out_shape=[jax.ShapeDtypeStruct((G, rc, C * N), f32), jax.ShapeDtypeStruct((G, rl, C * N), f32)],
        scratch_shapes=[pltpu.VMEM((rc + rl, 4 * P), f32), pltpu.VMEM((rc + rl, 4 * P), f32)],
        compiler_params=_cparams("parallel"),
        name="s5_scan",
    )(to_groups(h_ctx), to_groups(h_lat), wb, wk, wc, lam)
    return from_groups(yl), from_groups(yc)


def _diff_maps(q):
    lane = lax.broadcasted_iota(jnp.int32, q.shape, 1)
    qs = (q.astype(f32) * (DIFF_HEAD_DIM ** -0.5 * math.log2(math.e))).astype(bf16)
    zero = jnp.zeros_like(qs)
    return jnp.where(lane < DIFF_HEAD_DIM, qs, zero), jnp.where(lane >= DIFF_HEAD_DIM, qs, zero)


def _key_chunks(k_refs):
    out = []
    for seg, k_ref in enumerate(k_refs):
        n = k_ref.shape[1]
        out += [(seg, c, min(c + DIFF_KEY_CHUNK, n)) for c in range(0, n, DIFF_KEY_CHUNK)]
    return out


def _diff_step(q_next, cur, nxt, k_refs, v_refs, lamv_ref, g_ref, o_ref, lam_init):
    chunks = _key_chunks(k_refs)
    cur_s, cur_m = cur
    nxt_s, nxt_m = nxt
    tq = cur_m.shape[1]
    m1, m2 = cur_m[0, :, 0:1], cur_m[1, :, 0:1]
    q1n, q2n = _diff_maps(q_next)
    n1 = n2 = l1 = l2 = None
    for seg, a, b in chunks:
        s1 = _dot_nt(q1n, k_refs[seg][0, a:b, :])
        s2 = _dot_nt(q2n, k_refs[seg][0, a:b, :])
        nxt_s[seg][0, :, a:b] = s1
        nxt_s[seg][1, :, a:b] = s2
        c1 = jnp.max(s1, axis=-1, keepdims=True)
        c2 = jnp.max(s2, axis=-1, keepdims=True)
        n1 = c1 if n1 is None else jnp.maximum(n1, c1)
        n2 = c2 if n2 is None else jnp.maximum(n2, c2)
        p1 = jnp.exp2(cur_s[seg][0, :, a:b] - m1)
        p2 = jnp.exp2(cur_s[seg][1, :, a:b] - m2)
        cur_s[seg][0, :, a:b] = p1
        cur_s[seg][1, :, a:b] = p2
        r1 = jnp.sum(p1, axis=-1, keepdims=True)
        r2 = jnp.sum(p2, axis=-1, keepdims=True)
        l1 = r1 if l1 is None else l1 + r1
        l2 = r2 if l2 is None else l2 + r2
    nxt_m[0] = jnp.broadcast_to(n1, (tq, LANES))
    nxt_m[1] = jnp.broadcast_to(n2, (tq, LANES))
    lv = lamv_ref[...]
    lam = (jnp.exp(jnp.sum(lv[0:1] * lv[1:2], axis=-1, keepdims=True))
           - jnp.exp(jnp.sum(lv[2:3] * lv[3:4], axis=-1, keepdims=True)) + lam_init)
    c = lam * l1 / l2
    o = None
    for seg, a, b in chunks:
        w = (cur_s[seg][0, :, a:b] - c * cur_s[seg][1, :, a:b]).astype(bf16)
        part = _dot(w, v_refs[seg][0, a:b, :])
        o = part if o is None else o + part
    o = o / l1
    o_ref[0] = (_rms(o, g_ref[...]) * (1 - lam_init)).astype(o_ref.dtype)


def _diff_kernel(*refs, n_seg, lam_init):
    q_ref, qn_ref = refs[:2]
    k_refs = refs[2:2 + 3 * n_seg:3]
    v_refs = refs[3:3 + 3 * n_seg:3]
    kn_refs = refs[4:4 + 3 * n_seg:3]
    lamv_ref, g_ref, o_ref = refs[2 + 3 * n_seg:5 + 3 * n_seg]
    scratch = refs[5 + 3 * n_seg:]
    slots = tuple((scratch[p * (n_seg + 1):p * (n_seg + 1) + n_seg], scratch[p * (n_seg + 1) + n_seg])
                  for p in range(2))
    unit = (pl.program_id(0) * pl.num_programs(1) + pl.program_id(1)) * pl.num_programs(2) + pl.program_id(2)

    @pl.when(unit == 0)
    def _():
        q1, q2 = _diff_maps(q_ref[0])
        s_refs, m_ref = slots[0]
        for mp, qm in enumerate((q1, q2)):
            parts = [_dot_nt(qm, k_ref[0]) for k_ref in k_refs]
            for s_ref, sc in zip(s_refs, parts):
                s_ref[mp] = sc
            mx = functools.reduce(jnp.maximum, [jnp.max(sc, axis=-1, keepdims=True) for sc in parts])
            m_ref[mp] = jnp.broadcast_to(mx, m_ref.shape[1:])

    for parity in range(2):
        @pl.when((unit & 1) == parity)
        def _(parity=parity):
            _diff_step(qn_ref[0], slots[parity], slots[1 - parity], kn_refs, v_refs, lamv_ref, g_ref, o_ref,
                       lam_init)


def _diff_attention(qkv_q, qkv_ctx, qkv_lat, lamv, subln, lam_init):
    B, T, n3 = qkv_q.shape
    D = n3 // 3
    H = D // LANES
    tq = min(_tok_tile(T), ATTN_QUERY_TILE)
    nq = T // tq
    kv = [qkv_ctx] if qkv_lat is None else [qkv_ctx, qkv_lat]

    def following(b, h, t):
        u = jnp.minimum((b * H + h) * nq + t + 1, B * H * nq - 1)
        return u // (H * nq), (u // nq) % H, u % nq

    def q_next(b, h, t):
        nb, nh, nt = following(b, h, t)
        return nb, nt, nh

    def k_next(b, h, t):
        nb, nh, _ = following(b, h, t)
        return nb, 0, H + nh

    args = [qkv_q, qkv_q]
    specs = [pl.BlockSpec((1, tq, LANES), lambda b, h, t: (b, t, h)), pl.BlockSpec((1, tq, LANES), q_next)]
    for a in kv:
        args += [a, a, a]
        specs += [pl.BlockSpec((1, a.shape[1], LANES), lambda b, h, t: (b, 0, H + h)),
                  pl.BlockSpec((1, a.shape[1], LANES), lambda b, h, t: (b, 0, 2 * H + h)),
                  pl.BlockSpec((1, a.shape[1], LANES), k_next)]
    args += [lamv, subln.reshape(1, LANES)]
    specs += [_full(lamv.shape), _full((1, LANES))]
    slot = [pltpu.VMEM((2, tq, a.shape[1]), f32) for a in kv] + [pltpu.VMEM((2, tq, LANES), f32)]
    return pl.pallas_call(
        functools.partial(_diff_kernel, n_seg=len(kv), lam_init=lam_init),
        grid=(B, H, nq),
        in_specs=specs,
        out_specs=pl.BlockSpec((1, tq, LANES), lambda b, h, t: (b, t, h)),
        out_shape=jax.ShapeDtypeStruct((B, T, D), bf16),
        scratch_shapes=slot + slot,
        compiler_params=_cparams("arbitrary", "arbitrary", "arbitrary"),
        name="diff_attention",
    )(*args)


def _rope_tables(n_tokens, head_dim):
    n_freq = head_dim // 4
    inv_freq = ROPE_BASE ** (-jnp.arange(n_freq, dtype=f32) / n_freq)
    t = jnp.arange(n_tokens)
    row = (t // GRID_W).astype(f32)
    col = (t % GRID_W).astype(f32)
    ang = jnp.concatenate([row[:, None] * inv_freq, col[:, None] * inv_freq], axis=-1)
    cos = jnp.repeat(jnp.cos(ang), 2, axis=-1)
    sin = jnp.repeat(jnp.sin(ang), 2, axis=-1) * jnp.tile(jnp.array([-1.0, 1.0], f32), head_dim // 2)
    reps = LANES // head_dim
    return jnp.tile(cos, (1, reps)), jnp.tile(sin, (1, reps))


def _na_head(q, hh):
    lane = lax.broadcasted_iota(jnp.int32, q.shape, 1)
    qs = (q.astype(f32) * (NA_HEAD_DIM ** -0.5 * math.log2(math.e))).astype(bf16)
    sel = (lane < NA_HEAD_DIM) if hh == 0 else (lane >= NA_HEAD_DIM)
    return jnp.where(sel, qs, jnp.zeros_like(qs))


def _na_ctx_kernel(q_ref, kc_ref, vc_ref, o_ref):
    q = q_ref[0]
    lane = lax.broadcasted_iota(jnp.int32, q.shape, 1)
    outs = []
    for hh in range(2):
        s = _dot_nt(_na_head(q, hh), kc_ref[0])
        p = jnp.exp2(s - jnp.max(s, axis=-1, keepdims=True))
        outs.append(_dot(p.astype(bf16), vc_ref[0]) / jnp.sum(p, axis=-1, keepdims=True))
    o_ref[0] = jnp.where(lane < NA_HEAD_DIM, outs[0], outs[1]).astype(o_ref.dtype)


def _na_window_table(rows):
    table = []
    for t in range(rows // NA_TILE_ROWS):
        w0 = min(max(t * NA_TILE_ROWS - (NA_WIN_ROWS - NA_TILE_ROWS) // 2, 0), rows - NA_WIN_ROWS)
        line = [w0 * GRID_W]
        for i in range(NA_TILE_ROWS):
            qr = t * NA_TILE_ROWS + i
            r0 = min(max(qr - WIN_H // 2, 0), rows - WIN_H)
            for j in range(NA_WIN_ROWS // 2):
                kr = w0 + 2 * j
                e = kr - qr + WIN_H
                ok0 = r0 <= kr < r0 + WIN_H
                ok1 = r0 <= kr + 1 < r0 + WIN_H
                line.append(e if ok0 and ok1 else 2 * WIN_H - 1 + e if ok0 else 4 * WIN_H - 1 + e if ok1 else 0)
        table.append(line)
    return jnp.array(table, jnp.int32)


def _na_window(win_ref, t):
    n_pairs = NA_WIN_ROWS // 2
    entry = [[win_ref[t, 1 + i * n_pairs + j] for j in range(n_pairs)] for i in range(NA_TILE_ROWS)]
    return pl.multiple_of(win_ref[t, 0], 4 * GRID_W), entry


def _lane_fold(x, op):
    return functools.reduce(op, [x[:, c:c + LANES] for c in range(0, x.shape[1], LANES)])


def _na_biased(sl, hh, entry, j0, tb_ref, s_ref, c0):
    tiles = []
    for i in range(NA_TILE_ROWS):
        r = slice(i * GRID_W, (i + 1) * GRID_W)
        mx = None
        for jj in range(sl.shape[1] // LANES):
            blk = sl[r, jj * LANES:(jj + 1) * LANES] + tb_ref[0, hh, entry[i][j0 + jj]]
            s_ref[r, c0 + jj * LANES:c0 + (jj + 1) * LANES] = blk
            mx = blk if mx is None else jnp.maximum(mx, blk)
        tiles.append(mx)
    return jnp.concatenate(tiles, axis=0)


def _na_unit(make, use, kc_ref, vc_ref, kl_ref, vl_ref, tb_ref):
    q, hh_m, tok_m, entry, (mc_ref, ml_ref, mm_ref) = make
    hh_u, tok_u, (uc_ref, ul_ref, um_ref) = use
    qh = _na_head(q, hh_m)
    m = um_ref[:, 0:1]
    run_max = l = o = None
    for c0 in [None] + list(range(0, NA_WIN_ROWS * GRID_W, NA_KEY_CHUNK)):
        if c0 is None:
            sc = _dot_nt(qh, kc_ref[0])
            mc_ref[...] = sc
            run_max = _lane_fold(sc, jnp.maximum)
            p = jnp.exp2(uc_ref[...] - m)
            v = vc_ref[0]
        else:
            sl = _dot_nt(qh, kl_ref[0, pl.ds(tok_m + c0, NA_KEY_CHUNK), :])
            run_max = jnp.maximum(run_max, _na_biased(sl, hh_m, entry, c0 // LANES, tb_ref, ml_ref, c0))
            p = jnp.exp2(ul_ref[:, c0:c0 + NA_KEY_CHUNK] - m)
            v = vl_ref[0, pl.ds(tok_u + c0, NA_KEY_CHUNK), :]
        r = _lane_fold(p, jnp.add)
        part = _dot(p.astype(bf16), v)
        l = r if l is None else l + r
        o = part if o is None else o + part
    mm_ref[...] = jnp.broadcast_to(jnp.max(run_max, axis=-1, keepdims=True), mm_ref.shape)
    return o / jnp.sum(l, axis=-1, keepdims=True)


def _na_kernel(win_ref, q_ref, qn_ref, kc_ref, vc_ref, kl_ref, vl_ref, tb_ref, o_ref, *scratch, n_tiles, tps):
    slots = (scratch[:3], scratch[3:])
    t = pl.program_id(2)
    tq = qn_ref.shape[1]
    tile_ids = [t * tps + k for k in range(tps)] + [jnp.minimum((t + 1) * tps, n_tiles - 1)]
    windows = [_na_window(win_ref, i) for i in tile_ids]
    qs = [q_ref[0, k * tq:(k + 1) * tq, :] for k in range(tps)] + [qn_ref[0]]
    shared = (kc_ref, vc_ref, kl_ref, vl_ref, tb_ref)

    @pl.when(t == 0)
    def _():
        tok, entry = windows[0]
        qh = _na_head(qs[0], 0)
        sc = _dot_nt(qh, kc_ref[0])
        slots[0][0][...] = sc
        sl = _dot_nt(qh, kl_ref[0, pl.ds(tok, NA_WIN_ROWS * GRID_W), :])
        run_max = jnp.maximum(_lane_fold(sc, jnp.maximum), _na_biased(sl, 0, entry, 0, tb_ref, slots[0][1], 0))
        slots[0][2][...] = jnp.broadcast_to(jnp.max(run_max, axis=-1, keepdims=True), slots[0][2].shape)

    units = [(k, hh) for k in range(tps) for hh in range(2)] + [(tps, 0)]
    lane = lax.broadcasted_iota(jnp.int32, (tq, LANES), 1)
    outs = {}
    for (k, hh), (nk, nhh) in zip(units[:-1], units[1:]):
        make = (qs[nk], nhh, windows[nk][0], windows[nk][1], slots[nhh])
        outs[k, hh] = _na_unit(make, (hh, windows[k][0], slots[hh]), *shared)
        if hh == 1:
            o_ref[0, k * tq:(k + 1) * tq, :] = jnp.where(lane < NA_HEAD_DIM, outs[k, 0], outs[k, 1]).astype(o_ref.dtype)


def _na_bias_blocks(rpb):
    H, n_dr, n_dc = rpb.shape
    col = jnp.arange(GRID_W)
    c0 = jnp.clip(col - WIN_W // 2, 0, GRID_W - WIN_W)
    col_ok = (col[None, :] >= c0[:, None]) & (col[None, :] < c0[:, None] + WIN_W)
    dc = jnp.clip(col[None, :] - col[:, None] + WIN_W - 1, 0, n_dc - 1)
    onehot = (dc.reshape(1, -1) == jnp.arange(n_dc)[:, None]).astype(f32)
    blk = jnp.einsum('hrj,jx->hrx', rpb.astype(f32) * math.log2(math.e), onehot, precision=lax.Precision.HIGHEST)
    blk = jnp.where(col_ok[None, None], blk.reshape(H, n_dr, GRID_W, GRID_W), NEG_INF)
    blk = jnp.concatenate([jnp.full((H, 1, GRID_W, GRID_W), NEG_INF, f32), blk], axis=1)
    n_e = 2 * WIN_H - 1
    left = [0] + list(range(1, n_e)) + [0] + list(range(1, n_e + 1)) + [0] * n_e
    right = [0] + list(range(2, n_e + 1)) + [0] + [0] * n_e + list(range(1, n_e + 1))
    tab = jnp.concatenate([jnp.take(blk, jnp.array(left), axis=1), jnp.take(blk, jnp.array(right), axis=1)], axis=-1)
    return tab.reshape((H // 2, 2) + tab.shape[1:])


def _na_attention(qkv_q, qkv_ctx, qkv_lat, bias):
    B, T, n3 = qkv_q.shape
    D = n3 // 3
    HP = D // LANES
    L = qkv_ctx.shape[1]
    ctx_k = pl.BlockSpec((1, L, LANES), lambda b, h, t: (b, 0, HP + h))
    ctx_v = pl.BlockSpec((1, L, LANES), lambda b, h, t: (b, 0, 2 * HP + h))
    if qkv_lat is None:
        tq = _tok_tile(T)
        tile = pl.BlockSpec((1, tq, LANES), lambda b, h, t: (b, t, h))
        return pl.pallas_call(
            _na_ctx_kernel,
            grid=(B, HP, T // tq),
            in_specs=[tile, ctx_k, ctx_v],
            out_specs=tile,
            out_shape=jax.ShapeDtypeStruct((B, T, D), bf16),
            compiler_params=_cparams("parallel", "parallel", "parallel"),
            name="na_attention_ctx",
        )(qkv_q, qkv_ctx, qkv_ctx)
    tq = NA_TILE_ROWS * GRID_W
    n_tiles = T // tq
    n_win = NA_WIN_ROWS * GRID_W
    assert T % tq == 0 and T // GRID_W >= NA_WIN_ROWS and bias.shape[2] == 6 * WIN_H - 2
    tps = max(d for d in range(1, NA_TILES_PER_STEP + 1) if n_tiles % d == 0)
    tile = pl.BlockSpec((1, tps * tq, LANES), lambda b, h, t: (b, t, h))
    slot = [pltpu.VMEM((tq, L), f32), pltpu.VMEM((tq, n_win), f32), pltpu.VMEM((tq, LANES), f32)]
    return pl.pallas_call(
        functools.partial(_na_kernel, n_tiles=n_tiles, tps=tps),
        grid=(B, HP, n_tiles // tps),
        in_specs=[pl.BlockSpec(memory_space=pltpu.SMEM), tile,
                  pl.BlockSpec((1, tq, LANES), lambda b, h, t: (b, jnp.minimum((t + 1) * tps, n_tiles - 1), h)),
                  ctx_k, ctx_v,
                  pl.BlockSpec((1, T, LANES), lambda b, h, t: (b, 0, HP + h)),
                  pl.BlockSpec((1, T, LANES), lambda b, h, t: (b, 0, 2 * HP + h)),
                  pl.BlockSpec((1,) + bias.shape[1:], lambda b, h, t: (h, 0, 0, 0, 0))],
        out_specs=tile,
        out_shape=jax.ShapeDtypeStruct((B, T, D), bf16),
        scratch_shapes=slot + slot,
        compiler_params=_cparams("parallel", "parallel", "arbitrary"),
        name="na_attention",
    )(_na_window_table(T // GRID_W), qkv_q, qkv_q, qkv_ctx, qkv_ctx, qkv_lat, qkv_lat, bias)


def _chunk_cumsum(x, reverse):
    n = x.shape[0]
    pos = lax.broadcasted_iota(jnp.int32, x.shape, 0) & (HG_CHUNK - 1)
    k = 1
    while k < HG_CHUNK:
        if reverse:
            x = x + jnp.where(pos < HG_CHUNK - k, pltpu.roll(x, n - k, 0), 0.0)
        else:
            x = x + jnp.where(pos >= k, pltpu.roll(x, k, 0), 0.0)
        k *= 2
    return x


def _hg_mask(n, reverse):
    ri = lax.broadcasted_iota(jnp.int32, (n, n), 0)
    ci = lax.broadcasted_iota(jnp.int32, (n, n), 1)
    shift = HG_CHUNK.bit_length() - 1
    return ((ri >> shift) == (ci >> shift)) & ((ci >= ri) if reverse else (ci <= ri))


def _hg_block(z, v, q, lb, st, reverse, tri):
    with_out = q is not None
    n = z.shape[0]
    nch = n // HG_CHUNK
    f = lb + (1 - lb) * jax.nn.sigmoid(z)
    lf = jnp.log(f)
    kk = 1 - f
    g = _chunk_cumsum(lf, reverse)
    out = None
    if with_out:
        q_dec = q * jnp.exp(g)
        k_inv = kk * jnp.exp(-g)
        att = jnp.where(tri, _dot_nt(q_dec.astype(bf16), k_inv.astype(bf16)), 0.0)
        out_intra = _dot(att.astype(bf16), v.astype(bf16))
        outs = [None] * nch
    order = range(nch - 1, -1, -1) if reverse else range(nch)
    for c in order:
        sl = slice(c * HG_CHUNK, (c + 1) * HG_CHUNK)
        end = c * HG_CHUNK if reverse else (c + 1) * HG_CHUNK - 1
        g_tot = g[end:end + 1]
        if with_out:
            outs[c] = _dot_nt(q_dec[sl].astype(bf16), st.astype(bf16))
        kdec = kk[sl] * jnp.exp(g_tot - g[sl])
        ds_t = _dot(v[sl].T.astype(bf16), kdec.astype(bf16))
        st = st * jnp.exp(g_tot) + ds_t
    if with_out:
        out = out_intra + jnp.concatenate(outs, axis=0)
    return out, st


def _hg_kernel(q_ref, i_ref, gate_ref, zf_ref, zb_ref, ic_ref, zfc_ref, zbc_ref, lb_ref, gn_ref, o_ref,
               accf_ref, accb_ref, *, s_tok):
    lb = lb_ref[...]
    nblk = s_tok // HG_BLOCK
    zero = jnp.zeros((HG_EXPAND, HG_EXPAND), f32)
    _, st_f = _hg_block(zfc_ref[0], ic_ref[0], None, lb, zero, False, None)
    _, st_b = _hg_block(zbc_ref[0], ic_ref[0], None, lb, zero, True, None)
    tri_f, tri_b = _hg_mask(HG_BLOCK, False), _hg_mask(HG_BLOCK, True)

    def body(n, carry):
        st_f, st_b = carry
        rows_f = pl.ds(pl.multiple_of(n * HG_BLOCK, HG_BLOCK), HG_BLOCK)
        rows_b = pl.ds(pl.multiple_of((nblk - 1 - n) * HG_BLOCK, HG_BLOCK), HG_BLOCK)
        out_f, st_f = _hg_block(zf_ref[0, rows_f, :], i_ref[0, rows_f, :], q_ref[0, rows_f, :], lb, st_f, False, tri_f)
        out_b, st_b = _hg_block(zb_ref[0, rows_b, :], i_ref[0, rows_b, :], q_ref[0, rows_b, :], lb, st_b, True, tri_b)
        accf_ref[rows_f, :] = out_f
        accb_ref[rows_b, :] = out_b
        return st_f, st_b

    lax.fori_loop(0, nblk, body, (st_f, st_b), unroll=16 if nblk % 16 == 0 else 1)
    gate = gate_ref[0]
    o = accf_ref[...] + accb_ref[...]
    o_ref[0] = (_rms(o, gn_ref[...]) * (gate * jax.nn.sigmoid(gate))).astype(o_ref.dtype)


def _hgrn2_core(p_lat, p_ctx, lb, gn_g):
    B, S, n5 = p_lat.shape
    D = n5 // 5
    H = D // HG_EXPAND
    L = p_ctx.shape[1]
    assert S % HG_BLOCK == 0 and L % HG_CHUNK == 0

    def lat(k):
        return pl.BlockSpec((1, S, HG_EXPAND), lambda b, h: (b, 0, k * H + h))

    def ctx(k):
        return pl.BlockSpec((1, L, HG_EXPAND), lambda b, h: (b, 0, k * H + h))

    vec = pl.BlockSpec((1, HG_EXPAND), lambda b, h: (0, h))
    return pl.pallas_call(
        functools.partial(_hg_kernel, s_tok=S),
        grid=(B, H),
        in_specs=[lat(0), lat(1), lat(2), lat(3), lat(4), ctx(1), ctx(3), ctx(4), vec, _full((1, HG_EXPAND))],
        out_specs=pl.BlockSpec((1, S, HG_EXPAND), lambda b, h: (b, 0, h)),
        out_shape=jax.ShapeDtypeStruct((B, S, D), bf16),
        scratch_shapes=[pltpu.VMEM((S, HG_EXPAND), f32), pltpu.VMEM((S, HG_EXPAND), f32)],
        compiler_params=_cparams("parallel", "parallel"),
        name="hgrn2_core",
    )(p_lat, p_lat, p_lat, p_lat, p_lat, p_ctx, p_ctx, p_ctx, lb.reshape(1, D), gn_g.reshape(1, HG_EXPAND))


def kernel(x, c, ctx, c_ctx, w_ada, b_ada, g_pre, g_post, w_ff1, w_ff3, w_ff2, s5_a_re, s5_a_im, s5_log_dt, s5_b_re, s5_b_im, s5_c_re, s5_c_im, s5_d, s5_w_glu, s5_b_glu, da_w_qkv, da_w_o, da_lam_q1, da_lam_k1, da_lam_q2, da_lam_k2, da_subln, na_w_qkv, na_w_o, na_rpb, hg_w_qig, hg_w_f, hg_b_f, hg_lb_logits, hg_gnorm, hg_w_o):
    B, S, D = x.shape
    depth = w_ada.shape[0]
    n_mix = 4
    rows_pad = -(-(B + 1) // SUBLANES) * SUBLANES
    c_all = jnp.concatenate([c, c_ctx[None], jnp.zeros((rows_pad - B - 1, D), f32)], axis=0)
    mods = _ada(c_all, w_ada, b_ada)
    lb_p = jax.nn.softmax(hg_lb_logits.astype(f32), axis=0)
    lower_bounds = jnp.cumsum(lb_p, axis=0) - lb_p[0]
    w1b, w3b, w2b = w_ff1.astype(bf16), w_ff3.astype(bf16), w_ff2.astype(bf16)

    x_lat, x_ctx = x, ctx
    for i in range(depth):
        last = i == depth - 1
        occ, kind = i // n_mix, i % n_mix
        m_lat = mods[i, :B].reshape(B, 3 * N_SUB, D)
        m_ctx = mods[i, B:B + 1].reshape(1, 3 * N_SUB, D)
        ffn1 = functools.partial(_half_ffn, g_in=g_pre[i, 0], g_out=g_post[i, 0], w1=w1b[i, 0], w3=w3b[i, 0],
                                 w2=w2b[i, 0], j=0)
        ffn2 = functools.partial(_half_ffn, g_in=g_pre[i, 2], g_out=g_post[i, 2], w1=w1b[i, 1], w3=w3b[i, 1],
                                 w2=w2b[i, 1], j=2)
        x_lat, x_ctx = ffn1(x_lat, m_lat), ffn1(x_ctx, m_ctx)

        y_ctx = None
        if kind == 0:
            h_lat = _prenorm(x_lat, m_lat, g_pre[i, 1], 1)
            h_ctx = _prenorm(x_ctx, m_ctx, g_pre[i, 1], 1)
            y_lat, y_ctx = _s5_scan(h_lat, h_ctx, s5_a_re[occ], s5_a_im[occ], s5_log_dt[occ], s5_b_re[occ],
                                    s5_b_im[occ], s5_c_re[occ], s5_c_im[occ])
            mix = "glu"
            mix_head = (g_pre[i, 1], s5_d[occ])
            mix_tail = (g_post[i, 1], s5_w_glu[occ].astype(bf16), s5_b_glu[occ])
        elif kind == 1:
            wq = da_w_qkv[occ].astype(bf16)
            rope = _rope_tables(S, DIFF_HEAD_DIM)
            qkv_lat = _prenorm_proj(x_lat, m_lat, g_pre[i, 1], 1, wq, rope=rope, n_rope=2 * D)
            qkv_ctx = _prenorm_proj(x_ctx, m_ctx, g_pre[i, 1], 1, wq)
            lamv = jnp.stack([da_lam_q1[occ], da_lam_k1[occ], da_lam_q2[occ], da_lam_k2[occ]]).astype(f32)
            lam_init = 0.8 - 0.6 * math.exp(-0.3 * i)
            y_lat = _diff_attention(qkv_lat, qkv_ctx, qkv_lat, lamv, da_subln[occ], lam_init)
            if not last:
                y_ctx = _diff_attention(qkv_ctx, qkv_ctx, None, lamv, da_subln[occ], lam_init)
            mix, mix_head, mix_tail = "outproj", (), (g_post[i, 1], da_w_o[occ].astype(bf16))
        elif kind == 2:
            wq = na_w_qkv[occ].astype(bf16)
            qkv_lat = _prenorm_proj(x_lat, m_lat, g_pre[i, 1], 1, wq)
            qkv_ctx = _prenorm_proj(x_ctx, m_ctx, g_pre[i, 1], 1, wq)
            y_lat = _na_attention(qkv_lat, qkv_ctx, qkv_lat, _na_bias_blocks(na_rpb[occ]))
            if not last:
                y_ctx = _na_attention(qkv_ctx, qkv_ctx, None, None)
            mix, mix_head, mix_tail = "outproj", (), (g_post[i, 1], na_w_o[occ].astype(bf16))
        else:
            assert last, "HGRN2 context outputs are not needed when it is the last layer"
            wp = jnp.concatenate([hg_w_qig[occ], hg_w_f[occ, 0], hg_w_f[occ, 1]], axis=1).astype(bf16)
            bp = jnp.concatenate([jnp.zeros((3 * D,), f32), hg_b_f[occ, 0], hg_b_f[occ, 1]])
            p_lat = _prenorm_proj(x_lat, m_lat, g_pre[i, 1], 1, wp, bias=bp, out_dtype=f32)
            p_ctx = _prenorm_proj(x_ctx, m_ctx, g_pre[i, 1], 1, wp, bias=bp, out_dtype=f32)
            y_lat = _hgrn2_core(p_lat, p_ctx, lower_bounds[i], hg_gnorm[occ])
            mix, mix_head, mix_tail = "outproj", (), (g_post[i, 1], hg_w_o[occ].astype(bf16))

        x_lat = ffn2(x_lat, m_lat, mix=mix, mix_args=(y_lat,) + mix_head + mix_tail)
        if not last:
            x_ctx = ffn2(x_ctx, m_ctx, mix=mix, mix_args=(y_ctx,) + mix_head + mix_tail)
    return x_lat
```

```python
import functools
import math

import jax
import jax.numpy as jnp
from jax import lax
from jax.experimental import pallas as pl
from jax.experimental.pallas import tpu as pltpu

f32 = jnp.float32
bf16 = jnp.bfloat16

N_SUB = 3
RMS_EPS = 1e-6
NEG_INF = -1e30
ROPE_BASE = 10000.0
GRID_W = 64
S5_GROUP = 16
S5_STATE = 64
S5_CHUNK = 16
DIFF_HEAD_DIM = 64
DIFF_KEY_CHUNK = 1024
NA_HEAD_DIM = 64
WIN_H = 8
WIN_W = 16
NA_TILE_ROWS = 8
NA_WIN_ROWS = 16
NA_KEY_CHUNK = 512
NA_TILES_PER_STEP = 8
HG_EXPAND = 128
HG_CHUNK = 64
HG_BLOCK = 256
LANES = 128
SUBLANES = 8
VMEM_LIMIT = 56 * 1024 * 1024
TOKEN_TILES = (512, 256, 128)
PROJ_TOKEN_TILE = 256
PROJ_COL_STEP = 512
FFN_COL_STEP = 1024
ATTN_QUERY_TILE = 512
ADA_COL_TILES = 4


def _cparams(*sem):
    return pltpu.CompilerParams(dimension_semantics=sem, vmem_limit_bytes=VMEM_LIMIT)


def _dot(a, b):
    return jnp.dot(a, b, preferred_element_type=f32)


def _dot_nt(a, b):
    return lax.dot_general(a, b, (((1,), (1,)), ((), ())), preferred_element_type=f32)


def _rms(x, g):
    return x * lax.rsqrt(jnp.mean(jnp.square(x), axis=-1, keepdims=True) + RMS_EPS) * g


def _pre_norm(x, g, mod_ref, j):
    shift = mod_ref[0, 3 * j:3 * j + 1, :]
    scale = mod_ref[0, 3 * j + 1:3 * j + 2, :]
    return _rms(x, g) * (1 + scale) + shift


def _post_residual(x, y, g, mod_ref, j, weight):
    gate = mod_ref[0, 3 * j + 2:3 * j + 3, :]
    return x + weight * gate * _rms(y, g)


def _tok_tile(t):
    for tm in TOKEN_TILES:
        if t % tm == 0:
            return tm
    raise ValueError(f"token count {t} is not a multiple of {TOKEN_TILES[-1]}")


def _full(shape):
    return pl.BlockSpec(shape, lambda *_: (0,) * len(shape))


def _mod_spec(mod):
    nd = mod.shape[1]
    d = mod.shape[2]
    if mod.shape[0] == 1:
        return pl.BlockSpec((1, nd, d), lambda b, t: (0, 0, 0))
    return pl.BlockSpec((1, nd, d), lambda b, t: (b, 0, 0))


def _ada_kernel(c_ref, w_ref, b_ref, o_ref):
    c = c_ref[...]
    sc = (c * jax.nn.sigmoid(c)).astype(bf16)
    o_ref[0] = _dot(sc, w_ref[0].astype(bf16)) + b_ref[0]


def _ada(c_all, w_ada, b_ada):
    depth, d, n = w_ada.shape
    rows = c_all.shape[0]
    tn = n // ADA_COL_TILES
    return pl.pallas_call(
        _ada_kernel,
        grid=(depth, n // tn),
        in_specs=[pl.BlockSpec((rows, d), lambda i, t: (0, 0)),
                  pl.BlockSpec((1, d, tn), lambda i, t: (i, 0, t)),
                  pl.BlockSpec((1, 1, tn), lambda i, t: (i, 0, t))],
        out_specs=pl.BlockSpec((1, rows, tn), lambda i, t: (i, 0, t)),
        out_shape=jax.ShapeDtypeStruct((depth, rows, n), f32),
        compiler_params=_cparams("arbitrary", "arbitrary"),
        name="ada",
    )(c_all, w_ada, b_ada.reshape(depth, 1, n))


def _ffn_kernel(x_ref, mod_ref, gin_ref, gout_ref, w1_ref, w3_ref, w2_ref, *rest, j, chunks, mix, emit_h):
    if emit_h:
        gh_ref, o_ref, h_ref = rest[-3:]
        rest = rest[:-3]
    else:
        o_ref = rest[-1]
        rest = rest[:-1]
    x = x_ref[0]
    if mix == "outproj":
        y_ref, gmix_ref, wo_ref = rest
        x = _post_residual(x, _dot(y_ref[0], wo_ref[...]), gmix_ref[...], mod_ref, 1, 1.0)
    elif mix == "glu":
        y_ref, gpre_ref, d_ref, gmix_ref, wg_ref, bg_ref = rest
        z = jax.nn.gelu(d_ref[...] * _pre_norm(x, gpre_ref[...], mod_ref, 1) + y_ref[0])
        u = _dot(z.astype(bf16), wg_ref[...]) + bg_ref[...]
        x = _post_residual(x, z * jax.nn.sigmoid(u), gmix_ref[...], mod_ref, 1, 1.0)
    h = _pre_norm(x, gin_ref[...], mod_ref, j).astype(bf16)
    acc = None
    for s, n in chunks:
        a = _dot(h, w1_ref[:, s:s + n])
        b = _dot(h, w3_ref[:, s:s + n])
        g = (a * jax.nn.sigmoid(a) * b).astype(bf16)
        y = _dot(g, w2_ref[s:s + n, :])
        acc = y if acc is None else acc + y
    out = _post_residual(x, acc, gout_ref[...], mod_ref, j, 0.5)
    o_ref[0] = out
    if emit_h:
        h_ref[0] = _pre_norm(out, gh_ref[...], mod_ref, 1).astype(h_ref.dtype)


def _ff_chunks(dff, step=FFN_COL_STEP):
    out, s = [], 0
    while s < dff:
        n = min(step, dff - s)
        out.append((s, n))
        s += n
    return tuple(out)


def _half_ffn(x, mod, g_in, g_out, w1, w3, w2, j, mix=None, mix_args=(), g_h=None):
    b, t, d = x.shape
    dff = w1.shape[1]
    tm = _tok_tile(t)
    tok = pl.BlockSpec((1, tm, d), lambda bb, tt: (bb, tt, 0))
    vec = _full((1, d))
    args = [x, mod, g_in.reshape(1, d), g_out.reshape(1, d), w1, w3, w2]
    specs = [tok, _mod_spec(mod), vec, vec, _full((d, dff)), _full((d, dff)), _full((dff, d))]
    if mix == "outproj":
        y, g_mix, w_o = mix_args
        args += [y, g_mix.reshape(1, d), w_o]
        specs += [tok, vec, _full((d, d))]
    elif mix == "glu":
        y, g_pre, d_skip, g_mix, w_g, b_g = mix_args
        args += [y, g_pre.reshape(1, d), d_skip.astype(f32).reshape(1, d), g_mix.reshape(1, d), w_g, b_g.reshape(1, d)]
        specs += [tok, vec, vec, vec, _full((d, d)), vec]
    out_specs, out_shape = tok, jax.ShapeDtypeStruct((b, t, d), f32)
    if g_h is not None:
        args.append(g_h.reshape(1, d))
        specs.append(vec)
        out_specs, out_shape = [tok, tok], [out_shape, jax.ShapeDtypeStruct((b, t, d), bf16)]
    return pl.pallas_call(
        functools.partial(_ffn_kernel, j=j, chunks=_ff_chunks(dff), mix=mix, emit_h=g_h is not None),
        grid=(b, t // tm),
        in_specs=specs,
        out_specs=out_specs,
        out_shape=out_shape,
        compiler_params=_cparams("parallel", "parallel"),
        name="half_ffn",
    )(*args)


def _swap_pairs(x):
    lane = lax.broadcasted_iota(jnp.int32, x.shape, 1)
    nxt = pltpu.roll(x, LANES - 1, 1)
    prv = pltpu.roll(x, 1, 1)
    return jnp.where((lane & 1) == 0, nxt, prv)


def _proj_kernel(*refs, j, n_out, n_rope, has_bias, step):
    x_ref, mod_ref, g_ref, w_ref = refs[:4]
    k = 4
    b_ref = None
    if has_bias:
        b_ref = refs[k]
        k += 1
    if n_rope:
        cos_ref, sin_ref = refs[k], refs[k + 1]
        k += 2
    o_ref = refs[k]
    h = _pre_norm(x_ref[0], g_ref[...], mod_ref, j).astype(bf16)
    for s in range(0, n_out, step):
        y = _dot(h, w_ref[:, s:s + step])
        if has_bias:
            y = y + b_ref[:, s:s + step]
        if s < n_rope:
            cos = cos_ref[...]
            sin = sin_ref[...]
            parts = []
            for c in range(0, step, LANES):
                yc = y[:, c:c + LANES]
                parts.append(yc * cos + _swap_pairs(yc) * sin)
            y = jnp.concatenate(parts, axis=1)
        o_ref[0, :, s:s + step] = y.astype(o_ref.dtype)


def _prenorm_proj(x, mod, g, j, w, bias=None, rope=None, n_rope=0, out_dtype=bf16):
    b, t, d = x.shape
    n = w.shape[1]
    tm = min(_tok_tile(t), PROJ_TOKEN_TILE)
    step = PROJ_COL_STEP
    assert n % step == 0 and n_rope % step == 0
    args = [x, mod, g.reshape(1, d), w]
    specs = [pl.BlockSpec((1, tm, d), lambda bb, tt: (bb, tt, 0)), _mod_spec(mod), _full((1, d)), _full((d, n))]
    if bias is not None:
        args.append(bias.reshape(1, n))
        specs.append(_full((1, n)))
    if n_rope:
        args += list(rope)
        specs += [pl.BlockSpec((tm, LANES), lambda bb, tt: (tt, 0))] * 2
    return pl.pallas_call(
        functools.partial(_proj_kernel, j=j, n_out=n, n_rope=n_rope, has_bias=bias is not None, step=step),
        grid=(b, t // tm),
        in_specs=specs,
        out_specs=pl.BlockSpec((1, tm, n), lambda bb, tt: (bb, tt, 0)),
        out_shape=jax.ShapeDtypeStruct((b, t, n), out_dtype),
        compiler_params=_cparams("parallel", "parallel"),
        name="prenorm_proj",
    )(*args)


def _s5_kernel(xc_ref, xl_ref, wb_ref, wk_ref, wc_ref, lam_ref, yc_ref, yl_ref, s_ref, hin_ref, *, nb, nc_ctx, nc):
    rc = nc_ctx * nb
    xc, xl = xc_ref[0], xl_ref[0]
    s_ref[0:rc, :] = _dot(xc, wb_ref[0])
    s_ref[rc:, :] = _dot(xl, wb_ref[0])
    P = S5_STATE
    lam = lam_ref[0]
    a_re, a_im = lam[0:1], lam[1:2]
    fwd_lane = lax.broadcasted_iota(jnp.int32, (nb, 2 * P), 1) < P

    def step(n, carry):
        h_re, h_im = carry
        r_f = pl.multiple_of(n * nb, nb)
        n_b = jnp.where(n < nc_ctx, nc_ctx - 1 - n, nc - 1 - (n - nc_ctx))
        r_b = pl.multiple_of(n_b * nb, nb)
        rows_f, rows_b = pl.ds(r_f, nb), pl.ds(r_b, nb)
        hin_ref[rows_f, 0:P] = h_re[:, 0:P]
        hin_ref[rows_b, P:2 * P] = h_re[:, P:2 * P]
        hin_ref[rows_f, 2 * P:3 * P] = h_im[:, 0:P]
        hin_ref[rows_b, 3 * P:4 * P] = h_im[:, P:2 * P]
        s_re = jnp.where(fwd_lane, s_ref[rows_f, 0:2 * P], s_ref[rows_b, 0:2 * P])
        s_im = jnp.where(fwd_lane, s_ref[rows_f, 2 * P:4 * P], s_ref[rows_b, 2 * P:4 * P])
        return a_re * h_re - a_im * h_im + s_re, a_re * h_im + a_im * h_re + s_im

    zero = jnp.zeros((nb, 2 * P), f32)
    lax.fori_loop(0, nc, step, (zero, zero), unroll=4 if nc % 4 == 0 else 1)
    yc_ref[0] = _dot(xc, wk_ref[0]) + _dot(hin_ref[0:rc, :].astype(bf16), wc_ref[0])
    yl_ref[0] = _dot(xl, wk_ref[0]) + _dot(hin_ref[rc:, :].astype(bf16), wc_ref[0])

def _s5_weights(a_re, a_im, log_dt, b_re, b_im, c_re, c_im):
    hp = lax.Precision.HIGHEST
    C = S5_CHUNK
    G, P = a_re.shape[1:]
    N = b_re.shape[-1]
    j = jnp.arange(C + 1, dtype=f32)
    wbs, wks, wcs, lams = [], [], [], []
    for dirn in range(2):
        are, aim = a_re[dirn].astype(f32), a_im[dirn].astype(f32)
        dt = jnp.exp(log_dt[dirn].astype(f32))[:, None]
        pw_mag = jnp.exp(j[:, None, None] * (are * dt)[None])
        pw_re = pw_mag * jnp.cos(j[:, None, None] * (aim * dt)[None])
        pw_im = pw_mag * jnp.sin(j[:, None, None] * (aim * dt)[None])
        nr, ni = pw_re[1] - 1.0, pw_im[1]
        den = are * are + aim * aim
        fr = (nr * are + ni * aim) / den
        fi = (ni * are - nr * aim) / den
        bre, bim = b_re[dirn].astype(f32), b_im[dirn].astype(f32)
        bbr = fr[..., None] * bre - fi[..., None] * bim
        bbi = fr[..., None] * bim + fi[..., None] * bre
        cre, cim = c_re[dirn].astype(f32), c_im[dirn].astype(f32)
        dr, di = (pw_re[:C][::-1], pw_im[:C][::-1]) if dirn == 0 else (pw_re[:C], pw_im[:C])
        sr = dr[..., None] * bbr[None] - di[..., None] * bbi[None]
        si = dr[..., None] * bbi[None] + di[..., None] * bbr[None]
        wb = jnp.concatenate([sr, si], axis=2)
        wbs.append(wb.transpose(1, 0, 3, 2).reshape(G, C * N, 2 * P))
        pr, pi = pw_re[1:], pw_im[1:]
        if dirn == 1:
            pr, pi = pr[::-1], pi[::-1]
        or_ = cre[None] * pr[:, :, None, :] - cim[None] * pi[:, :, None, :]
        oi_ = -(cre[None] * pi[:, :, None, :] + cim[None] * pr[:, :, None, :])
        wc = jnp.concatenate([or_, oi_], axis=3)
        wcs.append(wc.transpose(1, 3, 0, 2).reshape(G, 2 * P, C * N))
        clr = cre[None] * pw_re[:C, :, None, :] - cim[None] * pw_im[:C, :, None, :]
        cli = cre[None] * pw_im[:C, :, None, :] + cim[None] * pw_re[:C, :, None, :]
        kk = (jnp.einsum('jgnp,gpm->gjnm', clr, bbr, precision=hp)
              - jnp.einsum('jgnp,gpm->gjnm', cli, bbi, precision=hp))
        s_idx = jnp.arange(C)[:, None]
        t_idx = jnp.arange(C)[None, :]
        lag = (t_idx - s_idx) if dirn == 0 else (s_idx - t_idx)
        onehot = (lag[:, :, None] == jnp.arange(C)[None, None, :]).astype(f32)
        kt = jnp.einsum('stj,gjnm->gsmtn', onehot, kk, precision=hp)
        wks.append(kt.reshape(G, C * N, C * N))
        lams.append((pw_re[C], pw_im[C]))
    wb = jnp.concatenate([wbs[0][..., :P], wbs[1][..., :P], wbs[0][..., P:], wbs[1][..., P:]], axis=2).astype(bf16)
    wc = jnp.concatenate([wcs[0][:, :P], wcs[1][:, :P], wcs[0][:, P:], wcs[1][:, P:]], axis=1).astype(bf16)
    wk = (wks[0] + wks[1]).astype(bf16)
    lam = jnp.stack([jnp.concatenate([lams[0][0], lams[1][0]], axis=-1),
                     jnp.concatenate([lams[0][1], lams[1][1]], axis=-1)], axis=1)
    return wb, wk, wc, lam


def _s5_scan(h_lat, h_ctx, a_re, a_im, log_dt, b_re, b_im, c_re, c_im):
    B, S, D = h_lat.shape
    L = h_ctx.shape[1]
    C, N, P = S5_CHUNK, S5_GROUP, S5_STATE
    G = D // N
    nc_ctx, nc = L // C, (L + S) // C
    assert 2 * P == LANES and L % C == 0 and S % C == 0
    wb, wk, wc, lam = _s5_weights(a_re, a_im, log_dt, b_re, b_im, c_re, c_im)

    def to_groups(h):
        n = h.shape[1] // C
        return h.reshape(B, n, C, G, N).transpose(3, 1, 0, 2, 4).reshape(G, n * B, C * N)

    def from_groups(y):
        n = y.shape[1] // B
        return y.reshape(G, n, B, C, N).transpose(2, 1, 3, 0, 4).reshape(B, n * C, D)

    rc, rl = nc_ctx * B, (nc - nc_ctx) * B

    def rows(r):
        return pl.BlockSpec((1, r, C * N), lambda g: (g, 0, 0))

    yc, yl = pl.pallas_call(
        functools.partial(_s5_kernel, nb=B, nc_ctx=nc_ctx, nc=nc),
        grid=(G,),
        in_specs=[rows(rc), rows(rl),
                  pl.BlockSpec((1, C * N, 4 * P), lambda g: (g, 0, 0)),
                  pl.BlockSpec((1, C * N, C * N), lambda g: (g, 0, 0)),
                  pl.BlockSpec((1, 4 * P, C * N), lambda g: (g, 0, 0)),
                  pl.BlockSpec((1, 2, 2 * P), lambda g: (g, 0, 0))],
        out_specs=[rows(rc), rows(rl)],
        out_shape=[jax.ShapeDtypeStruct((G, rc, C * N), f32), jax.ShapeDtypeStruct((G, rl, C * N), f32)],
        scratch_shapes=[pltpu.VMEM((rc + rl, 4 * P), f32), pltpu.VMEM((rc + rl, 4 * P), f32)],
        compiler_params=_cparams("parallel"),
        name="s5_scan",
    )(to_groups(h_ctx), to_groups(h_lat), wb, wk, wc, lam)
    return from_groups(yl), from_groups(yc)


def _diff_maps(q):
    lane = lax.broadcasted_iota(jnp.int32, q.shape, 1)
    qs = (q.astype(f32) * (DIFF_HEAD_DIM ** -0.5 * math.log2(math.e))).astype(bf16)
    zero = jnp.zeros_like(qs)
    return jnp.where(lane < DIFF_HEAD_DIM, qs, zero), jnp.where(lane >= DIFF_HEAD_DIM, qs, zero)


def _key_chunks(k_refs):
    out = []
    for seg, k_ref in enumerate(k_refs):
        n = k_ref.shape[1]
        out += [(seg, c, min(c + DIFF_KEY_CHUNK, n)) for c in range(0, n, DIFF_KEY_CHUNK)]
    return out


def _diff_step(q_next, cur, nxt, k_refs, v_refs, lamv_ref, g_ref, o_ref, lam_init):
    chunks = _key_chunks(k_refs)
    cur_s, cur_m = cur
    nxt_s, nxt_m = nxt
    tq = cur_m.shape[1]
    m1, m2 = cur_m[0, :, 0:1], cur_m[1, :, 0:1]
    q1n, q2n = _diff_maps(q_next)
    n1 = n2 = l1 = l2 = None
    for seg, a, b in chunks:
        s1 = _dot_nt(q1n, k_refs[seg][0, a:b, :])
        s2 = _dot_nt(q2n, k_refs[seg][0, a:b, :])
        nxt_s[seg][0, :, a:b] = s1
        nxt_s[seg][1, :, a:b] = s2
        c1 = jnp.max(s1, axis=-1, keepdims=True)
        c2 = jnp.max(s2, axis=-1, keepdims=True)
        n1 = c1 if n1 is None else jnp.maximum(n1, c1)
        n2 = c2 if n2 is None else jnp.maximum(n2, c2)
        p1 = jnp.exp2(cur_s[seg][0, :, a:b] - m1)
        p2 = jnp.exp2(cur_s[seg][1, :, a:b] - m2)
        cur_s[seg][0, :, a:b] = p1
        cur_s[seg][1, :, a:b] = p2
        r1 = jnp.sum(p1, axis=-1, keepdims=True)
        r2 = jnp.sum(p2, axis=-1, keepdims=True)
        l1 = r1 if l1 is None else l1 + r1
        l2 = r2 if l2 is None else l2 + r2
    nxt_m[0] = jnp.broadcast_to(n1, (tq, LANES))
    nxt_m[1] = jnp.broadcast_to(n2, (tq, LANES))
    lv = lamv_ref[...]
    lam = (jnp.exp(jnp.sum(lv[0:1] * lv[1:2], axis=-1, keepdims=True))
           - jnp.exp(jnp.sum(lv[2:3] * lv[3:4], axis=-1, keepdims=True)) + lam_init)
    c = lam * l1 / l2
    o = None
    for seg, a, b in chunks:
        w = (cur_s[seg][0, :, a:b] - c * cur_s[seg][1, :, a:b]).astype(bf16)
        part = _dot(w, v_refs[seg][0, a:b, :])
        o = part if o is None else o + part
    o = o / l1
    o_ref[0] = (_rms(o, g_ref[...]) * (1 - lam_init)).astype(o_ref.dtype)


def _diff_kernel(*refs, n_seg, lam_init):
    q_ref, qn_ref = refs[:2]
    k_refs = refs[2:2 + 3 * n_seg:3]
    v_refs = refs[3:3 + 3 * n_seg:3]
    kn_refs = refs[4:4 + 3 * n_seg:3]
    lamv_ref, g_ref, o_ref = refs[2 + 3 * n_seg:5 + 3 * n_seg]
    scratch = refs[5 + 3 * n_seg:]
    slots = tuple((scratch[p * (n_seg + 1):p * (n_seg + 1) + n_seg], scratch[p * (n_seg + 1) + n_seg])
                  for p in range(2))
    unit = (pl.program_id(0) * pl.num_programs(1) + pl.program_id(1)) * pl.num_programs(2) + pl.program_id(2)

    @pl.when(unit == 0)
    def _():
        q1, q2 = _diff_maps(q_ref[0])
        s_refs, m_ref = slots[0]
        for mp, qm in enumerate((q1, q2)):
            parts = [_dot_nt(qm, k_ref[0]) for k_ref in k_refs]
            for s_ref, sc in zip(s_refs, parts):
                s_ref[mp] = sc
            mx = functools.reduce(jnp.maximum, [jnp.max(sc, axis=-1, keepdims=True) for sc in parts])
            m_ref[mp] = jnp.broadcast_to(mx, m_ref.shape[1:])

    for parity in range(2):
        @pl.when((unit & 1) == parity)
        def _(parity=parity):
            _diff_step(qn_ref[0], slots[parity], slots[1 - parity], kn_refs, v_refs, lamv_ref, g_ref, o_ref,
                       lam_init)


def _diff_attention(qkv_q, qkv_ctx, qkv_lat, lamv, subln, lam_init):
    B, T, n3 = qkv_q.shape
    D = n3 // 3
    H = D // LANES
    tq = min(_tok_tile(T), ATTN_QUERY_TILE)
    nq = T // tq
    kv = [qkv_ctx] if qkv_lat is None else [qkv_ctx, qkv_lat]

    def following(b, h, t):
        u = jnp.minimum((b * H + h) * nq + t + 1, B * H * nq - 1)
        return u // (H * nq), (u // nq) % H, u % nq

    def q_next(b, h, t):
        nb, nh, nt = following(b, h, t)
        return nb, nt, nh

    def k_next(b, h, t):
        nb, nh, _ = following(b, h, t)
        return nb, 0, H + nh

    args = [qkv_q, qkv_q]
    specs = [pl.BlockSpec((1, tq, LANES), lambda b, h, t: (b, t, h)), pl.BlockSpec((1, tq, LANES), q_next)]
    for a in kv:
        args += [a, a, a]
        specs += [pl.BlockSpec((1, a.shape[1], LANES), lambda b, h, t: (b, 0, H + h)),
                  pl.BlockSpec((1, a.shape[1], LANES), lambda b, h, t: (b, 0, 2 * H + h)),
                  pl.BlockSpec((1, a.shape[1], LANES), k_next)]
    args += [lamv, subln.reshape(1, LANES)]
    specs += [_full(lamv.shape), _full((1, LANES))]
    slot = [pltpu.VMEM((2, tq, a.shape[1]), f32) for a in kv] + [pltpu.VMEM((2, tq, LANES), f32)]
    return pl.pallas_call(
        functools.partial(_diff_kernel, n_seg=len(kv), lam_init=lam_init),
        grid=(B, H, nq),
        in_specs=specs,
        out_specs=pl.BlockSpec((1, tq, LANES), lambda b, h, t: (b, t, h)),
        out_shape=jax.ShapeDtypeStruct((B, T, D), bf16),
        scratch_shapes=slot + slot,
        compiler_params=_cparams("arbitrary", "arbitrary", "arbitrary"),
        name="diff_attention",
    )(*args)


def _rope_tables(n_tokens, head_dim):
    n_freq = head_dim // 4
    inv_freq = ROPE_BASE ** (-jnp.arange(n_freq, dtype=f32) / n_freq)
    t = jnp.arange(n_tokens)
    row = (t // GRID_W).astype(f32)
    col = (t % GRID_W).astype(f32)
    ang = jnp.concatenate([row[:, None] * inv_freq, col[:, None] * inv_freq], axis=-1)
    cos = jnp.repeat(jnp.cos(ang), 2, axis=-1)
    sin = jnp.repeat(jnp.sin(ang), 2, axis=-1) * jnp.tile(jnp.array([-1.0, 1.0], f32), head_dim // 2)
    reps = LANES // head_dim
    return jnp.tile(cos, (1, reps)), jnp.tile(sin, (1, reps))


def _na_head(q, hh):
    lane = lax.broadcasted_iota(jnp.int32, q.shape, 1)
    qs = (q.astype(f32) * (NA_HEAD_DIM ** -0.5 * math.log2(math.e))).astype(bf16)
    sel = (lane < NA_HEAD_DIM) if hh == 0 else (lane >= NA_HEAD_DIM)
    return jnp.where(sel, qs, jnp.zeros_like(qs))


def _na_ctx_kernel(q_ref, kc_ref, vc_ref, o_ref):
    q = q_ref[0]
    lane = lax.broadcasted_iota(jnp.int32, q.shape, 1)
    outs = []
    for hh in range(2):
        s = _dot_nt(_na_head(q, hh), kc_ref[0])
        p = jnp.exp2(s - jnp.max(s, axis=-1, keepdims=True))
        outs.append(_dot(p.astype(bf16), vc_ref[0]) / jnp.sum(p, axis=-1, keepdims=True))
    o_ref[0] = jnp.where(lane < NA_HEAD_DIM, outs[0], outs[1]).astype(o_ref.dtype)


def _na_window_table(rows):
    table = []
    for t in range(rows // NA_TILE_ROWS):
        w0 = min(max(t * NA_TILE_ROWS - (NA_WIN_ROWS - NA_TILE_ROWS) // 2, 0), rows - NA_WIN_ROWS)
        line = [w0 * GRID_W]
        for i in range(NA_TILE_ROWS):
            qr = t * NA_TILE_ROWS + i
            r0 = min(max(qr - WIN_H // 2, 0), rows - WIN_H)
            for j in range(NA_WIN_ROWS // 2):
                kr = w0 + 2 * j
                e = kr - qr + WIN_H
                ok0 = r0 <= kr < r0 + WIN_H
                ok1 = r0 <= kr + 1 < r0 + WIN_H
                line.append(e if ok0 and ok1 else 2 * WIN_H - 1 + e if ok0 else 4 * WIN_H - 1 + e if ok1 else 0)
        table.append(line)
    return jnp.array(table, jnp.int32)


def _na_window(win_ref, t):
    n_pairs = NA_WIN_ROWS // 2
    entry = [[win_ref[t, 1 + i * n_pairs + j] for j in range(n_pairs)] for i in range(NA_TILE_ROWS)]
    return pl.multiple_of(win_ref[t, 0], 4 * GRID_W), entry


def _lane_fold(x, op):
    return functools.reduce(op, [x[:, c:c + LANES] for c in range(0, x.shape[1], LANES)])


def _na_biased(sl, hh, entry, j0, tb_ref, s_ref, c0):
    tiles = []
    for i in range(NA_TILE_ROWS):
        r = slice(i * GRID_W, (i + 1) * GRID_W)
        mx = None
        for jj in range(sl.shape[1] // LANES):
            blk = sl[r, jj * LANES:(jj + 1) * LANES] + tb_ref[0, hh, entry[i][j0 + jj]]
            s_ref[r, c0 + jj * LANES:c0 + (jj + 1) * LANES] = blk
            mx = blk if mx is None else jnp.maximum(mx, blk)
        tiles.append(mx)
    return jnp.concatenate(tiles, axis=0)


def _na_unit(make, use, kc_ref, vc_ref, kl_ref, vl_ref, tb_ref):
    q, hh_m, tok_m, entry, (mc_ref, ml_ref, mm_ref) = make
    hh_u, tok_u, (uc_ref, ul_ref, um_ref) = use
    qh = _na_head(q, hh_m)
    m = um_ref[:, 0:1]
    run_max = l = o = None
    for c0 in [None] + list(range(0, NA_WIN_ROWS * GRID_W, NA_KEY_CHUNK)):
        if c0 is None:
            sc = _dot_nt(qh, kc_ref[0])
            mc_ref[...] = sc
            run_max = _lane_fold(sc, jnp.maximum)
            p = jnp.exp2(uc_ref[...] - m)
            v = vc_ref[0]
        else:
            sl = _dot_nt(qh, kl_ref[0, pl.ds(tok_m + c0, NA_KEY_CHUNK), :])
            run_max = jnp.maximum(run_max, _na_biased(sl, hh_m, entry, c0 // LANES, tb_ref, ml_ref, c0))
            p = jnp.exp2(ul_ref[:, c0:c0 + NA_KEY_CHUNK] - m)
            v = vl_ref[0, pl.ds(tok_u + c0, NA_KEY_CHUNK), :]
        r = _lane_fold(p, jnp.add)
        part = _dot(p.astype(bf16), v)
        l = r if l is None else l + r
        o = part if o is None else o + part
    mm_ref[...] = jnp.broadcast_to(jnp.max(run_max, axis=-1, keepdims=True), mm_ref.shape)
    return o / jnp.sum(l, axis=-1, keepdims=True)


def _na_kernel(win_ref, q_ref, qn_ref, kc_ref, vc_ref, kl_ref, vl_ref, tb_ref, o_ref, *scratch, n_tiles, tps):
    slots = (scratch[:3], scratch[3:])
    t = pl.program_id(2)
    tq = qn_ref.shape[1]
    tile_ids = [t * tps + k for k in range(tps)] + [jnp.minimum((t + 1) * tps, n_tiles - 1)]
    windows = [_na_window(win_ref, i) for i in tile_ids]
    qs = [q_ref[0, k * tq:(k + 1) * tq, :] for k in range(tps)] + [qn_ref[0]]
    shared = (kc_ref, vc_ref, kl_ref, vl_ref, tb_ref)

    @pl.when(t == 0)
    def _():
        tok, entry = windows[0]
        qh = _na_head(qs[0], 0)
        sc = _dot_nt(qh, kc_ref[0])
        slots[0][0][...] = sc
        sl = _dot_nt(qh, kl_ref[0, pl.ds(tok, NA_WIN_ROWS * GRID_W), :])
        run_max = jnp.maximum(_lane_fold(sc, jnp.maximum), _na_biased(sl, 0, entry, 0, tb_ref, slots[0][1], 0))
        slots[0][2][...] = jnp.broadcast_to(jnp.max(run_max, axis=-1, keepdims=True), slots[0][2].shape)

    units = [(k, hh) for k in range(tps) for hh in range(2)] + [(tps, 0)]
    lane = lax.broadcasted_iota(jnp.int32, (tq, LANES), 1)
    outs = {}
    for (k, hh), (nk, nhh) in zip(units[:-1], units[1:]):
        make = (qs[nk], nhh, windows[nk][0], windows[nk][1], slots[nhh])
        outs[k, hh] = _na_unit(make, (hh, windows[k][0], slots[hh]), *shared)
        if hh == 1:
            o_ref[0, k * tq:(k + 1) * tq, :] = jnp.where(lane < NA_HEAD_DIM, outs[k, 0], outs[k, 1]).astype(o_ref.dtype)


def _na_bias_blocks(rpb):
    H, n_dr, n_dc = rpb.shape
    col = jnp.arange(GRID_W)
    c0 = jnp.clip(col - WIN_W // 2, 0, GRID_W - WIN_W)
    col_ok = (col[None, :] >= c0[:, None]) & (col[None, :] < c0[:, None] + WIN_W)
    dc = jnp.clip(col[None, :] - col[:, None] + WIN_W - 1, 0, n_dc - 1)
    onehot = (dc.reshape(1, -1) == jnp.arange(n_dc)[:, None]).astype(f32)
    blk = jnp.einsum('hrj,jx->hrx', rpb.astype(f32) * math.log2(math.e), onehot, precision=lax.Precision.HIGHEST)
    blk = jnp.where(col_ok[None, None], blk.reshape(H, n_dr, GRID_W, GRID_W), NEG_INF)
    blk = jnp.concatenate([jnp.full((H, 1, GRID_W, GRID_W), NEG_INF, f32), blk], axis=1)
    n_e = 2 * WIN_H - 1
    none = blk[:, 0:1]
    none_run = jnp.broadcast_to(none, (H, n_e, GRID_W, GRID_W))
    left = jnp.concatenate([none, blk[:, 1:n_e], none, blk[:, 1:n_e + 1], none_run], axis=1)
    right = jnp.concatenate([none, blk[:, 2:n_e + 1], none, none_run, blk[:, 1:n_e + 1]], axis=1)
    tab = jnp.concatenate([left, right], axis=-1)
    return tab.reshape((H // 2, 2) + tab.shape[1:])


def _na_attention(qkv_q, qkv_ctx, qkv_lat, bias):
    B, T, n3 = qkv_q.shape
    D = n3 // 3
    HP = D // LANES
    L = qkv_ctx.shape[1]
    ctx_k = pl.BlockSpec((1, L, LANES), lambda b, h, t: (b, 0, HP + h))
    ctx_v = pl.BlockSpec((1, L, LANES), lambda b, h, t: (b, 0, 2 * HP + h))
    if qkv_lat is None:
        tq = _tok_tile(T)
        tile = pl.BlockSpec((1, tq, LANES), lambda b, h, t: (b, t, h))
        return pl.pallas_call(
            _na_ctx_kernel,
            grid=(B, HP, T // tq),
            in_specs=[tile, ctx_k, ctx_v],
            out_specs=tile,
            out_shape=jax.ShapeDtypeStruct((B, T, D), bf16),
            compiler_params=_cparams("parallel", "parallel", "parallel"),
            name="na_attention_ctx",
        )(qkv_q, qkv_ctx, qkv_ctx)
    tq = NA_TILE_ROWS * GRID_W
    n_tiles = T // tq
    n_win = NA_WIN_ROWS * GRID_W
    assert T % tq == 0 and T // GRID_W >= NA_WIN_ROWS and bias.shape[2] == 6 * WIN_H - 2
    tps = max(d for d in range(1, NA_TILES_PER_STEP + 1) if n_tiles % d == 0)
    tile = pl.BlockSpec((1, tps * tq, LANES), lambda b, h, t: (b, t, h))
    slot = [pltpu.VMEM((tq, L), f32), pltpu.VMEM((tq, n_win), f32), pltpu.VMEM((tq, LANES), f32)]
    return pl.pallas_call(
        functools.partial(_na_kernel, n_tiles=n_tiles, tps=tps),
        grid=(B, HP, n_tiles // tps),
        in_specs=[pl.BlockSpec(memory_space=pltpu.SMEM), tile,
                  pl.BlockSpec((1, tq, LANES), lambda b, h, t: (b, jnp.minimum((t + 1) * tps, n_tiles - 1), h)),
                  ctx_k, ctx_v,
                  pl.BlockSpec((1, T, LANES), lambda b, h, t: (b, 0, HP + h)),
                  pl.BlockSpec((1, T, LANES), lambda b, h, t: (b, 0, 2 * HP + h)),
                  pl.BlockSpec((1,) + bias.shape[1:], lambda b, h, t: (h, 0, 0, 0, 0))],
        out_specs=tile,
        out_shape=jax.ShapeDtypeStruct((B, T, D), bf16),
        scratch_shapes=slot + slot,
        compiler_params=_cparams("parallel", "parallel", "arbitrary"),
        name="na_attention",
    )(_na_window_table(T // GRID_W), qkv_q, qkv_q, qkv_ctx, qkv_ctx, qkv_lat, qkv_lat, bias)


def _chunk_cumsum(x, reverse):
    n = x.shape[0]
    pos = lax.broadcasted_iota(jnp.int32, x.shape, 0) & (HG_CHUNK - 1)
    k = 1
    while k < HG_CHUNK:
        if reverse:
            x = x + jnp.where(pos < HG_CHUNK - k, pltpu.roll(x, n - k, 0), 0.0)
        else:
            x = x + jnp.where(pos >= k, pltpu.roll(x, k, 0), 0.0)
        k *= 2
    return x


def _hg_mask(n, reverse):
    ri = lax.broadcasted_iota(jnp.int32, (n, n), 0)
    ci = lax.broadcasted_iota(jnp.int32, (n, n), 1)
    shift = HG_CHUNK.bit_length() - 1
    return ((ri >> shift) == (ci >> shift)) & ((ci >= ri) if reverse else (ci <= ri))


def _hg_block(z, v, q, lb, st, reverse, tri):
    with_out = q is not None
    n = z.shape[0]
    nch = n // HG_CHUNK
    f = lb + (1 - lb) * jax.nn.sigmoid(z)
    lf = jnp.log(f)
    kk = 1 - f
    g = _chunk_cumsum(lf, reverse)
    out = None
    if with_out:
        q_dec = q * jnp.exp(g)
        k_inv = kk * jnp.exp(-g)
        att = jnp.where(tri, _dot_nt(q_dec.astype(bf16), k_inv.astype(bf16)), 0.0)
        out_intra = _dot(att.astype(bf16), v.astype(bf16))
        outs = [None] * nch
    order = range(nch - 1, -1, -1) if reverse else range(nch)
    for c in order:
        sl = slice(c * HG_CHUNK, (c + 1) * HG_CHUNK)
        end = c * HG_CHUNK if reverse else (c + 1) * HG_CHUNK - 1
        g_tot = g[end:end + 1]
        if with_out:
            outs[c] = _dot_nt(q_dec[sl].astype(bf16), st.astype(bf16))
        kdec = kk[sl] * jnp.exp(g_tot - g[sl])
        ds_t = _dot(v[sl].T.astype(bf16), kdec.astype(bf16))
        st = st * jnp.exp(g_tot) + ds_t
    if with_out:
        out = out_intra + jnp.concatenate(outs, axis=0)
    return out, st


def _hg_kernel(q_ref, i_ref, gate_ref, zf_ref, zb_ref, ic_ref, zfc_ref, zbc_ref, lb_ref, gn_ref, o_ref,
               accf_ref, accb_ref, *, s_tok):
    lb = lb_ref[...]
    nblk = s_tok // HG_BLOCK
    zero = jnp.zeros((HG_EXPAND, HG_EXPAND), f32)
    _, st_f = _hg_block(zfc_ref[0], ic_ref[0], None, lb, zero, False, None)
    _, st_b = _hg_block(zbc_ref[0], ic_ref[0], None, lb, zero, True, None)
    tri_f, tri_b = _hg_mask(HG_BLOCK, False), _hg_mask(HG_BLOCK, True)

    def body(n, carry):
        st_f, st_b = carry
        rows_f = pl.ds(pl.multiple_of(n * HG_BLOCK, HG_BLOCK), HG_BLOCK)
        rows_b = pl.ds(pl.multiple_of((nblk - 1 - n) * HG_BLOCK, HG_BLOCK), HG_BLOCK)
        out_f, st_f = _hg_block(zf_ref[0, rows_f, :], i_ref[0, rows_f, :], q_ref[0, rows_f, :], lb, st_f, False, tri_f)
        out_b, st_b = _hg_block(zb_ref[0, rows_b, :], i_ref[0, rows_b, :], q_ref[0, rows_b, :], lb, st_b, True, tri_b)
        accf_ref[rows_f, :] = out_f
        accb_ref[rows_b, :] = out_b
        return st_f, st_b

    lax.fori_loop(0, nblk, body, (st_f, st_b), unroll=16 if nblk % 16 == 0 else 1)
    gate = gate_ref[0]
    o = accf_ref[...] + accb_ref[...]
    o_ref[0] = (_rms(o, gn_ref[...]) * (gate * jax.nn.sigmoid(gate))).astype(o_ref.dtype)


def _hgrn2_core(p_lat, p_ctx, lb, gn_g):
    B, S, n5 = p_lat.shape
    D = n5 // 5
    H = D // HG_EXPAND
    L = p_ctx.shape[1]
    assert S % HG_BLOCK == 0 and L % HG_CHUNK == 0

    def lat(k):
        return pl.BlockSpec((1, S, HG_EXPAND), lambda b, h: (b, 0, k * H + h))

    def ctx(k):
        return pl.BlockSpec((1, L, HG_EXPAND), lambda b, h: (b, 0, k * H + h))

    vec = pl.BlockSpec((1, HG_EXPAND), lambda b, h: (0, h))
    return pl.pallas_call(
        functools.partial(_hg_kernel, s_tok=S),
        grid=(B, H),
        in_specs=[lat(0), lat(1), lat(2), lat(3), lat(4), ctx(1), ctx(3), ctx(4), vec, _full((1, HG_EXPAND))],
        out_specs=pl.BlockSpec((1, S, HG_EXPAND), lambda b, h: (b, 0, h)),
        out_shape=jax.ShapeDtypeStruct((B, S, D), bf16),
        scratch_shapes=[pltpu.VMEM((S, HG_EXPAND), f32), pltpu.VMEM((S, HG_EXPAND), f32)],
        compiler_params=_cparams("parallel", "parallel"),
        name="hgrn2_core",
    )(p_lat, p_lat, p_lat, p_lat, p_lat, p_ctx, p_ctx, p_ctx, lb.reshape(1, D), gn_g.reshape(1, HG_EXPAND))


def kernel(x, c, ctx, c_ctx, w_ada, b_ada, g_pre, g_post, w_ff1, w_ff3, w_ff2, s5_a_re, s5_a_im, s5_log_dt, s5_b_re, s5_b_im, s5_c_re, s5_c_im, s5_d, s5_w_glu, s5_b_glu, da_w_qkv, da_w_o, da_lam_q1, da_lam_k1, da_lam_q2, da_lam_k2, da_subln, na_w_qkv, na_w_o, na_rpb, hg_w_qig, hg_w_f, hg_b_f, hg_lb_logits, hg_gnorm, hg_w_o):
    B, S, D = x.shape
    depth = w_ada.shape[0]
    n_mix = 4
    rows_pad = -(-(B + 1) // SUBLANES) * SUBLANES
    c_all = jnp.concatenate([c, c_ctx[None], jnp.zeros((rows_pad - B - 1, D), f32)], axis=0)
    mods = _ada(c_all, w_ada, b_ada)
    lb_p = jax.nn.softmax(hg_lb_logits.astype(f32), axis=0)
    lower_bounds = jnp.cumsum(lb_p, axis=0) - lb_p[0]
    w1b, w3b, w2b = w_ff1.astype(bf16), w_ff3.astype(bf16), w_ff2.astype(bf16)

    x_lat, x_ctx = x, ctx
    for i in range(depth):
        last = i == depth - 1
        occ, kind = i // n_mix, i % n_mix
        m_lat = mods[i, :B].reshape(B, 3 * N_SUB, D)
        m_ctx = mods[i, B:B + 1].reshape(1, 3 * N_SUB, D)
        ffn1 = functools.partial(_half_ffn, g_in=g_pre[i, 0], g_out=g_post[i, 0], w1=w1b[i, 0], w3=w3b[i, 0],
                                 w2=w2b[i, 0], j=0)
        ffn2 = functools.partial(_half_ffn, g_in=g_pre[i, 2], g_out=g_post[i, 2], w1=w1b[i, 1], w3=w3b[i, 1],
                                 w2=w2b[i, 1], j=2)
        if kind == 0:
            (x_lat, h_lat), (x_ctx, h_ctx) = (ffn1(x_lat, m_lat, g_h=g_pre[i, 1]),
                                              ffn1(x_ctx, m_ctx, g_h=g_pre[i, 1]))
        else:
            x_lat, x_ctx = ffn1(x_lat, m_lat), ffn1(x_ctx, m_ctx)

        y_ctx = None
        if kind == 0:
            y_lat, y_ctx = _s5_scan(h_lat, h_ctx, s5_a_re[occ], s5_a_im[occ], s5_log_dt[occ], s5_b_re[occ],
                                    s5_b_im[occ], s5_c_re[occ], s5_c_im[occ])
            mix = "glu"
            mix_head = (g_pre[i, 1], s5_d[occ])
            mix_tail = (g_post[i, 1], s5_w_glu[occ].astype(bf16), s5_b_glu[occ])
        elif kind == 1:
            wq = da_w_qkv[occ].astype(bf16)
            rope = _rope_tables(S, DIFF_HEAD_DIM)
            qkv_lat = _prenorm_proj(x_lat, m_lat, g_pre[i, 1], 1, wq, rope=rope, n_rope=2 * D)
            qkv_ctx = _prenorm_proj(x_ctx, m_ctx, g_pre[i, 1], 1, wq)
            lamv = jnp.stack([da_lam_q1[occ], da_lam_k1[occ], da_lam_q2[occ], da_lam_k2[occ]]).astype(f32)
            lam_init = 0.8 - 0.6 * math.exp(-0.3 * i)
            y_lat = _diff_attention(qkv_lat, qkv_ctx, qkv_lat, lamv, da_subln[occ], lam_init)
            if not last:
                y_ctx = _diff_attention(qkv_ctx, qkv_ctx, None, lamv, da_subln[occ], lam_init)
            mix, mix_head, mix_tail = "outproj", (), (g_post[i, 1], da_w_o[occ].astype(bf16))
        elif kind == 2:
            wq = na_w_qkv[occ].astype(bf16)
            qkv_lat = _prenorm_proj(x_lat, m_lat, g_pre[i, 1], 1, wq)
            qkv_ctx = _prenorm_proj(x_ctx, m_ctx, g_pre[i, 1], 1, wq)
            y_lat = _na_attention(qkv_lat, qkv_ctx, qkv_lat, _na_bias_blocks(na_rpb[occ]))
            if not last:
                y_ctx = _na_attention(qkv_ctx, qkv_ctx, None, None)
            mix, mix_head, mix_tail = "outproj", (), (g_post[i, 1], na_w_o[occ].astype(bf16))
        else:
            assert last, "HGRN2 context outputs are not needed when it is the last layer"
            wp = jnp.concatenate([hg_w_qig[occ], hg_w_f[occ, 0], hg_w_f[occ, 1]], axis=1).astype(bf16)
            bp = jnp.concatenate([jnp.zeros((3 * D,), f32), hg_b_f[occ, 0], hg_b_f[occ, 1]])
            p_lat = _prenorm_proj(x_lat, m_lat, g_pre[i, 1], 1, wp, bias=bp, out_dtype=f32)
            p_ctx = _prenorm_proj(x_ctx, m_ctx, g_pre[i, 1], 1, wp, bias=bp, out_dtype=f32)
            y_lat = _hgrn2_core(p_lat, p_ctx, lower_bounds[i], hg_gnorm[occ])
            mix, mix_head, mix_tail = "outproj", (), (g_post[i, 1], hg_w_o[occ].astype(bf16))

        x_lat = ffn2(x_lat, m_lat, mix=mix, mix_args=(y_lat,) + mix_head + mix_tail)
        if not last:
            x_ctx = ffn2(x_ctx, m_ctx, mix=mix, mix_args=(y_ctx,) + mix_head + mix_tail)
    return x_lat
```

```python
import functools
import math

import jax
import jax.numpy as jnp
from jax import lax
from jax.experimental import pallas as pl
from jax.experimental.pallas import tpu as pltpu

f32 = jnp.float32
bf16 = jnp.bfloat16

N_SUB = 3
RMS_EPS = 1e-6
NEG_INF = -1e30
ROPE_BASE = 10000.0
GRID_W = 64
S5_GROUP = 16
S5_STATE = 64
S5_CHUNK = 16
DIFF_HEAD_DIM = 64
DIFF_KEY_CHUNK = 1024
NA_HEAD_DIM = 64
WIN_H = 8
WIN_W = 16
NA_TILE_ROWS = 8
NA_WIN_ROWS = 16
NA_KEY_CHUNK = 512
NA_TILES_PER_STEP = 8
HG_EXPAND = 128
HG_CHUNK = 64
HG_BLOCK = 256
LANES = 128
SUBLANES = 8
VMEM_LIMIT = 56 * 1024 * 1024
TOKEN_TILES = (512, 256, 128)
PROJ_TOKEN_TILE = 512
PROJ_COL_STEP = 512
FFN_COL_STEP = 1024
ATTN_QUERY_TILE = 512
ADA_COL_TILES = 4


def _cparams(*sem):
    return pltpu.CompilerParams(dimension_semantics=sem, vmem_limit_bytes=VMEM_LIMIT)


def _dot(a, b):
    return jnp.dot(a, b, preferred_element_type=f32)


def _dot_nt(a, b):
    return lax.dot_general(a, b, (((1,), (1,)), ((), ())), preferred_element_type=f32)


def _rms(x, g):
    return x * lax.rsqrt(jnp.mean(jnp.square(x), axis=-1, keepdims=True) + RMS_EPS) * g


def _pre_norm(x, g, mod_ref, j):
    shift = mod_ref[0, 3 * j:3 * j + 1, :]
    scale = mod_ref[0, 3 * j + 1:3 * j + 2, :]
    return _rms(x, g) * (1 + scale) + shift


def _post_residual(x, y, g, mod_ref, j, weight):
    gate = mod_ref[0, 3 * j + 2:3 * j + 3, :]
    return x + weight * gate * _rms(y, g)


def _tok_tile(t):
    for tm in TOKEN_TILES:
        if t % tm == 0:
            return tm
    raise ValueError(f"token count {t} is not a multiple of {TOKEN_TILES[-1]}")


def _full(shape):
    return pl.BlockSpec(shape, lambda *_: (0,) * len(shape))


def _mod_spec(mod):
    nd = mod.shape[1]
    d = mod.shape[2]
    if mod.shape[0] == 1:
        return pl.BlockSpec((1, nd, d), lambda b, t: (0, 0, 0))
    return pl.BlockSpec((1, nd, d), lambda b, t: (b, 0, 0))


def _ada_kernel(c_ref, w_ref, b_ref, o_ref):
    c = c_ref[...]
    sc = (c * jax.nn.sigmoid(c)).astype(bf16)
    o_ref[0] = _dot(sc, w_ref[0].astype(bf16)) + b_ref[0]


def _ada(c_all, w_ada, b_ada):
    depth, d, n = w_ada.shape
    rows = c_all.shape[0]
    tn = n // ADA_COL_TILES
    return pl.pallas_call(
        _ada_kernel,
        grid=(depth, n // tn),
        in_specs=[pl.BlockSpec((rows, d), lambda i, t: (0, 0)),
                  pl.BlockSpec((1, d, tn), lambda i, t: (i, 0, t)),
                  pl.BlockSpec((1, 1, tn), lambda i, t: (i, 0, t))],
        out_specs=pl.BlockSpec((1, rows, tn), lambda i, t: (i, 0, t)),
        out_shape=jax.ShapeDtypeStruct((depth, rows, n), f32),
        compiler_params=_cparams("arbitrary", "arbitrary"),
        name="ada",
    )(c_all, w_ada, b_ada.reshape(depth, 1, n))


def _ffn_kernel(x_ref, mod_ref, gin_ref, gout_ref, w1_ref, w3_ref, w2_ref, *rest, j, chunks, mix, emit_h):
    if emit_h:
        gh_ref, o_ref, h_ref = rest[-3:]
        rest = rest[:-3]
    else:
        o_ref = rest[-1]
        rest = rest[:-1]
    x = x_ref[0]
    if mix == "outproj":
        y_ref, gmix_ref, wo_ref = rest
        x = _post_residual(x, _dot(y_ref[0], wo_ref[...]), gmix_ref[...], mod_ref, 1, 1.0)
    elif mix == "glu":
        y_ref, gpre_ref, d_ref, gmix_ref, wg_ref, bg_ref = rest
        z = jax.nn.gelu(d_ref[...] * _pre_norm(x, gpre_ref[...], mod_ref, 1) + y_ref[0])
        u = _dot(z.astype(bf16), wg_ref[...]) + bg_ref[...]
        x = _post_residual(x, z * jax.nn.sigmoid(u), gmix_ref[...], mod_ref, 1, 1.0)
    h = _pre_norm(x, gin_ref[...], mod_ref, j).astype(bf16)
    acc = None
    for s, n in chunks:
        a = _dot(h, w1_ref[:, s:s + n])
        b = _dot(h, w3_ref[:, s:s + n])
        g = (a * jax.nn.sigmoid(a) * b).astype(bf16)
        y = _dot(g, w2_ref[s:s + n, :])
        acc = y if acc is None else acc + y
    out = _post_residual(x, acc, gout_ref[...], mod_ref, j, 0.5)
    o_ref[0] = out
    if emit_h:
        h_ref[0] = _pre_norm(out, gh_ref[...], mod_ref, 1).astype(h_ref.dtype)


def _ff_chunks(dff, step=FFN_COL_STEP):
    out, s = [], 0
    while s < dff:
        n = min(step, dff - s)
        out.append((s, n))
        s += n
    return tuple(out)


def _half_ffn(x, mod, g_in, g_out, w1, w3, w2, j, mix=None, mix_args=(), g_h=None):
    b, t, d = x.shape
    dff = w1.shape[1]
    tm = _tok_tile(t)
    tok = pl.BlockSpec((1, tm, d), lambda bb, tt: (bb, tt, 0))
    vec = _full((1, d))
    args = [x, mod, g_in.reshape(1, d), g_out.reshape(1, d), w1, w3, w2]
    specs = [tok, _mod_spec(mod), vec, vec, _full((d, dff)), _full((d, dff)), _full((dff, d))]
    if mix == "outproj":
        y, g_mix, w_o = mix_args
        args += [y, g_mix.reshape(1, d), w_o]
        specs += [tok, vec, _full((d, d))]
    elif mix == "glu":
        y, g_pre, d_skip, g_mix, w_g, b_g = mix_args
        args += [y, g_pre.reshape(1, d), d_skip.astype(f32).reshape(1, d), g_mix.reshape(1, d), w_g, b_g.reshape(1, d)]
        specs += [tok, vec, vec, vec, _full((d, d)), vec]
    out_specs, out_shape = tok, jax.ShapeDtypeStruct((b, t, d), f32)
    if g_h is not None:
        args.append(g_h.reshape(1, d))
        specs.append(vec)
        out_specs, out_shape = [tok, tok], [out_shape, jax.ShapeDtypeStruct((b, t, d), bf16)]
    return pl.pallas_call(
        functools.partial(_ffn_kernel, j=j, chunks=_ff_chunks(dff), mix=mix, emit_h=g_h is not None),
        grid=(b, t // tm),
        in_specs=specs,
        out_specs=out_specs,
        out_shape=out_shape,
        compiler_params=_cparams("parallel", "parallel"),
        name="half_ffn",
    )(*args)


def _swap_pairs(x):
    lane = lax.broadcasted_iota(jnp.int32, x.shape, 1)
    nxt = pltpu.roll(x, LANES - 1, 1)
    prv = pltpu.roll(x, 1, 1)
    return jnp.where((lane & 1) == 0, nxt, prv)


def _proj_kernel(*refs, j, n_out, n_rope, has_bias, step):
    x_ref, mod_ref, g_ref, w_ref = refs[:4]
    k = 4
    b_ref = None
    if has_bias:
        b_ref = refs[k]
        k += 1
    if n_rope:
        cos_ref, sin_ref = refs[k], refs[k + 1]
        k += 2
    o_ref = refs[k]
    h = _pre_norm(x_ref[0], g_ref[...], mod_ref, j).astype(bf16)
    for s in range(0, n_out, step):
        y = _dot(h, w_ref[:, s:s + step])
        if has_bias:
            y = y + b_ref[:, s:s + step]
        if s < n_rope:
            cos = cos_ref[...]
            sin = sin_ref[...]
            parts = []
            for c in range(0, step, LANES):
                yc = y[:, c:c + LANES]
                parts.append(yc * cos + _swap_pairs(yc) * sin)
            y = jnp.concatenate(parts, axis=1)
        o_ref[0, :, s:s + step] = y.astype(o_ref.dtype)


def _prenorm_proj(x, mod, g, j, w, bias=None, rope=None, n_rope=0, out_dtype=bf16):
    b, t, d = x.shape
    n = w.shape[1]
    tm = min(_tok_tile(t), PROJ_TOKEN_TILE)
    step = PROJ_COL_STEP
    assert n % step == 0 and n_rope % step == 0
    args = [x, mod, g.reshape(1, d), w]
    specs = [pl.BlockSpec((1, tm, d), lambda bb, tt: (bb, tt, 0)), _mod_spec(mod), _full((1, d)), _full((d, n))]
    if bias is not None:
        args.append(bias.reshape(1, n))
        specs.append(_full((1, n)))
    if n_rope:
        args += list(rope)
        specs += [pl.BlockSpec((tm, LANES), lambda bb, tt: (tt, 0))] * 2
    return pl.pallas_call(
        functools.partial(_proj_kernel, j=j, n_out=n, n_rope=n_rope, has_bias=bias is not None, step=step),
        grid=(b, t // tm),
        in_specs=specs,
        out_specs=pl.BlockSpec((1, tm, n), lambda bb, tt: (bb, tt, 0)),
        out_shape=jax.ShapeDtypeStruct((b, t, n), out_dtype),
        compiler_params=_cparams("parallel", "parallel"),
        name="prenorm_proj",
    )(*args)


def _s5_kernel(xc_ref, xl_ref, wb_ref, wk_ref, wc_ref, lam_ref, yc_ref, yl_ref, s_ref, hin_ref, *, nb, nc_ctx, nc):
    rc = nc_ctx * nb
    xc, xl = xc_ref[0], xl_ref[0]
    s_ref[0:rc, :] = _dot(xc, wb_ref[0])
    s_ref[rc:, :] = _dot(xl, wb_ref[0])
    P = S5_STATE
    lam = lam_ref[0]
    a_re, a_im = lam[0:1], lam[1:2]
    fwd_lane = lax.broadcasted_iota(jnp.int32, (nb, 2 * P), 1) < P

    def step(n, carry):
        h_re, h_im = carry
        r_f = pl.multiple_of(n * nb, nb)
        n_b = jnp.where(n < nc_ctx, nc_ctx - 1 - n, nc - 1 - (n - nc_ctx))
        r_b = pl.multiple_of(n_b * nb, nb)
        rows_f, rows_b = pl.ds(r_f, nb), pl.ds(r_b, nb)
        hin_ref[rows_f, 0:P] = h_re[:, 0:P]
        hin_ref[rows_b, P:2 * P] = h_re[:, P:2 * P]
        hin_ref[rows_f, 2 * P:3 * P] = h_im[:, 0:P]
        hin_ref[rows_b, 3 * P:4 * P] = h_im[:, P:2 * P]
        s_re = jnp.where(fwd_lane, s_ref[rows_f, 0:2 * P], s_ref[rows_b, 0:2 * P])
        s_im = jnp.where(fwd_lane, s_ref[rows_f, 2 * P:4 * P], s_ref[rows_b, 2 * P:4 * P])
        return a_re * h_re - a_im * h_im + s_re, a_re * h_im + a_im * h_re + s_im

    zero = jnp.zeros((nb, 2 * P), f32)
    lax.fori_loop(0, nc, step, (zero, zero), unroll=4 if nc % 4 == 0 else 1)
    yc_ref[0] = _dot(xc, wk_ref[0]) + _dot(hin_ref[0:rc, :].astype(bf16), wc_ref[0])
    yl_ref[0] = _dot(xl, wk_ref[0]) + _dot(hin_ref[rc:, :].astype(bf16), wc_ref[0])

def _s5_weights(a_re, a_im, log_dt, b_re, b_im, c_re, c_im):
    hp = lax.Precision.HIGHEST
    C = S5_CHUNK
    G, P = a_re.shape[1:]
    N = b_re.shape[-1]
    j = jnp.arange(C + 1, dtype=f32)
    wbs, wks, wcs, lams = [], [], [], []
    for dirn in range(2):
        are, aim = a_re[dirn].astype(f32), a_im[dirn].astype(f32)
        dt = jnp.exp(log_dt[dirn].astype(f32))[:, None]
        pw_mag = jnp.exp(j[:, None, None] * (are * dt)[None])
        pw_re = pw_mag * jnp.cos(j[:, None, None] * (aim * dt)[None])
        pw_im = pw_mag * jnp.sin(j[:, None, None] * (aim * dt)[None])
        nr, ni = pw_re[1] - 1.0, pw_im[1]
        den = are * are + aim * aim
        fr = (nr * are + ni * aim) / den
        fi = (ni * are - nr * aim) / den
        bre, bim = b_re[dirn].astype(f32), b_im[dirn].astype(f32)
        bbr = fr[..., None] * bre - fi[..., None] * bim
        bbi = fr[..., None] * bim + fi[..., None] * bre
        cre, cim = c_re[dirn].astype(f32), c_im[dirn].astype(f32)
        dr, di = (pw_re[:C][::-1], pw_im[:C][::-1]) if dirn == 0 else (pw_re[:C], pw_im[:C])
        sr = dr[..., None] * bbr[None] - di[..., None] * bbi[None]
        si = dr[..., None] * bbi[None] + di[..., None] * bbr[None]
        wb = jnp.concatenate([sr, si], axis=2)
        wbs.append(wb.transpose(1, 0, 3, 2).reshape(G, C * N, 2 * P))
        pr, pi = pw_re[1:], pw_im[1:]
        if dirn == 1:
            pr, pi = pr[::-1], pi[::-1]
        or_ = cre[None] * pr[:, :, None, :] - cim[None] * pi[:, :, None, :]
        oi_ = -(cre[None] * pi[:, :, None, :] + cim[None] * pr[:, :, None, :])
        wc = jnp.concatenate([or_, oi_], axis=3)
        wcs.append(wc.transpose(1, 3, 0, 2).reshape(G, 2 * P, C * N))
        clr = cre[None] * pw_re[:C, :, None, :] - cim[None] * pw_im[:C, :, None, :]
        cli = cre[None] * pw_im[:C, :, None, :] + cim[None] * pw_re[:C, :, None, :]
        kk = (jnp.einsum('jgnp,gpm->gjnm', clr, bbr, precision=hp)
              - jnp.einsum('jgnp,gpm->gjnm', cli, bbi, precision=hp))
        s_idx = jnp.arange(C)[:, None]
        t_idx = jnp.arange(C)[None, :]
        lag = (t_idx - s_idx) if dirn == 0 else (s_idx - t_idx)
        onehot = (lag[:, :, None] == jnp.arange(C)[None, None, :]).astype(f32)
        kt = jnp.einsum('stj,gjnm->gsmtn', onehot, kk, precision=hp)
        wks.append(kt.reshape(G, C * N, C * N))
        lams.append((pw_re[C], pw_im[C]))
    wb = jnp.concatenate([wbs[0][..., :P], wbs[1][..., :P], wbs[0][..., P:], wbs[1][..., P:]], axis=2).astype(bf16)
    wc = jnp.concatenate([wcs[0][:, :P], wcs[1][:, :P], wcs[0][:, P:], wcs[1][:, P:]], axis=1).astype(bf16)
    wk = (wks[0] + wks[1]).astype(bf16)
    lam = jnp.stack([jnp.concatenate([lams[0][0], lams[1][0]], axis=-1),
                     jnp.concatenate([lams[0][1], lams[1][1]], axis=-1)], axis=1)
    return wb, wk, wc, lam


def _s5_scan(h_lat, h_ctx, a_re, a_im, log_dt, b_re, b_im, c_re, c_im):
    B, S, D = h_lat.shape
    L = h_ctx.shape[1]
    C, N, P = S5_CHUNK, S5_GROUP, S5_STATE
    G = D // N
    nc_ctx, nc = L // C, (L + S) // C
    assert 2 * P == LANES and L % C == 0 and S % C == 0
    wb, wk, wc, lam = _s5_weights(a_re, a_im, log_dt, b_re, b_im, c_re, c_im)

    def to_groups(h):
        n = h.shape[1] // C
        return h.reshape(B, n, C, G, N).transpose(3, 1, 0, 2, 4).reshape(G, n * B, C * N)

    def from_groups(y):
        n = y.shape[1] // B
        return y.reshape(G, n, B, C, N).transpose(2, 1, 3, 0, 4).reshape(B, n * C, D)

    rc, rl = nc_ctx * B, (nc - nc_ctx) * B

    def rows(r):
        return pl.BlockSpec((1, r, C * N), lambda g: (g, 0, 0))

    yc, yl = pl.pallas_call(
        functools.partial(_s5_kernel, nb=B, nc_ctx=nc_ctx, nc=nc),
        grid=(G,),
        in_specs=[rows(rc), rows(rl),
                  pl.BlockSpec((1, C * N, 4 * P), lambda g: (g, 0, 0)),
                  pl.BlockSpec((1, C * N, C * N), lambda g: (g, 0, 0)),
                  pl.BlockSpec((1, 4 * P, C * N), lambda g: (g, 0, 0)),
                  pl.BlockSpec((1, 2, 2 * P), lambda g: (g, 0, 0))],
        out_specs=[rows(rc), rows(rl)],
        out_shape=[jax.ShapeDtypeStruct((G, rc, C * N), f32), jax.ShapeDtypeStruct((G, rl, C * N), f32)],
        scratch_shapes=[pltpu.VMEM((rc + rl, 4 * P), f32), pltpu.VMEM((rc + rl, 4 * P), f32)],
        compiler_params=_cparams("parallel"),
        name="s5_scan",
    )(to_groups(h_ctx), to_groups(h_lat), wb, wk, wc, lam)
    return from_groups(yl), from_groups(yc)


def _diff_maps(q):
    lane = lax.broadcasted_iota(jnp.int32, q.shape, 1)
    qs = (q.astype(f32) * (DIFF_HEAD_DIM ** -0.5 * math.log2(math.e))).astype(bf16)
    zero = jnp.zeros_like(qs)
    return jnp.where(lane < DIFF_HEAD_DIM, qs, zero), jnp.where(lane >= DIFF_HEAD_DIM, qs, zero)


def _key_chunks(k_refs):
    out = []
    for seg, k_ref in enumerate(k_refs):
        n = k_ref.shape[1]
        out += [(seg, c, min(c + DIFF_KEY_CHUNK, n)) for c in range(0, n, DIFF_KEY_CHUNK)]
    return out


def _diff_step(q_next, cur, nxt, k_refs, v_refs, lamv_ref, g_ref, o_ref, lam_init):
    chunks = _key_chunks(k_refs)
    cur_s, cur_m = cur
    nxt_s, nxt_m = nxt
    tq = cur_m.shape[1]
    m1, m2 = cur_m[0, :, 0:1], cur_m[1, :, 0:1]
    q1n, q2n = _diff_maps(q_next)
    n1 = n2 = l1 = l2 = None
    for seg, a, b in chunks:
        s1 = _dot_nt(q1n, k_refs[seg][0, a:b, :])
        s2 = _dot_nt(q2n, k_refs[seg][0, a:b, :])
        nxt_s[seg][0, :, a:b] = s1
        nxt_s[seg][1, :, a:b] = s2
        c1 = jnp.max(s1, axis=-1, keepdims=True)
        c2 = jnp.max(s2, axis=-1, keepdims=True)
        n1 = c1 if n1 is None else jnp.maximum(n1, c1)
        n2 = c2 if n2 is None else jnp.maximum(n2, c2)
        p1 = jnp.exp2(cur_s[seg][0, :, a:b] - m1)
        p2 = jnp.exp2(cur_s[seg][1, :, a:b] - m2)
        cur_s[seg][0, :, a:b] = p1
        cur_s[seg][1, :, a:b] = p2
        r1 = jnp.sum(p1, axis=-1, keepdims=True)
        r2 = jnp.sum(p2, axis=-1, keepdims=True)
        l1 = r1 if l1 is None else l1 + r1
        l2 = r2 if l2 is None else l2 + r2
    nxt_m[0] = jnp.broadcast_to(n1, (tq, LANES))
    nxt_m[1] = jnp.broadcast_to(n2, (tq, LANES))
    lv = lamv_ref[...]
    lam = (jnp.exp(jnp.sum(lv[0:1] * lv[1:2], axis=-1, keepdims=True))
           - jnp.exp(jnp.sum(lv[2:3] * lv[3:4], axis=-1, keepdims=True)) + lam_init)
    c = lam * l1 / l2
    o = None
    for seg, a, b in chunks:
        w = (cur_s[seg][0, :, a:b] - c * cur_s[seg][1, :, a:b]).astype(bf16)
        part = _dot(w, v_refs[seg][0, a:b, :])
        o = part if o is None else o + part
    o = o / l1
    o_ref[0] = (_rms(o, g_ref[...]) * (1 - lam_init)).astype(o_ref.dtype)


def _diff_kernel(*refs, n_seg, lam_init):
    q_ref, qn_ref = refs[:2]
    k_refs = refs[2:2 + 3 * n_seg:3]
    v_refs = refs[3:3 + 3 * n_seg:3]
    kn_refs = refs[4:4 + 3 * n_seg:3]
    lamv_ref, g_ref, o_ref = refs[2 + 3 * n_seg:5 + 3 * n_seg]
    scratch = refs[5 + 3 * n_seg:]
    slots = tuple((scratch[p * (n_seg + 1):p * (n_seg + 1) + n_seg], scratch[p * (n_seg + 1) + n_seg])
                  for p in range(2))
    unit = (pl.program_id(0) * pl.num_programs(1) + pl.program_id(1)) * pl.num_programs(2) + pl.program_id(2)

    @pl.when(unit == 0)
    def _():
        q1, q2 = _diff_maps(q_ref[0])
        s_refs, m_ref = slots[0]
        for mp, qm in enumerate((q1, q2)):
            parts = [_dot_nt(qm, k_ref[0]) for k_ref in k_refs]
            for s_ref, sc in zip(s_refs, parts):
                s_ref[mp] = sc
            mx = functools.reduce(jnp.maximum, [jnp.max(sc, axis=-1, keepdims=True) for sc in parts])
            m_ref[mp] = jnp.broadcast_to(mx, m_ref.shape[1:])

    for parity in range(2):
        @pl.when((unit & 1) == parity)
        def _(parity=parity):
            _diff_step(qn_ref[0], slots[parity], slots[1 - parity], kn_refs, v_refs, lamv_ref, g_ref, o_ref,
                       lam_init)


def _diff_attention(qkv_q, qkv_ctx, qkv_lat, lamv, subln, lam_init):
    B, T, n3 = qkv_q.shape
    D = n3 // 3
    H = D // LANES
    tq = min(_tok_tile(T), ATTN_QUERY_TILE)
    nq = T // tq
    kv = [qkv_ctx] if qkv_lat is None else [qkv_ctx, qkv_lat]

    def following(b, h, t):
        u = jnp.minimum((b * H + h) * nq + t + 1, B * H * nq - 1)
        return u // (H * nq), (u // nq) % H, u % nq

    def q_next(b, h, t):
        nb, nh, nt = following(b, h, t)
        return nb, nt, nh

    def k_next(b, h, t):
        nb, nh, _ = following(b, h, t)
        return nb, 0, H + nh

    args = [qkv_q, qkv_q]
    specs = [pl.BlockSpec((1, tq, LANES), lambda b, h, t: (b, t, h)), pl.BlockSpec((1, tq, LANES), q_next)]
    for a in kv:
        args += [a, a, a]
        specs += [pl.BlockSpec((1, a.shape[1], LANES), lambda b, h, t: (b, 0, H + h)),
                  pl.BlockSpec((1, a.shape[1], LANES), lambda b, h, t: (b, 0, 2 * H + h)),
                  pl.BlockSpec((1, a.shape[1], LANES), k_next)]
    args += [lamv, subln.reshape(1, LANES)]
    specs += [_full(lamv.shape), _full((1, LANES))]
    slot = [pltpu.VMEM((2, tq, a.shape[1]), f32) for a in kv] + [pltpu.VMEM((2, tq, LANES), f32)]
    return pl.pallas_call(
        functools.partial(_diff_kernel, n_seg=len(kv), lam_init=lam_init),
        grid=(B, H, nq),
        in_specs=specs,
        out_specs=pl.BlockSpec((1, tq, LANES), lambda b, h, t: (b, t, h)),
        out_shape=jax.ShapeDtypeStruct((B, T, D), bf16),
        scratch_shapes=slot + slot,
        compiler_params=_cparams("arbitrary", "arbitrary", "arbitrary"),
        name="diff_attention",
    )(*args)


def _rope_tables(n_tokens, head_dim):
    n_freq = head_dim // 4
    inv_freq = ROPE_BASE ** (-jnp.arange(n_freq, dtype=f32) / n_freq)
    t = jnp.arange(n_tokens)
    row = (t // GRID_W).astype(f32)
    col = (t % GRID_W).astype(f32)
    ang = jnp.concatenate([row[:, None] * inv_freq, col[:, None] * inv_freq], axis=-1)
    cos = jnp.repeat(jnp.cos(ang), 2, axis=-1)
    sin = jnp.repeat(jnp.sin(ang), 2, axis=-1) * jnp.tile(jnp.array([-1.0, 1.0], f32), head_dim // 2)
    reps = LANES // head_dim
    return jnp.tile(cos, (1, reps)), jnp.tile(sin, (1, reps))


def _na_head(q, hh):
    lane = lax.broadcasted_iota(jnp.int32, q.shape, 1)
    qs = (q.astype(f32) * (NA_HEAD_DIM ** -0.5 * math.log2(math.e))).astype(bf16)
    sel = (lane < NA_HEAD_DIM) if hh == 0 else (lane >= NA_HEAD_DIM)
    return jnp.where(sel, qs, jnp.zeros_like(qs))


def _na_ctx_kernel(q_ref, kc_ref, vc_ref, o_ref):
    q = q_ref[0]
    lane = lax.broadcasted_iota(jnp.int32, q.shape, 1)
    outs = []
    for hh in range(2):
        s = _dot_nt(_na_head(q, hh), kc_ref[0])
        p = jnp.exp2(s - jnp.max(s, axis=-1, keepdims=True))
        outs.append(_dot(p.astype(bf16), vc_ref[0]) / jnp.sum(p, axis=-1, keepdims=True))
    o_ref[0] = jnp.where(lane < NA_HEAD_DIM, outs[0], outs[1]).astype(o_ref.dtype)


def _na_window_table(rows):
    table = []
    for t in range(rows // NA_TILE_ROWS):
        w0 = min(max(t * NA_TILE_ROWS - (NA_WIN_ROWS - NA_TILE_ROWS) // 2, 0), rows - NA_WIN_ROWS)
        line = [w0 * GRID_W]
        for i in range(NA_TILE_ROWS):
            qr = t * NA_TILE_ROWS + i
            r0 = min(max(qr - WIN_H // 2, 0), rows - WIN_H)
            for j in range(NA_WIN_ROWS // 2):
                kr = w0 + 2 * j
                e = kr - qr + WIN_H
                ok0 = r0 <= kr < r0 + WIN_H
                ok1 = r0 <= kr + 1 < r0 + WIN_H
                line.append(e if ok0 and ok1 else 2 * WIN_H - 1 + e if ok0 else 4 * WIN_H - 1 + e if ok1 else 0)
        table.append(line)
    return jnp.array(table, jnp.int32)


def _na_window(win_ref, t):
    n_pairs = NA_WIN_ROWS // 2
    entry = [[win_ref[t, 1 + i * n_pairs + j] for j in range(n_pairs)] for i in range(NA_TILE_ROWS)]
    return pl.multiple_of(win_ref[t, 0], 4 * GRID_W), entry


def _lane_fold(x, op):
    return functools.reduce(op, [x[:, c:c + LANES] for c in range(0, x.shape[1], LANES)])


def _na_biased(sl, hh, entry, j0, tb_ref, s_ref, c0):
    tiles = []
    for i in range(NA_TILE_ROWS):
        r = slice(i * GRID_W, (i + 1) * GRID_W)
        mx = None
        for jj in range(sl.shape[1] // LANES):
            blk = sl[r, jj * LANES:(jj + 1) * LANES] + tb_ref[0, hh, entry[i][j0 + jj]]
            s_ref[r, c0 + jj * LANES:c0 + (jj + 1) * LANES] = blk
            mx = blk if mx is None else jnp.maximum(mx, blk)
        tiles.append(mx)
    return jnp.concatenate(tiles, axis=0)


def _na_unit(make, use, kc_ref, vc_ref, kl_ref, vl_ref, tb_ref):
    q, hh_m, tok_m, entry, (mc_ref, ml_ref, mm_ref) = make
    hh_u, tok_u, (uc_ref, ul_ref, um_ref) = use
    qh = _na_head(q, hh_m)
    m = um_ref[:, 0:1]
    run_max = l = o = None
    for c0 in [None] + list(range(0, NA_WIN_ROWS * GRID_W, NA_KEY_CHUNK)):
        if c0 is None:
            sc = _dot_nt(qh, kc_ref[0])
            mc_ref[...] = sc
            run_max = _lane_fold(sc, jnp.maximum)
            p = jnp.exp2(uc_ref[...] - m)
            v = vc_ref[0]
        else:
            sl = _dot_nt(qh, kl_ref[0, pl.ds(tok_m + c0, NA_KEY_CHUNK), :])
            run_max = jnp.maximum(run_max, _na_biased(sl, hh_m, entry, c0 // LANES, tb_ref, ml_ref, c0))
            p = jnp.exp2(ul_ref[:, c0:c0 + NA_KEY_CHUNK] - m)
            v = vl_ref[0, pl.ds(tok_u + c0, NA_KEY_CHUNK), :]
        r = _lane_fold(p, jnp.add)
        part = _dot(p.astype(bf16), v)
        l = r if l is None else l + r
        o = part if o is None else o + part
    mm_ref[...] = jnp.broadcast_to(jnp.max(run_max, axis=-1, keepdims=True), mm_ref.shape)
    return o / jnp.sum(l, axis=-1, keepdims=True)


def _na_kernel(win_ref, q_ref, qn_ref, kc_ref, vc_ref, kl_ref, vl_ref, tb_ref, o_ref, *scratch, n_tiles, tps):
    slots = (scratch[:3], scratch[3:])
    t = pl.program_id(2)
    tq = qn_ref.shape[1]
    tile_ids = [t * tps + k for k in range(tps)] + [jnp.minimum((t + 1) * tps, n_tiles - 1)]
    windows = [_na_window(win_ref, i) for i in tile_ids]
    qs = [q_ref[0, k * tq:(k + 1) * tq, :] for k in range(tps)] + [qn_ref[0]]
    shared = (kc_ref, vc_ref, kl_ref, vl_ref, tb_ref)

    @pl.when(t == 0)
    def _():
        tok, entry = windows[0]
        qh = _na_head(qs[0], 0)
        sc = _dot_nt(qh, kc_ref[0])
        slots[0][0][...] = sc
        sl = _dot_nt(qh, kl_ref[0, pl.ds(tok, NA_WIN_ROWS * GRID_W), :])
        run_max = jnp.maximum(_lane_fold(sc, jnp.maximum), _na_biased(sl, 0, entry, 0, tb_ref, slots[0][1], 0))
        slots[0][2][...] = jnp.broadcast_to(jnp.max(run_max, axis=-1, keepdims=True), slots[0][2].shape)

    units = [(k, hh) for k in range(tps) for hh in range(2)] + [(tps, 0)]
    lane = lax.broadcasted_iota(jnp.int32, (tq, LANES), 1)
    outs = {}
    for (k, hh), (nk, nhh) in zip(units[:-1], units[1:]):
        make = (qs[nk], nhh, windows[nk][0], windows[nk][1], slots[nhh])
        outs[k, hh] = _na_unit(make, (hh, windows[k][0], slots[hh]), *shared)
        if hh == 1:
            o_ref[0, k * tq:(k + 1) * tq, :] = jnp.where(lane < NA_HEAD_DIM, outs[k, 0], outs[k, 1]).astype(o_ref.dtype)


def _na_bias_blocks(rpb):
    H, n_dr, n_dc = rpb.shape
    col = jnp.arange(GRID_W)
    c0 = jnp.clip(col - WIN_W // 2, 0, GRID_W - WIN_W)
    col_ok = (col[None, :] >= c0[:, None]) & (col[None, :] < c0[:, None] + WIN_W)
    dc = jnp.clip(col[None, :] - col[:, None] + WIN_W - 1, 0, n_dc - 1)
    onehot = (dc.reshape(1, -1) == jnp.arange(n_dc)[:, None]).astype(f32)
    blk = jnp.einsum('hrj,jx->hrx', rpb.astype(f32) * math.log2(math.e), onehot, precision=lax.Precision.HIGHEST)
    blk = jnp.where(col_ok[None, None], blk.reshape(H, n_dr, GRID_W, GRID_W), NEG_INF)
    blk = jnp.concatenate([jnp.full((H, 1, GRID_W, GRID_W), NEG_INF, f32), blk], axis=1)
    n_e = 2 * WIN_H - 1
    none = blk[:, 0:1]
    none_run = jnp.broadcast_to(none, (H, n_e, GRID_W, GRID_W))
    left = jnp.concatenate([none, blk[:, 1:n_e], none, blk[:, 1:n_e + 1], none_run], axis=1)
    right = jnp.concatenate([none, blk[:, 2:n_e + 1], none, none_run, blk[:, 1:n_e + 1]], axis=1)
    tab = jnp.concatenate([left, right], axis=-1)
    return tab.reshape((H // 2, 2) + tab.shape[1:])


def _na_attention(qkv_q, qkv_ctx, qkv_lat, bias):
    B, T, n3 = qkv_q.shape
    D = n3 // 3
    HP = D // LANES
    L = qkv_ctx.shape[1]
    ctx_k = pl.BlockSpec((1, L, LANES), lambda b, h, t: (b, 0, HP + h))
    ctx_v = pl.BlockSpec((1, L, LANES), lambda b, h, t: (b, 0, 2 * HP + h))
    if qkv_lat is None:
        tq = _tok_tile(T)
        tile = pl.BlockSpec((1, tq, LANES), lambda b, h, t: (b, t, h))
        return pl.pallas_call(
            _na_ctx_kernel,
            grid=(B, HP, T // tq),
            in_specs=[tile, ctx_k, ctx_v],
            out_specs=tile,
            out_shape=jax.ShapeDtypeStruct((B, T, D), bf16),
            compiler_params=_cparams("parallel", "parallel", "parallel"),
            name="na_attention_ctx",
        )(qkv_q, qkv_ctx, qkv_ctx)
    tq = NA_TILE_ROWS * GRID_W
    n_tiles = T // tq
    n_win = NA_WIN_ROWS * GRID_W
    assert T % tq == 0 and T // GRID_W >= NA_WIN_ROWS and bias.shape[2] == 6 * WIN_H - 2
    tps = max(d for d in range(1, NA_TILES_PER_STEP + 1) if n_tiles % d == 0)
    tile = pl.BlockSpec((1, tps * tq, LANES), lambda b, h, t: (b, t, h))
    slot = [pltpu.VMEM((tq, L), f32), pltpu.VMEM((tq, n_win), f32), pltpu.VMEM((tq, LANES), f32)]
    return pl.pallas_call(
        functools.partial(_na_kernel, n_tiles=n_tiles, tps=tps),
        grid=(B, HP, n_tiles // tps),
        in_specs=[pl.BlockSpec(memory_space=pltpu.SMEM), tile,
                  pl.BlockSpec((1, tq, LANES), lambda b, h, t: (b, jnp.minimum((t + 1) * tps, n_tiles - 1), h)),
                  ctx_k, ctx_v,
                  pl.BlockSpec((1, T, LANES), lambda b, h, t: (b, 0, HP + h)),
                  pl.BlockSpec((1, T, LANES), lambda b, h, t: (b, 0, 2 * HP + h)),
                  pl.BlockSpec((1,) + bias.shape[1:], lambda b, h, t: (h, 0, 0, 0, 0))],
        out_specs=tile,
        out_shape=jax.ShapeDtypeStruct((B, T, D), bf16),
        scratch_shapes=slot + slot,
        compiler_params=_cparams("parallel", "parallel", "arbitrary"),
        name="na_attention",
    )(_na_window_table(T // GRID_W), qkv_q, qkv_q, qkv_ctx, qkv_ctx, qkv_lat, qkv_lat, bias)


def _chunk_cumsum(x, reverse):
    n = x.shape[0]
    pos = lax.broadcasted_iota(jnp.int32, x.shape, 0) & (HG_CHUNK - 1)
    k = 1
    while k < HG_CHUNK:
        if reverse:
            x = x + jnp.where(pos < HG_CHUNK - k, pltpu.roll(x, n - k, 0), 0.0)
        else:
            x = x + jnp.where(pos >= k, pltpu.roll(x, k, 0), 0.0)
        k *= 2
    return x


def _hg_mask(n, reverse):
    ri = lax.broadcasted_iota(jnp.int32, (n, n), 0)
    ci = lax.broadcasted_iota(jnp.int32, (n, n), 1)
    shift = HG_CHUNK.bit_length() - 1
    return ((ri >> shift) == (ci >> shift)) & ((ci >= ri) if reverse else (ci <= ri))


def _hg_block(z, v, q, lb, st, reverse, tri):
    with_out = q is not None
    n = z.shape[0]
    nch = n // HG_CHUNK
    f = lb + (1 - lb) * jax.nn.sigmoid(z)
    lf = jnp.log(f)
    kk = 1 - f
    g = _chunk_cumsum(lf, reverse)
    out = None
    if with_out:
        q_dec = q * jnp.exp(g)
        k_inv = kk * jnp.exp(-g)
        att = jnp.where(tri, _dot_nt(q_dec.astype(bf16), k_inv.astype(bf16)), 0.0)
        out_intra = _dot(att.astype(bf16), v.astype(bf16))
        outs = [None] * nch
    order = range(nch - 1, -1, -1) if reverse else range(nch)
    for c in order:
        sl = slice(c * HG_CHUNK, (c + 1) * HG_CHUNK)
        end = c * HG_CHUNK if reverse else (c + 1) * HG_CHUNK - 1
        g_tot = g[end:end + 1]
        if with_out:
            outs[c] = _dot_nt(q_dec[sl].astype(bf16), st.astype(bf16))
        kdec = kk[sl] * jnp.exp(g_tot - g[sl])
        ds_t = _dot(v[sl].T.astype(bf16), kdec.astype(bf16))
        st = st * jnp.exp(g_tot) + ds_t
    if with_out:
        out = out_intra + jnp.concatenate(outs, axis=0)
    return out, st


def _hg_kernel(q_ref, i_ref, gate_ref, zf_ref, zb_ref, ic_ref, zfc_ref, zbc_ref, lb_ref, gn_ref, o_ref,
               accf_ref, accb_ref, *, s_tok):
    lb = lb_ref[...]
    nblk = s_tok // HG_BLOCK
    zero = jnp.zeros((HG_EXPAND, HG_EXPAND), f32)
    _, st_f = _hg_block(zfc_ref[0], ic_ref[0], None, lb, zero, False, None)
    _, st_b = _hg_block(zbc_ref[0], ic_ref[0], None, lb, zero, True, None)
    tri_f, tri_b = _hg_mask(HG_BLOCK, False), _hg_mask(HG_BLOCK, True)

    def body(n, carry):
        st_f, st_b = carry
        rows_f = pl.ds(pl.multiple_of(n * HG_BLOCK, HG_BLOCK), HG_BLOCK)
        rows_b = pl.ds(pl.multiple_of((nblk - 1 - n) * HG_BLOCK, HG_BLOCK), HG_BLOCK)
        out_f, st_f = _hg_block(zf_ref[0, rows_f, :], i_ref[0, rows_f, :], q_ref[0, rows_f, :], lb, st_f, False, tri_f)
        out_b, st_b = _hg_block(zb_ref[0, rows_b, :], i_ref[0, rows_b, :], q_ref[0, rows_b, :], lb, st_b, True, tri_b)
        accf_ref[rows_f, :] = out_f
        accb_ref[rows_b, :] = out_b
        return st_f, st_b

    lax.fori_loop(0, nblk, body, (st_f, st_b), unroll=16 if nblk % 16 == 0 else 1)
    gate = gate_ref[0]
    o = accf_ref[...] + accb_ref[...]
    o_ref[0] = (_rms(o, gn_ref[...]) * (gate * jax.nn.sigmoid(gate))).astype(o_ref.dtype)


def _hgrn2_core(p_lat, p_ctx, lb, gn_g):
    B, S, n5 = p_lat.shape
    D = n5 // 5
    H = D // HG_EXPAND
    L = p_ctx.shape[1]
    assert S % HG_BLOCK == 0 and L % HG_CHUNK == 0

    def lat(k):
        return pl.BlockSpec((1, S, HG_EXPAND), lambda b, h: (b, 0, k * H + h))

    def ctx(k):
        return pl.BlockSpec((1, L, HG_EXPAND), lambda b, h: (b, 0, k * H + h))

    vec = pl.BlockSpec((1, HG_EXPAND), lambda b, h: (0, h))
    return pl.pallas_call(
        functools.partial(_hg_kernel, s_tok=S),
        grid=(B, H),
        in_specs=[lat(0), lat(1), lat(2), lat(3), lat(4), ctx(1), ctx(3), ctx(4), vec, _full((1, HG_EXPAND))],
        out_specs=pl.BlockSpec((1, S, HG_EXPAND), lambda b, h: (b, 0, h)),
        out_shape=jax.ShapeDtypeStruct((B, S, D), bf16),
        scratch_shapes=[pltpu.VMEM((S, HG_EXPAND), f32), pltpu.VMEM((S, HG_EXPAND), f32)],
        compiler_params=_cparams("parallel", "parallel"),
        name="hgrn2_core",
    )(p_lat, p_lat, p_lat, p_lat, p_lat, p_ctx, p_ctx, p_ctx, lb.reshape(1, D), gn_g.reshape(1, HG_EXPAND))


def kernel(x, c, ctx, c_ctx, w_ada, b_ada, g_pre, g_post, w_ff1, w_ff3, w_ff2, s5_a_re, s5_a_im, s5_log_dt, s5_b_re, s5_b_im, s5_c_re, s5_c_im, s5_d, s5_w_glu, s5_b_glu, da_w_qkv, da_w_o, da_lam_q1, da_lam_k1, da_lam_q2, da_lam_k2, da_subln, na_w_qkv, na_w_o, na_rpb, hg_w_qig, hg_w_f, hg_b_f, hg_lb_logits, hg_gnorm, hg_w_o):
    B, S, D = x.shape
    depth = w_ada.shape[0]
    n_mix = 4
    rows_pad = -(-(B + 1) // SUBLANES) * SUBLANES
    c_all = jnp.concatenate([c, c_ctx[None], jnp.zeros((rows_pad - B - 1, D), f32)], axis=0)
    mods = _ada(c_all, w_ada, b_ada)
    lb_p = jax.nn.softmax(hg_lb_logits.astype(f32), axis=0)
    lower_bounds = jnp.cumsum(lb_p, axis=0) - lb_p[0]
    w1b, w3b, w2b = w_ff1.astype(bf16), w_ff3.astype(bf16), w_ff2.astype(bf16)

    x_lat, x_ctx = x, ctx
    for i in range(depth):
        last = i == depth - 1
        occ, kind = i // n_mix, i % n_mix
        m_lat = mods[i, :B].reshape(B, 3 * N_SUB, D)
        m_ctx = mods[i, B:B + 1].reshape(1, 3 * N_SUB, D)
        ffn1 = functools.partial(_half_ffn, g_in=g_pre[i, 0], g_out=g_post[i, 0], w1=w1b[i, 0], w3=w3b[i, 0],
                                 w2=w2b[i, 0], j=0)
        ffn2 = functools.partial(_half_ffn, g_in=g_pre[i, 2], g_out=g_post[i, 2], w1=w1b[i, 1], w3=w3b[i, 1],
                                 w2=w2b[i, 1], j=2)
        if kind == 0:
            (x_lat, h_lat), (x_ctx, h_ctx) = (ffn1(x_lat, m_lat, g_h=g_pre[i, 1]),
                                              ffn1(x_ctx, m_ctx, g_h=g_pre[i, 1]))
        else:
            x_lat, x_ctx = ffn1(x_lat, m_lat), ffn1(x_ctx, m_ctx)

        y_ctx = None
        if kind == 0:
            y_lat, y_ctx = _s5_scan(h_lat, h_ctx, s5_a_re[occ], s5_a_im[occ], s5_log_dt[occ], s5_b_re[occ],
                                    s5_b_im[occ], s5_c_re[occ], s5_c_im[occ])
            mix = "glu"
            mix_head = (g_pre[i, 1], s5_d[occ])
            mix_tail = (g_post[i, 1], s5_w_glu[occ].astype(bf16), s5_b_glu[occ])
        elif kind == 1:
            wq = da_w_qkv[occ].astype(bf16)
            rope = _rope_tables(S, DIFF_HEAD_DIM)
            qkv_lat = _prenorm_proj(x_lat, m_lat, g_pre[i, 1], 1, wq, rope=rope, n_rope=2 * D)
            qkv_ctx = _prenorm_proj(x_ctx, m_ctx, g_pre[i, 1], 1, wq)
            lamv = jnp.stack([da_lam_q1[occ], da_lam_k1[occ], da_lam_q2[occ], da_lam_k2[occ]]).astype(f32)
            lam_init = 0.8 - 0.6 * math.exp(-0.3 * i)
            y_lat = _diff_attention(qkv_lat, qkv_ctx, qkv_lat, lamv, da_subln[occ], lam_init)
            if not last:
                y_ctx = _diff_attention(qkv_ctx, qkv_ctx, None, lamv, da_subln[occ], lam_init)
            mix, mix_head, mix_tail = "outproj", (), (g_post[i, 1], da_w_o[occ].astype(bf16))
        elif kind == 2:
            wq = na_w_qkv[occ].astype(bf16)
            qkv_lat = _prenorm_proj(x_lat, m_lat, g_pre[i, 1], 1, wq)
            qkv_ctx = _prenorm_proj(x_ctx, m_ctx, g_pre[i, 1], 1, wq)
            y_lat = _na_attention(qkv_lat, qkv_ctx, qkv_lat, _na_bias_blocks(na_rpb[occ]))
            if not last:
                y_ctx = _na_attention(qkv_ctx, qkv_ctx, None, None)
            mix, mix_head, mix_tail = "outproj", (), (g_post[i, 1], na_w_o[occ].astype(bf16))
        else:
            assert last, "HGRN2 context outputs are not needed when it is the last layer"
            wp = jnp.concatenate([hg_w_qig[occ], hg_w_f[occ, 0], hg_w_f[occ, 1]], axis=1).astype(bf16)
            bp = jnp.concatenate([jnp.zeros((3 * D,), f32), hg_b_f[occ, 0], hg_b_f[occ, 1]])
            p_lat = _prenorm_proj(x_lat, m_lat, g_pre[i, 1], 1, wp, bias=bp, out_dtype=f32)
            p_ctx = _prenorm_proj(x_ctx, m_ctx, g_pre[i, 1], 1, wp, bias=bp, out_dtype=f32)
            y_lat = _hgrn2_core(p_lat, p_ctx, lower_bounds[i], hg_gnorm[occ])
            mix, mix_head, mix_tail = "outproj", (), (g_post[i, 1], hg_w_o[occ].astype(bf16))

        x_lat = ffn2(x_lat, m_lat, mix=mix, mix_args=(y_lat,) + mix_head + mix_tail)
        if not last:
            x_ctx = ffn2(x_ctx, m_ctx, mix=mix, mix_args=(y_ctx,) + mix_head + mix_tail)
    return x_lat
```

```python
import functools
import math

import jax
import jax.numpy as jnp
from jax import lax
from jax.experimental import pallas as pl
from jax.experimental.pallas import tpu as pltpu

f32 = jnp.float32
bf16 = jnp.bfloat16

N_SUB = 3
RMS_EPS = 1e-6
NEG_INF = -1e30
ROPE_BASE = 10000.0
GRID_W = 64
S5_GROUP = 16
S5_STATE = 64
S5_CHUNK = 16
DIFF_HEAD_DIM = 64
DIFF_KEY_CHUNK = 1024
NA_HEAD_DIM = 64
WIN_H = 8
WIN_W = 16
NA_TILE_ROWS = 8
NA_WIN_ROWS = 16
NA_KEY_CHUNK = 512
NA_TILES_PER_STEP = 8
HG_EXPAND = 128
HG_CHUNK = 64
HG_BLOCK = 256
HG_READOUT_ROWS = 256
LANES = 128
SUBLANES = 8
VMEM_LIMIT = 56 * 1024 * 1024
TOKEN_TILES = (512, 256, 128)
PROJ_TOKEN_TILE = 512
PROJ_COL_STEP = 512
FFN_COL_STEP = 1024
ATTN_QUERY_TILE = 512
ADA_COL_TILES = 4


def _cparams(*sem):
    return pltpu.CompilerParams(dimension_semantics=sem, vmem_limit_bytes=VMEM_LIMIT)


def _dot(a, b):
    return jnp.dot(a, b, preferred_element_type=f32)


def _dot_nt(a, b):
    return lax.dot_general(a, b, (((1,), (1,)), ((), ())), preferred_element_type=f32)


def _rms(x, g):
    return x * lax.rsqrt(jnp.mean(jnp.square(x), axis=-1, keepdims=True) + RMS_EPS) * g


def _pre_norm(x, g, mod_ref, j):
    shift = mod_ref[0, 3 * j:3 * j + 1, :]
    scale = mod_ref[0, 3 * j + 1:3 * j + 2, :]
    return _rms(x, g) * (1 + scale) + shift


def _post_residual(x, y, g, mod_ref, j, weight):
    gate = mod_ref[0, 3 * j + 2:3 * j + 3, :]
    return x + weight * gate * _rms(y, g)


def _tok_tile(t):
    for tm in TOKEN_TILES:
        if t % tm == 0:
            return tm
    raise ValueError(f"token count {t} is not a multiple of {TOKEN_TILES[-1]}")


def _full(shape):
    return pl.BlockSpec(shape, lambda *_: (0,) * len(shape))


def _mod_spec(mod):
    nd = mod.shape[1]
    d = mod.shape[2]
    if mod.shape[0] == 1:
        return pl.BlockSpec((1, nd, d), lambda b, t: (0, 0, 0))
    return pl.BlockSpec((1, nd, d), lambda b, t: (b, 0, 0))


def _ada_kernel(c_ref, w_ref, b_ref, o_ref):
    c = c_ref[...]
    sc = (c * jax.nn.sigmoid(c)).astype(bf16)
    o_ref[0] = _dot(sc, w_ref[0].astype(bf16)) + b_ref[0]


def _ada(c_all, w_ada, b_ada):
    depth, d, n = w_ada.shape
    rows = c_all.shape[0]
    tn = n // ADA_COL_TILES
    return pl.pallas_call(
        _ada_kernel,
        grid=(depth, n // tn),
        in_specs=[pl.BlockSpec((rows, d), lambda i, t: (0, 0)),
                  pl.BlockSpec((1, d, tn), lambda i, t: (i, 0, t)),
                  pl.BlockSpec((1, 1, tn), lambda i, t: (i, 0, t))],
        out_specs=pl.BlockSpec((1, rows, tn), lambda i, t: (i, 0, t)),
        out_shape=jax.ShapeDtypeStruct((depth, rows, n), f32),
        compiler_params=_cparams("arbitrary", "arbitrary"),
        name="ada",
    )(c_all, w_ada, b_ada.reshape(depth, 1, n))


def _ffn_kernel(x_ref, mod_ref, gin_ref, gout_ref, w1_ref, w3_ref, w2_ref, *rest, j, chunks, mix, emit_h):
    if emit_h:
        gh_ref, o_ref, h_ref = rest[-3:]
        rest = rest[:-3]
    else:
        o_ref = rest[-1]
        rest = rest[:-1]
    x = x_ref[0]
    if mix == "outproj":
        y_ref, gmix_ref, wo_ref = rest
        x = _post_residual(x, _dot(y_ref[0], wo_ref[...]), gmix_ref[...], mod_ref, 1, 1.0)
    elif mix == "glu":
        y_ref, gpre_ref, d_ref, gmix_ref, wg_ref, bg_ref = rest
        z = jax.nn.gelu(d_ref[...] * _pre_norm(x, gpre_ref[...], mod_ref, 1) + y_ref[0])
        u = _dot(z.astype(bf16), wg_ref[...]) + bg_ref[...]
        x = _post_residual(x, z * jax.nn.sigmoid(u), gmix_ref[...], mod_ref, 1, 1.0)
    h = _pre_norm(x, gin_ref[...], mod_ref, j).astype(bf16)
    acc = None
    for s, n in chunks:
        a = _dot(h, w1_ref[:, s:s + n])
        b = _dot(h, w3_ref[:, s:s + n])
        g = (a * jax.nn.sigmoid(a) * b).astype(bf16)
        y = _dot(g, w2_ref[s:s + n, :])
        acc = y if acc is None else acc + y
    out = _post_residual(x, acc, gout_ref[...], mod_ref, j, 0.5)
    o_ref[0] = out
    if emit_h:
        h_ref[0] = _pre_norm(out, gh_ref[...], mod_ref, 1).astype(h_ref.dtype)


def _ff_chunks(dff, step=FFN_COL_STEP):
    out, s = [], 0
    while s < dff:
        n = min(step, dff - s)
        out.append((s, n))
        s += n
    return tuple(out)


def _half_ffn(x, mod, g_in, g_out, w1, w3, w2, j, mix=None, mix_args=(), g_h=None):
    b, t, d = x.shape
    dff = w1.shape[1]
    tm = _tok_tile(t)
    tok = pl.BlockSpec((1, tm, d), lambda bb, tt: (bb, tt, 0))
    vec = _full((1, d))
    args = [x, mod, g_in.reshape(1, d), g_out.reshape(1, d), w1, w3, w2]
    specs = [tok, _mod_spec(mod), vec, vec, _full((d, dff)), _full((d, dff)), _full((dff, d))]
    if mix == "outproj":
        y, g_mix, w_o = mix_args
        args += [y, g_mix.reshape(1, d), w_o]
        specs += [tok, vec, _full((d, d))]
    elif mix == "glu":
        y, g_pre, d_skip, g_mix, w_g, b_g = mix_args
        args += [y, g_pre.reshape(1, d), d_skip.astype(f32).reshape(1, d), g_mix.reshape(1, d), w_g, b_g.reshape(1, d)]
        specs += [tok, vec, vec, vec, _full((d, d)), vec]
    out_specs, out_shape = tok, jax.ShapeDtypeStruct((b, t, d), f32)
    if g_h is not None:
        args.append(g_h.reshape(1, d))
        specs.append(vec)
        out_specs, out_shape = [tok, tok], [out_shape, jax.ShapeDtypeStruct((b, t, d), bf16)]
    return pl.pallas_call(
        functools.partial(_ffn_kernel, j=j, chunks=_ff_chunks(dff), mix=mix, emit_h=g_h is not None),
        grid=(b, t // tm),
        in_specs=specs,
        out_specs=out_specs,
        out_shape=out_shape,
        compiler_params=_cparams("parallel", "parallel"),
        name="half_ffn",
    )(*args)


def _swap_pairs(x):
    lane = lax.broadcasted_iota(jnp.int32, x.shape, 1)
    nxt = pltpu.roll(x, LANES - 1, 1)
    prv = pltpu.roll(x, 1, 1)
    return jnp.where((lane & 1) == 0, nxt, prv)


def _proj_kernel(*refs, j, n_out, n_rope, has_bias, step):
    x_ref, mod_ref, g_ref, w_ref = refs[:4]
    k = 4
    b_ref = None
    if has_bias:
        b_ref = refs[k]
        k += 1
    if n_rope:
        cos_ref, sin_ref = refs[k], refs[k + 1]
        k += 2
    o_ref = refs[k]
    h = _pre_norm(x_ref[0], g_ref[...], mod_ref, j).astype(bf16)
    for s in range(0, n_out, step):
        y = _dot(h, w_ref[:, s:s + step])
        if has_bias:
            y = y + b_ref[:, s:s + step]
        if s < n_rope:
            cos = cos_ref[...]
            sin = sin_ref[...]
            parts = []
            for c in range(0, step, LANES):
                yc = y[:, c:c + LANES]
                parts.append(yc * cos + _swap_pairs(yc) * sin)
            y = jnp.concatenate(parts, axis=1)
        o_ref[0, :, s:s + step] = y.astype(o_ref.dtype)


def _prenorm_proj(x, mod, g, j, w, bias=None, rope=None, n_rope=0, out_dtype=bf16):
    b, t, d = x.shape
    n = w.shape[1]
    tm = min(_tok_tile(t), PROJ_TOKEN_TILE)
    step = PROJ_COL_STEP
    assert n % step == 0 and n_rope % step == 0
    args = [x, mod, g.reshape(1, d), w]
    specs = [pl.BlockSpec((1, tm, d), lambda bb, tt: (bb, tt, 0)), _mod_spec(mod), _full((1, d)), _full((d, n))]
    if bias is not None:
        args.append(bias.reshape(1, n))
        specs.append(_full((1, n)))
    if n_rope:
        args += list(rope)
        specs += [pl.BlockSpec((tm, LANES), lambda bb, tt: (tt, 0))] * 2
    return pl.pallas_call(
        functools.partial(_proj_kernel, j=j, n_out=n, n_rope=n_rope, has_bias=bias is not None, step=step),
        grid=(b, t // tm),
        in_specs=specs,
        out_specs=pl.BlockSpec((1, tm, n), lambda bb, tt: (bb, tt, 0)),
        out_shape=jax.ShapeDtypeStruct((b, t, n), out_dtype),
        compiler_params=_cparams("parallel", "parallel"),
        name="prenorm_proj",
    )(*args)


def _s5_kernel(xc_ref, xl_ref, wb_ref, wk_ref, wc_ref, lam_ref, yc_ref, yl_ref, s_ref, hin_ref, *, nb, nc_ctx, nc):
    rc = nc_ctx * nb
    xc, xl = xc_ref[0], xl_ref[0]
    s_ref[0:rc, :] = _dot(xc, wb_ref[0])
    s_ref[rc:, :] = _dot(xl, wb_ref[0])
    P = S5_STATE
    lam = lam_ref[0]
    a_re, a_im = lam[0:1], lam[1:2]
    fwd_lane = lax.broadcasted_iota(jnp.int32, (nb, 2 * P), 1) < P

    def step(n, carry):
        h_re, h_im = carry
        r_f = pl.multiple_of(n * nb, nb)
        n_b = jnp.where(n < nc_ctx, nc_ctx - 1 - n, nc - 1 - (n - nc_ctx))
        r_b = pl.multiple_of(n_b * nb, nb)
        rows_f, rows_b = pl.ds(r_f, nb), pl.ds(r_b, nb)
        hin_ref[rows_f, 0:P] = h_re[:, 0:P]
        hin_ref[rows_b, P:2 * P] = h_re[:, P:2 * P]
        hin_ref[rows_f, 2 * P:3 * P] = h_im[:, 0:P]
        hin_ref[rows_b, 3 * P:4 * P] = h_im[:, P:2 * P]
        s_re = jnp.where(fwd_lane, s_ref[rows_f, 0:2 * P], s_ref[rows_b, 0:2 * P])
        s_im = jnp.where(fwd_lane, s_ref[rows_f, 2 * P:4 * P], s_ref[rows_b, 2 * P:4 * P])
        return a_re * h_re - a_im * h_im + s_re, a_re * h_im + a_im * h_re + s_im

    zero = jnp.zeros((nb, 2 * P), f32)
    lax.fori_loop(0, nc, step, (zero, zero), unroll=4 if nc % 4 == 0 else 1)
    yc_ref[0] = _dot(xc, wk_ref[0]) + _dot(hin_ref[0:rc, :].astype(bf16), wc_ref[0])
    yl_ref[0] = _dot(xl, wk_ref[0]) + _dot(hin_ref[rc:, :].astype(bf16), wc_ref[0])

def _s5_weights(a_re, a_im, log_dt, b_re, b_im, c_re, c_im):
    hp = lax.Precision.HIGHEST
    C = S5_CHUNK
    G, P = a_re.shape[1:]
    N = b_re.shape[-1]
    j = jnp.arange(C + 1, dtype=f32)
    wbs, wks, wcs, lams = [], [], [], []
    for dirn in range(2):
        are, aim = a_re[dirn].astype(f32), a_im[dirn].astype(f32)
        dt = jnp.exp(log_dt[dirn].astype(f32))[:, None]
        pw_mag = jnp.exp(j[:, None, None] * (are * dt)[None])
        pw_re = pw_mag * jnp.cos(j[:, None, None] * (aim * dt)[None])
        pw_im = pw_mag * jnp.sin(j[:, None, None] * (aim * dt)[None])
        nr, ni = pw_re[1] - 1.0, pw_im[1]
        den = are * are + aim * aim
        fr = (nr * are + ni * aim) / den
        fi = (ni * are - nr * aim) / den
        bre, bim = b_re[dirn].astype(f32), b_im[dirn].astype(f32)
        bbr = fr[..., None] * bre - fi[..., None] * bim
        bbi = fr[..., None] * bim + fi[..., None] * bre
        cre, cim = c_re[dirn].astype(f32), c_im[dirn].astype(f32)
        dr, di = (pw_re[:C][::-1], pw_im[:C][::-1]) if dirn == 0 else (pw_re[:C], pw_im[:C])
        sr = dr[..., None] * bbr[None] - di[..., None] * bbi[None]
        si = dr[..., None] * bbi[None] + di[..., None] * bbr[None]
        wb = jnp.concatenate([sr, si], axis=2)
        wbs.append(wb.transpose(1, 0, 3, 2).reshape(G, C * N, 2 * P))
        pr, pi = pw_re[1:], pw_im[1:]
        if dirn == 1:
            pr, pi = pr[::-1], pi[::-1]
        or_ = cre[None] * pr[:, :, None, :] - cim[None] * pi[:, :, None, :]
        oi_ = -(cre[None] * pi[:, :, None, :] + cim[None] * pr[:, :, None, :])
        wc = jnp.concatenate([or_, oi_], axis=3)
        wcs.append(wc.transpose(1, 3, 0, 2).reshape(G, 2 * P, C * N))
        clr = cre[None] * pw_re[:C, :, None, :] - cim[None] * pw_im[:C, :, None, :]
        cli = cre[None] * pw_im[:C, :, None, :] + cim[None] * pw_re[:C, :, None, :]
        kk = (jnp.einsum('jgnp,gpm->gjnm', clr, bbr, precision=hp)
              - jnp.einsum('jgnp,gpm->gjnm', cli, bbi, precision=hp))
        s_idx = jnp.arange(C)[:, None]
        t_idx = jnp.arange(C)[None, :]
        lag = (t_idx - s_idx) if dirn == 0 else (s_idx - t_idx)
        onehot = (lag[:, :, None] == jnp.arange(C)[None, None, :]).astype(f32)
        kt = jnp.einsum('stj,gjnm->gsmtn', onehot, kk, precision=hp)
        wks.append(kt.reshape(G, C * N, C * N))
        lams.append((pw_re[C], pw_im[C]))
    wb = jnp.concatenate([wbs[0][..., :P], wbs[1][..., :P], wbs[0][..., P:], wbs[1][..., P:]], axis=2).astype(bf16)
    wc = jnp.concatenate([wcs[0][:, :P], wcs[1][:, :P], wcs[0][:, P:], wcs[1][:, P:]], axis=1).astype(bf16)
    wk = (wks[0] + wks[1]).astype(bf16)
    lam = jnp.stack([jnp.concatenate([lams[0][0], lams[1][0]], axis=-1),
                     jnp.concatenate([lams[0][1], lams[1][1]], axis=-1)], axis=1)
    return wb, wk, wc, lam


def _s5_scan(h_lat, h_ctx, a_re, a_im, log_dt, b_re, b_im, c_re, c_im):
    B, S, D = h_lat.shape
    L = h_ctx.shape[1]
    C, N, P = S5_CHUNK, S5_GROUP, S5_STATE
    G = D // N
    nc_ctx, nc = L // C, (L + S) // C
    assert 2 * P == LANES and L % C == 0 and S % C == 0
    wb, wk, wc, lam = _s5_weights(a_re, a_im, log_dt, b_re, b_im, c_re, c_im)

    def to_groups(h):
        n = h.shape[1] // C
        return h.reshape(B, n, C, G, N).transpose(3, 1, 0, 2, 4).reshape(G, n * B, C * N)

    def from_groups(y):
        n = y.shape[1] // B
        return y.reshape(G, n, B, C, N).transpose(2, 1, 3, 0, 4).reshape(B, n * C, D)

    rc, rl = nc_ctx * B, (nc - nc_ctx) * B

    def rows(r):
        return pl.BlockSpec((1, r, C * N), lambda g: (g, 0, 0))

    yc, yl = pl.pallas_call(
        functools.partial(_s5_kernel, nb=B, nc_ctx=nc_ctx, nc=nc),
        grid=(G,),
        in_specs=[rows(rc), rows(rl),
                  pl.BlockSpec((1, C * N, 4 * P), lambda g: (g, 0, 0)),
                  pl.BlockSpec((1, C * N, C * N), lambda g: (g, 0, 0)),
                  pl.BlockSpec((1, 4 * P, C * N), lambda g: (g, 0, 0)),
                  pl.BlockSpec((1, 2, 2 * P), lambda g: (g, 0, 0))],
        out_specs=[rows(rc), rows(rl)],
        out_shape=[jax.ShapeDtypeStruct((G, rc, C * N), f32), jax.ShapeDtypeStruct((G, rl, C * N), f32)],
        scratch_shapes=[pltpu.VMEM((rc + rl, 4 * P), f32), pltpu.VMEM((rc + rl, 4 * P), f32)],
        compiler_params=_cparams("parallel"),
        name="s5_scan",
    )(to_groups(h_ctx), to_groups(h_lat), wb, wk, wc, lam)
    return from_groups(yl), from_groups(yc)


def _diff_maps(q):
    lane = lax.broadcasted_iota(jnp.int32, q.shape, 1)
    qs = (q.astype(f32) * (DIFF_HEAD_DIM ** -0.5 * math.log2(math.e))).astype(bf16)
    zero = jnp.zeros_like(qs)
    return jnp.where(lane < DIFF_HEAD_DIM, qs, zero), jnp.where(lane >= DIFF_HEAD_DIM, qs, zero)


def _key_chunks(k_refs):
    out = []
    for seg, k_ref in enumerate(k_refs):
        n = k_ref.shape[1]
        out += [(seg, c, min(c + DIFF_KEY_CHUNK, n)) for c in range(0, n, DIFF_KEY_CHUNK)]
    return out


def _diff_step(q_next, cur, nxt, k_refs, v_refs, lamv_ref, g_ref, o_ref, lam_init):
    chunks = _key_chunks(k_refs)
    cur_s, cur_m = cur
    nxt_s, nxt_m = nxt
    tq = cur_m.shape[1]
    m1, m2 = cur_m[0, :, 0:1], cur_m[1, :, 0:1]
    q1n, q2n = _diff_maps(q_next)
    n1 = n2 = l1 = l2 = None
    for seg, a, b in chunks:
        s1 = _dot_nt(q1n, k_refs[seg][0, a:b, :])
        s2 = _dot_nt(q2n, k_refs[seg][0, a:b, :])
        nxt_s[seg][0, :, a:b] = s1
        nxt_s[seg][1, :, a:b] = s2
        c1 = jnp.max(s1, axis=-1, keepdims=True)
        c2 = jnp.max(s2, axis=-1, keepdims=True)
        n1 = c1 if n1 is None else jnp.maximum(n1, c1)
        n2 = c2 if n2 is None else jnp.maximum(n2, c2)
        p1 = jnp.exp2(cur_s[seg][0, :, a:b] - m1)
        p2 = jnp.exp2(cur_s[seg][1, :, a:b] - m2)
        cur_s[seg][0, :, a:b] = p1
        cur_s[seg][1, :, a:b] = p2
        r1 = jnp.sum(p1, axis=-1, keepdims=True)
        r2 = jnp.sum(p2, axis=-1, keepdims=True)
        l1 = r1 if l1 is None else l1 + r1
        l2 = r2 if l2 is None else l2 + r2
    nxt_m[0] = jnp.broadcast_to(n1, (tq, LANES))
    nxt_m[1] = jnp.broadcast_to(n2, (tq, LANES))
    lv = lamv_ref[...]
    lam = (jnp.exp(jnp.sum(lv[0:1] * lv[1:2], axis=-1, keepdims=True))
           - jnp.exp(jnp.sum(lv[2:3] * lv[3:4], axis=-1, keepdims=True)) + lam_init)
    c = lam * l1 / l2
    o = None
    for seg, a, b in chunks:
        w = (cur_s[seg][0, :, a:b] - c * cur_s[seg][1, :, a:b]).astype(bf16)
        part = _dot(w, v_refs[seg][0, a:b, :])
        o = part if o is None else o + part
    o = o / l1
    o_ref[0] = (_rms(o, g_ref[...]) * (1 - lam_init)).astype(o_ref.dtype)


def _diff_kernel(*refs, n_seg, lam_init):
    q_ref, qn_ref = refs[:2]
    k_refs = refs[2:2 + 3 * n_seg:3]
    v_refs = refs[3:3 + 3 * n_seg:3]
    kn_refs = refs[4:4 + 3 * n_seg:3]
    lamv_ref, g_ref, o_ref = refs[2 + 3 * n_seg:5 + 3 * n_seg]
    scratch = refs[5 + 3 * n_seg:]
    slots = tuple((scratch[p * (n_seg + 1):p * (n_seg + 1) + n_seg], scratch[p * (n_seg + 1) + n_seg])
                  for p in range(2))
    unit = (pl.program_id(0) * pl.num_programs(1) + pl.program_id(1)) * pl.num_programs(2) + pl.program_id(2)

    @pl.when(unit == 0)
    def _():
        q1, q2 = _diff_maps(q_ref[0])
        s_refs, m_ref = slots[0]
        for mp, qm in enumerate((q1, q2)):
            parts = [_dot_nt(qm, k_ref[0]) for k_ref in k_refs]
            for s_ref, sc in zip(s_refs, parts):
                s_ref[mp] = sc
            mx = functools.reduce(jnp.maximum, [jnp.max(sc, axis=-1, keepdims=True) for sc in parts])
            m_ref[mp] = jnp.broadcast_to(mx, m_ref.shape[1:])

    for parity in range(2):
        @pl.when((unit & 1) == parity)
        def _(parity=parity):
            _diff_step(qn_ref[0], slots[parity], slots[1 - parity], kn_refs, v_refs, lamv_ref, g_ref, o_ref,
                       lam_init)


def _diff_attention(qkv_q, qkv_ctx, qkv_lat, lamv, subln, lam_init):
    B, T, n3 = qkv_q.shape
    D = n3 // 3
    H = D // LANES
    tq = min(_tok_tile(T), ATTN_QUERY_TILE)
    nq = T // tq
    kv = [qkv_ctx] if qkv_lat is None else [qkv_ctx, qkv_lat]

    def following(b, h, t):
        u = jnp.minimum((b * H + h) * nq + t + 1, B * H * nq - 1)
        return u // (H * nq), (u // nq) % H, u % nq

    def q_next(b, h, t):
        nb, nh, nt = following(b, h, t)
        return nb, nt, nh

    def k_next(b, h, t):
        nb, nh, _ = following(b, h, t)
        return nb, 0, H + nh

    args = [qkv_q, qkv_q]
    specs = [pl.BlockSpec((1, tq, LANES), lambda b, h, t: (b, t, h)), pl.BlockSpec((1, tq, LANES), q_next)]
    for a in kv:
        args += [a, a, a]
        specs += [pl.BlockSpec((1, a.shape[1], LANES), lambda b, h, t: (b, 0, H + h)),
                  pl.BlockSpec((1, a.shape[1], LANES), lambda b, h, t: (b, 0, 2 * H + h)),
                  pl.BlockSpec((1, a.shape[1], LANES), k_next)]
    args += [lamv, subln.reshape(1, LANES)]
    specs += [_full(lamv.shape), _full((1, LANES))]
    slot = [pltpu.VMEM((2, tq, a.shape[1]), f32) for a in kv] + [pltpu.VMEM((2, tq, LANES), f32)]
    return pl.pallas_call(
        functools.partial(_diff_kernel, n_seg=len(kv), lam_init=lam_init),
        grid=(B, H, nq),
        in_specs=specs,
        out_specs=pl.BlockSpec((1, tq, LANES), lambda b, h, t: (b, t, h)),
        out_shape=jax.ShapeDtypeStruct((B, T, D), bf16),
        scratch_shapes=slot + slot,
        compiler_params=_cparams("arbitrary", "arbitrary", "arbitrary"),
        name="diff_attention",
    )(*args)


def _rope_tables(n_tokens, head_dim):
    n_freq = head_dim // 4
    inv_freq = ROPE_BASE ** (-jnp.arange(n_freq, dtype=f32) / n_freq)
    t = jnp.arange(n_tokens)
    row = (t // GRID_W).astype(f32)
    col = (t % GRID_W).astype(f32)
    ang = jnp.concatenate([row[:, None] * inv_freq, col[:, None] * inv_freq], axis=-1)
    cos = jnp.repeat(jnp.cos(ang), 2, axis=-1)
    sin = jnp.repeat(jnp.sin(ang), 2, axis=-1) * jnp.tile(jnp.array([-1.0, 1.0], f32), head_dim // 2)
    reps = LANES // head_dim
    return jnp.tile(cos, (1, reps)), jnp.tile(sin, (1, reps))


def _na_head(q, hh):
    lane = lax.broadcasted_iota(jnp.int32, q.shape, 1)
    qs = (q.astype(f32) * (NA_HEAD_DIM ** -0.5 * math.log2(math.e))).astype(bf16)
    sel = (lane < NA_HEAD_DIM) if hh == 0 else (lane >= NA_HEAD_DIM)
    return jnp.where(sel, qs, jnp.zeros_like(qs))


def _na_ctx_kernel(q_ref, kc_ref, vc_ref, o_ref):
    q = q_ref[0]
    lane = lax.broadcasted_iota(jnp.int32, q.shape, 1)
    outs = []
    for hh in range(2):
        s = _dot_nt(_na_head(q, hh), kc_ref[0])
        p = jnp.exp2(s - jnp.max(s, axis=-1, keepdims=True))
        outs.append(_dot(p.astype(bf16), vc_ref[0]) / jnp.sum(p, axis=-1, keepdims=True))
    o_ref[0] = jnp.where(lane < NA_HEAD_DIM, outs[0], outs[1]).astype(o_ref.dtype)


def _na_window_table(rows):
    table = []
    for t in range(rows // NA_TILE_ROWS):
        w0 = min(max(t * NA_TILE_ROWS - (NA_WIN_ROWS - NA_TILE_ROWS) // 2, 0), rows - NA_WIN_ROWS)
        line = [w0 * GRID_W]
        for i in range(NA_TILE_ROWS):
            qr = t * NA_TILE_ROWS + i
            r0 = min(max(qr - WIN_H // 2, 0), rows - WIN_H)
            for j in range(NA_WIN_ROWS // 2):
                kr = w0 + 2 * j
                e = kr - qr + WIN_H
                ok0 = r0 <= kr < r0 + WIN_H
                ok1 = r0 <= kr + 1 < r0 + WIN_H
                line.append(e if ok0 and ok1 else 2 * WIN_H - 1 + e if ok0 else 4 * WIN_H - 1 + e if ok1 else 0)
        table.append(line)
    return jnp.array(table, jnp.int32)


def _na_window(win_ref, t):
    n_pairs = NA_WIN_ROWS // 2
    entry = [[win_ref[t, 1 + i * n_pairs + j] for j in range(n_pairs)] for i in range(NA_TILE_ROWS)]
    return pl.multiple_of(win_ref[t, 0], 4 * GRID_W), entry


def _lane_fold(x, op):
    return functools.reduce(op, [x[:, c:c + LANES] for c in range(0, x.shape[1], LANES)])


def _na_biased(sl, hh, entry, j0, tb_ref, s_ref, c0):
    tiles = []
    for i in range(NA_TILE_ROWS):
        r = slice(i * GRID_W, (i + 1) * GRID_W)
        mx = None
        for jj in range(sl.shape[1] // LANES):
            blk = sl[r, jj * LANES:(jj + 1) * LANES] + tb_ref[0, hh, entry[i][j0 + jj]]
            s_ref[r, c0 + jj * LANES:c0 + (jj + 1) * LANES] = blk
            mx = blk if mx is None else jnp.maximum(mx, blk)
        tiles.append(mx)
    return jnp.concatenate(tiles, axis=0)


def _na_unit(make, use, kc_ref, vc_ref, kl_ref, vl_ref, tb_ref):
    q, hh_m, tok_m, entry, (mc_ref, ml_ref, mm_ref) = make
    hh_u, tok_u, (uc_ref, ul_ref, um_ref) = use
    qh = _na_head(q, hh_m)
    m = um_ref[:, 0:1]
    run_max = l = o = None
    for c0 in [None] + list(range(0, NA_WIN_ROWS * GRID_W, NA_KEY_CHUNK)):
        if c0 is None:
            sc = _dot_nt(qh, kc_ref[0])
            mc_ref[...] = sc
            run_max = _lane_fold(sc, jnp.maximum)
            p = jnp.exp2(uc_ref[...] - m)
            v = vc_ref[0]
        else:
            sl = _dot_nt(qh, kl_ref[0, pl.ds(tok_m + c0, NA_KEY_CHUNK), :])
            run_max = jnp.maximum(run_max, _na_biased(sl, hh_m, entry, c0 // LANES, tb_ref, ml_ref, c0))
            p = jnp.exp2(ul_ref[:, c0:c0 + NA_KEY_CHUNK] - m)
            v = vl_ref[0, pl.ds(tok_u + c0, NA_KEY_CHUNK), :]
        r = _lane_fold(p, jnp.add)
        part = _dot(p.astype(bf16), v)
        l = r if l is None else l + r
        o = part if o is None else o + part
    mm_ref[...] = jnp.broadcast_to(jnp.max(run_max, axis=-1, keepdims=True), mm_ref.shape)
    return o / jnp.sum(l, axis=-1, keepdims=True)


def _na_kernel(win_ref, q_ref, qn_ref, kc_ref, vc_ref, kl_ref, vl_ref, tb_ref, o_ref, *scratch, n_tiles, tps):
    slots = (scratch[:3], scratch[3:])
    t = pl.program_id(2)
    tq = qn_ref.shape[1]
    tile_ids = [t * tps + k for k in range(tps)] + [jnp.minimum((t + 1) * tps, n_tiles - 1)]
    windows = [_na_window(win_ref, i) for i in tile_ids]
    qs = [q_ref[0, k * tq:(k + 1) * tq, :] for k in range(tps)] + [qn_ref[0]]
    shared = (kc_ref, vc_ref, kl_ref, vl_ref, tb_ref)

    @pl.when(t == 0)
    def _():
        tok, entry = windows[0]
        qh = _na_head(qs[0], 0)
        sc = _dot_nt(qh, kc_ref[0])
        slots[0][0][...] = sc
        sl = _dot_nt(qh, kl_ref[0, pl.ds(tok, NA_WIN_ROWS * GRID_W), :])
        run_max = jnp.maximum(_lane_fold(sc, jnp.maximum), _na_biased(sl, 0, entry, 0, tb_ref, slots[0][1], 0))
        slots[0][2][...] = jnp.broadcast_to(jnp.max(run_max, axis=-1, keepdims=True), slots[0][2].shape)

    units = [(k, hh) for k in range(tps) for hh in range(2)] + [(tps, 0)]
    lane = lax.broadcasted_iota(jnp.int32, (tq, LANES), 1)
    outs = {}
    for (k, hh), (nk, nhh) in zip(units[:-1], units[1:]):
        make = (qs[nk], nhh, windows[nk][0], windows[nk][1], slots[nhh])
        outs[k, hh] = _na_unit(make, (hh, windows[k][0], slots[hh]), *shared)
        if hh == 1:
            o_ref[0, k * tq:(k + 1) * tq, :] = jnp.where(lane < NA_HEAD_DIM, outs[k, 0], outs[k, 1]).astype(o_ref.dtype)


def _na_bias_blocks(rpb):
    H, n_dr, n_dc = rpb.shape
    col = jnp.arange(GRID_W)
    c0 = jnp.clip(col - WIN_W // 2, 0, GRID_W - WIN_W)
    col_ok = (col[None, :] >= c0[:, None]) & (col[None, :] < c0[:, None] + WIN_W)
    dc = jnp.clip(col[None, :] - col[:, None] + WIN_W - 1, 0, n_dc - 1)
    onehot = (dc.reshape(1, -1) == jnp.arange(n_dc)[:, None]).astype(f32)
    blk = jnp.einsum('hrj,jx->hrx', rpb.astype(f32) * math.log2(math.e), onehot, precision=lax.Precision.HIGHEST)
    blk = jnp.where(col_ok[None, None], blk.reshape(H, n_dr, GRID_W, GRID_W), NEG_INF)
    blk = jnp.concatenate([jnp.full((H, 1, GRID_W, GRID_W), NEG_INF, f32), blk], axis=1)
    n_e = 2 * WIN_H - 1
    none = blk[:, 0:1]
    none_run = jnp.broadcast_to(none, (H, n_e, GRID_W, GRID_W))
    left = jnp.concatenate([none, blk[:, 1:n_e], none, blk[:, 1:n_e + 1], none_run], axis=1)
    right = jnp.concatenate([none, blk[:, 2:n_e + 1], none, none_run, blk[:, 1:n_e + 1]], axis=1)
    tab = jnp.concatenate([left, right], axis=-1)
    return tab.reshape((H // 2, 2) + tab.shape[1:])


def _na_attention(qkv_q, qkv_ctx, qkv_lat, bias):
    B, T, n3 = qkv_q.shape
    D = n3 // 3
    HP = D // LANES
    L = qkv_ctx.shape[1]
    ctx_k = pl.BlockSpec((1, L, LANES), lambda b, h, t: (b, 0, HP + h))
    ctx_v = pl.BlockSpec((1, L, LANES), lambda b, h, t: (b, 0, 2 * HP + h))
    if qkv_lat is None:
        tq = _tok_tile(T)
        tile = pl.BlockSpec((1, tq, LANES), lambda b, h, t: (b, t, h))
        return pl.pallas_call(
            _na_ctx_kernel,
            grid=(B, HP, T // tq),
            in_specs=[tile, ctx_k, ctx_v],
            out_specs=tile,
            out_shape=jax.ShapeDtypeStruct((B, T, D), bf16),
            compiler_params=_cparams("parallel", "parallel", "parallel"),
            name="na_attention_ctx",
        )(qkv_q, qkv_ctx, qkv_ctx)
    tq = NA_TILE_ROWS * GRID_W
    n_tiles = T // tq
    n_win = NA_WIN_ROWS * GRID_W
    assert T % tq == 0 and T // GRID_W >= NA_WIN_ROWS and bias.shape[2] == 6 * WIN_H - 2
    tps = max(d for d in range(1, NA_TILES_PER_STEP + 1) if n_tiles % d == 0)
    tile = pl.BlockSpec((1, tps * tq, LANES), lambda b, h, t: (b, t, h))
    slot = [pltpu.VMEM((tq, L), f32), pltpu.VMEM((tq, n_win), f32), pltpu.VMEM((tq, LANES), f32)]
    return pl.pallas_call(
        functools.partial(_na_kernel, n_tiles=n_tiles, tps=tps),
        grid=(B, HP, n_tiles // tps),
        in_specs=[pl.BlockSpec(memory_space=pltpu.SMEM), tile,
                  pl.BlockSpec((1, tq, LANES), lambda b, h, t: (b, jnp.minimum((t + 1) * tps, n_tiles - 1), h)),
                  ctx_k, ctx_v,
                  pl.BlockSpec((1, T, LANES), lambda b, h, t: (b, 0, HP + h)),
                  pl.BlockSpec((1, T, LANES), lambda b, h, t: (b, 0, 2 * HP + h)),
                  pl.BlockSpec((1,) + bias.shape[1:], lambda b, h, t: (h, 0, 0, 0, 0))],
        out_specs=tile,
        out_shape=jax.ShapeDtypeStruct((B, T, D), bf16),
        scratch_shapes=slot + slot,
        compiler_params=_cparams("parallel", "parallel", "arbitrary"),
        name="na_attention",
    )(_na_window_table(T // GRID_W), qkv_q, qkv_q, qkv_ctx, qkv_ctx, qkv_lat, qkv_lat, bias)


def _chunk_cumsum(x, reverse):
    n = x.shape[0]
    pos = lax.broadcasted_iota(jnp.int32, x.shape, 0) & (HG_CHUNK - 1)
    k = 1
    while k < HG_CHUNK:
        if reverse:
            x = x + jnp.where(pos < HG_CHUNK - k, pltpu.roll(x, n - k, 0), 0.0)
        else:
            x = x + jnp.where(pos >= k, pltpu.roll(x, k, 0), 0.0)
        k *= 2
    return x


def _hg_mask(n, reverse):
    ri = lax.broadcasted_iota(jnp.int32, (n, n), 0)
    ci = lax.broadcasted_iota(jnp.int32, (n, n), 1)
    shift = HG_CHUNK.bit_length() - 1
    return ((ri >> shift) == (ci >> shift)) & ((ci >= ri) if reverse else (ci <= ri))


def _hg_block(z, v, q, lb, st, reverse, tri):
    with_out = q is not None
    n = z.shape[0]
    nch = n // HG_CHUNK
    f = lb + (1 - lb) * jax.nn.sigmoid(z)
    lf = jnp.log(f)
    kk = 1 - f
    g = _chunk_cumsum(lf, reverse)
    out = None
    if with_out:
        q_dec = q * jnp.exp(g)
        k_inv = kk * jnp.exp(-g)
        att = jnp.where(tri, _dot_nt(q_dec.astype(bf16), k_inv.astype(bf16)), 0.0)
        out_intra = _dot(att.astype(bf16), v.astype(bf16))
        outs = [None] * nch
    order = range(nch - 1, -1, -1) if reverse else range(nch)
    for c in order:
        sl = slice(c * HG_CHUNK, (c + 1) * HG_CHUNK)
        end = c * HG_CHUNK if reverse else (c + 1) * HG_CHUNK - 1
        g_tot = g[end:end + 1]
        if with_out:
            outs[c] = _dot_nt(q_dec[sl].astype(bf16), st.astype(bf16))
        kdec = kk[sl] * jnp.exp(g_tot - g[sl])
        ds_t = _dot(v[sl].T.astype(bf16), kdec.astype(bf16))
        st = st * jnp.exp(g_tot) + ds_t
    if with_out:
        out = out_intra + jnp.concatenate(outs, axis=0)
    return out, st


def _hg_kernel(q_ref, i_ref, gate_ref, zf_ref, zb_ref, ic_ref, zfc_ref, zbc_ref, lb_ref, gn_ref, o_ref,
               accf_ref, accb_ref, *, s_tok):
    lb = lb_ref[...]
    nblk = s_tok // HG_BLOCK
    zero = jnp.zeros((HG_EXPAND, HG_EXPAND), f32)
    _, st_f = _hg_block(zfc_ref[0], ic_ref[0], None, lb, zero, False, None)
    _, st_b = _hg_block(zbc_ref[0], ic_ref[0], None, lb, zero, True, None)
    tri_f, tri_b = _hg_mask(HG_BLOCK, False), _hg_mask(HG_BLOCK, True)

    def body(n, carry):
        st_f, st_b = carry
        rows_f = pl.ds(pl.multiple_of(n * HG_BLOCK, HG_BLOCK), HG_BLOCK)
        rows_b = pl.ds(pl.multiple_of((nblk - 1 - n) * HG_BLOCK, HG_BLOCK), HG_BLOCK)
        out_f, st_f = _hg_block(zf_ref[0, rows_f, :], i_ref[0, rows_f, :], q_ref[0, rows_f, :], lb, st_f, False, tri_f)
        out_b, st_b = _hg_block(zb_ref[0, rows_b, :], i_ref[0, rows_b, :], q_ref[0, rows_b, :], lb, st_b, True, tri_b)
        accf_ref[rows_f, :] = out_f
        accb_ref[rows_b, :] = out_b
        return st_f, st_b

    lax.fori_loop(0, nblk, body, (st_f, st_b), unroll=16 if nblk % 16 == 0 else 1)
    for r0 in range(0, s_tok, HG_READOUT_ROWS):
        rows = slice(r0, min(r0 + HG_READOUT_ROWS, s_tok))
        gate = gate_ref[0, rows, :]
        o = accf_ref[rows, :] + accb_ref[rows, :]
        o_ref[0, rows, :] = (_rms(o, gn_ref[...]) * (gate * jax.nn.sigmoid(gate))).astype(o_ref.dtype)


def _hgrn2_core(p_lat, p_ctx, lb, gn_g):
    B, S, n5 = p_lat.shape
    D = n5 // 5
    H = D // HG_EXPAND
    L = p_ctx.shape[1]
    assert S % HG_BLOCK == 0 and L % HG_CHUNK == 0

    def lat(k):
        return pl.BlockSpec((1, S, HG_EXPAND), lambda b, h: (b, 0, k * H + h))

    def ctx(k):
        return pl.BlockSpec((1, L, HG_EXPAND), lambda b, h: (b, 0, k * H + h))

    vec = pl.BlockSpec((1, HG_EXPAND), lambda b, h: (0, h))
    return pl.pallas_call(
        functools.partial(_hg_kernel, s_tok=S),
        grid=(B, H),
        in_specs=[lat(0), lat(1), lat(2), lat(3), lat(4), ctx(1), ctx(3), ctx(4), vec, _full((1, HG_EXPAND))],
        out_specs=pl.BlockSpec((1, S, HG_EXPAND), lambda b, h: (b, 0, h)),
        out_shape=jax.ShapeDtypeStruct((B, S, D), bf16),
        scratch_shapes=[pltpu.VMEM((S, HG_EXPAND), f32), pltpu.VMEM((S, HG_EXPAND), f32)],
        compiler_params=_cparams("parallel", "parallel"),
        name="hgrn2_core",
    )(p_lat, p_lat, p_lat, p_lat, p_lat, p_ctx, p_ctx, p_ctx, lb.reshape(1, D), gn_g.reshape(1, HG_EXPAND))


def kernel(x, c, ctx, c_ctx, w_ada, b_ada, g_pre, g_post, w_ff1, w_ff3, w_ff2, s5_a_re, s5_a_im, s5_log_dt, s5_b_re, s5_b_im, s5_c_re, s5_c_im, s5_d, s5_w_glu, s5_b_glu, da_w_qkv, da_w_o, da_lam_q1, da_lam_k1, da_lam_q2, da_lam_k2, da_subln, na_w_qkv, na_w_o, na_rpb, hg_w_qig, hg_w_f, hg_b_f, hg_lb_logits, hg_gnorm, hg_w_o):
    B, S, D = x.shape
    depth = w_ada.shape[0]
    n_mix = 4
    rows_pad = -(-(B + 1) // SUBLANES) * SUBLANES
    c_all = jnp.concatenate([c, c_ctx[None], jnp.zeros((rows_pad - B - 1, D), f32)], axis=0)
    mods = _ada(c_all, w_ada, b_ada)
    lb_p = jax.nn.softmax(hg_lb_logits.astype(f32), axis=0)
    lower_bounds = jnp.cumsum(lb_p, axis=0) - lb_p[0]
    w1b, w3b, w2b = w_ff1.astype(bf16), w_ff3.astype(bf16), w_ff2.astype(bf16)

    x_lat, x_ctx = x, ctx
    for i in range(depth):
        last = i == depth - 1
        occ, kind = i // n_mix, i % n_mix
        m_lat = mods[i, :B].reshape(B, 3 * N_SUB, D)
        m_ctx = mods[i, B:B + 1].reshape(1, 3 * N_SUB, D)
        ffn1 = functools.partial(_half_ffn, g_in=g_pre[i, 0], g_out=g_post[i, 0], w1=w1b[i, 0], w3=w3b[i, 0],
                                 w2=w2b[i, 0], j=0)
        ffn2 = functools.partial(_half_ffn, g_in=g_pre[i, 2], g_out=g_post[i, 2], w1=w1b[i, 1], w3=w3b[i, 1],
                                 w2=w2b[i, 1], j=2)
        if kind == 0:
            (x_lat, h_lat), (x_ctx, h_ctx) = (ffn1(x_lat, m_lat, g_h=g_pre[i, 1]),
                                              ffn1(x_ctx, m_ctx, g_h=g_pre[i, 1]))
        else:
            x_lat, x_ctx = ffn1(x_lat, m_lat), ffn1(x_ctx, m_ctx)

        y_ctx = None
        if kind == 0:
            y_lat, y_ctx = _s5_scan(h_lat, h_ctx, s5_a_re[occ], s5_a_im[occ], s5_log_dt[occ], s5_b_re[occ],
                                    s5_b_im[occ], s5_c_re[occ], s5_c_im[occ])
            mix = "glu"
            mix_head = (g_pre[i, 1], s5_d[occ])
            mix_tail = (g_post[i, 1], s5_w_glu[occ].astype(bf16), s5_b_glu[occ])
        elif kind == 1:
            wq = da_w_qkv[occ].astype(bf16)
            rope = _rope_tables(S, DIFF_HEAD_DIM)
            qkv_lat = _prenorm_proj(x_lat, m_lat, g_pre[i, 1], 1, wq, rope=rope, n_rope=2 * D)
            qkv_ctx = _prenorm_proj(x_ctx, m_ctx, g_pre[i, 1], 1, wq)
            lamv = jnp.stack([da_lam_q1[occ], da_lam_k1[occ], da_lam_q2[occ], da_lam_k2[occ]]).astype(f32)
            lam_init = 0.8 - 0.6 * math.exp(-0.3 * i)
            y_lat = _diff_attention(qkv_lat, qkv_ctx, qkv_lat, lamv, da_subln[occ], lam_init)
            if not last:
                y_ctx = _diff_attention(qkv_ctx, qkv_ctx, None, lamv, da_subln[occ], lam_init)
            mix, mix_head, mix_tail = "outproj", (), (g_post[i, 1], da_w_o[occ].astype(bf16))
        elif kind == 2:
            wq = na_w_qkv[occ].astype(bf16)
            qkv_lat = _prenorm_proj(x_lat, m_lat, g_pre[i, 1], 1, wq)
            qkv_ctx = _prenorm_proj(x_ctx, m_ctx, g_pre[i, 1], 1, wq)
            y_lat = _na_attention(qkv_lat, qkv_ctx, qkv_lat, _na_bias_blocks(na_rpb[occ]))
            if not last:
                y_ctx = _na_attention(qkv_ctx, qkv_ctx, None, None)
            mix, mix_head, mix_tail = "outproj", (), (g_post[i, 1], na_w_o[occ].astype(bf16))
        else:
            assert last, "HGRN2 context outputs are not needed when it is the last layer"
            wp = jnp.concatenate([hg_w_qig[occ], hg_w_f[occ, 0], hg_w_f[occ, 1]], axis=1).astype(bf16)
            bp = jnp.concatenate([jnp.zeros((3 * D,), f32), hg_b_f[occ, 0], hg_b_f[occ, 1]])
            p_lat = _prenorm_proj(x_lat, m_lat, g_pre[i, 1], 1, wp, bias=bp, out_dtype=f32)
            p_ctx = _prenorm_proj(x_ctx, m_ctx, g_pre[i, 1], 1, wp, bias=bp, out_dtype=f32)
            y_lat = _hgrn2_core(p_lat, p_ctx, lower_bounds[i], hg_gnorm[occ])
            mix, mix_head, mix_tail = "outproj", (), (g_post[i, 1], hg_w_o[occ].astype(bf16))

        x_lat = ffn2(x_lat, m_lat, mix=mix, mix_args=(y_lat,) + mix_head + mix_tail)
        if not last:
            x_ctx = ffn2(x_ctx, m_ctx, mix=mix, mix_args=(y_ctx,) + mix_head + mix_tail)
    return x_lat
```
